```python
import math
import jax, jax.numpy as jnp
from jax import lax
import numpy as np

D_MODEL = 2048
BATCH = 4
SEQ = 2048
DEPTH = 4
DEC_BATCH = 8
DEC_SEQ = 4
PAST_LEN = 16384
PAGE_SIZE = 128

N_MIXERS = 2
N_ATTN_LAYERS = (DEPTH + 1) // 2
N_CONV_LAYERS = DEPTH // 2
N_HEADS = 16
HEAD_DIM = D_MODEL // N_HEADS
N_KV_HEADS = 4
GROUP_SIZE = N_HEADS // N_KV_HEADS
CMP_BLOCK = 32
SEL_BLOCK = 64
SEL_RATIO = SEL_BLOCK // CMP_BLOCK
N_SELECT = 16
WINDOW = 512
WIN_BLOCK = 128
SEL_QBLOCK = 32
ROPE_THETA = 10000.0
D_CONV = D_MODEL
CONV_WIDTH = 31
D_FF = 5632
N_MOD = 9
RMS_EPS = 1e-6
LN_EPS = 1e-5
NEG_INF = -1e30
FORCED_SCORE = 1e9
Q_WIDTH = N_HEADS * HEAD_DIM
KV_WIDTH = 2 * N_KV_HEADS * HEAD_DIM
IN_WIDTH = Q_WIDTH + 3 * KV_WIDTH + 3 * N_HEADS
ATTN_SCALE = HEAD_DIM ** -0.5

kernel_name = 'nsa_conformer_macaron_adaln_decode_step'


def rms_norm(x, g):
    xf = x.astype(jnp.float32)
    y = xf * lax.rsqrt(jnp.mean(xf * xf, axis=-1, keepdims=True) + RMS_EPS)
    return (y * g.astype(jnp.float32)).astype(x.dtype)


def modulate(x, g, shift, scale):
    xf = x.astype(jnp.float32)
    y = xf * lax.rsqrt(jnp.mean(xf * xf, axis=-1, keepdims=True) + RMS_EPS) * g.astype(jnp.float32)
    y = y * (1.0 + scale[:, None, :].astype(jnp.float32)) + shift[:, None, :].astype(jnp.float32)
    return y.astype(x.dtype)


def rope(x, pos):
    half = HEAD_DIM // 2
    inv = ROPE_THETA ** (-jnp.arange(half, dtype=jnp.float32) / half)
    ang = pos.astype(jnp.float32)[:, None] * inv[None, :]
    cos = jnp.cos(ang)[:, None, :]
    sin = jnp.sin(ang)[:, None, :]
    xf = x.astype(jnp.float32)
    x1, x2 = xf[..., :half], xf[..., half:]
    return jnp.concatenate([x1 * cos - x2 * sin, x2 * cos + x1 * sin], axis=-1).astype(x.dtype)


def swiglu(h, w_gate, w_up, w_down):
    return (jax.nn.silu(h @ w_gate) * (h @ w_up)) @ w_down


def masked_softmax(s, m):
    return jax.nn.softmax(jnp.where(m, s, NEG_INF), axis=-1)


def nsa_project(h, pos, w_in):
    B, T, _ = h.shape
    z = h @ w_in
    q = rope(z[..., :Q_WIDTH].reshape(B, T, N_HEADS, HEAD_DIM), pos)
    q = q.reshape(B, T, N_KV_HEADS, GROUP_SIZE, HEAD_DIM)
    kvs = []
    for br in range(3):
        lo = Q_WIDTH + br * KV_WIDTH
        kv = z[..., lo:lo + KV_WIDTH].reshape(B, T, N_KV_HEADS, 2, HEAD_DIM)
        kvs.append(jnp.stack([rope(kv[..., 0, :], pos), kv[..., 1, :]], axis=3))
    gates = jax.nn.sigmoid(z[..., Q_WIDTH + 3 * KV_WIDTH:].astype(jnp.float32))
    gates = gates.reshape(B, T, N_KV_HEADS, GROUP_SIZE, 3).astype(h.dtype)
    return q, kvs[0], kvs[1], kvs[2], gates


def compress_blocks(rows, pe, w1, w2):
    B, L = rows.shape[:2]
    n = L // CMP_BLOCK
    blk = rows.reshape(B, n, CMP_BLOCK, N_KV_HEADS, 2, HEAD_DIM) + jnp.transpose(pe, (1, 0, 2))[:, None]
    blk = jnp.transpose(blk, (0, 1, 3, 4, 2, 5)).reshape(B, n, N_KV_HEADS, 2, CMP_BLOCK * HEAD_DIM)
    hid = jax.nn.silu(jnp.einsum('bngkf,kfe->bngke', blk, w1))
    return jnp.einsum('bngke,ked->bngkd', hid, w2)


def cmp_branch(q, q_pos, ckv):
    ck, cv = ckv[..., 0, :], ckv[..., 1, :]
    n_cmp = ck.shape[1]
    s = jnp.einsum('btghd,bcgd->btghc', q, ck).astype(jnp.float32) * ATTN_SCALE
    vis = (jnp.arange(n_cmp)[None, :] + 1) * CMP_BLOCK - 1 <= q_pos[:, None]
    m = vis[None, :, None, None, :]
    p = masked_softmax(s, m) * m
    o = jnp.einsum('btghc,bcgd->btghd', p.astype(q.dtype), cv)
    return o, p.sum(axis=3)


def select_blocks(imp, q_pos, n_total):
    B, T, G, n_cmp = imp.shape
    ns = -(-n_total // SEL_BLOCK)
    imp = jnp.pad(imp, ((0, 0), (0, 0), (0, 0), (0, ns * SEL_RATIO - n_cmp)))
    imp = imp.reshape(B, T, G, ns, SEL_RATIO).sum(-1)
    j = jnp.arange(ns)[None, :]
    cur = (q_pos // SEL_BLOCK)[:, None]
    valid = (j * SEL_BLOCK <= q_pos[:, None])[None, :, None, :]
    forced = ((j == 0) | (j == cur) | (j == cur - 1))[None, :, None, :]
    score = jnp.where(forced, FORCED_SCORE, jnp.where(valid, imp, -1.0))
    return lax.top_k(score, min(N_SELECT, ns))[1].astype(jnp.int32)


def make_prompt_gather(kv):
    B, S = kv.shape[:2]
    bi = jnp.arange(B)[:, None, None, None, None]
    gi = jnp.arange(N_KV_HEADS)[None, None, :, None, None]

    def gather(pos):
        return kv[bi, jnp.clip(pos, 0, S - 1), gi]
    return gather


def make_sample_gather(pool, page_table, kv_new, past_len):
    B, T = kv_new.shape[:2]
    page = pool.shape[1]
    bi = jnp.arange(B)[:, None, None, None, None]
    gi = jnp.arange(N_KV_HEADS)[None, None, :, None, None]

    def gather(pos):
        pp = jnp.clip(pos, 0, past_len - 1)
        phys = page_table[bi, pp // page]
        past_rows = pool[phys, pp % page, gi]
        new_rows = kv_new[bi, jnp.clip(pos - past_len, 0, T - 1), gi]
        return jnp.where((pos < past_len)[..., None, None], past_rows, new_rows)
    return gather


def sel_branch(q, q_pos, idx, gather_rows):
    B, T = q.shape[:2]
    qb = math.gcd(T, SEL_QBLOCK)
    nb = T // qb

    def split(a):
        return jnp.moveaxis(a.reshape((B, nb, qb) + a.shape[2:]), 1, 0)
    offs = jnp.arange(SEL_BLOCK, dtype=jnp.int32)

    def one_block(args):
        qq, qp, ii = args
        pos = ii[..., None] * SEL_BLOCK + offs
        rows = gather_rows(pos)
        s = jnp.einsum('btghd,btgnsd->btghns', qq, rows[..., 0, :]).astype(jnp.float32) * ATTN_SCALE
        m = (pos <= qp[None, :, None, None, None])[:, :, :, None]
        sh = s.shape
        p = masked_softmax(s, m).reshape(sh[:-2] + (-1,))
        p = jax.nn.softmax(jnp.where(jnp.broadcast_to(m, sh).reshape(p.shape), s.reshape(p.shape), NEG_INF), axis=-1).reshape(sh)
        return jnp.einsum('btghns,btgnsd->btghd', p.astype(qq.dtype), rows[..., 1, :])
    out = lax.map(one_block, (split(q), q_pos.reshape(nb, qb), split(idx)))
    return jnp.moveaxis(out, 0, 1).reshape(q.shape)


def win_branch_prompt(q, kv):
    B, S = kv.shape[:2]
    nb = S // WIN_BLOCK
    nprev = WINDOW // WIN_BLOCK
    span = (nprev + 1) * WIN_BLOCK
    kp = jnp.pad(kv, ((0, 0), (WINDOW, 0), (0, 0), (0, 0), (0, 0)))
    kp = kp.reshape(B, nb + nprev, WIN_BLOCK, N_KV_HEADS, 2, HEAD_DIM)
    bidx = jnp.arange(nb)[:, None] + jnp.arange(nprev + 1)[None, :]
    kb = kp[:, bidx].reshape(B, nb, span, N_KV_HEADS, 2, HEAD_DIM)
    qq = q.reshape(B, nb, WIN_BLOCK, N_KV_HEADS, GROUP_SIZE, HEAD_DIM)
    s = jnp.einsum('bnqghd,bnsgd->bnqghs', qq, kb[..., 0, :]).astype(jnp.float32) * ATTN_SCALE
    qpos = jnp.arange(nb)[:, None] * WIN_BLOCK + jnp.arange(WIN_BLOCK)[None, :]
    kpos = jnp.arange(nb)[:, None] * WIN_BLOCK - WINDOW + jnp.arange(span)[None, :]
    d = qpos[:, :, None] - kpos[:, None, :]
    m = ((d >= 0) & (d < WINDOW) & (kpos[:, None, :] >= 0))[None, :, :, None, None, :]
    p = masked_softmax(s, m)
    o = jnp.einsum('bnqghs,bnsgd->bnqghd', p.astype(q.dtype), kb[..., 1, :])
    return o.reshape(q.shape)


def win_branch_cached(q, q_pos, kv, k_pos):
    s = jnp.einsum('btghd,bsgd->btghs', q, kv[..., 0, :]).astype(jnp.float32) * ATTN_SCALE
    d = q_pos[:, None] - k_pos[None, :]
    m = ((d >= 0) & (d < WINDOW))[None, :, None, None, :]
    p = masked_softmax(s, m)
    return jnp.einsum('btghs,bsgd->btghd', p.astype(q.dtype), kv[..., 1, :])


def nsa_mixer(h, pos, w_in, w_out, cmp_pe, cmp_w1, cmp_w2, cache):
    B, T, _ = h.shape
    q, kv_c, kv_s, kv_w, gates = nsa_project(h, pos, w_in)
    if cache is None:
        ckv = compress_blocks(kv_c, cmp_pe, cmp_w1, cmp_w2)
        n_total = T
        gather = make_prompt_gather(kv_s)
        o_w = win_branch_prompt(q, kv_w)
        win_state = kv_w[:, T - min(WINDOW, T):]
    else:
        pool_c, pool_s, win_buf, page_table = cache
        past = page_table.shape[1] * pool_c.shape[1]
        past_rows = pool_c[page_table].reshape(B, past, N_KV_HEADS, 2, HEAD_DIM)
        ckv = compress_blocks(past_rows, cmp_pe, cmp_w1, cmp_w2)
        n_new = T // CMP_BLOCK
        if n_new > 0:
            ckv = jnp.concatenate([ckv, compress_blocks(kv_c[:, :n_new * CMP_BLOCK], cmp_pe, cmp_w1, cmp_w2)], axis=1)
        n_total = past + T
        gather = make_sample_gather(pool_s, page_table, kv_s, past)
        L = win_buf.shape[1]
        keys = jnp.concatenate([win_buf, kv_w], axis=1)
        k_pos = past - L + jnp.arange(L + T, dtype=jnp.int32)
        o_w = win_branch_cached(q, pos, keys, k_pos)
        win_state = keys[:, L + T - min(WINDOW, L + T):]
    o_c, imp = cmp_branch(q, pos, ckv)
    idx = select_blocks(imp, pos, n_total)
    o_s = sel_branch(q, pos, idx, gather)
    o = gates[..., 0:1] * o_c + gates[..., 1:2] * o_s + gates[..., 2:3] * o_w
    return o.reshape(B, T, Q_WIDTH) @ w_out, kv_c, kv_s, win_state


def conv_mixer(h, buf, w_pw1, w_dw, b_dw, ln_g, ln_b, w_pw2):
    u = h @ w_pw1
    glu = u[..., :D_CONV] * jax.nn.sigmoid(u[..., D_CONV:])
    full = jnp.concatenate([buf, glu], axis=1)
    y = lax.conv_general_dilated(full, w_dw[:, None, :], window_strides=(1,), padding='VALID',
                                 dimension_numbers=('NWC', 'WIO', 'NWC'),
                                 feature_group_count=D_CONV) + b_dw
    yf = y.astype(jnp.float32)
    mu = jnp.mean(yf, axis=-1, keepdims=True)
    var = jnp.mean(jnp.square(yf - mu), axis=-1, keepdims=True)
    yn = ((yf - mu) * lax.rsqrt(var + LN_EPS) * ln_g.astype(jnp.float32) + ln_b.astype(jnp.float32)).astype(h.dtype)
    return jax.nn.silu(yn) @ w_pw2, full[:, full.shape[1] - (CONV_WIDTH - 1):]


def run_trunk(x, c, pos, prm, caches):
    B = x.shape[0]
    c_act = jax.nn.silu(c)
    new_c, new_s, new_w, new_conv = [], [], [], []
    for i in range(DEPTH):
        mod = (c_act @ prm['w_mod'][i] + prm['b_mod'][i]).reshape(B, N_MOD, D_MODEL)
        h = modulate(x, prm['norm_g'][i, 0], mod[:, 0], mod[:, 1])
        x = x + 0.5 * mod[:, 2][:, None, :] * swiglu(h, prm['ffn_w_gate'][i, 0], prm['ffn_w_up'][i, 0], prm['ffn_w_down'][i, 0])
        h = modulate(x, prm['norm_g'][i, 1], mod[:, 3], mod[:, 4])
        if i % N_MIXERS == 0:
            a = i // N_MIXERS
            cache = None if caches is None else (caches[0][a], caches[1][a], caches[2][a], caches[4])
            y, kc, ks, kw = nsa_mixer(h, pos, prm['attn_w_in'][a], prm['attn_w_out'][a], prm['cmp_pe'][a],
                                      prm['cmp_w1'][a], prm['cmp_w2'][a], cache)
            new_c.append(kc)
            new_s.append(ks)
            new_w.append(kw)
        else:
            ci = i // N_MIXERS
            buf = jnp.zeros((B, CONV_WIDTH - 1, D_CONV), x.dtype) if caches is None else caches[3][ci]
            y, st = conv_mixer(h, buf, prm['conv_w_pw1'][ci], prm['conv_w_dw'][ci], prm['conv_b_dw'][ci],
                               prm['conv_ln_g'][ci], prm['conv_ln_b'][ci], prm['conv_w_pw2'][ci])
            new_conv.append(st)
        x = x + mod[:, 5][:, None, :] * y
        h = modulate(x, prm['norm_g'][i, 2], mod[:, 6], mod[:, 7])
        x = x + 0.5 * mod[:, 8][:, None, :] * swiglu(h, prm['ffn_w_gate'][i, 1], prm['ffn_w_up'][i, 1], prm['ffn_w_down'][i, 1])
    y = rms_norm(x, prm['final_norm_g'])
    return y, jnp.stack(new_c), jnp.stack(new_s), jnp.stack(new_w), jnp.stack(new_conv)


def _normal(key, shape, scale):
    return scale * jax.random.normal(key, shape, jnp.float32)


def setup_inputs(seed: int = 0) -> dict:
    key = jax.random.key(seed)
    ks = jax.random.split(key, 28)
    n_pages = PAST_LEN // PAGE_SIZE
    n_used = DEC_BATCH * n_pages
    n_pool = n_used + n_used // 4
    win_rows = min(WINDOW, PAST_LEN)
    page_table = jax.random.permutation(ks[6], n_pool)[:n_used].reshape(DEC_BATCH, n_pages).astype(jnp.int32)
    return {
        'x_prompt': _normal(ks[0], (BATCH, SEQ, D_MODEL), 1.0),
        'x_sample': _normal(ks[1], (DEC_BATCH, DEC_SEQ, D_MODEL), 1.0),
        'cache_cmp_kv': _normal(ks[2], (N_ATTN_LAYERS, n_pool, PAGE_SIZE, N_KV_HEADS, 2, HEAD_DIM), 1.0),
        'cache_sel_kv': _normal(ks[3], (N_ATTN_LAYERS, n_pool, PAGE_SIZE, N_KV_HEADS, 2, HEAD_DIM), 1.0),
        'cache_win_kv': _normal(ks[4], (N_ATTN_LAYERS, DEC_BATCH, win_rows, N_KV_HEADS, 2, HEAD_DIM), 1.0),
        'state_conv': _normal(ks[5], (N_CONV_LAYERS, DEC_BATCH, CONV_WIDTH - 1, D_CONV), 0.5),
        'page_table': page_table,
        'c_prompt': _normal(ks[7], (BATCH, D_MODEL), 1.0),
        'c_sample': _normal(ks[8], (DEC_BATCH, D_MODEL), 1.0),
        'w_mod': _normal(ks[9], (DEPTH, D_MODEL, N_MOD * D_MODEL), 0.5 * D_MODEL ** -0.5),
        'b_mod': _normal(ks[10], (DEPTH, N_MOD * D_MODEL), 0.01),
        'norm_g': 1.0 + _normal(ks[11], (DEPTH, 3, D_MODEL), 0.01),
        'ffn_w_gate': _normal(ks[12], (DEPTH, 2, D_MODEL, D_FF), D_MODEL ** -0.5),
        'ffn_w_up': _normal(ks[13], (DEPTH, 2, D_MODEL, D_FF), D_MODEL ** -0.5),
        'ffn_w_down': _normal(ks[14], (DEPTH, 2, D_FF, D_MODEL), D_FF ** -0.5),
        'attn_w_in': _normal(ks[15], (N_ATTN_LAYERS, D_MODEL, IN_WIDTH), D_MODEL ** -0.5),
        'attn_w_out': _normal(ks[16], (N_ATTN_LAYERS, Q_WIDTH, D_MODEL), Q_WIDTH ** -0.5),
        'cmp_pe': _normal(ks[17], (N_ATTN_LAYERS, 2, CMP_BLOCK, HEAD_DIM), 0.1),
        'cmp_w1': _normal(ks[18], (N_ATTN_LAYERS, 2, CMP_BLOCK * HEAD_DIM, HEAD_DIM), (CMP_BLOCK * HEAD_DIM) ** -0.5),
        'cmp_w2': _normal(ks[19], (N_ATTN_LAYERS, 2, HEAD_DIM, HEAD_DIM), HEAD_DIM ** -0.5),
        'conv_w_pw1': _normal(ks[20], (N_CONV_LAYERS, D_MODEL, 2 * D_CONV), D_MODEL ** -0.5),
        'conv_w_dw': _normal(ks[21], (N_CONV_LAYERS, CONV_WIDTH, D_CONV), CONV_WIDTH ** -0.5),
        'conv_b_dw': _normal(ks[22], (N_CONV_LAYERS, D_CONV), 0.01),
        'conv_ln_g': 1.0 + _normal(ks[23], (N_CONV_LAYERS, D_CONV), 0.01),
        'conv_ln_b': _normal(ks[24], (N_CONV_LAYERS, D_CONV), 0.01),
        'conv_w_pw2': _normal(ks[25], (N_CONV_LAYERS, D_CONV, D_MODEL), D_CONV ** -0.5),
        'final_norm_g': 1.0 + _normal(ks[26], (D_MODEL,), 0.01),
    }


def reference(x_prompt, x_sample, cache_cmp_kv, cache_sel_kv, cache_win_kv, state_conv, page_table,
              c_prompt, c_sample, w_mod, b_mod, norm_g, ffn_w_gate, ffn_w_up, ffn_w_down,
              attn_w_in, attn_w_out, cmp_pe, cmp_w1, cmp_w2, conv_w_pw1, conv_w_dw, conv_b_dw,
              conv_ln_g, conv_ln_b, conv_w_pw2, final_norm_g):
    prm = {'w_mod': w_mod, 'b_mod': b_mod, 'norm_g': norm_g, 'ffn_w_gate': ffn_w_gate,
           'ffn_w_up': ffn_w_up, 'ffn_w_down': ffn_w_down, 'attn_w_in': attn_w_in,
           'attn_w_out': attn_w_out, 'cmp_pe': cmp_pe, 'cmp_w1': cmp_w1, 'cmp_w2': cmp_w2,
           'conv_w_pw1': conv_w_pw1, 'conv_w_dw': conv_w_dw, 'conv_b_dw': conv_b_dw,
           'conv_ln_g': conv_ln_g, 'conv_ln_b': conv_ln_b, 'conv_w_pw2': conv_w_pw2,
           'final_norm_g': final_norm_g}
    past_len = page_table.shape[1] * cache_cmp_kv.shape[2]
    pos_p = jnp.arange(x_prompt.shape[1], dtype=jnp.int32)
    pos_s = past_len + jnp.arange(x_sample.shape[1], dtype=jnp.int32)
    y_prompt, p_cmp, p_sel, p_win, p_conv = run_trunk(x_prompt, c_prompt, pos_p, prm, None)
    y_sample, s_cmp, s_sel, s_win, s_conv = run_trunk(
        x_sample, c_sample, pos_s, prm, (cache_cmp_kv, cache_sel_kv, cache_win_kv, state_conv, page_table))
    return (y_prompt, y_sample, p_cmp, p_sel, p_win, p_conv, s_cmp, s_sel, s_win, s_conv)
```

```python
import functools

import jax
import jax.numpy as jnp
from jax import lax
from jax.experimental import pallas as pl
from jax.experimental.pallas import tpu as pltpu

F32 = jnp.float32
MXU_DTYPE = jnp.bfloat16
VMEM_LIMIT_BYTES = 56 * 1024 * 1024
LANES = 128
SUBLANES = 8

N_HEADS = 16
HEAD_DIM = 128
N_KV_HEADS = 4
GROUP_SIZE = N_HEADS // N_KV_HEADS
CMP_BLOCK = 32
SEL_BLOCK = 64
SEL_RATIO = SEL_BLOCK // CMP_BLOCK
N_SELECT = 16
WINDOW = 512
ROPE_THETA = 10000.0
N_MOD = 9
RMS_EPS = 1e-6
LN_EPS = 1e-5
NEG_INF = -1e30
FORCED_SCORE = 1e9
KNOCKED_OUT = -3e38
ATTN_SCALE = HEAD_DIM ** -0.5
Q_WIDTH = N_HEADS * HEAD_DIM
KV_WIDTH = 2 * N_KV_HEADS * HEAD_DIM
GATE_COLS = GROUP_SIZE * 3
PAGES_PER_STEP = 16
DWCONV_COLS = 256


def _params(*sem):
    return pltpu.CompilerParams(dimension_semantics=sem, vmem_limit_bytes=VMEM_LIMIT_BYTES)


def _mxu(a, b):
    return jnp.dot(a.astype(MXU_DTYPE), b.astype(MXU_DTYPE), preferred_element_type=F32)


def _mxu_nt(a, b):
    return lax.dot_general(a.astype(MXU_DTYPE), b.astype(MXU_DTYPE),
                           (((1,), (1,)), ((), ())), preferred_element_type=F32)


def _silu(x):
    return x * jax.nn.sigmoid(x)


def _softmax_rows(s):
    e = jnp.exp(s - jnp.max(s, axis=-1, keepdims=True))
    return e / jnp.sum(e, axis=-1, keepdims=True)


def _mod_spec(tm, rows_per_mod, d, grid_rank):
    if rows_per_mod >= tm:
        assert rows_per_mod % tm == 0
        block, idx = (1, 1, d), (lambda i: (i * tm) // rows_per_mod)
    else:
        assert rows_per_mod == 1
        block, idx = (tm, 1, d), (lambda i: i)
    if grid_rank == 1:
        return pl.BlockSpec(block, lambda i: (idx(i), 0, 0))
    return pl.BlockSpec(block, lambda i, j: (idx(i), 0, 0))


def _modulated(x_ref, ng_ref, sh_ref, sc_ref):
    x = x_ref[...]
    y = x * lax.rsqrt(jnp.mean(x * x, axis=-1, keepdims=True) + RMS_EPS) * ng_ref[...]
    return y * (1.0 + sc_ref[:, 0, :]) + sh_ref[:, 0, :]


def _mod_kernel(c_ref, w_ref, b_ref, o_ref):
    o_ref[...] = _mxu(_silu(c_ref[...]), w_ref[...]) + b_ref[...]


def _mod_vectors(c_all, w_mod, b_mod, tn=1024):
    depth, d, n = w_mod.shape
    r = c_all.shape[0]
    return pl.pallas_call(
        _mod_kernel,
        grid=(depth, n // tn),
        in_specs=[pl.BlockSpec((r, d), lambda l, j: (0, 0)),
                  pl.BlockSpec((None, d, tn), lambda l, j: (l, 0, j)),
                  pl.BlockSpec((None, 1, tn), lambda l, j: (l, 0, j))],
        out_specs=pl.BlockSpec((None, r, tn), lambda l, j: (l, 0, j)),
        out_shape=jax.ShapeDtypeStruct((depth, r, n), F32),
        compiler_params=_params("arbitrary", "arbitrary"),
        name="mod_vectors",
    )(c_all, w_mod, b_mod.reshape(depth, 1, n))


def _ffn_kernel(x_ref, ng_ref, sh_ref, sc_ref, gt_ref, wg_ref, wu_ref, wd_ref, o_ref, h_scr, acc_scr):
    f = pl.program_id(1)

    @pl.when(f == 0)
    def _():
        h_scr[...] = _modulated(x_ref, ng_ref, sh_ref, sc_ref).astype(h_scr.dtype)
        acc_scr[...] = jnp.zeros_like(acc_scr)

    h = h_scr[...]
    a = _silu(_mxu(h, wg_ref[...])) * _mxu(h, wu_ref[...])
    acc_scr[...] += _mxu(a, wd_ref[...])

    @pl.when(f == pl.num_programs(1) - 1)
    def _():
        o_ref[...] = x_ref[...] + (0.5 * gt_ref[:, 0, :]) * acc_scr[...]


def _ffn(x, ng, shift, scale, gate, w_gate, w_up, w_down, *, layer, which, rows_per_mod, tm, tf):
    m, d = x.shape
    dff = w_gate.shape[-1]
    mod = _mod_spec(tm, rows_per_mod, d, 2)
    return pl.pallas_call(
        _ffn_kernel,
        grid=(m // tm, dff // tf),
        in_specs=[pl.BlockSpec((tm, d), lambda i, f: (i, 0)),
                  pl.BlockSpec((1, d), lambda i, f: (0, 0)),
                  mod, mod, mod,
                  pl.BlockSpec((None, None, d, tf), lambda i, f: (layer, which, 0, f)),
                  pl.BlockSpec((None, None, d, tf), lambda i, f: (layer, which, 0, f)),
                  pl.BlockSpec((None, None, tf, d), lambda i, f: (layer, which, f, 0))],
        out_specs=pl.BlockSpec((tm, d), lambda i, f: (i, 0)),
        out_shape=jax.ShapeDtypeStruct((m, d), F32),
        scratch_shapes=[pltpu.VMEM((tm, d), MXU_DTYPE), pltpu.VMEM((tm, d), F32)],
        compiler_params=_params("arbitrary", "arbitrary"),
        name="ffn",
    )(x, ng, shift, scale, gate, w_gate, w_up, w_down)


def _rope_chunk(z, cos, sin):
    return z * cos + pltpu.roll(z, HEAD_DIM // 2, 1) * sin


def _proj_kernel(x_ref, ng_ref, sh_ref, sc_ref, w_ref, cos_ref, sin_ref, *rest, mode):
    out_refs, h_scr = rest[:-1], rest[-1]

    @pl.when(pl.program_id(1) == 0)
    def _():
        h_scr[...] = _modulated(x_ref, ng_ref, sh_ref, sc_ref).astype(h_scr.dtype)

    z = _mxu(h_scr[...], w_ref[...])
    if mode == "sigmoid":
        out_refs[0][...] = jax.nn.sigmoid(z)
        return
    cos, sin = cos_ref[...], sin_ref[...]
    n_chunks = z.shape[1] // HEAD_DIM
    for c in range(n_chunks):
        zc = z[:, c * HEAD_DIM:(c + 1) * HEAD_DIM]
        if mode == "q" or c % 2 == 0:
            zc = _rope_chunk(zc, cos, sin)
        for o in out_refs:
            o[:, c * HEAD_DIM:(c + 1) * HEAD_DIM] = zc.astype(o.dtype)


def _proj(x, ng, shift, scale, w, cos, sin, *, layer, mode, col0, n, out_dtypes, rows_per_mod, rows_per_seq,
          tm, tn):
    m, d = x.shape
    mod = _mod_spec(tm, rows_per_mod, d, 2)
    seq_tiles = rows_per_seq // tm
    assert col0 % tn == 0 and n % tn == 0 and rows_per_seq % tm == 0
    if mode == "kv":
        per = KV_WIDTH // tn
        out_specs = [pl.BlockSpec((None, tm, tn), lambda i, j: (j // per, i, j % per)) for _ in out_dtypes]
        out_shape = [jax.ShapeDtypeStruct((n // KV_WIDTH, m, KV_WIDTH), dt) for dt in out_dtypes]
    else:
        out_specs = [pl.BlockSpec((tm, tn), lambda i, j: (i, j)) for _ in out_dtypes]
        out_shape = [jax.ShapeDtypeStruct((m, n), dt) for dt in out_dtypes]
    return pl.pallas_call(
        functools.partial(_proj_kernel, mode=mode),
        grid=(m // tm, n // tn),
        in_specs=[pl.BlockSpec((tm, d), lambda i, j: (i, 0)),
                  pl.BlockSpec((1, d), lambda i, j: (0, 0)),
                  mod, mod,
                  pl.BlockSpec((None, d, tn), lambda i, j: (layer, 0, col0 // tn + j)),
                  pl.BlockSpec((tm, HEAD_DIM), lambda i, j: (i % seq_tiles, 0)),
                  pl.BlockSpec((tm, HEAD_DIM), lambda i, j: (i % seq_tiles, 0))],
        out_specs=out_specs,
        out_shape=out_shape,
        scratch_shapes=[pltpu.VMEM((tm, d), MXU_DTYPE)],
        compiler_params=_params("arbitrary", "arbitrary"),
        name="proj_" + mode,
    )(x, ng, shift, scale, w, cos, sin)


def _glu_kernel(x_ref, ng_ref, sh_ref, sc_ref, wa_ref, wb_ref, o_ref, h_scr):
    @pl.when(pl.program_id(1) == 0)
    def _():
        h_scr[...] = _modulated(x_ref, ng_ref, sh_ref, sc_ref).astype(h_scr.dtype)

    h = h_scr[...]
    o_ref[...] = _mxu(h, wa_ref[...]) * jax.nn.sigmoid(_mxu(h, wb_ref[...]))


def _glu_proj(x, ng, shift, scale, w_pw1, *, layer, rows_per_mod, tm, tn):
    m, d = x.shape
    dc = w_pw1.shape[-1] // 2
    mod = _mod_spec(tm, rows_per_mod, d, 2)
    return pl.pallas_call(
        _glu_kernel,
        grid=(m // tm, dc // tn),
        in_specs=[pl.BlockSpec((tm, d), lambda i, j: (i, 0)),
                  pl.BlockSpec((1, d), lambda i, j: (0, 0)),
                  mod, mod,
                  pl.BlockSpec((None, d, tn), lambda i, j: (layer, 0, j)),
                  pl.BlockSpec((None, d, tn), lambda i, j: (layer, 0, dc // tn + j))],
        out_specs=pl.BlockSpec((tm, tn), lambda i, j: (i, j)),
        out_shape=jax.ShapeDtypeStruct((m, dc), F32),
        scratch_shapes=[pltpu.VMEM((tm, d), MXU_DTYPE)],
        compiler_params=_params("arbitrary", "arbitrary"),
        name="glu_proj",
    )(x, ng, shift, scale, w_pw1, w_pw1)


def _linres_kernel(a_ref, w_ref, x_ref, gt_ref, o_ref):
    o_ref[...] = x_ref[...] + gt_ref[:, 0, :] * _mxu(a_ref[...], w_ref[...])


def _linres(a, w, x, gate, *, layer, rows_per_mod, tm, tn):
    m, k = a.shape
    d = x.shape[1]
    if rows_per_mod >= tm:
        gspec = pl.BlockSpec((1, 1, tn), lambda i, j: ((i * tm) // rows_per_mod, 0, j))
    else:
        gspec = pl.BlockSpec((tm, 1, tn), lambda i, j: (i, 0, j))
    return pl.pallas_call(
        _linres_kernel,
        grid=(m // tm, d // tn),
        in_specs=[pl.BlockSpec((tm, k), lambda i, j: (i, 0)),
                  pl.BlockSpec((None, k, tn), lambda i, j: (layer, 0, j)),
                  pl.BlockSpec((tm, tn), lambda i, j: (i, j)),
                  gspec],
        out_specs=pl.BlockSpec((tm, tn), lambda i, j: (i, j)),
        out_shape=jax.ShapeDtypeStruct((m, d), F32),
        compiler_params=_params("arbitrary", "arbitrary"),
        name="linres",
    )(a, w, x, gate)


def _rms_kernel(x_ref, g_ref, o_ref):
    x = x_ref[...]
    o_ref[...] = x * lax.rsqrt(jnp.mean(x * x, axis=-1, keepdims=True) + RMS_EPS) * g_ref[...]


def _rms_norm(x, g, tm):
    m, d = x.shape
    return pl.pallas_call(
        _rms_kernel,
        grid=(m // tm,),
        in_specs=[pl.BlockSpec((tm, d), lambda i: (i, 0)), pl.BlockSpec((1, d), lambda i: (0, 0))],
        out_specs=pl.BlockSpec((tm, d), lambda i: (i, 0)),
        out_shape=jax.ShapeDtypeStruct((m, d), F32),
        compiler_params=_params("arbitrary"),
        name="final_norm",
    )(x, g.reshape(1, d))


def _compress_tail(acc, w2_ref, o_ref):
    o_ref[...] = _mxu(_silu(acc), w2_ref[...]).astype(o_ref.dtype)


def _compress_kernel(x_ref, pe_ref, w1_ref, w2_ref, o_ref, *, nb):
    acc = jnp.zeros((nb, HEAD_DIM), F32)
    for c in range(CMP_BLOCK):
        xc = x_ref[pl.ds(c, nb, stride=CMP_BLOCK), :] + pe_ref[c:c + 1, :]
        acc += _mxu(xc, w1_ref[c * HEAD_DIM:(c + 1) * HEAD_DIM, :])
    _compress_tail(acc, w2_ref, o_ref)


def _compress_rows(rows, pe, w1, w2, *, layer, nb):
    n_rows = rows.shape[0]
    n_blocks = n_rows // CMP_BLOCK
    n_chunks = KV_WIDTH // HEAD_DIM
    return pl.pallas_call(
        functools.partial(_compress_kernel, nb=nb),
        grid=(n_blocks // nb, n_chunks),
        in_specs=[pl.BlockSpec((nb * CMP_BLOCK, HEAD_DIM), lambda i, ch: (i, ch)),
                  pl.BlockSpec((None, None, CMP_BLOCK, HEAD_DIM), lambda i, ch: (layer, ch % 2, 0, 0)),
                  pl.BlockSpec((None, None, CMP_BLOCK * HEAD_DIM, HEAD_DIM), lambda i, ch: (layer, ch % 2, 0, 0)),
                  pl.BlockSpec((None, None, HEAD_DIM, HEAD_DIM), lambda i, ch: (layer, ch % 2, 0, 0))],
        out_specs=pl.BlockSpec((nb, HEAD_DIM), lambda i, ch: (i, ch)),
        out_shape=jax.ShapeDtypeStruct((n_blocks, KV_WIDTH), MXU_DTYPE),
        compiler_params=_params("arbitrary", "arbitrary"),
        name="compress_rows",
    )(rows, pe, w1, w2)


def _compress_pages_kernel(pt_ref, *refs, blocks_per_page):
    del pt_ref
    pages = refs[:PAGES_PER_STEP]
    pe_ref, w1_ref, w2_ref, o_ref, t_scr = refs[PAGES_PER_STEP:]
    nb = PAGES_PER_STEP * blocks_per_page
    pages_per_group = SUBLANES // blocks_per_page
    n_groups = PAGES_PER_STEP // pages_per_group
    rows = SUBLANES * CMP_BLOCK
    out_row = lax.broadcasted_iota(jnp.int32, (rows, rows), 0)
    in_row = lax.broadcasted_iota(jnp.int32, (rows, rows), 1)
    perm = jnp.where(in_row == (out_row % SUBLANES) * CMP_BLOCK + out_row // SUBLANES, 1.0, 0.0)
    pe_rows = jnp.concatenate([pe_ref[kv] for _ in range(N_KV_HEADS) for kv in range(2)], axis=1)
    pe_rows = jnp.concatenate([pe_rows] * SUBLANES, axis=0)
    for gp in range(n_groups):
        x = jnp.concatenate([pages[gp * pages_per_group + k][...] for k in range(pages_per_group)], axis=0)
        t_scr[gp] = _mxu(perm, x + pe_rows)
    for kv in range(2):
        acc = jnp.zeros((N_KV_HEADS * nb, HEAD_DIM), F32)
        for c in range(CMP_BLOCK):
            xc = jnp.concatenate(
                [t_scr[:, c * SUBLANES:(c + 1) * SUBLANES,
                       (g * 2 + kv) * HEAD_DIM:(g * 2 + kv + 1) * HEAD_DIM].reshape(nb, HEAD_DIM)
                 for g in range(N_KV_HEADS)], axis=0)
            acc += _mxu(xc, w1_ref[kv, c * HEAD_DIM:(c + 1) * HEAD_DIM, :])
        y = _mxu(_silu(acc), w2_ref[kv])
        for g in range(N_KV_HEADS):
            o_ref[:, (g * 2 + kv) * HEAD_DIM:(g * 2 + kv + 1) * HEAD_DIM] = (
                y[g * nb:(g + 1) * nb].astype(o_ref.dtype))


def _compress_pages(pool, page_table, pe, w1, w2, *, layer):
    _, _, page, width = pool.shape
    b, n_pages = page_table.shape
    bpp = page // CMP_BLOCK
    nb = PAGES_PER_STEP * bpp
    assert n_pages % PAGES_PER_STEP == 0 and page % CMP_BLOCK == 0 and SUBLANES % bpp == 0
    n_groups = nb // SUBLANES

    def page_spec(p):
        return pl.BlockSpec((None, None, page, width),
                            lambda bi, gi, pt: (layer, pt[bi * n_pages + gi * PAGES_PER_STEP + p], 0, 0))

    grid_spec = pltpu.PrefetchScalarGridSpec(
        num_scalar_prefetch=1,
        grid=(b, n_pages // PAGES_PER_STEP),
        in_specs=[page_spec(p) for p in range(PAGES_PER_STEP)] + [
            pl.BlockSpec((None, 2, CMP_BLOCK, HEAD_DIM), lambda bi, gi, pt: (layer, 0, 0, 0)),
            pl.BlockSpec((None, 2, CMP_BLOCK * HEAD_DIM, HEAD_DIM), lambda bi, gi, pt: (layer, 0, 0, 0)),
            pl.BlockSpec((None, 2, HEAD_DIM, HEAD_DIM), lambda bi, gi, pt: (layer, 0, 0, 0))],
        out_specs=pl.BlockSpec((None, nb, width), lambda bi, gi, pt: (bi, gi, 0)),
        scratch_shapes=[pltpu.VMEM((n_groups, SUBLANES * CMP_BLOCK, width), F32)],
    )
    return pl.pallas_call(
        functools.partial(_compress_pages_kernel, blocks_per_page=bpp),
        grid_spec=grid_spec,
        out_shape=jax.ShapeDtypeStruct((b, n_pages * bpp, width), MXU_DTYPE),
        compiler_params=_params("arbitrary", "arbitrary"),
        name="compress_pages",
    )(page_table.reshape(-1), *([pool] * PAGES_PER_STEP), pe, w1, w2)


def _pair_sums(imp):
    out = []
    for k in range(imp.shape[1] // LANES):
        x = imp[:, k * LANES:(k + 1) * LANES]
        even = lax.broadcasted_iota(jnp.int32, x.shape, 1) % 2 == 0
        out.append(x + jnp.where(even, pltpu.roll(x, LANES - 1, 1), pltpu.roll(x, 1, 1)))
    return out[0] if len(out) == 1 else jnp.concatenate(out, axis=1)


def _block_scores(imp, q_pos, n_sel_blocks):
    lane = lax.broadcasted_iota(jnp.int32, imp.shape, 1)
    blk = lane // SEL_RATIO
    cur = q_pos // SEL_BLOCK
    forced = (blk == 0) | (blk == cur) | (blk == cur - 1)
    valid = blk * SEL_BLOCK <= q_pos
    score = jnp.where(forced, FORCED_SCORE, jnp.where(valid, _pair_sums(imp), -1.0))
    eligible = (lane % SEL_RATIO == 0) & (blk < n_sel_blocks)
    return jnp.where(eligible, score, KNOCKED_OUT)


def _take_top(work):
    lane = lax.broadcasted_iota(jnp.int32, work.shape, 1).astype(F32)
    top = jnp.max(work, axis=-1, keepdims=True)
    pick = jnp.min(jnp.where(work == top, lane, float(work.shape[1])), axis=-1, keepdims=True)
    return lane == pick, pick


def _attn_prompt_kernel(q_ref, ck_ref, cv_ref, ks_ref, vs_ref, kw_ref, vw_ref, g_ref, o_ref, *, tq, tk, seq):
    i = pl.program_id(2)
    n_cmp = ck_ref.shape[0]
    n_sel_blocks = -(-seq // SEL_BLOCK)
    n_take = min(N_SELECT, n_sel_blocks)
    assert SEL_RATIO == 2 and n_cmp * CMP_BLOCK == seq and n_cmp <= LANES
    q = q_ref[...]
    q4 = jnp.concatenate([q[:, h * HEAD_DIM:(h + 1) * HEAD_DIM] for h in range(GROUP_SIZE)], axis=0)
    q_pos = i * tq + lax.broadcasted_iota(jnp.int32, (tq, 1), 0)

    pad = jnp.zeros((LANES - n_cmp, HEAD_DIM), ck_ref.dtype)
    ck = jnp.concatenate([ck_ref[...], pad], axis=0) if n_cmp < LANES else ck_ref[...]
    cv = jnp.concatenate([cv_ref[...], pad], axis=0) if n_cmp < LANES else cv_ref[...]
    lane = lax.broadcasted_iota(jnp.int32, (1, LANES), 1)
    vis = ((lane + 1) * CMP_BLOCK - 1 <= q_pos) & (lane < n_cmp)
    s = (_mxu_nt(q4, ck) * ATTN_SCALE).reshape(GROUP_SIZE, tq, LANES)
    p = jnp.where(vis[None], _softmax_rows(jnp.where(vis[None], s, NEG_INF)), 0.0)
    o_c = _mxu(p.reshape(GROUP_SIZE * tq, LANES), cv).reshape(GROUP_SIZE, tq, HEAD_DIM)
    imp = jnp.sum(p, axis=0)

    work = _block_scores(imp, q_pos, n_sel_blocks)
    sel = jnp.zeros((tq, LANES), F32)
    for _ in range(n_take):
        hit, _ = _take_top(work)
        work = jnp.where(hit, KNOCKED_OUT, work)
        sel = jnp.where(hit, 1.0, sel)
    sel = sel.astype(MXU_DTYPE)

    def sel_chunk(kc, carry):
        m_i, l_i, acc = carry
        start = pl.multiple_of(kc * tk, tk)
        k_pos = start + lax.broadcasted_iota(jnp.int32, (1, tk), 1)
        expand = lax.broadcasted_iota(jnp.int32, (LANES, 1), 0) == SEL_RATIO * (k_pos // SEL_BLOCK)
        picked = jnp.dot(sel, jnp.where(expand, 1.0, 0.0).astype(MXU_DTYPE), preferred_element_type=F32)
        mask = ((picked > 0.5) & (k_pos <= q_pos))[None]
        sc = (_mxu_nt(q4, ks_ref[pl.ds(start, tk), :]) * ATTN_SCALE).reshape(GROUP_SIZE, tq, tk)
        sc = jnp.where(mask, sc, NEG_INF)
        m_new = jnp.maximum(m_i, jnp.max(sc, axis=-1, keepdims=True))
        alpha = jnp.exp(m_i - m_new)
        e = jnp.where(mask, jnp.exp(sc - m_new), 0.0)
        l_new = alpha * l_i + jnp.sum(e, axis=-1, keepdims=True)
        pv = _mxu(e.reshape(GROUP_SIZE * tq, tk), vs_ref[pl.ds(start, tk), :])
        return m_new, l_new, alpha * acc + pv.reshape(GROUP_SIZE, tq, HEAD_DIM)

    n_chunks = ((i + 1) * tq + tk - 1) // tk
    init = (jnp.full((GROUP_SIZE, tq, 1), NEG_INF, F32), jnp.zeros((GROUP_SIZE, tq, 1), F32),
            jnp.zeros((GROUP_SIZE, tq, HEAD_DIM), F32))
    _, l_s, acc_s = lax.fori_loop(0, n_chunks, sel_chunk, init)
    o_s = acc_s / l_s

    span = tq + WINDOW
    w_start = pl.multiple_of(jnp.maximum(i * tq - WINDOW, 0), tq)
    k_pos = w_start + lax.broadcasted_iota(jnp.int32, (1, span), 1)
    dist = q_pos - k_pos
    mask = ((dist >= 0) & (dist < WINDOW))[None]
    sw = (_mxu_nt(q4, kw_ref[pl.ds(w_start, span), :]) * ATTN_SCALE).reshape(GROUP_SIZE, tq, span)
    pw = _softmax_rows(jnp.where(mask, sw, NEG_INF))
    o_w = _mxu(pw.reshape(GROUP_SIZE * tq, span), vw_ref[pl.ds(w_start, span), :]).reshape(GROUP_SIZE, tq, HEAD_DIM)

    gates = g_ref[...]
    for h in range(GROUP_SIZE):
        o = (gates[:, 3 * h:3 * h + 1] * o_c[h] + gates[:, 3 * h + 1:3 * h + 2] * o_s[h]
             + gates[:, 3 * h + 2:3 * h + 3] * o_w[h])
        o_ref[:, h * HEAD_DIM:(h + 1) * HEAD_DIM] = o.astype(o_ref.dtype)


def _attn_prompt(q, ckv, kv, gates, *, tq, tk):
    b, t, _ = q.shape
    n_cmp = ckv.shape[1]
    assert t % tq == 0 and t % tk == 0 and t >= tq + WINDOW and WINDOW % tq == 0
    gw = GROUP_SIZE * HEAD_DIM

    def kv_spec(branch, part):
        return pl.BlockSpec((None, None, t, HEAD_DIM), lambda bi, g, i: (branch, bi, 0, 2 * g + part))

    return pl.pallas_call(
        functools.partial(_attn_prompt_kernel, tq=tq, tk=tk, seq=t),
        grid=(b, N_KV_HEADS, t // tq),
        in_specs=[pl.BlockSpec((None, tq, gw), lambda bi, g, i: (bi, i, g)),
                  pl.BlockSpec((None, n_cmp, HEAD_DIM), lambda bi, g, i: (bi, 0, 2 * g)),
                  pl.BlockSpec((None, n_cmp, HEAD_DIM), lambda bi, g, i: (bi, 0, 2 * g + 1)),
                  kv_spec(1, 0), kv_spec(1, 1), kv_spec(2, 0), kv_spec(2, 1),
                  pl.BlockSpec((None, tq, LANES), lambda bi, g, i: (bi, i, g))],
        out_specs=pl.BlockSpec((None, tq, gw), lambda bi, g, i: (bi, i, g)),
        out_shape=jax.ShapeDtypeStruct((b, t, Q_WIDTH), MXU_DTYPE),
        compiler_params=_params("arbitrary", "arbitrary", "arbitrary"),
        name="attn_prompt",
    )(q, ckv, ckv, kv, kv, kv, kv, gates)


def _attn_cmp_sample_kernel(q_ref, ck_ref, cv_ref, o_ref, idx_ref, *, past, n_total, width):
    t = pl.program_id(2)
    n_cmp = ck_ref.shape[0]
    n_sel_blocks = -(-n_total // SEL_BLOCK)
    n_take = min(N_SELECT, n_sel_blocks)
    assert SEL_RATIO == 2 and n_take <= LANES and SEL_RATIO * n_sel_blocks <= width
    q_pos = past + t
    pad = jnp.zeros((width - n_cmp, HEAD_DIM), ck_ref.dtype)
    ck = jnp.concatenate([ck_ref[...], pad], axis=0)
    cv = jnp.concatenate([cv_ref[...], pad], axis=0)
    lane = lax.broadcasted_iota(jnp.int32, (1, width), 1)
    vis = ((lane + 1) * CMP_BLOCK - 1 <= q_pos) & (lane < n_cmp)
    s = _mxu_nt(q_ref[...], ck) * ATTN_SCALE
    p = jnp.where(vis, _softmax_rows(jnp.where(vis, s, NEG_INF)), 0.0)
    o_ref[...] = _mxu(p, cv)
    imp = jnp.sum(p, axis=0, keepdims=True)

    work = _block_scores(imp, jnp.full((1, 1), q_pos, jnp.int32), n_sel_blocks)
    out_lane = lax.broadcasted_iota(jnp.int32, (1, LANES), 1)
    picks = jnp.zeros((1, LANES), F32)
    for r in range(n_take):
        hit, pick = _take_top(work)
        work = jnp.where(hit, KNOCKED_OUT, work)
        picks = jnp.where(out_lane == r, pick, picks)
    idx_ref[...] = (picks * (1.0 / SEL_RATIO)).astype(jnp.int32)


def _attn_cmp_sample(q, ckv, *, past, n_total):
    b, t, g, hg, hd = q.shape
    n_cmp = ckv.shape[1]
    width = -(-max(n_cmp, SEL_RATIO * -(-n_total // SEL_BLOCK)) // LANES) * LANES
    return pl.pallas_call(
        functools.partial(_attn_cmp_sample_kernel, past=past, n_total=n_total, width=width),
        grid=(b, g, t),
        in_specs=[pl.BlockSpec((None, None, None, hg, hd), lambda bi, gi, ti: (bi, ti, gi, 0, 0)),
                  pl.BlockSpec((None, n_cmp, HEAD_DIM), lambda bi, gi, ti: (bi, 0, 2 * gi)),
                  pl.BlockSpec((None, n_cmp, HEAD_DIM), lambda bi, gi, ti: (bi, 0, 2 * gi + 1))],
        out_specs=[pl.BlockSpec((None, None, None, hg, hd), lambda bi, gi, ti: (bi, ti, gi, 0, 0)),
                   pl.BlockSpec((None, None, None, 1, LANES), lambda bi, gi, ti: (bi, gi, ti, 0, 0))],
        out_shape=[jax.ShapeDtypeStruct(q.shape, F32),
                   jax.ShapeDtypeStruct((b, g, t, 1, LANES), jnp.int32)],
        compiler_params=_params("arbitrary", "arbitrary", "arbitrary"),
        name="attn_cmp_sample",
    )(q, ckv, ckv)


def _attn_win_sample_kernel(q_ref, buf_ref, new_ref, o_ref, *, past, t_new):
    t = pl.program_id(2)
    q_pos = past + t
    n_buf = buf_ref.shape[0]
    q = q_ref[...]
    buf, new = buf_ref[...], new_ref[...]
    pos_a = past - n_buf + lax.broadcasted_iota(jnp.int32, (1, n_buf), 1)
    row_b = lax.broadcasted_iota(jnp.int32, (1, new.shape[0]), 1)
    mask_a = (q_pos - pos_a >= 0) & (q_pos - pos_a < WINDOW)
    mask_b = (row_b <= t) & (t - row_b < WINDOW) & (row_b < t_new)
    s_a = jnp.where(mask_a, _mxu_nt(q, buf[:, :HEAD_DIM]) * ATTN_SCALE, NEG_INF)
    s_b = jnp.where(mask_b, _mxu_nt(q, new[:, :HEAD_DIM]) * ATTN_SCALE, NEG_INF)
    m = jnp.maximum(jnp.max(s_a, axis=-1, keepdims=True), jnp.max(s_b, axis=-1, keepdims=True))
    e_a, e_b = jnp.exp(s_a - m), jnp.exp(s_b - m)
    l = jnp.sum(e_a, axis=-1, keepdims=True) + jnp.sum(e_b, axis=-1, keepdims=True)
    o_ref[...] = _mxu(e_a / l, buf[:, HEAD_DIM:]) + _mxu(e_b / l, new[:, HEAD_DIM:])


def _attn_win_sample(q, win_buf, kv_new, *, layer, past, t_new):
    b, t, g, hg, hd = q.shape
    n_buf = win_buf.shape[2]
    return pl.pallas_call(
        functools.partial(_attn_win_sample_kernel, past=past, t_new=t_new),
        grid=(b, g, t),
        in_specs=[pl.BlockSpec((None, None, None, hg, hd), lambda bi, gi, ti: (bi, ti, gi, 0, 0)),
                  pl.BlockSpec((None, None, n_buf, 2 * HEAD_DIM), lambda bi, gi, ti: (layer, bi, 0, gi)),
                  pl.BlockSpec((None, kv_new.shape[1], 2 * HEAD_DIM), lambda bi, gi, ti: (bi, 0, gi))],
        out_specs=pl.BlockSpec((None, None, None, hg, hd), lambda bi, gi, ti: (bi, ti, gi, 0, 0)),
        out_shape=jax.ShapeDtypeStruct(q.shape, F32),
        compiler_params=_params("arbitrary", "arbitrary", "arbitrary"),
        name="attn_win_sample",
    )(q, win_buf, kv_new)


def _attn_sel_sample_kernel(idx_ref, pt_ref, q_ref, *refs, past, t_new, n_slots, t_steps):
    del pt_ref
    blocks = refs[:n_slots]
    new_ref, oc_ref, ow_ref, g_ref, o_ref = refs[n_slots:]
    bi, t, gi = pl.program_id(0), pl.program_id(1), pl.program_id(2)
    base = ((bi * t_steps + t) * N_KV_HEADS + gi) * n_slots
    n_past_blocks = past // SEL_BLOCK
    q_pos = past + t
    q = q_ref[...]

    keys = jnp.concatenate([blk[:, :HEAD_DIM].astype(MXU_DTYPE) for blk in blocks], axis=0)
    vals = jnp.concatenate([blk[:, HEAD_DIM:].astype(MXU_DTYPE) for blk in blocks], axis=0)
    lane = lax.broadcasted_iota(jnp.int32, (1, n_slots * SEL_BLOCK), 1)
    pos = lane % SEL_BLOCK
    has_new = jnp.int32(0)
    for n in range(n_slots):
        blk = idx_ref[base + n]
        start = jnp.where(blk < n_past_blocks, blk * SEL_BLOCK, q_pos + 1)
        pos = pos + jnp.where(lane // SEL_BLOCK == n, start, 0)
        has_new = has_new | (blk == n_past_blocks).astype(jnp.int32)
    mask_a = pos <= q_pos
    new = new_ref[...]
    row_b = lax.broadcasted_iota(jnp.int32, (1, new.shape[0]), 1)
    mask_b = (row_b <= t) & (row_b < t_new) & (has_new > 0)
    s_a = jnp.where(mask_a, _mxu_nt(q, keys) * ATTN_SCALE, NEG_INF)
    s_b = jnp.where(mask_b, _mxu_nt(q, new[:, :HEAD_DIM]) * ATTN_SCALE, NEG_INF)
    m = jnp.maximum(jnp.max(s_a, axis=-1, keepdims=True), jnp.max(s_b, axis=-1, keepdims=True))
    e_a, e_b = jnp.exp(s_a - m), jnp.exp(s_b - m)
    l = jnp.sum(e_a, axis=-1, keepdims=True) + jnp.sum(e_b, axis=-1, keepdims=True)
    o_s = _mxu(e_a / l, vals) + _mxu(e_b / l, new[:, HEAD_DIM:])
    gates = g_ref[...]
    o_ref[...] = gates[:, 0:1] * oc_ref[...] + gates[:, 1:2] * o_s + gates[:, 2:3] * ow_ref[...]


def _attn_sel_sample(idx, page_table, q, pool, kv_new, o_c, o_w, gates, *, layer, past, t_new):
    b, t, g, hg, hd = q.shape
    n_slots = idx.shape[-1]
    n_pages = page_table.shape[1]
    page = past // n_pages
    per_page = page // SEL_BLOCK
    n_past_blocks = past // SEL_BLOCK

    def slot_spec(n):
        def index(bi, ti, gi, idx_s, pt_s):
            blk = jnp.minimum(idx_s[((bi * t + ti) * g + gi) * n_slots + n], n_past_blocks - 1)
            phys = pt_s[bi * n_pages + blk // per_page]
            return (layer, phys * per_page + blk % per_page, 0, gi)
        return pl.BlockSpec((None, None, SEL_BLOCK, 2 * HEAD_DIM), index)

    head_spec = pl.BlockSpec((None, None, None, hg, hd), lambda bi, ti, gi, idx_s, pt_s: (bi, ti, gi, 0, 0))
    grid_spec = pltpu.PrefetchScalarGridSpec(
        num_scalar_prefetch=2,
        grid=(b, t, g),
        in_specs=[head_spec] + [slot_spec(n) for n in range(n_slots)] + [
            pl.BlockSpec((None, kv_new.shape[1], 2 * HEAD_DIM), lambda bi, ti, gi, idx_s, pt_s: (bi, 0, gi)),
            head_spec, head_spec,
            pl.BlockSpec((None, None, None, hg, 3), lambda bi, ti, gi, idx_s, pt_s: (bi, ti, gi, 0, 0))],
        out_specs=head_spec,
    )
    return pl.pallas_call(
        functools.partial(_attn_sel_sample_kernel, past=past, t_new=t_new, n_slots=n_slots, t_steps=t),
        grid_spec=grid_spec,
        out_shape=jax.ShapeDtypeStruct(q.shape, F32),
        compiler_params=_params("arbitrary", "arbitrary", "arbitrary"),
        name="attn_sel_sample",
    )(idx.reshape(-1), page_table.reshape(-1), q, *([pool] * n_slots), kv_new, o_c, o_w, gates)


def _dwconv_kernel(cur_ref, prev_ref, buf_ref, w_ref, b_ref, lg_ref, lb_ref, o_ref, full_scr, y_scr, *,
                   tt, halo, width):
    i = pl.program_id(1)
    lead = halo - (width - 1)
    full_scr[0:halo, :] = prev_ref[...]

    @pl.when(i == 0)
    def _():
        full_scr[0:halo, :] = buf_ref[...]

    full_scr[halo:halo + tt, :] = cur_ref[...]
    for c0 in range(0, cur_ref.shape[1], DWCONV_COLS):
        cols = slice(c0, c0 + DWCONV_COLS)
        acc = jnp.zeros((tt, DWCONV_COLS), F32) + b_ref[:, cols]
        for j in range(width):
            acc = acc + full_scr[lead + j:lead + j + tt, cols] * w_ref[j:j + 1, cols]
        y_scr[:, cols] = acc
    y = y_scr[...]
    mu = jnp.mean(y, axis=-1, keepdims=True)
    var = jnp.mean(jnp.square(y - mu), axis=-1, keepdims=True)
    yn = (y - mu) * lax.rsqrt(var + LN_EPS) * lg_ref[...] + lb_ref[...]
    o_ref[...] = _silu(yn).astype(o_ref.dtype)


def _dwconv(glu, buf, w_dw, b_dw, ln_g, ln_b, *, layer, tt):
    b, t, d = glu.shape
    width = w_dw.shape[1]
    halo = buf.shape[1]
    assert tt % halo == 0 and t % tt == 0 and halo >= width - 1
    per = tt // halo

    def vec_spec():
        return pl.BlockSpec((None, 1, d), lambda bi, i: (layer, 0, 0))

    return pl.pallas_call(
        functools.partial(_dwconv_kernel, tt=tt, halo=halo, width=width),
        grid=(b, t // tt),
        in_specs=[pl.BlockSpec((None, tt, d), lambda bi, i: (bi, i, 0)),
                  pl.BlockSpec((None, halo, d), lambda bi, i: (bi, jnp.maximum(i * per - 1, 0), 0)),
                  pl.BlockSpec((None, halo, d), lambda bi, i: (bi, 0, 0)),
                  pl.BlockSpec((None, width, d), lambda bi, i: (layer, 0, 0)),
                  vec_spec(), vec_spec(), vec_spec()],
        out_specs=pl.BlockSpec((None, tt, d), lambda bi, i: (bi, i, 0)),
        out_shape=jax.ShapeDtypeStruct((b, t, d), MXU_DTYPE),
        scratch_shapes=[pltpu.VMEM((halo + tt, d), F32), pltpu.VMEM((tt, d), F32)],
        compiler_params=_params("arbitrary", "arbitrary"),
        name="dwconv",
    )(glu, glu, buf, w_dw, b_dw, ln_g, ln_b)


def _rope_tables(pos):
    half = HEAD_DIM // 2
    inv = ROPE_THETA ** (-jnp.arange(half, dtype=F32) / half)
    ang = pos.astype(F32)[:, None] * inv[None, :]
    cos, sin = jnp.cos(ang), jnp.sin(ang)
    return jnp.concatenate([cos, cos], axis=1), jnp.concatenate([-sin, sin], axis=1)


def _group_gate_weights(w_in_l):
    d = w_in_l.shape[0]
    wg = w_in_l[:, Q_WIDTH + 3 * KV_WIDTH:].reshape(d, N_KV_HEADS, GATE_COLS)
    return jnp.pad(wg, ((0, 0), (0, 0), (0, LANES - GATE_COLS))).reshape(d, N_KV_HEADS * LANES)


def _run_trunk(x, mod, pos, prm, caches, tiles):
    b, t, d = x.shape
    m = b * t
    depth = prm["w_mod"].shape[0]
    tm = tiles["tm"]
    rpm = t if caches is None else 1
    cos, sin = _rope_tables(pos)
    if caches is not None:
        cos, sin = jnp.tile(cos, (b, 1)), jnp.tile(sin, (b, 1))
    rows_per_seq = t if caches is None else m
    x = x.reshape(m, d)
    new_c, new_s, new_w, new_conv = [], [], [], []

    def mod_rows(layer, k):
        v = mod[layer, :, k, :]
        if caches is not None:
            v = jnp.repeat(v, t, axis=0)
        return v[:, None, :]

    def ffn(x, layer, which, k0):
        return _ffn(x, prm["norm_g"][layer, which * 2].reshape(1, d), mod_rows(layer, k0), mod_rows(layer, k0 + 1),
                    mod_rows(layer, k0 + 2), prm["ffn_w_gate"], prm["ffn_w_up"], prm["ffn_w_down"],
                    layer=layer, which=which, rows_per_mod=rpm, tm=tm, tf=tiles["tf"])

    for i in range(depth):
        x = ffn(x, i, 0, 0)
        ng = prm["norm_g"][i, 1].reshape(1, d)
        shift, scale, gate = mod_rows(i, 3), mod_rows(i, 4), mod_rows(i, 5)
        a = i // 2
        if i % 2 == 0:
            w_in = prm["attn_w_in"]
            common = dict(rows_per_mod=rpm, rows_per_seq=rows_per_seq, tm=tm)
            q_dtype = MXU_DTYPE if caches is None else F32
            q, = _proj(x, ng, shift, scale, w_in, cos, sin, layer=a, mode="q", col0=0, n=Q_WIDTH,
                       out_dtypes=[q_dtype], tn=512, **common)
            kv_outs = [F32, MXU_DTYPE] if caches is None else [F32]
            kv = _proj(x, ng, shift, scale, w_in, cos, sin, layer=a, mode="kv", col0=Q_WIDTH, n=3 * KV_WIDTH,
                       out_dtypes=kv_outs, tn=512, **common)
            gates, = _proj(x, ng, shift, scale, _group_gate_weights(w_in[a])[None], cos, sin, layer=0,
                           mode="sigmoid", col0=0, n=N_KV_HEADS * LANES, out_dtypes=[F32],
                           tn=N_KV_HEADS * LANES, **common)
            kv32 = kv[0]
            new_c.append(kv32[0].reshape(b, t, N_KV_HEADS, 2, HEAD_DIM))
            new_s.append(kv32[1].reshape(b, t, N_KV_HEADS, 2, HEAD_DIM))
            if caches is None:
                ckv = _compress_rows(kv32[0], prm["cmp_pe"], prm["cmp_w1"], prm["cmp_w2"], layer=a,
                                     nb=min(256, m // CMP_BLOCK))
                o = _attn_prompt(q.reshape(b, t, Q_WIDTH), ckv.reshape(b, t // CMP_BLOCK, KV_WIDTH),
                                 kv[1].reshape(3, b, t, KV_WIDTH), gates.reshape(b, t, N_KV_HEADS * LANES),
                                 tq=tiles["tq"], tk=tiles["tk"]).reshape(m, Q_WIDTH)
                new_w.append(kv32[2].reshape(b, t, N_KV_HEADS, 2, HEAD_DIM)[:, t - min(WINDOW, t):])
            else:
                pool_c, pool_s, win_buf, _, page_table = caches
                n_layers, n_pool, page = pool_c.shape[:3]
                past = page_table.shape[1] * page
                assert t < CMP_BLOCK and page % SEL_BLOCK == 0
                ckv = _compress_pages(pool_c.reshape(n_layers, n_pool, page, KV_WIDTH), page_table,
                                      prm["cmp_pe"], prm["cmp_w1"], prm["cmp_w2"], layer=a)
                q5 = q.reshape(b, t, N_KV_HEADS, GROUP_SIZE, HEAD_DIM)
                o_c, idx = _attn_cmp_sample(q5, ckv, past=past, n_total=past + t)
                t_pad = -(-t // 8) * 8
                pad_rows = lambda r: jnp.pad(r.reshape(b, t, KV_WIDTH), ((0, 0), (0, t_pad - t), (0, 0)))
                n_buf = win_buf.shape[2]
                o_w = _attn_win_sample(q5, win_buf.reshape(n_layers, b, n_buf, KV_WIDTH), pad_rows(kv32[2]),
                                       layer=a, past=past, t_new=t)
                n_take = min(N_SELECT, -(-(past + t) // SEL_BLOCK))
                idx = jnp.transpose(idx[:, :, :, 0, :n_take], (0, 2, 1, 3))
                g5 = gates.reshape(b, t, N_KV_HEADS, LANES)[..., :GATE_COLS].reshape(b, t, N_KV_HEADS, GROUP_SIZE, 3)
                o = _attn_sel_sample(idx, page_table,  q5,
                                     pool_s.reshape(n_layers, n_pool * (page // SEL_BLOCK), SEL_BLOCK, KV_WIDTH),
                                     pad_rows(kv32[1]), o_c, o_w, g5, layer=a, past=past, t_new=t).reshape(m, Q_WIDTH)
                keys = jnp.concatenate([win_buf[a], kv32[2].reshape(b, t, N_KV_HEADS, 2, HEAD_DIM)], axis=1)
                new_w.append(keys[:, n_buf + t - min(WINDOW, n_buf + t):])
            x = _linres(o, prm["attn_w_out"], x, gate, layer=a, rows_per_mod=rpm, tm=tm, tn=tiles["tn_out"])
        else:
            glu = _glu_proj(x, ng, shift, scale, prm["conv_w_pw1"], layer=a, rows_per_mod=rpm, tm=tm, tn=512)
            glu = glu.reshape(b, t, d)
            width = prm["conv_w_dw"].shape[1]
            halo = 32
            if caches is None:
                buf = jnp.zeros((b, width - 1, d), F32)
                t_conv = t
                cur = glu
            else:
                buf = caches[3][a]
                t_conv = halo
                cur = jnp.pad(glu, ((0, 0), (0, t_conv - t), (0, 0)))
            full = jnp.concatenate([buf, glu], axis=1)
            new_conv.append(full[:, full.shape[1] - (width - 1):])
            buf = jnp.pad(buf, ((0, 0), (halo - (width - 1), 0), (0, 0)))
            vec = lambda v: v.reshape(v.shape[0], 1, d)
            act = _dwconv(cur, buf, prm["conv_w_dw"], vec(prm["conv_b_dw"]), vec(prm["conv_ln_g"]),
                          vec(prm["conv_ln_b"]), layer=a, tt=min(tiles["tt"], t_conv))
            act = act[:, :t].reshape(m, d)
            x = _linres(act, prm["conv_w_pw2"], x, gate, layer=a, rows_per_mod=rpm, tm=tm, tn=tiles["tn_out"])
        x = ffn(x, i, 1, 6)
    y = _rms_norm(x, prm["final_norm_g"], tm).reshape(b, t, d)
    return y, jnp.stack(new_c), jnp.stack(new_s), jnp.stack(new_w), jnp.stack(new_conv)


PROMPT_TILES = dict(tm=512, tf=256, tq=256, tk=512, tn_out=512, tt=128)
SAMPLE_TILES = dict(tm=32, tf=512, tn_out=512, tt=32)


def kernel(x_prompt, x_sample, cache_cmp_kv, cache_sel_kv, cache_win_kv, state_conv, page_table, c_prompt, c_sample, w_mod, b_mod, norm_g, ffn_w_gate, ffn_w_up, ffn_w_down, attn_w_in, attn_w_out, cmp_pe, cmp_w1, cmp_w2, conv_w_pw1, conv_w_dw, conv_b_dw, conv_ln_g, conv_ln_b, conv_w_pw2, final_norm_g):
    prm = {"w_mod": w_mod, "norm_g": norm_g, "ffn_w_gate": ffn_w_gate, "ffn_w_up": ffn_w_up,
           "ffn_w_down": ffn_w_down, "attn_w_in": attn_w_in, "attn_w_out": attn_w_out, "cmp_pe": cmp_pe,
           "cmp_w1": cmp_w1, "cmp_w2": cmp_w2, "conv_w_pw1": conv_w_pw1, "conv_w_dw": conv_w_dw,
           "conv_b_dw": conv_b_dw, "conv_ln_g": conv_ln_g, "conv_ln_b": conv_ln_b, "conv_w_pw2": conv_w_pw2,
           "final_norm_g": final_norm_g}
    depth, d, _ = w_mod.shape
    bp, tp = x_prompt.shape[:2]
    bs, ts = x_sample.shape[:2]
    past = page_table.shape[1] * cache_cmp_kv.shape[2]

    n_req = bp + bs
    r_pad = -(-n_req // 8) * 8
    c_all = jnp.pad(jnp.concatenate([c_prompt, c_sample], axis=0), ((0, r_pad - n_req), (0, 0)))
    mod = _mod_vectors(c_all, w_mod, b_mod).reshape(depth, r_pad, N_MOD, d)

    pos_p = jnp.arange(tp, dtype=jnp.int32)
    pos_s = past + jnp.arange(ts, dtype=jnp.int32)
    y_p, p_cmp, p_sel, p_win, p_conv = _run_trunk(x_prompt, mod[:, :bp], pos_p, prm, None, PROMPT_TILES)
    y_s, s_cmp, s_sel, s_win, s_conv = _run_trunk(
        x_sample, mod[:, bp:n_req], pos_s, prm,
        (cache_cmp_kv, cache_sel_kv, cache_win_kv, state_conv, page_table), SAMPLE_TILES)
    return (y_p, y_s, p_cmp, p_sel, p_win, p_conv, s_cmp, s_sel, s_win, s_conv)
```

```python
import functools

import jax
import jax.numpy as jnp
from jax import lax
from jax.experimental import pallas as pl
from jax.experimental.pallas import tpu as pltpu

F32 = jnp.float32
MXU_DTYPE = jnp.bfloat16
VMEM_LIMIT_BYTES = 56 * 1024 * 1024
LANES = 128
SUBLANES = 8

N_HEADS = 16
HEAD_DIM = 128
N_KV_HEADS = 4
GROUP_SIZE = N_HEADS // N_KV_HEADS
CMP_BLOCK = 32
SEL_BLOCK = 64
SEL_RATIO = SEL_BLOCK // CMP_BLOCK
N_SELECT = 16
WINDOW = 512
ROPE_THETA = 10000.0
N_MOD = 9
RMS_EPS = 1e-6
LN_EPS = 1e-5
NEG_INF = -1e30
FORCED_SCORE = 1e9
KNOCKED_OUT = -3e38
ATTN_SCALE = HEAD_DIM ** -0.5
Q_WIDTH = N_HEADS * HEAD_DIM
KV_WIDTH = 2 * N_KV_HEADS * HEAD_DIM
GATE_COLS = GROUP_SIZE * 3
PAGES_PER_STEP = 16
DWCONV_COLS = 128
EXP2_SCALE = ATTN_SCALE * 1.4426950408889634


def _params(*sem):
    return pltpu.CompilerParams(dimension_semantics=sem, vmem_limit_bytes=VMEM_LIMIT_BYTES)


def _mxu(a, b):
    return jnp.dot(a.astype(MXU_DTYPE), b.astype(MXU_DTYPE), preferred_element_type=F32)


def _mxu_nt(a, b):
    return lax.dot_general(a.astype(MXU_DTYPE), b.astype(MXU_DTYPE),
                           (((1,), (1,)), ((), ())), preferred_element_type=F32)


def _silu(x):
    return x * jax.nn.sigmoid(x)


def _softmax_rows(s):
    e = jnp.exp(s - jnp.max(s, axis=-1, keepdims=True))
    return e / jnp.sum(e, axis=-1, keepdims=True)


def _softmax_scaled(s):
    e = jnp.exp2((s - jnp.max(s, axis=-1, keepdims=True)) * EXP2_SCALE)
    return e / jnp.sum(e, axis=-1, keepdims=True)


def _mod_spec(tm, rows_per_mod, d, grid_rank):
    if rows_per_mod >= tm:
        assert rows_per_mod % tm == 0
        block, idx = (1, 1, d), (lambda i: (i * tm) // rows_per_mod)
    else:
        assert rows_per_mod == 1
        block, idx = (tm, 1, d), (lambda i: i)
    if grid_rank == 1:
        return pl.BlockSpec(block, lambda i: (idx(i), 0, 0))
    return pl.BlockSpec(block, lambda i, j: (idx(i), 0, 0))


def _modulated(x_ref, ng_ref, sh_ref, sc_ref):
    x = x_ref[...]
    y = x * lax.rsqrt(jnp.mean(x * x, axis=-1, keepdims=True) + RMS_EPS) * ng_ref[...]
    return y * (1.0 + sc_ref[:, 0, :]) + sh_ref[:, 0, :]


def _mod_kernel(c_ref, w_ref, b_ref, o_ref):
    o_ref[...] = _mxu(_silu(c_ref[...]), w_ref[...]) + b_ref[...]


def _mod_vectors(c_all, w_mod, b_mod, tn=1024):
    depth, d, n = w_mod.shape
    r = c_all.shape[0]
    return pl.pallas_call(
        _mod_kernel,
        grid=(depth, n // tn),
        in_specs=[pl.BlockSpec((r, d), lambda l, j: (0, 0)),
                  pl.BlockSpec((None, d, tn), lambda l, j: (l, 0, j)),
                  pl.BlockSpec((None, 1, tn), lambda l, j: (l, 0, j))],
        out_specs=pl.BlockSpec((None, r, tn), lambda l, j: (l, 0, j)),
        out_shape=jax.ShapeDtypeStruct((depth, r, n), F32),
        compiler_params=_params("arbitrary", "arbitrary"),
        name="mod_vectors",
    )(c_all, w_mod, b_mod.reshape(depth, 1, n))


def _ffn_kernel(x_ref, ng_ref, sh_ref, sc_ref, gt_ref, wg_ref, wu_ref, wd_ref, o_ref, h_scr):
    f = pl.program_id(1)

    @pl.when(f == 0)
    def _():
        h_scr[...] = _modulated(x_ref, ng_ref, sh_ref, sc_ref).astype(h_scr.dtype)

    h = h_scr[...]
    a = _silu(_mxu(h, wg_ref[...])) * _mxu(h, wu_ref[...])
    y = _mxu(a, wd_ref[...])

    @pl.when(f == 0)
    def _():
        o_ref[...] = y

    @pl.when(f > 0)
    def _():
        o_ref[...] += y

    @pl.when(f == pl.num_programs(1) - 1)
    def _():
        o_ref[...] = x_ref[...] + (0.5 * gt_ref[:, 0, :]) * o_ref[...]


def _ffn(x, ng, shift, scale, gate, w_gate, w_up, w_down, *, layer, which, rows_per_mod, tm, tf):
    m, d = x.shape
    dff = w_gate.shape[-1]
    mod = _mod_spec(tm, rows_per_mod, d, 2)
    return pl.pallas_call(
        _ffn_kernel,
        grid=(m // tm, dff // tf),
        in_specs=[pl.BlockSpec((tm, d), lambda i, f: (i, 0), pipeline_mode=pl.Buffered(1)),
                  pl.BlockSpec((1, d), lambda i, f: (0, 0)),
                  mod, mod, mod,
                  pl.BlockSpec((None, None, d, tf), lambda i, f: (layer, which, 0, f)),
                  pl.BlockSpec((None, None, d, tf), lambda i, f: (layer, which, 0, f)),
                  pl.BlockSpec((None, None, tf, d), lambda i, f: (layer, which, f, 0))],
        out_specs=pl.BlockSpec((tm, d), lambda i, f: (i, 0)),
        out_shape=jax.ShapeDtypeStruct((m, d), F32),
        scratch_shapes=[pltpu.VMEM((tm, d), MXU_DTYPE)],
        compiler_params=_params("arbitrary", "arbitrary"),
        name="ffn",
    )(x, ng, shift, scale, gate, w_gate, w_up, w_down)


def _rope_chunk(z, cos, sin):
    return z * cos + pltpu.roll(z, HEAD_DIM // 2, 1) * sin


def _proj_kernel(x_ref, ng_ref, sh_ref, sc_ref, w_ref, cos_ref, sin_ref, *rest, mode):
    out_refs, h_scr = rest[:-1], rest[-1]

    @pl.when(pl.program_id(1) == 0)
    def _():
        h_scr[...] = _modulated(x_ref, ng_ref, sh_ref, sc_ref).astype(h_scr.dtype)

    z = _mxu(h_scr[...], w_ref[...])
    if mode == "sigmoid":
        out_refs[0][...] = jax.nn.sigmoid(z)
        return
    cos, sin = cos_ref[...], sin_ref[...]
    n_chunks = z.shape[1] // HEAD_DIM
    for c in range(n_chunks):
        zc = z[:, c * HEAD_DIM:(c + 1) * HEAD_DIM]
        if mode == "q" or c % 2 == 0:
            zc = _rope_chunk(zc, cos, sin)
        for o in out_refs:
            o[:, c * HEAD_DIM:(c + 1) * HEAD_DIM] = zc.astype(o.dtype)


def _proj(x, ng, shift, scale, w, cos, sin, *, layer, mode, col0, n, out_dtypes, rows_per_mod, rows_per_seq,
          tm, tn):
    m, d = x.shape
    mod = _mod_spec(tm, rows_per_mod, d, 2)
    seq_tiles = rows_per_seq // tm
    assert col0 % tn == 0 and n % tn == 0 and rows_per_seq % tm == 0
    if mode == "kv":
        per = KV_WIDTH // tn
        out_specs = [pl.BlockSpec((None, tm, tn), lambda i, j: (j // per, i, j % per)) for _ in out_dtypes]
        out_shape = [jax.ShapeDtypeStruct((n // KV_WIDTH, m, KV_WIDTH), dt) for dt in out_dtypes]
    else:
        out_specs = [pl.BlockSpec((tm, tn), lambda i, j: (i, j)) for _ in out_dtypes]
        out_shape = [jax.ShapeDtypeStruct((m, n), dt) for dt in out_dtypes]
    return pl.pallas_call(
        functools.partial(_proj_kernel, mode=mode),
        grid=(m // tm, n // tn),
        in_specs=[pl.BlockSpec((tm, d), lambda i, j: (i, 0)),
                  pl.BlockSpec((1, d), lambda i, j: (0, 0)),
                  mod, mod,
                  pl.BlockSpec((None, d, tn), lambda i, j: (layer, 0, col0 // tn + j)),
                  pl.BlockSpec((tm, HEAD_DIM), lambda i, j: (i % seq_tiles, 0)),
                  pl.BlockSpec((tm, HEAD_DIM), lambda i, j: (i % seq_tiles, 0))],
        out_specs=out_specs,
        out_shape=out_shape,
        scratch_shapes=[pltpu.VMEM((tm, d), MXU_DTYPE)],
        compiler_params=_params("arbitrary", "arbitrary"),
        name="proj_" + mode,
    )(x, ng, shift, scale, w, cos, sin)


def _glu_kernel(x_ref, ng_ref, sh_ref, sc_ref, wa_ref, wb_ref, o_ref, h_scr):
    @pl.when(pl.program_id(1) == 0)
    def _():
        h_scr[...] = _modulated(x_ref, ng_ref, sh_ref, sc_ref).astype(h_scr.dtype)

    h = h_scr[...]
    o_ref[...] = _mxu(h, wa_ref[...]) * jax.nn.sigmoid(_mxu(h, wb_ref[...]))


def _glu_proj(x, ng, shift, scale, w_pw1, *, layer, rows_per_mod, tm, tn):
    m, d = x.shape
    dc = w_pw1.shape[-1] // 2
    mod = _mod_spec(tm, rows_per_mod, d, 2)
    return pl.pallas_call(
        _glu_kernel,
        grid=(m // tm, dc // tn),
        in_specs=[pl.BlockSpec((tm, d), lambda i, j: (i, 0)),
                  pl.BlockSpec((1, d), lambda i, j: (0, 0)),
                  mod, mod,
                  pl.BlockSpec((None, d, tn), lambda i, j: (layer, 0, j)),
                  pl.BlockSpec((None, d, tn), lambda i, j: (layer, 0, dc // tn + j))],
        out_specs=pl.BlockSpec((tm, tn), lambda i, j: (i, j)),
        out_shape=jax.ShapeDtypeStruct((m, dc), F32),
        scratch_shapes=[pltpu.VMEM((tm, d), MXU_DTYPE)],
        compiler_params=_params("arbitrary", "arbitrary"),
        name="glu_proj",
    )(x, ng, shift, scale, w_pw1, w_pw1)


def _linres_kernel(a_ref, w_ref, x_ref, gt_ref, o_ref):
    o_ref[...] = x_ref[...] + gt_ref[:, 0, :] * _mxu(a_ref[...], w_ref[...])


def _linres(a, w, x, gate, *, layer, rows_per_mod, tm, tn):
    m, k = a.shape
    d = x.shape[1]
    if rows_per_mod >= tm:
        gspec = pl.BlockSpec((1, 1, tn), lambda i, j: ((i * tm) // rows_per_mod, 0, j))
    else:
        gspec = pl.BlockSpec((tm, 1, tn), lambda i, j: (i, 0, j))
    return pl.pallas_call(
        _linres_kernel,
        grid=(m // tm, d // tn),
        in_specs=[pl.BlockSpec((tm, k), lambda i, j: (i, 0)),
                  pl.BlockSpec((None, k, tn), lambda i, j: (layer, 0, j)),
                  pl.BlockSpec((tm, tn), lambda i, j: (i, j)),
                  gspec],
        out_specs=pl.BlockSpec((tm, tn), lambda i, j: (i, j)),
        out_shape=jax.ShapeDtypeStruct((m, d), F32),
        compiler_params=_params("arbitrary", "arbitrary"),
        name="linres",
    )(a, w, x, gate)


def _rms_kernel(x_ref, g_ref, o_ref):
    x = x_ref[...]
    o_ref[...] = x * lax.rsqrt(jnp.mean(x * x, axis=-1, keepdims=True) + RMS_EPS) * g_ref[...]


def _rms_norm(x, g, tm):
    m, d = x.shape
    return pl.pallas_call(
        _rms_kernel,
        grid=(m // tm,),
        in_specs=[pl.BlockSpec((tm, d), lambda i: (i, 0)), pl.BlockSpec((1, d), lambda i: (0, 0))],
        out_specs=pl.BlockSpec((tm, d), lambda i: (i, 0)),
        out_shape=jax.ShapeDtypeStruct((m, d), F32),
        compiler_params=_params("arbitrary"),
        name="final_norm",
    )(x, g.reshape(1, d))


def _compress_tail(acc, w2_ref, o_ref):
    o_ref[...] = _mxu(_silu(acc), w2_ref[...]).astype(o_ref.dtype)


def _compress_kernel(x_ref, pe_ref, w1_ref, w2_ref, o_ref, *, nb):
    acc = jnp.zeros((nb, HEAD_DIM), F32)
    for c in range(CMP_BLOCK):
        xc = x_ref[pl.ds(c, nb, stride=CMP_BLOCK), :] + pe_ref[c:c + 1, :]
        acc += _mxu(xc, w1_ref[c * HEAD_DIM:(c + 1) * HEAD_DIM, :])
    _compress_tail(acc, w2_ref, o_ref)


def _compress_rows(rows, pe, w1, w2, *, layer, nb):
    n_rows = rows.shape[0]
    n_blocks = n_rows // CMP_BLOCK
    n_chunks = KV_WIDTH // HEAD_DIM
    return pl.pallas_call(
        functools.partial(_compress_kernel, nb=nb),
        grid=(n_blocks // nb, n_chunks),
        in_specs=[pl.BlockSpec((nb * CMP_BLOCK, HEAD_DIM), lambda i, ch: (i, ch)),
                  pl.BlockSpec((None, None, CMP_BLOCK, HEAD_DIM), lambda i, ch: (layer, ch % 2, 0, 0)),
                  pl.BlockSpec((None, None, CMP_BLOCK * HEAD_DIM, HEAD_DIM), lambda i, ch: (layer, ch % 2, 0, 0)),
                  pl.BlockSpec((None, None, HEAD_DIM, HEAD_DIM), lambda i, ch: (layer, ch % 2, 0, 0))],
        out_specs=pl.BlockSpec((nb, HEAD_DIM), lambda i, ch: (i, ch)),
        out_shape=jax.ShapeDtypeStruct((n_blocks, KV_WIDTH), MXU_DTYPE),
        compiler_params=_params("arbitrary", "arbitrary"),
        name="compress_rows",
    )(rows, pe, w1, w2)


def _compress_pages_kernel(pt_ref, *refs, blocks_per_page):
    del pt_ref
    pages = refs[:PAGES_PER_STEP]
    pe_ref, w1_ref, w2_ref, o_ref, t_scr = refs[PAGES_PER_STEP:]
    nb = PAGES_PER_STEP * blocks_per_page
    pages_per_group = SUBLANES // blocks_per_page
    n_groups = PAGES_PER_STEP // pages_per_group
    rows = SUBLANES * CMP_BLOCK
    out_row = lax.broadcasted_iota(jnp.int32, (rows, rows), 0)
    in_row = lax.broadcasted_iota(jnp.int32, (rows, rows), 1)
    perm = jnp.where(in_row == (out_row % SUBLANES) * CMP_BLOCK + out_row // SUBLANES, 1.0, 0.0)
    pe_rows = jnp.concatenate([pe_ref[kv] for _ in range(N_KV_HEADS) for kv in range(2)], axis=1)
    pe_rows = jnp.concatenate([pe_rows] * SUBLANES, axis=0)
    n_chunks = pages[0].shape[1]
    for gp in range(n_groups):
        x = jnp.concatenate(
            [jnp.concatenate([pages[gp * pages_per_group + k][:, ch, :] for ch in range(n_chunks)], axis=1)
             for k in range(pages_per_group)], axis=0)
        t_scr[gp] = _mxu(perm, x + pe_rows)
    for kv in range(2):
        acc = jnp.zeros((N_KV_HEADS * nb, HEAD_DIM), F32)
        for c in range(CMP_BLOCK):
            xc = jnp.concatenate(
                [t_scr[:, c * SUBLANES:(c + 1) * SUBLANES,
                       (g * 2 + kv) * HEAD_DIM:(g * 2 + kv + 1) * HEAD_DIM].reshape(nb, HEAD_DIM)
                 for g in range(N_KV_HEADS)], axis=0)
            acc += _mxu(xc, w1_ref[kv, c * HEAD_DIM:(c + 1) * HEAD_DIM, :])
        y = _mxu(_silu(acc), w2_ref[kv])
        for g in range(N_KV_HEADS):
            o_ref[:, (g * 2 + kv) * HEAD_DIM:(g * 2 + kv + 1) * HEAD_DIM] = (
                y[g * nb:(g + 1) * nb].astype(o_ref.dtype))


def _compress_pages(pool, page_table, pe, w1, w2, *, layer):
    _, _, page, n_chunks, _ = pool.shape
    width = n_chunks * HEAD_DIM
    b, n_pages = page_table.shape
    bpp = page // CMP_BLOCK
    nb = PAGES_PER_STEP * bpp
    assert n_pages % PAGES_PER_STEP == 0 and page % CMP_BLOCK == 0 and SUBLANES % bpp == 0
    n_groups = nb // SUBLANES

    def page_spec(p):
        return pl.BlockSpec((None, None, page, n_chunks, HEAD_DIM),
                            lambda bi, gi, pt: (layer, pt[bi * n_pages + gi * PAGES_PER_STEP + p], 0, 0, 0))

    grid_spec = pltpu.PrefetchScalarGridSpec(
        num_scalar_prefetch=1,
        grid=(b, n_pages // PAGES_PER_STEP),
        in_specs=[page_spec(p) for p in range(PAGES_PER_STEP)] + [
            pl.BlockSpec((None, 2, CMP_BLOCK, HEAD_DIM), lambda bi, gi, pt: (layer, 0, 0, 0)),
            pl.BlockSpec((None, 2, CMP_BLOCK * HEAD_DIM, HEAD_DIM), lambda bi, gi, pt: (layer, 0, 0, 0)),
            pl.BlockSpec((None, 2, HEAD_DIM, HEAD_DIM), lambda bi, gi, pt: (layer, 0, 0, 0))],
        out_specs=pl.BlockSpec((None, nb, width), lambda bi, gi, pt: (bi, gi, 0)),
        scratch_shapes=[pltpu.VMEM((n_groups, SUBLANES * CMP_BLOCK, width), F32)],
    )
    return pl.pallas_call(
        functools.partial(_compress_pages_kernel, blocks_per_page=bpp),
        grid_spec=grid_spec,
        out_shape=jax.ShapeDtypeStruct((b, n_pages * bpp, width), MXU_DTYPE),
        compiler_params=_params("arbitrary", "arbitrary"),
        name="compress_pages",
    )(page_table.reshape(-1), *([pool] * PAGES_PER_STEP), pe, w1, w2)


def _pair_sums(imp):
    out = []
    for k in range(imp.shape[1] // LANES):
        x = imp[:, k * LANES:(k + 1) * LANES]
        even = lax.broadcasted_iota(jnp.int32, x.shape, 1) % 2 == 0
        out.append(x + jnp.where(even, pltpu.roll(x, LANES - 1, 1), pltpu.roll(x, 1, 1)))
    return out[0] if len(out) == 1 else jnp.concatenate(out, axis=1)


def _block_scores(imp, q_pos, n_sel_blocks):
    lane = lax.broadcasted_iota(jnp.int32, imp.shape, 1)
    blk = lane // SEL_RATIO
    cur = q_pos // SEL_BLOCK
    forced = (blk == 0) | (blk == cur) | (blk == cur - 1)
    valid = blk * SEL_BLOCK <= q_pos
    score = jnp.where(forced, FORCED_SCORE, jnp.where(valid, _pair_sums(imp), -1.0))
    eligible = (lane % SEL_RATIO == 0) & (blk < n_sel_blocks)
    return jnp.where(eligible, score, KNOCKED_OUT)


def _take_top(work):
    lane = lax.broadcasted_iota(jnp.int32, work.shape, 1).astype(F32)
    top = jnp.max(work, axis=-1, keepdims=True)
    pick = jnp.min(jnp.where(work == top, lane, float(work.shape[1])), axis=-1, keepdims=True)
    return lane == pick, pick


def _attn_prompt_kernel(q_ref, ck_ref, cv_ref, ks_ref, vs_ref, kw_ref, vw_ref, g_ref, o_ref, *, tq, tk, seq):
    i = pl.program_id(2)
    n_cmp = ck_ref.shape[0]
    n_sel_blocks = -(-seq // SEL_BLOCK)
    n_take = min(N_SELECT, n_sel_blocks)
    assert SEL_RATIO == 2 and n_cmp * CMP_BLOCK == seq and n_cmp <= LANES
    q = q_ref[...]
    q4 = jnp.concatenate([q[:, h * HEAD_DIM:(h + 1) * HEAD_DIM] for h in range(GROUP_SIZE)], axis=0)
    q_pos = i * tq + lax.broadcasted_iota(jnp.int32, (tq, 1), 0)

    pad = jnp.zeros((LANES - n_cmp, HEAD_DIM), ck_ref.dtype)
    ck = jnp.concatenate([ck_ref[...], pad], axis=0) if n_cmp < LANES else ck_ref[...]
    cv = jnp.concatenate([cv_ref[...], pad], axis=0) if n_cmp < LANES else cv_ref[...]
    lane = lax.broadcasted_iota(jnp.int32, (1, LANES), 1)
    vis = ((lane + 1) * CMP_BLOCK - 1 <= q_pos) & (lane < n_cmp)
    s = _mxu_nt(q4, ck).reshape(GROUP_SIZE, tq, LANES)
    p = jnp.where(vis[None], _softmax_scaled(jnp.where(vis[None], s, NEG_INF)), 0.0)
    o_c = _mxu(p.reshape(GROUP_SIZE * tq, LANES), cv).reshape(GROUP_SIZE, tq, HEAD_DIM)
    imp = jnp.sum(p, axis=0)

    def top_blocks():
        work = _block_scores(imp, q_pos, n_sel_blocks)
        sel = jnp.zeros((tq, LANES), F32)
        for _ in range(n_take):
            hit, _ = _take_top(work)
            work = jnp.where(hit, KNOCKED_OUT, work)
            sel = jnp.where(hit, 1.0, sel)
        return sel

    def all_blocks():
        return jnp.where((lane % SEL_RATIO == 0) & (lane // SEL_RATIO < n_sel_blocks), 1.0,
                         jnp.zeros((tq, LANES), F32))

    sel = lax.cond(((i + 1) * tq - 1) // SEL_BLOCK + 1 <= n_take, all_blocks, top_blocks).astype(MXU_DTYPE)

    def sel_chunk(kc, carry):
        m_i, l_i, acc = carry
        start = pl.multiple_of(kc * tk, tk)
        k_pos = start + lax.broadcasted_iota(jnp.int32, (1, tk), 1)
        expand = lax.broadcasted_iota(jnp.int32, (LANES, 1), 0) == SEL_RATIO * (k_pos // SEL_BLOCK)
        picked = jnp.dot(sel, jnp.where(expand, 1.0, 0.0).astype(MXU_DTYPE), preferred_element_type=F32)
        mask = ((picked > 0.5) & (k_pos <= q_pos))[None]
        sc = jnp.where(mask, _mxu_nt(q4, ks_ref[pl.ds(start, tk), :]).reshape(GROUP_SIZE, tq, tk), NEG_INF)
        m_new = jnp.maximum(m_i, jnp.max(sc, axis=-1, keepdims=True))
        alpha = jnp.exp2((m_i - m_new) * EXP2_SCALE)
        e = jnp.exp2((sc - m_new) * EXP2_SCALE)
        l_new = alpha * l_i + jnp.sum(e, axis=-1, keepdims=True)
        pv = _mxu(e.reshape(GROUP_SIZE * tq, tk), vs_ref[pl.ds(start, tk), :])
        return m_new, l_new, alpha * acc + pv.reshape(GROUP_SIZE, tq, HEAD_DIM)

    n_chunks = ((i + 1) * tq + tk - 1) // tk
    init = (jnp.full((GROUP_SIZE, tq, 1), NEG_INF, F32), jnp.zeros((GROUP_SIZE, tq, 1), F32),
            jnp.zeros((GROUP_SIZE, tq, HEAD_DIM), F32))
    _, l_s, acc_s = lax.fori_loop(0, n_chunks, sel_chunk, init)
    o_s = acc_s / l_s

    span = tq + WINDOW
    w_start = pl.multiple_of(jnp.maximum(i * tq - WINDOW, 0), tq)
    k_pos = w_start + lax.broadcasted_iota(jnp.int32, (1, span), 1)
    dist = q_pos - k_pos
    mask = ((dist >= 0) & (dist < WINDOW))[None]
    sw = _mxu_nt(q4, kw_ref[pl.ds(w_start, span), :]).reshape(GROUP_SIZE, tq, span)
    pw = _softmax_scaled(jnp.where(mask, sw, NEG_INF))
    o_w = _mxu(pw.reshape(GROUP_SIZE * tq, span), vw_ref[pl.ds(w_start, span), :]).reshape(GROUP_SIZE, tq, HEAD_DIM)

    gates = g_ref[...]
    for h in range(GROUP_SIZE):
        o = (gates[:, 3 * h:3 * h + 1] * o_c[h] + gates[:, 3 * h + 1:3 * h + 2] * o_s[h]
             + gates[:, 3 * h + 2:3 * h + 3] * o_w[h])
        o_ref[:, h * HEAD_DIM:(h + 1) * HEAD_DIM] = o.astype(o_ref.dtype)


def _attn_prompt(q, ckv, kv, gates, *, tq, tk):
    b, t, _ = q.shape
    n_cmp = ckv.shape[1]
    assert t % tq == 0 and t % tk == 0 and t >= tq + WINDOW and WINDOW % tq == 0
    gw = GROUP_SIZE * HEAD_DIM

    def kv_spec(branch, part):
        return pl.BlockSpec((None, None, t, HEAD_DIM), lambda bi, g, i: (branch, bi, 0, 2 * g + part))

    return pl.pallas_call(
        functools.partial(_attn_prompt_kernel, tq=tq, tk=tk, seq=t),
        grid=(b, N_KV_HEADS, t // tq),
        in_specs=[pl.BlockSpec((None, tq, gw), lambda bi, g, i: (bi, i, g)),
                  pl.BlockSpec((None, n_cmp, HEAD_DIM), lambda bi, g, i: (bi, 0, 2 * g)),
                  pl.BlockSpec((None, n_cmp, HEAD_DIM), lambda bi, g, i: (bi, 0, 2 * g + 1)),
                  kv_spec(1, 0), kv_spec(1, 1), kv_spec(2, 0), kv_spec(2, 1),
                  pl.BlockSpec((None, tq, LANES), lambda bi, g, i: (bi, i, g))],
        out_specs=pl.BlockSpec((None, tq, gw), lambda bi, g, i: (bi, i, g)),
        out_shape=jax.ShapeDtypeStruct((b, t, Q_WIDTH), MXU_DTYPE),
        compiler_params=_params("arbitrary", "arbitrary", "arbitrary"),
        name="attn_prompt",
    )(q, ckv, ckv, kv, kv, kv, kv, gates)


def _attn_cmp_sample_kernel(q_ref, ckv_ref, o_ref, idx_ref, *, past, n_total, width):
    t_rows = q_ref.shape[0]
    n_cmp = ckv_ref.shape[0]
    n_sel_blocks = -(-n_total // SEL_BLOCK)
    n_take = min(N_SELECT, n_sel_blocks)
    assert SEL_RATIO == 2 and n_take <= LANES and SEL_RATIO * n_sel_blocks <= width
    q_pos = past + lax.broadcasted_iota(jnp.int32, (t_rows, 1), 0)
    lane = lax.broadcasted_iota(jnp.int32, (1, width), 1)
    vis = ((lane + 1) * CMP_BLOCK - 1 <= q_pos) & (lane < n_cmp)
    out_lane = lax.broadcasted_iota(jnp.int32, (1, LANES), 1)
    pad = jnp.zeros((width - n_cmp, HEAD_DIM), ckv_ref.dtype)
    for g in range(N_KV_HEADS):
        ck = jnp.concatenate([ckv_ref[:, 2 * g * HEAD_DIM:(2 * g + 1) * HEAD_DIM], pad], axis=0)
        cv = jnp.concatenate([ckv_ref[:, (2 * g + 1) * HEAD_DIM:(2 * g + 2) * HEAD_DIM], pad], axis=0)
        imp = jnp.zeros((t_rows, width), F32)
        for h in range(GROUP_SIZE):
            cols = slice((g * GROUP_SIZE + h) * HEAD_DIM, (g * GROUP_SIZE + h + 1) * HEAD_DIM)
            s = _mxu_nt(q_ref[:, cols], ck) * ATTN_SCALE
            p = jnp.where(vis, _softmax_rows(jnp.where(vis, s, NEG_INF)), 0.0)
            o_ref[:, cols] = _mxu(p, cv)
            imp = imp + p
        work = _block_scores(imp, q_pos, n_sel_blocks)
        picks = jnp.zeros((t_rows, LANES), F32)
        for r in range(n_take):
            hit, pick = _take_top(work)
            work = jnp.where(hit, KNOCKED_OUT, work)
            picks = jnp.where(out_lane == r, pick, picks)
        idx_ref[g] = (picks * (1.0 / SEL_RATIO)).astype(jnp.int32)


def _attn_cmp_sample(q, ckv, *, past, n_total):
    b, t, qw = q.shape
    n_cmp = ckv.shape[1]
    width = -(-max(n_cmp, SEL_RATIO * -(-n_total // SEL_BLOCK)) // LANES) * LANES
    return pl.pallas_call(
        functools.partial(_attn_cmp_sample_kernel, past=past, n_total=n_total, width=width),
        grid=(b,),
        in_specs=[pl.BlockSpec((None, t, qw), lambda bi: (bi, 0, 0)),
                  pl.BlockSpec((None, n_cmp, KV_WIDTH), lambda bi: (bi, 0, 0))],
        out_specs=[pl.BlockSpec((None, t, qw), lambda bi: (bi, 0, 0)),
                   pl.BlockSpec((None, N_KV_HEADS, t, LANES), lambda bi: (bi, 0, 0, 0))],
        out_shape=[jax.ShapeDtypeStruct(q.shape, F32),
                   jax.ShapeDtypeStruct((b, N_KV_HEADS, t, LANES), jnp.int32)],
        compiler_params=_params("arbitrary"),
        name="attn_cmp_sample",
    )(q, ckv)


def _attn_win_sample_kernel(q_ref, buf_ref, new_ref, o_ref, *, t_new):
    t_rows = q_ref.shape[0]
    n_buf = buf_ref.shape[0]
    t_idx = lax.broadcasted_iota(jnp.int32, (t_rows, 1), 0)
    dist_a = t_idx + n_buf - lax.broadcasted_iota(jnp.int32, (1, n_buf), 1)
    row_b = lax.broadcasted_iota(jnp.int32, (1, new_ref.shape[0]), 1)
    mask_a = (dist_a >= 0) & (dist_a < WINDOW)
    mask_b = (row_b <= t_idx) & (t_idx - row_b < WINDOW) & (row_b < t_new)
    for g in range(N_KV_HEADS):
        k_a, v_a = buf_ref[:, g, 0, :], buf_ref[:, g, 1, :]
        k_b = new_ref[:, 2 * g * HEAD_DIM:(2 * g + 1) * HEAD_DIM]
        v_b = new_ref[:, (2 * g + 1) * HEAD_DIM:(2 * g + 2) * HEAD_DIM]
        for h in range(GROUP_SIZE):
            cols = slice((g * GROUP_SIZE + h) * HEAD_DIM, (g * GROUP_SIZE + h + 1) * HEAD_DIM)
            q = q_ref[:, cols]
            s_a = jnp.where(mask_a, _mxu_nt(q, k_a) * ATTN_SCALE, NEG_INF)
            s_b = jnp.where(mask_b, _mxu_nt(q, k_b) * ATTN_SCALE, NEG_INF)
            m = jnp.maximum(jnp.max(s_a, axis=-1, keepdims=True), jnp.max(s_b, axis=-1, keepdims=True))
            e_a, e_b = jnp.exp(s_a - m), jnp.exp(s_b - m)
            l = jnp.sum(e_a, axis=-1, keepdims=True) + jnp.sum(e_b, axis=-1, keepdims=True)
            o_ref[:, cols] = _mxu(e_a / l, v_a) + _mxu(e_b / l, v_b)


def _attn_win_sample(q, win_buf, kv_new, *, layer, t_new):
    b, t, qw = q.shape
    n_buf = win_buf.shape[2]
    return pl.pallas_call(
        functools.partial(_attn_win_sample_kernel, t_new=t_new),
        grid=(b,),
        in_specs=[pl.BlockSpec((None, t, qw), lambda bi: (bi, 0, 0)),
                  pl.BlockSpec((None, None, n_buf, N_KV_HEADS, 2, HEAD_DIM), lambda bi: (layer, bi, 0, 0, 0, 0)),
                  pl.BlockSpec((None, kv_new.shape[1], KV_WIDTH), lambda bi: (bi, 0, 0))],
        out_specs=pl.BlockSpec((None, t, qw), lambda bi: (bi, 0, 0)),
        out_shape=jax.ShapeDtypeStruct(q.shape, F32),
        compiler_params=_params("arbitrary"),
        name="attn_win_sample",
    )(q, win_buf, kv_new)


def _attn_sel_sample_kernel(idx_ref, pt_ref, q_ref, *refs, past, t_new, n_slots, t_steps):
    del pt_ref
    blocks = refs[:n_slots]
    new_ref, oc_ref, ow_ref, g_ref, o_ref = refs[n_slots:]
    bi, t, gi = pl.program_id(0), pl.program_id(1), pl.program_id(2)
    base = ((bi * t_steps + t) * N_KV_HEADS + gi) * n_slots
    n_past_blocks = past // SEL_BLOCK
    q_pos = past + t
    q = q_ref[...]

    keys = jnp.concatenate([blk[:, 0, :].astype(MXU_DTYPE) for blk in blocks], axis=0)
    vals = jnp.concatenate([blk[:, 1, :].astype(MXU_DTYPE) for blk in blocks], axis=0)
    lane = lax.broadcasted_iota(jnp.int32, (1, n_slots * SEL_BLOCK), 1)
    pos = lane % SEL_BLOCK
    has_new = jnp.int32(0)
    for n in range(n_slots):
        blk = idx_ref[base + n]
        start = jnp.where(blk < n_past_blocks, blk * SEL_BLOCK, q_pos + 1)
        pos = pos + jnp.where(lane // SEL_BLOCK == n, start, 0)
        has_new = has_new | (blk == n_past_blocks).astype(jnp.int32)
    mask_a = pos <= q_pos
    new = new_ref[...]
    row_b = lax.broadcasted_iota(jnp.int32, (1, new.shape[0]), 1)
    mask_b = (row_b <= t) & (row_b < t_new) & (has_new > 0)
    s_a = jnp.where(mask_a, _mxu_nt(q, keys) * ATTN_SCALE, NEG_INF)
    s_b = jnp.where(mask_b, _mxu_nt(q, new[:, :HEAD_DIM]) * ATTN_SCALE, NEG_INF)
    m = jnp.maximum(jnp.max(s_a, axis=-1, keepdims=True), jnp.max(s_b, axis=-1, keepdims=True))
    e_a, e_b = jnp.exp(s_a - m), jnp.exp(s_b - m)
    l = jnp.sum(e_a, axis=-1, keepdims=True) + jnp.sum(e_b, axis=-1, keepdims=True)
    o_s = _mxu(e_a / l, vals) + _mxu(e_b / l, new[:, HEAD_DIM:])
    gates = g_ref[...]
    o_ref[...] = gates[:, 0:1] * oc_ref[...] + gates[:, 1:2] * o_s + gates[:, 2:3] * ow_ref[...]


def _attn_sel_sample(idx, page_table, q, pool, kv_new, o_c, o_w, gates, *, layer, past, t_new):
    b, t, g, hg, hd = q.shape
    n_slots = idx.shape[-1]
    n_pages = page_table.shape[1]
    page = past // n_pages
    per_page = page // SEL_BLOCK
    n_past_blocks = past // SEL_BLOCK

    def slot_spec(n):
        def index(bi, ti, gi, idx_s, pt_s):
            blk = jnp.minimum(idx_s[((bi * t + ti) * g + gi) * n_slots + n], n_past_blocks - 1)
            phys = pt_s[bi * n_pages + blk // per_page]
            return (layer, phys, blk % per_page, gi, 0, 0)
        return pl.BlockSpec((None, None, SEL_BLOCK, None, 2, HEAD_DIM), index)

    head_spec = pl.BlockSpec((None, None, None, hg, hd), lambda bi, ti, gi, idx_s, pt_s: (bi, ti, gi, 0, 0))
    grid_spec = pltpu.PrefetchScalarGridSpec(
        num_scalar_prefetch=2,
        grid=(b, t, g),
        in_specs=[head_spec] + [slot_spec(n) for n in range(n_slots)] + [
            pl.BlockSpec((None, kv_new.shape[1], 2 * HEAD_DIM), lambda bi, ti, gi, idx_s, pt_s: (bi, 0, gi)),
            head_spec, head_spec,
            pl.BlockSpec((None, None, None, hg, 3), lambda bi, ti, gi, idx_s, pt_s: (bi, ti, gi, 0, 0))],
        out_specs=head_spec,
    )
    return pl.pallas_call(
        functools.partial(_attn_sel_sample_kernel, past=past, t_new=t_new, n_slots=n_slots, t_steps=t),
        grid_spec=grid_spec,
        out_shape=jax.ShapeDtypeStruct(q.shape, F32),
        compiler_params=_params("arbitrary", "arbitrary", "arbitrary"),
        name="attn_sel_sample",
    )(idx.reshape(-1), page_table.reshape(-1), q, *([pool] * n_slots), kv_new, o_c, o_w, gates)


def _dwconv_kernel(cur_ref, prev_ref, buf_ref, w_ref, b_ref, lg_ref, lb_ref, o_ref, full_scr, y_scr, *,
                   tt, halo, width):
    i = pl.program_id(1)
    lead = halo - (width - 1)
    full_scr[0:halo, :] = prev_ref[...]

    @pl.when(i == 0)
    def _():
        full_scr[0:halo, :] = buf_ref[...]

    full_scr[halo:halo + tt, :] = cur_ref[...]
    for c0 in range(0, cur_ref.shape[1], DWCONV_COLS):
        cols = slice(c0, c0 + DWCONV_COLS)
        acc = jnp.zeros((tt, DWCONV_COLS), F32) + b_ref[:, cols]
        for r in range(SUBLANES):
            offsets = [o for o in range(r, lead + width, SUBLANES) if o >= lead]
            if not offsets:
                continue
            shifted = full_scr[r:offsets[-1] + tt, cols]
            for o in offsets:
                acc = acc + shifted[o - r:o - r + tt] * w_ref[o - lead:o - lead + 1, cols]
        y_scr[:, cols] = acc
    y = y_scr[...]
    mu = jnp.mean(y, axis=-1, keepdims=True)
    var = jnp.mean(jnp.square(y - mu), axis=-1, keepdims=True)
    yn = (y - mu) * lax.rsqrt(var + LN_EPS) * lg_ref[...] + lb_ref[...]
    o_ref[...] = _silu(yn).astype(o_ref.dtype)


def _dwconv(glu, buf, w_dw, b_dw, ln_g, ln_b, *, layer, tt):
    b, t, d = glu.shape
    width = w_dw.shape[1]
    halo = buf.shape[1]
    assert tt % halo == 0 and t % tt == 0 and halo >= width - 1
    per = tt // halo

    def vec_spec():
        return pl.BlockSpec((None, 1, d), lambda bi, i: (layer, 0, 0))

    return pl.pallas_call(
        functools.partial(_dwconv_kernel, tt=tt, halo=halo, width=width),
        grid=(b, t // tt),
        in_specs=[pl.BlockSpec((None, tt, d), lambda bi, i: (bi, i, 0)),
                  pl.BlockSpec((None, halo, d), lambda bi, i: (bi, jnp.maximum(i * per - 1, 0), 0)),
                  pl.BlockSpec((None, halo, d), lambda bi, i: (bi, 0, 0)),
                  pl.BlockSpec((None, width, d), lambda bi, i: (layer, 0, 0)),
                  vec_spec(), vec_spec(), vec_spec()],
        out_specs=pl.BlockSpec((None, tt, d), lambda bi, i: (bi, i, 0)),
        out_shape=jax.ShapeDtypeStruct((b, t, d), MXU_DTYPE),
        scratch_shapes=[pltpu.VMEM((halo + tt, d), F32), pltpu.VMEM((tt, d), F32)],
        compiler_params=_params("arbitrary", "arbitrary"),
        name="dwconv",
    )(glu, glu, buf, w_dw, b_dw, ln_g, ln_b)


def _rope_tables(pos):
    half = HEAD_DIM // 2
    inv = ROPE_THETA ** (-jnp.arange(half, dtype=F32) / half)
    ang = pos.astype(F32)[:, None] * inv[None, :]
    cos, sin = jnp.cos(ang), jnp.sin(ang)
    return jnp.concatenate([cos, cos], axis=1), jnp.concatenate([-sin, sin], axis=1)


def _group_gate_weights(w_in_l):
    d = w_in_l.shape[0]
    wg = w_in_l[:, Q_WIDTH + 3 * KV_WIDTH:].reshape(d, N_KV_HEADS, GATE_COLS)
    return jnp.pad(wg, ((0, 0), (0, 0), (0, LANES - GATE_COLS))).reshape(d, N_KV_HEADS * LANES)


def _run_trunk(x, mod, pos, prm, caches, tiles):
    b, t, d = x.shape
    m = b * t
    depth = prm["w_mod"].shape[0]
    tm = tiles["tm"]
    rpm = t if caches is None else 1
    cos, sin = _rope_tables(pos)
    if caches is not None:
        cos, sin = jnp.tile(cos, (b, 1)), jnp.tile(sin, (b, 1))
    rows_per_seq = t if caches is None else m
    x = x.reshape(m, d)
    new_c, new_s, new_w, new_conv = [], [], [], []

    def mod_rows(layer, k):
        v = mod[layer, :, k, :]
        if caches is not None:
            v = jnp.repeat(v, t, axis=0)
        return v[:, None, :]

    def ffn(x, layer, which, k0):
        return _ffn(x, prm["norm_g"][layer, which * 2].reshape(1, d), mod_rows(layer, k0), mod_rows(layer, k0 + 1),
                    mod_rows(layer, k0 + 2), prm["ffn_w_gate"], prm["ffn_w_up"], prm["ffn_w_down"],
                    layer=layer, which=which, rows_per_mod=rpm, tm=tiles["tm_ffn"], tf=tiles["tf"])

    for i in range(depth):
        x = ffn(x, i, 0, 0)
        ng = prm["norm_g"][i, 1].reshape(1, d)
        shift, scale, gate = mod_rows(i, 3), mod_rows(i, 4), mod_rows(i, 5)
        a = i // 2
        if i % 2 == 0:
            w_in = prm["attn_w_in"]
            common = dict(rows_per_mod=rpm, rows_per_seq=rows_per_seq, tm=tm)
            q_dtype = MXU_DTYPE if caches is None else F32
            q, = _proj(x, ng, shift, scale, w_in, cos, sin, layer=a, mode="q", col0=0, n=Q_WIDTH,
                       out_dtypes=[q_dtype], tn=512, **common)
            kv_outs = [F32, MXU_DTYPE] if caches is None else [F32]
            kv = _proj(x, ng, shift, scale, w_in, cos, sin, layer=a, mode="kv", col0=Q_WIDTH, n=3 * KV_WIDTH,
                       out_dtypes=kv_outs, tn=512, **common)
            gates, = _proj(x, ng, shift, scale, _group_gate_weights(w_in[a])[None], cos, sin, layer=0,
                           mode="sigmoid", col0=0, n=N_KV_HEADS * LANES, out_dtypes=[F32],
                           tn=N_KV_HEADS * LANES, **common)
            kv32 = kv[0]
            new_c.append(kv32[0].reshape(b, t, N_KV_HEADS, 2, HEAD_DIM))
            new_s.append(kv32[1].reshape(b, t, N_KV_HEADS, 2, HEAD_DIM))
            if caches is None:
                ckv = _compress_rows(kv32[0], prm["cmp_pe"], prm["cmp_w1"], prm["cmp_w2"], layer=a,
                                     nb=min(256, m // CMP_BLOCK))
                o = _attn_prompt(q.reshape(b, t, Q_WIDTH), ckv.reshape(b, t // CMP_BLOCK, KV_WIDTH),
                                 kv[1].reshape(3, b, t, KV_WIDTH), gates.reshape(b, t, N_KV_HEADS * LANES),
                                 tq=tiles["tq"], tk=tiles["tk"]).reshape(m, Q_WIDTH)
                new_w.append(kv32[2].reshape(b, t, N_KV_HEADS, 2, HEAD_DIM)[:, t - min(WINDOW, t):])
            else:
                pool_c, pool_s, win_buf, _, page_table = caches
                n_layers, n_pool, page = pool_c.shape[:3]
                past = page_table.shape[1] * page
                assert t < CMP_BLOCK and page % SEL_BLOCK == 0
                ckv = _compress_pages(pool_c.reshape(n_layers, n_pool, page, KV_WIDTH // HEAD_DIM, HEAD_DIM),
                                      page_table, prm["cmp_pe"], prm["cmp_w1"], prm["cmp_w2"], layer=a)
                q3 = q.reshape(b, t, Q_WIDTH)
                o_c, idx = _attn_cmp_sample(q3, ckv, past=past, n_total=past + t)
                t_pad = -(-t // SUBLANES) * SUBLANES
                pad_rows = lambda r: jnp.pad(r.reshape(b, t, KV_WIDTH), ((0, 0), (0, t_pad - t), (0, 0)))
                n_buf = win_buf.shape[2]
                o_w = _attn_win_sample(q3, win_buf, pad_rows(kv32[2]), layer=a, t_new=t)
                n_take = min(N_SELECT, -(-(past + t) // SEL_BLOCK))
                idx = jnp.transpose(idx[..., :n_take], (0, 2, 1, 3))
                heads = lambda v: v.reshape(b, t, N_KV_HEADS, GROUP_SIZE, HEAD_DIM)
                g5 = gates.reshape(b, t, N_KV_HEADS, LANES)[..., :GATE_COLS].reshape(b, t, N_KV_HEADS, GROUP_SIZE, 3)
                o = _attn_sel_sample(idx, page_table, heads(q3), pool_s, pad_rows(kv32[1]), heads(o_c), heads(o_w),
                                     g5, layer=a, past=past, t_new=t).reshape(m, Q_WIDTH)
                keys = jnp.concatenate([win_buf[a], kv32[2].reshape(b, t, N_KV_HEADS, 2, HEAD_DIM)], axis=1)
                new_w.append(keys[:, n_buf + t - min(WINDOW, n_buf + t):])
            x = _linres(o, prm["attn_w_out"], x, gate, layer=a, rows_per_mod=rpm, tm=tm, tn=tiles["tn_out"])
        else:
            glu = _glu_proj(x, ng, shift, scale, prm["conv_w_pw1"], layer=a, rows_per_mod=rpm, tm=tm, tn=512)
            glu = glu.reshape(b, t, d)
            width = prm["conv_w_dw"].shape[1]
            halo = 32
            if caches is None:
                buf = jnp.zeros((b, width - 1, d), F32)
                t_conv = t
                cur = glu
            else:
                buf = caches[3][a]
                t_conv = halo
                cur = jnp.pad(glu, ((0, 0), (0, t_conv - t), (0, 0)))
            full = jnp.concatenate([buf, glu], axis=1)
            new_conv.append(full[:, full.shape[1] - (width - 1):])
            buf = jnp.pad(buf, ((0, 0), (halo - (width - 1), 0), (0, 0)))
            vec = lambda v: v.reshape(v.shape[0], 1, d)
            act = _dwconv(cur, buf, prm["conv_w_dw"], vec(prm["conv_b_dw"]), vec(prm["conv_ln_g"]),
                          vec(prm["conv_ln_b"]), layer=a, tt=min(tiles["tt"], t_conv))
            act = act[:, :t].reshape(m, d)
            x = _linres(act, prm["conv_w_pw2"], x, gate, layer=a, rows_per_mod=rpm, tm=tm, tn=tiles["tn_out"])
        x = ffn(x, i, 1, 6)
    y = _rms_norm(x, prm["final_norm_g"], tm).reshape(b, t, d)
    return y, jnp.stack(new_c), jnp.stack(new_s), jnp.stack(new_w), jnp.stack(new_conv)


PROMPT_TILES = dict(tm=512, tm_ffn=1024, tf=256, tq=256, tk=512, tn_out=512, tt=128)
SAMPLE_TILES = dict(tm=32, tm_ffn=32, tf=512, tn_out=512, tt=32)


def kernel(x_prompt, x_sample, cache_cmp_kv, cache_sel_kv, cache_win_kv, state_conv, page_table, c_prompt, c_sample, w_mod, b_mod, norm_g, ffn_w_gate, ffn_w_up, ffn_w_down, attn_w_in, attn_w_out, cmp_pe, cmp_w1, cmp_w2, conv_w_pw1, conv_w_dw, conv_b_dw, conv_ln_g, conv_ln_b, conv_w_pw2, final_norm_g):
    prm = {"w_mod": w_mod, "norm_g": norm_g, "ffn_w_gate": ffn_w_gate, "ffn_w_up": ffn_w_up,
           "ffn_w_down": ffn_w_down, "attn_w_in": attn_w_in, "attn_w_out": attn_w_out, "cmp_pe": cmp_pe,
           "cmp_w1": cmp_w1, "cmp_w2": cmp_w2, "conv_w_pw1": conv_w_pw1, "conv_w_dw": conv_w_dw,
           "conv_b_dw": conv_b_dw, "conv_ln_g": conv_ln_g, "conv_ln_b": conv_ln_b, "conv_w_pw2": conv_w_pw2,
           "final_norm_g": final_norm_g}
    depth, d, _ = w_mod.shape
    bp, tp = x_prompt.shape[:2]
    bs, ts = x_sample.shape[:2]
    past = page_table.shape[1] * cache_cmp_kv.shape[2]

    n_req = bp + bs
    r_pad = -(-n_req // 8) * 8
    c_all = jnp.pad(jnp.concatenate([c_prompt, c_sample], axis=0), ((0, r_pad - n_req), (0, 0)))
    mod = _mod_vectors(c_all, w_mod, b_mod).reshape(depth, r_pad, N_MOD, d)

    pos_p = jnp.arange(tp, dtype=jnp.int32)
    pos_s = past + jnp.arange(ts, dtype=jnp.int32)
    y_p, p_cmp, p_sel, p_win, p_conv = _run_trunk(x_prompt, mod[:, :bp], pos_p, prm, None, PROMPT_TILES)
    y_s, s_cmp, s_sel, s_win, s_conv = _run_trunk(
        x_sample, mod[:, bp:n_req], pos_s, prm,
        (cache_cmp_kv, cache_sel_kv, cache_win_kv, state_conv, page_table), SAMPLE_TILES)
    return (y_p, y_s, p_cmp, p_sel, p_win, p_conv, s_cmp, s_sel, s_win, s_conv)
```

```python
import functools

import jax
import jax.numpy as jnp
from jax import lax
from jax.experimental import pallas as pl
from jax.experimental.pallas import tpu as pltpu

F32 = jnp.float32
MXU_DTYPE = jnp.bfloat16
VMEM_LIMIT_BYTES = 56 * 1024 * 1024
LANES = 128
SUBLANES = 8

N_HEADS = 16
HEAD_DIM = 128
N_KV_HEADS = 4
GROUP_SIZE = N_HEADS // N_KV_HEADS
CMP_BLOCK = 32
SEL_BLOCK = 64
SEL_RATIO = SEL_BLOCK // CMP_BLOCK
N_SELECT = 16
WINDOW = 512
ROPE_THETA = 10000.0
N_MOD = 9
RMS_EPS = 1e-6
LN_EPS = 1e-5
NEG_INF = -1e30
FORCED_SCORE = 1e9
KNOCKED_OUT = -3e38
ATTN_SCALE = HEAD_DIM ** -0.5
Q_WIDTH = N_HEADS * HEAD_DIM
KV_WIDTH = 2 * N_KV_HEADS * HEAD_DIM
GATE_COLS = GROUP_SIZE * 3
PAGES_PER_STEP = 16
DWCONV_COLS = 128
ROW_BLOCK = 32
EXP2_SCALE = ATTN_SCALE * 1.4426950408889634


def _params(*sem):
    return pltpu.CompilerParams(dimension_semantics=sem, vmem_limit_bytes=VMEM_LIMIT_BYTES)


def _mxu(a, b):
    return jnp.dot(a.astype(MXU_DTYPE), b.astype(MXU_DTYPE), preferred_element_type=F32)


def _mxu_nt(a, b):
    return lax.dot_general(a.astype(MXU_DTYPE), b.astype(MXU_DTYPE),
                           (((1,), (1,)), ((), ())), preferred_element_type=F32)


def _silu(x):
    return x * jax.nn.sigmoid(x)


def _softmax_rows(s):
    e = jnp.exp(s - jnp.max(s, axis=-1, keepdims=True))
    return e / jnp.sum(e, axis=-1, keepdims=True)


def _softmax_scaled(s):
    e = jnp.exp2((s - jnp.max(s, axis=-1, keepdims=True)) * EXP2_SCALE)
    return e / jnp.sum(e, axis=-1, keepdims=True)


def _mod_spec(tm, rows_per_mod, d, grid_rank):
    if rows_per_mod >= tm:
        assert rows_per_mod % tm == 0
        block, idx = (1, 1, d), (lambda i: (i * tm) // rows_per_mod)
    else:
        assert rows_per_mod == 1
        block, idx = (tm, 1, d), (lambda i: i)
    if grid_rank == 1:
        return pl.BlockSpec(block, lambda i: (idx(i), 0, 0))
    return pl.BlockSpec(block, lambda i, j: (idx(i), 0, 0))


def _modulated(x_ref, ng_ref, sh_ref, sc_ref):
    x = x_ref[...]
    y = x * lax.rsqrt(jnp.mean(x * x, axis=-1, keepdims=True) + RMS_EPS) * ng_ref[...]
    return y * (1.0 + sc_ref[:, 0, :]) + sh_ref[:, 0, :]


def _mod_kernel(c_ref, w_ref, b_ref, o_ref):
    o_ref[...] = _mxu(_silu(c_ref[...]), w_ref[...]) + b_ref[...]


def _mod_vectors(c_all, w_mod, b_mod, tn=1024):
    depth, d, n = w_mod.shape
    r = c_all.shape[0]
    return pl.pallas_call(
        _mod_kernel,
        grid=(depth, n // tn),
        in_specs=[pl.BlockSpec((r, d), lambda l, j: (0, 0)),
                  pl.BlockSpec((None, d, tn), lambda l, j: (l, 0, j)),
                  pl.BlockSpec((None, 1, tn), lambda l, j: (l, 0, j))],
        out_specs=pl.BlockSpec((None, r, tn), lambda l, j: (l, 0, j)),
        out_shape=jax.ShapeDtypeStruct((depth, r, n), F32),
        compiler_params=_params("arbitrary", "arbitrary"),
        name="mod_vectors",
    )(c_all, w_mod, b_mod.reshape(depth, 1, n))


def _ffn_kernel(x_ref, ng_ref, sh_ref, sc_ref, gt_ref, wg_ref, wu_ref, wd_ref, o_ref, h_scr):
    f = pl.program_id(1)

    @pl.when(f == 0)
    def _():
        h_scr[...] = _modulated(x_ref, ng_ref, sh_ref, sc_ref).astype(h_scr.dtype)

    h = h_scr[...]
    g = _mxu(h, wg_ref[...])
    u = _mxu(h, wu_ref[...])
    a = jnp.concatenate([(_silu(g[r0:r0 + ROW_BLOCK]) * u[r0:r0 + ROW_BLOCK]).astype(MXU_DTYPE)
                         for r0 in range(0, g.shape[0], ROW_BLOCK)], axis=0)
    y = _mxu(a, wd_ref[...])

    @pl.when(f == 0)
    def _():
        o_ref[...] = y

    @pl.when(f > 0)
    def _():
        o_ref[...] += y

    @pl.when(f == pl.num_programs(1) - 1)
    def _():
        o_ref[...] = x_ref[...] + (0.5 * gt_ref[:, 0, :]) * o_ref[...]


def _ffn(x, ng, shift, scale, gate, w_gate, w_up, w_down, *, layer, which, rows_per_mod, tm, tf):
    m, d = x.shape
    dff = w_gate.shape[-1]
    mod = _mod_spec(tm, rows_per_mod, d, 2)
    return pl.pallas_call(
        _ffn_kernel,
        grid=(m // tm, dff // tf),
        in_specs=[pl.BlockSpec((tm, d), lambda i, f: (i, 0), pipeline_mode=pl.Buffered(1)),
                  pl.BlockSpec((1, d), lambda i, f: (0, 0)),
                  mod, mod, mod,
                  pl.BlockSpec((None, None, d, tf), lambda i, f: (layer, which, 0, f)),
                  pl.BlockSpec((None, None, d, tf), lambda i, f: (layer, which, 0, f)),
                  pl.BlockSpec((None, None, tf, d), lambda i, f: (layer, which, f, 0))],
        out_specs=pl.BlockSpec((tm, d), lambda i, f: (i, 0)),
        out_shape=jax.ShapeDtypeStruct((m, d), F32),
        scratch_shapes=[pltpu.VMEM((tm, d), MXU_DTYPE)],
        compiler_params=_params("arbitrary", "arbitrary"),
        name="ffn",
    )(x, ng, shift, scale, gate, w_gate, w_up, w_down)


def _rope_chunk(z, cos, sin):
    return z * cos + pltpu.roll(z, HEAD_DIM // 2, 1) * sin


def _proj_kernel(x_ref, ng_ref, sh_ref, sc_ref, w_ref, cos_ref, sin_ref, *rest, mode):
    out_refs, h_scr = rest[:-1], rest[-1]

    @pl.when(pl.program_id(1) == 0)
    def _():
        h_scr[...] = _modulated(x_ref, ng_ref, sh_ref, sc_ref).astype(h_scr.dtype)

    z = _mxu(h_scr[...], w_ref[...])
    if mode == "sigmoid":
        out_refs[0][...] = jax.nn.sigmoid(z)
        return
    cos, sin = cos_ref[...], sin_ref[...]
    n_chunks = z.shape[1] // HEAD_DIM
    for c in range(n_chunks):
        zc = z[:, c * HEAD_DIM:(c + 1) * HEAD_DIM]
        if mode == "q" or c % 2 == 0:
            zc = _rope_chunk(zc, cos, sin)
        for o in out_refs:
            o[:, c * HEAD_DIM:(c + 1) * HEAD_DIM] = zc.astype(o.dtype)


def _proj(x, ng, shift, scale, w, cos, sin, *, layer, mode, col0, n, out_dtypes, rows_per_mod, rows_per_seq,
          tm, tn):
    m, d = x.shape
    mod = _mod_spec(tm, rows_per_mod, d, 2)
    seq_tiles = rows_per_seq // tm
    assert col0 % tn == 0 and n % tn == 0 and rows_per_seq % tm == 0
    if mode == "kv":
        per = KV_WIDTH // tn
        out_specs = [pl.BlockSpec((None, tm, tn), lambda i, j: (j // per, i, j % per)) for _ in out_dtypes]
        out_shape = [jax.ShapeDtypeStruct((n // KV_WIDTH, m, KV_WIDTH), dt) for dt in out_dtypes]
    else:
        out_specs = [pl.BlockSpec((tm, tn), lambda i, j: (i, j)) for _ in out_dtypes]
        out_shape = [jax.ShapeDtypeStruct((m, n), dt) for dt in out_dtypes]
    return pl.pallas_call(
        functools.partial(_proj_kernel, mode=mode),
        grid=(m // tm, n // tn),
        in_specs=[pl.BlockSpec((tm, d), lambda i, j: (i, 0)),
                  pl.BlockSpec((1, d), lambda i, j: (0, 0)),
                  mod, mod,
                  pl.BlockSpec((None, d, tn), lambda i, j: (layer, 0, col0 // tn + j)),
                  pl.BlockSpec((tm, HEAD_DIM), lambda i, j: (i % seq_tiles, 0)),
                  pl.BlockSpec((tm, HEAD_DIM), lambda i, j: (i % seq_tiles, 0))],
        out_specs=out_specs,
        out_shape=out_shape,
        scratch_shapes=[pltpu.VMEM((tm, d), MXU_DTYPE)],
        compiler_params=_params("arbitrary", "arbitrary"),
        name="proj_" + mode,
    )(x, ng, shift, scale, w, cos, sin)


def _glu_kernel(x_ref, ng_ref, sh_ref, sc_ref, wa_ref, wb_ref, o_ref, h_scr):
    @pl.when(pl.program_id(1) == 0)
    def _():
        h_scr[...] = _modulated(x_ref, ng_ref, sh_ref, sc_ref).astype(h_scr.dtype)

    h = h_scr[...]
    o_ref[...] = _mxu(h, wa_ref[...]) * jax.nn.sigmoid(_mxu(h, wb_ref[...]))


def _glu_proj(x, ng, shift, scale, w_pw1, *, layer, rows_per_mod, tm, tn):
    m, d = x.shape
    dc = w_pw1.shape[-1] // 2
    mod = _mod_spec(tm, rows_per_mod, d, 2)
    return pl.pallas_call(
        _glu_kernel,
        grid=(m // tm, dc // tn),
        in_specs=[pl.BlockSpec((tm, d), lambda i, j: (i, 0)),
                  pl.BlockSpec((1, d), lambda i, j: (0, 0)),
                  mod, mod,
                  pl.BlockSpec((None, d, tn), lambda i, j: (layer, 0, j)),
                  pl.BlockSpec((None, d, tn), lambda i, j: (layer, 0, dc // tn + j))],
        out_specs=pl.BlockSpec((tm, tn), lambda i, j: (i, j)),
        out_shape=jax.ShapeDtypeStruct((m, dc), F32),
        scratch_shapes=[pltpu.VMEM((tm, d), MXU_DTYPE)],
        compiler_params=_params("arbitrary", "arbitrary"),
        name="glu_proj",
    )(x, ng, shift, scale, w_pw1, w_pw1)


def _linres_kernel(a_ref, w_ref, x_ref, gt_ref, o_ref):
    o_ref[...] = x_ref[...] + gt_ref[:, 0, :] * _mxu(a_ref[...], w_ref[...])


def _linres(a, w, x, gate, *, layer, rows_per_mod, tm, tn):
    m, k = a.shape
    d = x.shape[1]
    if rows_per_mod >= tm:
        gspec = pl.BlockSpec((1, 1, tn), lambda i, j: ((i * tm) // rows_per_mod, 0, j))
    else:
        gspec = pl.BlockSpec((tm, 1, tn), lambda i, j: (i, 0, j))
    return pl.pallas_call(
        _linres_kernel,
        grid=(m // tm, d // tn),
        in_specs=[pl.BlockSpec((tm, k), lambda i, j: (i, 0)),
                  pl.BlockSpec((None, k, tn), lambda i, j: (layer, 0, j)),
                  pl.BlockSpec((tm, tn), lambda i, j: (i, j)),
                  gspec],
        out_specs=pl.BlockSpec((tm, tn), lambda i, j: (i, j)),
        out_shape=jax.ShapeDtypeStruct((m, d), F32),
        compiler_params=_params("arbitrary", "arbitrary"),
        name="linres",
    )(a, w, x, gate)


def _rms_kernel(x_ref, g_ref, o_ref):
    x = x_ref[...]
    o_ref[...] = x * lax.rsqrt(jnp.mean(x * x, axis=-1, keepdims=True) + RMS_EPS) * g_ref[...]


def _rms_norm(x, g, tm):
    m, d = x.shape
    return pl.pallas_call(
        _rms_kernel,
        grid=(m // tm,),
        in_specs=[pl.BlockSpec((tm, d), lambda i: (i, 0)), pl.BlockSpec((1, d), lambda i: (0, 0))],
        out_specs=pl.BlockSpec((tm, d), lambda i: (i, 0)),
        out_shape=jax.ShapeDtypeStruct((m, d), F32),
        compiler_params=_params("arbitrary"),
        name="final_norm",
    )(x, g.reshape(1, d))


def _compress_tail(acc, w2_ref, o_ref):
    o_ref[...] = _mxu(_silu(acc), w2_ref[...]).astype(o_ref.dtype)


def _compress_kernel(x_ref, pe_ref, w1_ref, w2_ref, o_ref, *, nb):
    acc = jnp.zeros((nb, HEAD_DIM), F32)
    for c in range(CMP_BLOCK):
        xc = x_ref[pl.ds(c, nb, stride=CMP_BLOCK), :] + pe_ref[c:c + 1, :]
        acc += _mxu(xc, w1_ref[c * HEAD_DIM:(c + 1) * HEAD_DIM, :])
    _compress_tail(acc, w2_ref, o_ref)


def _compress_rows(rows, pe, w1, w2, *, layer, nb):
    n_rows = rows.shape[0]
    n_blocks = n_rows // CMP_BLOCK
    n_chunks = KV_WIDTH // HEAD_DIM
    return pl.pallas_call(
        functools.partial(_compress_kernel, nb=nb),
        grid=(n_blocks // nb, n_chunks),
        in_specs=[pl.BlockSpec((nb * CMP_BLOCK, HEAD_DIM), lambda i, ch: (i, ch)),
                  pl.BlockSpec((None, None, CMP_BLOCK, HEAD_DIM), lambda i, ch: (layer, ch % 2, 0, 0)),
                  pl.BlockSpec((None, None, CMP_BLOCK * HEAD_DIM, HEAD_DIM), lambda i, ch: (layer, ch % 2, 0, 0)),
                  pl.BlockSpec((None, None, HEAD_DIM, HEAD_DIM), lambda i, ch: (layer, ch % 2, 0, 0))],
        out_specs=pl.BlockSpec((nb, HEAD_DIM), lambda i, ch: (i, ch)),
        out_shape=jax.ShapeDtypeStruct((n_blocks, KV_WIDTH), MXU_DTYPE),
        compiler_params=_params("arbitrary", "arbitrary"),
        name="compress_rows",
    )(rows, pe, w1, w2)


def _compress_pages_kernel(pt_ref, *refs, blocks_per_page):
    del pt_ref
    pages = refs[:PAGES_PER_STEP]
    pe_ref, w1_ref, w2_ref, o_ref, t_scr = refs[PAGES_PER_STEP:]
    nb = PAGES_PER_STEP * blocks_per_page
    pages_per_group = SUBLANES // blocks_per_page
    n_groups = PAGES_PER_STEP // pages_per_group
    rows = SUBLANES * CMP_BLOCK
    out_row = lax.broadcasted_iota(jnp.int32, (rows, rows), 0)
    in_row = lax.broadcasted_iota(jnp.int32, (rows, rows), 1)
    perm = jnp.where(in_row == (out_row % SUBLANES) * CMP_BLOCK + out_row // SUBLANES, 1.0, 0.0)
    pe_rows = jnp.concatenate([pe_ref[kv] for _ in range(N_KV_HEADS) for kv in range(2)], axis=1)
    pe_rows = jnp.concatenate([pe_rows] * SUBLANES, axis=0)
    n_chunks = pages[0].shape[1]
    for gp in range(n_groups):
        x = jnp.concatenate(
            [jnp.concatenate([pages[gp * pages_per_group + k][:, ch, :] for ch in range(n_chunks)], axis=1)
             for k in range(pages_per_group)], axis=0)
        t_scr[gp] = _mxu(perm, x + pe_rows)
    for kv in range(2):
        acc = jnp.zeros((N_KV_HEADS * nb, HEAD_DIM), F32)
        for c in range(CMP_BLOCK):
            xc = jnp.concatenate(
                [t_scr[:, c * SUBLANES:(c + 1) * SUBLANES,
                       (g * 2 + kv) * HEAD_DIM:(g * 2 + kv + 1) * HEAD_DIM].reshape(nb, HEAD_DIM)
                 for g in range(N_KV_HEADS)], axis=0)
            acc += _mxu(xc, w1_ref[kv, c * HEAD_DIM:(c + 1) * HEAD_DIM, :])
        y = _mxu(_silu(acc), w2_ref[kv])
        for g in range(N_KV_HEADS):
            o_ref[:, (g * 2 + kv) * HEAD_DIM:(g * 2 + kv + 1) * HEAD_DIM] = (
                y[g * nb:(g + 1) * nb].astype(o_ref.dtype))


def _compress_pages(pool, page_table, pe, w1, w2, *, layer):
    _, _, page, n_chunks, _ = pool.shape
    width = n_chunks * HEAD_DIM
    b, n_pages = page_table.shape
    bpp = page // CMP_BLOCK
    nb = PAGES_PER_STEP * bpp
    assert n_pages % PAGES_PER_STEP == 0 and page % CMP_BLOCK == 0 and SUBLANES % bpp == 0
    n_groups = nb // SUBLANES

    def page_spec(p):
        return pl.BlockSpec((None, None, page, n_chunks, HEAD_DIM),
                            lambda bi, gi, pt: (layer, pt[bi * n_pages + gi * PAGES_PER_STEP + p], 0, 0, 0))

    grid_spec = pltpu.PrefetchScalarGridSpec(
        num_scalar_prefetch=1,
        grid=(b, n_pages // PAGES_PER_STEP),
        in_specs=[page_spec(p) for p in range(PAGES_PER_STEP)] + [
            pl.BlockSpec((None, 2, CMP_BLOCK, HEAD_DIM), lambda bi, gi, pt: (layer, 0, 0, 0)),
            pl.BlockSpec((None, 2, CMP_BLOCK * HEAD_DIM, HEAD_DIM), lambda bi, gi, pt: (layer, 0, 0, 0)),
            pl.BlockSpec((None, 2, HEAD_DIM, HEAD_DIM), lambda bi, gi, pt: (layer, 0, 0, 0))],
        out_specs=pl.BlockSpec((None, nb, width), lambda bi, gi, pt: (bi, gi, 0)),
        scratch_shapes=[pltpu.VMEM((n_groups, SUBLANES * CMP_BLOCK, width), F32)],
    )
    return pl.pallas_call(
        functools.partial(_compress_pages_kernel, blocks_per_page=bpp),
        grid_spec=grid_spec,
        out_shape=jax.ShapeDtypeStruct((b, n_pages * bpp, width), MXU_DTYPE),
        compiler_params=_params("arbitrary", "arbitrary"),
        name="compress_pages",
    )(page_table.reshape(-1), *([pool] * PAGES_PER_STEP), pe, w1, w2)


def _pair_sums(imp):
    out = []
    for k in range(imp.shape[1] // LANES):
        x = imp[:, k * LANES:(k + 1) * LANES]
        even = lax.broadcasted_iota(jnp.int32, x.shape, 1) % 2 == 0
        out.append(x + jnp.where(even, pltpu.roll(x, LANES - 1, 1), pltpu.roll(x, 1, 1)))
    return out[0] if len(out) == 1 else jnp.concatenate(out, axis=1)


def _block_scores(imp, q_pos, n_sel_blocks):
    lane = lax.broadcasted_iota(jnp.int32, imp.shape, 1)
    blk = lane // SEL_RATIO
    cur = q_pos // SEL_BLOCK
    forced = (blk == 0) | (blk == cur) | (blk == cur - 1)
    valid = blk * SEL_BLOCK <= q_pos
    score = jnp.where(forced, FORCED_SCORE, jnp.where(valid, _pair_sums(imp), -1.0))
    eligible = (lane % SEL_RATIO == 0) & (blk < n_sel_blocks)
    return jnp.where(eligible, score, KNOCKED_OUT)


def _take_top(work):
    lane = lax.broadcasted_iota(jnp.int32, work.shape, 1).astype(F32)
    top = jnp.max(work, axis=-1, keepdims=True)
    pick = jnp.min(jnp.where(work == top, lane, float(work.shape[1])), axis=-1, keepdims=True)
    return lane == pick, pick


def _attn_prompt_kernel(q_ref, ck_ref, cv_ref, ks_ref, vs_ref, kw_ref, vw_ref, g_ref, o_ref, *, tq, tk, seq):
    i = pl.program_id(2)
    n_cmp = ck_ref.shape[0]
    n_sel_blocks = -(-seq // SEL_BLOCK)
    n_take = min(N_SELECT, n_sel_blocks)
    assert SEL_RATIO == 2 and n_cmp * CMP_BLOCK == seq and n_cmp <= LANES
    q = q_ref[...]
    q4 = jnp.concatenate([q[:, h * HEAD_DIM:(h + 1) * HEAD_DIM] for h in range(GROUP_SIZE)], axis=0)
    q_pos = i * tq + lax.broadcasted_iota(jnp.int32, (tq, 1), 0)

    pad = jnp.zeros((LANES - n_cmp, HEAD_DIM), ck_ref.dtype)
    ck = jnp.concatenate([ck_ref[...], pad], axis=0) if n_cmp < LANES else ck_ref[...]
    cv = jnp.concatenate([cv_ref[...], pad], axis=0) if n_cmp < LANES else cv_ref[...]
    lane = lax.broadcasted_iota(jnp.int32, (1, LANES), 1)
    vis = ((lane + 1) * CMP_BLOCK - 1 <= q_pos) & (lane < n_cmp)
    s = _mxu_nt(q4, ck).reshape(GROUP_SIZE, tq, LANES)
    p = jnp.where(vis[None], _softmax_scaled(jnp.where(vis[None], s, NEG_INF)), 0.0)
    o_c = _mxu(p.reshape(GROUP_SIZE * tq, LANES), cv).reshape(GROUP_SIZE, tq, HEAD_DIM)
    imp = jnp.sum(p, axis=0)

    def top_blocks():
        work = _block_scores(imp, q_pos, n_sel_blocks)
        sel = jnp.zeros((tq, LANES), F32)
        for _ in range(n_take):
            hit, _ = _take_top(work)
            work = jnp.where(hit, KNOCKED_OUT, work)
            sel = jnp.where(hit, 1.0, sel)
        return sel

    def all_blocks():
        return jnp.where((lane % SEL_RATIO == 0) & (lane // SEL_RATIO < n_sel_blocks), 1.0,
                         jnp.zeros((tq, LANES), F32))

    sel = lax.cond(((i + 1) * tq - 1) // SEL_BLOCK + 1 <= n_take, all_blocks, top_blocks).astype(MXU_DTYPE)

    def sel_chunk(kc, carry):
        m_i, l_i, acc = carry
        start = pl.multiple_of(kc * tk, tk)
        k_pos = start + lax.broadcasted_iota(jnp.int32, (1, tk), 1)
        expand = lax.broadcasted_iota(jnp.int32, (LANES, 1), 0) == SEL_RATIO * (k_pos // SEL_BLOCK)
        picked = jnp.dot(sel, jnp.where(expand, 1.0, 0.0).astype(MXU_DTYPE), preferred_element_type=F32)
        bias = jnp.where((picked > 0.5) & (k_pos <= q_pos), 0.0, NEG_INF)[None]
        sc = _mxu_nt(q4, ks_ref[pl.ds(start, tk), :]).reshape(GROUP_SIZE, tq, tk) + bias
        m_new = jnp.maximum(m_i, jnp.max(sc, axis=-1, keepdims=True))
        alpha = jnp.exp2((m_i - m_new) * EXP2_SCALE)
        e = jnp.exp2((sc - m_new) * EXP2_SCALE)
        l_new = alpha * l_i + jnp.sum(e, axis=-1, keepdims=True)
        pv = _mxu(e.reshape(GROUP_SIZE * tq, tk), vs_ref[pl.ds(start, tk), :])
        return m_new, l_new, alpha * acc + pv.reshape(GROUP_SIZE, tq, HEAD_DIM)

    init = (jnp.full((GROUP_SIZE, tq, 1), NEG_INF, F32), jnp.zeros((GROUP_SIZE, tq, 1), F32),
            jnp.zeros((GROUP_SIZE, tq, HEAD_DIM), F32))
    _, l_s, acc_s = lax.fori_loop(0, ((i + 1) * tq + tk - 1) // tk, sel_chunk, init)
    o_s = acc_s / l_s

    span = tq + WINDOW
    w_start = pl.multiple_of(jnp.maximum(i * tq - WINDOW, 0), tq)
    dist = q_pos - (w_start + lax.broadcasted_iota(jnp.int32, (1, span), 1))
    bias = jnp.where((dist >= 0) & (dist < WINDOW), 0.0, NEG_INF)[None]
    pw = _softmax_scaled(_mxu_nt(q4, kw_ref[pl.ds(w_start, span), :]).reshape(GROUP_SIZE, tq, span) + bias)
    o_w = _mxu(pw.reshape(GROUP_SIZE * tq, span), vw_ref[pl.ds(w_start, span), :]).reshape(GROUP_SIZE, tq, HEAD_DIM)

    gates = g_ref[...]
    for h in range(GROUP_SIZE):
        o = (gates[:, 3 * h:3 * h + 1] * o_c[h] + gates[:, 3 * h + 1:3 * h + 2] * o_s[h]
             + gates[:, 3 * h + 2:3 * h + 3] * o_w[h])
        o_ref[:, h * HEAD_DIM:(h + 1) * HEAD_DIM] = o.astype(o_ref.dtype)


def _attn_prompt(q, ckv, kv, gates, *, tq, tk):
    b, t, _ = q.shape
    n_cmp = ckv.shape[1]
    assert t % tq == 0 and t % tk == 0 and t >= tq + WINDOW and WINDOW % tq == 0 and tq % ROW_BLOCK == 0
    gw = GROUP_SIZE * HEAD_DIM

    def kv_spec(branch, part):
        return pl.BlockSpec((None, None, t, HEAD_DIM), lambda bi, g, i: (branch, bi, 0, 2 * g + part))

    return pl.pallas_call(
        functools.partial(_attn_prompt_kernel, tq=tq, tk=tk, seq=t),
        grid=(b, N_KV_HEADS, t // tq),
        in_specs=[pl.BlockSpec((None, tq, gw), lambda bi, g, i: (bi, i, g)),
                  pl.BlockSpec((None, n_cmp, HEAD_DIM), lambda bi, g, i: (bi, 0, 2 * g)),
                  pl.BlockSpec((None, n_cmp, HEAD_DIM), lambda bi, g, i: (bi, 0, 2 * g + 1)),
                  kv_spec(1, 0), kv_spec(1, 1), kv_spec(2, 0), kv_spec(2, 1),
                  pl.BlockSpec((None, tq, LANES), lambda bi, g, i: (bi, i, g))],
        out_specs=pl.BlockSpec((None, tq, gw), lambda bi, g, i: (bi, i, g)),
        out_shape=jax.ShapeDtypeStruct((b, t, Q_WIDTH), MXU_DTYPE),
        compiler_params=_params("arbitrary", "arbitrary", "arbitrary"),
        name="attn_prompt",
    )(q, ckv, ckv, kv, kv, kv, kv, gates)


def _attn_cmp_sample_kernel(q_ref, ckv_ref, o_ref, idx_ref, *, past, n_total, width):
    t_rows = q_ref.shape[0]
    n_cmp = ckv_ref.shape[0]
    n_sel_blocks = -(-n_total // SEL_BLOCK)
    n_take = min(N_SELECT, n_sel_blocks)
    assert SEL_RATIO == 2 and n_take <= LANES and SEL_RATIO * n_sel_blocks <= width
    q_pos = past + lax.broadcasted_iota(jnp.int32, (t_rows, 1), 0)
    lane = lax.broadcasted_iota(jnp.int32, (1, width), 1)
    vis = ((lane + 1) * CMP_BLOCK - 1 <= q_pos) & (lane < n_cmp)
    out_lane = lax.broadcasted_iota(jnp.int32, (1, LANES), 1)
    pad = jnp.zeros((width - n_cmp, HEAD_DIM), ckv_ref.dtype)
    for g in range(N_KV_HEADS):
        ck = jnp.concatenate([ckv_ref[:, 2 * g * HEAD_DIM:(2 * g + 1) * HEAD_DIM], pad], axis=0)
        cv = jnp.concatenate([ckv_ref[:, (2 * g + 1) * HEAD_DIM:(2 * g + 2) * HEAD_DIM], pad], axis=0)
        imp = jnp.zeros((t_rows, width), F32)
        for h in range(GROUP_SIZE):
            cols = slice((g * GROUP_SIZE + h) * HEAD_DIM, (g * GROUP_SIZE + h + 1) * HEAD_DIM)
            s = _mxu_nt(q_ref[:, cols], ck) * ATTN_SCALE
            p = jnp.where(vis, _softmax_rows(jnp.where(vis, s, NEG_INF)), 0.0)
            o_ref[:, cols] = _mxu(p, cv)
            imp = imp + p
        work = _block_scores(imp, q_pos, n_sel_blocks)
        picks = jnp.zeros((t_rows, LANES), F32)
        for r in range(n_take):
            hit, pick = _take_top(work)
            work = jnp.where(hit, KNOCKED_OUT, work)
            picks = jnp.where(out_lane == r, pick, picks)
        idx_ref[g] = (picks * (1.0 / SEL_RATIO)).astype(jnp.int32)


def _attn_cmp_sample(q, ckv, *, past, n_total):
    b, t, qw = q.shape
    n_cmp = ckv.shape[1]
    width = -(-max(n_cmp, SEL_RATIO * -(-n_total // SEL_BLOCK)) // LANES) * LANES
    return pl.pallas_call(
        functools.partial(_attn_cmp_sample_kernel, past=past, n_total=n_total, width=width),
        grid=(b,),
        in_specs=[pl.BlockSpec((None, t, qw), lambda bi: (bi, 0, 0)),
                  pl.BlockSpec((None, n_cmp, KV_WIDTH), lambda bi: (bi, 0, 0))],
        out_specs=[pl.BlockSpec((None, t, qw), lambda bi: (bi, 0, 0)),
                   pl.BlockSpec((None, N_KV_HEADS, t, LANES), lambda bi: (bi, 0, 0, 0))],
        out_shape=[jax.ShapeDtypeStruct(q.shape, F32),
                   jax.ShapeDtypeStruct((b, N_KV_HEADS, t, LANES), jnp.int32)],
        compiler_params=_params("arbitrary"),
        name="attn_cmp_sample",
    )(q, ckv)


def _attn_win_sample_kernel(q_ref, buf_ref, new_ref, o_ref, *, t_new):
    t_rows = q_ref.shape[0]
    n_buf = buf_ref.shape[0]
    t_idx = lax.broadcasted_iota(jnp.int32, (t_rows, 1), 0)
    dist_a = t_idx + n_buf - lax.broadcasted_iota(jnp.int32, (1, n_buf), 1)
    row_b = lax.broadcasted_iota(jnp.int32, (1, new_ref.shape[0]), 1)
    mask_a = (dist_a >= 0) & (dist_a < WINDOW)
    mask_b = (row_b <= t_idx) & (t_idx - row_b < WINDOW) & (row_b < t_new)
    for g in range(N_KV_HEADS):
        k_a, v_a = buf_ref[:, g, 0, :], buf_ref[:, g, 1, :]
        k_b = new_ref[:, 2 * g * HEAD_DIM:(2 * g + 1) * HEAD_DIM]
        v_b = new_ref[:, (2 * g + 1) * HEAD_DIM:(2 * g + 2) * HEAD_DIM]
        for h in range(GROUP_SIZE):
            cols = slice((g * GROUP_SIZE + h) * HEAD_DIM, (g * GROUP_SIZE + h + 1) * HEAD_DIM)
            q = q_ref[:, cols]
            s_a = jnp.where(mask_a, _mxu_nt(q, k_a) * ATTN_SCALE, NEG_INF)
            s_b = jnp.where(mask_b, _mxu_nt(q, k_b) * ATTN_SCALE, NEG_INF)
            m = jnp.maximum(jnp.max(s_a, axis=-1, keepdims=True), jnp.max(s_b, axis=-1, keepdims=True))
            e_a, e_b = jnp.exp(s_a - m), jnp.exp(s_b - m)
            l = jnp.sum(e_a, axis=-1, keepdims=True) + jnp.sum(e_b, axis=-1, keepdims=True)
            o_ref[:, cols] = _mxu(e_a / l, v_a) + _mxu(e_b / l, v_b)


def _attn_win_sample(q, win_buf, kv_new, *, layer, t_new):
    b, t, qw = q.shape
    n_buf = win_buf.shape[2]
    return pl.pallas_call(
        functools.partial(_attn_win_sample_kernel, t_new=t_new),
        grid=(b,),
        in_specs=[pl.BlockSpec((None, t, qw), lambda bi: (bi, 0, 0)),
                  pl.BlockSpec((None, None, n_buf, N_KV_HEADS, 2, HEAD_DIM), lambda bi: (layer, bi, 0, 0, 0, 0)),
                  pl.BlockSpec((None, kv_new.shape[1], KV_WIDTH), lambda bi: (bi, 0, 0))],
        out_specs=pl.BlockSpec((None, t, qw), lambda bi: (bi, 0, 0)),
        out_shape=jax.ShapeDtypeStruct(q.shape, F32),
        compiler_params=_params("arbitrary"),
        name="attn_win_sample",
    )(q, win_buf, kv_new)


def _attn_sel_sample_kernel(idx_ref, pt_ref, q_ref, *refs, past, t_new, n_slots, t_steps):
    del pt_ref
    blocks = refs[:n_slots]
    new_ref, oc_ref, ow_ref, g_ref, o_ref = refs[n_slots:]
    bi, t, gi = pl.program_id(0), pl.program_id(1), pl.program_id(2)
    base = ((bi * t_steps + t) * N_KV_HEADS + gi) * n_slots
    n_past_blocks = past // SEL_BLOCK
    q_pos = past + t
    q = q_ref[...]

    keys = jnp.concatenate([blk[:, 0, :].astype(MXU_DTYPE) for blk in blocks], axis=0)
    vals = jnp.concatenate([blk[:, 1, :].astype(MXU_DTYPE) for blk in blocks], axis=0)
    lane = lax.broadcasted_iota(jnp.int32, (1, n_slots * SEL_BLOCK), 1)
    pos = lane % SEL_BLOCK
    has_new = jnp.int32(0)
    for n in range(n_slots):
        blk = idx_ref[base + n]
        start = jnp.where(blk < n_past_blocks, blk * SEL_BLOCK, q_pos + 1)
        pos = pos + jnp.where(lane // SEL_BLOCK == n, start, 0)
        has_new = has_new | (blk == n_past_blocks).astype(jnp.int32)
    mask_a = pos <= q_pos
    new = new_ref[...]
    row_b = lax.broadcasted_iota(jnp.int32, (1, new.shape[0]), 1)
    mask_b = (row_b <= t) & (row_b < t_new) & (has_new > 0)
    s_a = jnp.where(mask_a, _mxu_nt(q, keys) * ATTN_SCALE, NEG_INF)
    s_b = jnp.where(mask_b, _mxu_nt(q, new[:, :HEAD_DIM]) * ATTN_SCALE, NEG_INF)
    m = jnp.maximum(jnp.max(s_a, axis=-1, keepdims=True), jnp.max(s_b, axis=-1, keepdims=True))
    e_a, e_b = jnp.exp(s_a - m), jnp.exp(s_b - m)
    l = jnp.sum(e_a, axis=-1, keepdims=True) + jnp.sum(e_b, axis=-1, keepdims=True)
    o_s = _mxu(e_a / l, vals) + _mxu(e_b / l, new[:, HEAD_DIM:])
    gates = g_ref[...]
    o_ref[...] = gates[:, 0:1] * oc_ref[...] + gates[:, 1:2] * o_s + gates[:, 2:3] * ow_ref[...]


def _attn_sel_sample(idx, page_table, q, pool, kv_new, o_c, o_w, gates, *, layer, past, t_new):
    b, t, g, hg, hd = q.shape
    n_slots = idx.shape[-1]
    n_pages = page_table.shape[1]
    page = past // n_pages
    per_page = page // SEL_BLOCK
    n_past_blocks = past // SEL_BLOCK

    def slot_spec(n):
        def index(bi, ti, gi, idx_s, pt_s):
            blk = jnp.minimum(idx_s[((bi * t + ti) * g + gi) * n_slots + n], n_past_blocks - 1)
            phys = pt_s[bi * n_pages + blk // per_page]
            return (layer, phys, blk % per_page, gi, 0, 0)
        return pl.BlockSpec((None, None, SEL_BLOCK, None, 2, HEAD_DIM), index)

    head_spec = pl.BlockSpec((None, None, None, hg, hd), lambda bi, ti, gi, idx_s, pt_s: (bi, ti, gi, 0, 0))
    grid_spec = pltpu.PrefetchScalarGridSpec(
        num_scalar_prefetch=2,
        grid=(b, t, g),
        in_specs=[head_spec] + [slot_spec(n) for n in range(n_slots)] + [
            pl.BlockSpec((None, kv_new.shape[1], 2 * HEAD_DIM), lambda bi, ti, gi, idx_s, pt_s: (bi, 0, gi)),
            head_spec, head_spec,
            pl.BlockSpec((None, None, None, hg, 3), lambda bi, ti, gi, idx_s, pt_s: (bi, ti, gi, 0, 0))],
        out_specs=head_spec,
    )
    return pl.pallas_call(
        functools.partial(_attn_sel_sample_kernel, past=past, t_new=t_new, n_slots=n_slots, t_steps=t),
        grid_spec=grid_spec,
        out_shape=jax.ShapeDtypeStruct(q.shape, F32),
        compiler_params=_params("arbitrary", "arbitrary", "arbitrary"),
        name="attn_sel_sample",
    )(idx.reshape(-1), page_table.reshape(-1), q, *([pool] * n_slots), kv_new, o_c, o_w, gates)


def _dwconv_kernel(cur_ref, prev_ref, buf_ref, w_ref, b_ref, lg_ref, lb_ref, o_ref, full_scr, y_scr, *,
                   tt, halo, width):
    i = pl.program_id(1)
    lead = halo - (width - 1)
    full_scr[0:halo, :] = prev_ref[...]

    @pl.when(i == 0)
    def _():
        full_scr[0:halo, :] = buf_ref[...]

    full_scr[halo:halo + tt, :] = cur_ref[...]
    for c0 in range(0, cur_ref.shape[1], DWCONV_COLS):
        cols = slice(c0, c0 + DWCONV_COLS)
        acc = jnp.zeros((tt, DWCONV_COLS), F32) + b_ref[:, cols]
        for r in range(SUBLANES):
            offsets = [o for o in range(r, lead + width, SUBLANES) if o >= lead]
            if not offsets:
                continue
            shifted = full_scr[r:offsets[-1] + tt, cols]
            for o in offsets:
                acc = acc + shifted[o - r:o - r + tt] * w_ref[o - lead:o - lead + 1, cols]
        y_scr[:, cols] = acc
    y = y_scr[...]
    mu = jnp.mean(y, axis=-1, keepdims=True)
    var = jnp.mean(jnp.square(y - mu), axis=-1, keepdims=True)
    yn = (y - mu) * lax.rsqrt(var + LN_EPS) * lg_ref[...] + lb_ref[...]
    o_ref[...] = _silu(yn).astype(o_ref.dtype)


def _dwconv(glu, buf, w_dw, b_dw, ln_g, ln_b, *, layer, tt):
    b, t, d = glu.shape
    width = w_dw.shape[1]
    halo = buf.shape[1]
    assert tt % halo == 0 and t % tt == 0 and halo >= width - 1
    per = tt // halo

    def vec_spec():
        return pl.BlockSpec((None, 1, d), lambda bi, i: (layer, 0, 0))

    return pl.pallas_call(
        functools.partial(_dwconv_kernel, tt=tt, halo=halo, width=width),
        grid=(b, t // tt),
        in_specs=[pl.BlockSpec((None, tt, d), lambda bi, i: (bi, i, 0)),
                  pl.BlockSpec((None, halo, d), lambda bi, i: (bi, jnp.maximum(i * per - 1, 0), 0)),
                  pl.BlockSpec((None, halo, d), lambda bi, i: (bi, 0, 0)),
                  pl.BlockSpec((None, width, d), lambda bi, i: (layer, 0, 0)),
                  vec_spec(), vec_spec(), vec_spec()],
        out_specs=pl.BlockSpec((None, tt, d), lambda bi, i: (bi, i, 0)),
        out_shape=jax.ShapeDtypeStruct((b, t, d), MXU_DTYPE),
        scratch_shapes=[pltpu.VMEM((halo + tt, d), F32), pltpu.VMEM((tt, d), F32)],
        compiler_params=_params("arbitrary", "arbitrary"),
        name="dwconv",
    )(glu, glu, buf, w_dw, b_dw, ln_g, ln_b)


def _rope_tables(pos):
    half = HEAD_DIM // 2
    inv = ROPE_THETA ** (-jnp.arange(half, dtype=F32) / half)
    ang = pos.astype(F32)[:, None] * inv[None, :]
    cos, sin = jnp.cos(ang), jnp.sin(ang)
    return jnp.concatenate([cos, cos], axis=1), jnp.concatenate([-sin, sin], axis=1)


def _group_gate_weights(w_gate_cols):
    d = w_gate_cols.shape[0]
    wg = w_gate_cols.reshape(d, N_KV_HEADS, GATE_COLS)
    return jnp.pad(wg, ((0, 0), (0, 0), (0, LANES - GATE_COLS))).reshape(d, N_KV_HEADS * LANES)


def _run_trunk(x, mod, pos, prm, caches, tiles):
    b, t, d = x.shape
    m = b * t
    depth = prm["w_mod"].shape[0]
    tm = tiles["tm"]
    rpm = t if caches is None else 1
    cos, sin = _rope_tables(pos)
    if caches is not None:
        cos, sin = jnp.tile(cos, (b, 1)), jnp.tile(sin, (b, 1))
    rows_per_seq = t if caches is None else m
    x = x.reshape(m, d)
    new_c, new_s, new_w, new_conv = [], [], [], []

    def mod_rows(layer, k):
        v = mod[layer, :, k, :]
        if caches is not None:
            v = jnp.repeat(v, t, axis=0)
        return v[:, None, :]

    def ffn(x, layer, which, k0):
        return _ffn(x, prm["norm_g"][layer, which * 2].reshape(1, d), mod_rows(layer, k0), mod_rows(layer, k0 + 1),
                    mod_rows(layer, k0 + 2), prm["ffn_w_gate"], prm["ffn_w_up"], prm["ffn_w_down"],
                    layer=layer, which=which, rows_per_mod=rpm, tm=tiles["tm_ffn"], tf=tiles["tf"])

    for i in range(depth):
        x = ffn(x, i, 0, 0)
        ng = prm["norm_g"][i, 1].reshape(1, d)
        shift, scale, gate = mod_rows(i, 3), mod_rows(i, 4), mod_rows(i, 5)
        a = i // 2
        if i % 2 == 0:
            w_in = prm["attn_w_in"]
            common = dict(rows_per_mod=rpm, rows_per_seq=rows_per_seq, tm=tm)
            q_dtype = MXU_DTYPE if caches is None else F32
            q, = _proj(x, ng, shift, scale, w_in, cos, sin, layer=a, mode="q", col0=0, n=Q_WIDTH,
                       out_dtypes=[q_dtype], tn=512, **common)
            kv_outs = [F32, MXU_DTYPE] if caches is None else [F32]
            kv = _proj(x, ng, shift, scale, w_in, cos, sin, layer=a, mode="kv", col0=Q_WIDTH, n=3 * KV_WIDTH,
                       out_dtypes=kv_outs, tn=512, **common)
            gates, = _proj(x, ng, shift, scale, _group_gate_weights(prm["attn_gate_w"][a])[None], cos, sin, layer=0,
                           mode="sigmoid", col0=0, n=N_KV_HEADS * LANES, out_dtypes=[F32],
                           tn=N_KV_HEADS * LANES, **common)
            kv32 = kv[0]
            new_c.append(kv32[0].reshape(b, t, N_KV_HEADS, 2, HEAD_DIM))
            new_s.append(kv32[1].reshape(b, t, N_KV_HEADS, 2, HEAD_DIM))
            if caches is None:
                ckv = _compress_rows(kv32[0], prm["cmp_pe"], prm["cmp_w1"], prm["cmp_w2"], layer=a,
                                     nb=min(256, m // CMP_BLOCK))
                o = _attn_prompt(q.reshape(b, t, Q_WIDTH), ckv.reshape(b, t // CMP_BLOCK, KV_WIDTH),
                                 kv[1].reshape(3, b, t, KV_WIDTH), gates.reshape(b, t, N_KV_HEADS * LANES),
                                 tq=tiles["tq"], tk=tiles["tk"]).reshape(m, Q_WIDTH)
                new_w.append(kv32[2].reshape(b, t, N_KV_HEADS, 2, HEAD_DIM)[:, t - min(WINDOW, t):])
            else:
                pool_c, pool_s, win_buf, _, page_table = caches
                n_layers, n_pool, page = pool_c.shape[:3]
                past = page_table.shape[1] * page
                assert t < CMP_BLOCK and page % SEL_BLOCK == 0
                ckv = _compress_pages(pool_c.reshape(n_layers, n_pool, page, KV_WIDTH // HEAD_DIM, HEAD_DIM),
                                      page_table, prm["cmp_pe"], prm["cmp_w1"], prm["cmp_w2"], layer=a)
                q3 = q.reshape(b, t, Q_WIDTH)
                o_c, idx = _attn_cmp_sample(q3, ckv, past=past, n_total=past + t)
                t_pad = -(-t // SUBLANES) * SUBLANES
                pad_rows = lambda r: jnp.pad(r.reshape(b, t, KV_WIDTH), ((0, 0), (0, t_pad - t), (0, 0)))
                n_buf = win_buf.shape[2]
                o_w = _attn_win_sample(q3, win_buf, pad_rows(kv32[2]), layer=a, t_new=t)
                n_take = min(N_SELECT, -(-(past + t) // SEL_BLOCK))
                idx = jnp.transpose(idx[..., :n_take], (0, 2, 1, 3))
                heads = lambda v: v.reshape(b, t, N_KV_HEADS, GROUP_SIZE, HEAD_DIM)
                g5 = gates.reshape(b, t, N_KV_HEADS, LANES)[..., :GATE_COLS].reshape(b, t, N_KV_HEADS, GROUP_SIZE, 3)
                o = _attn_sel_sample(idx, page_table, heads(q3), pool_s, pad_rows(kv32[1]), heads(o_c), heads(o_w),
                                     g5, layer=a, past=past, t_new=t).reshape(m, Q_WIDTH)
                keys = jnp.concatenate([win_buf[a], kv32[2].reshape(b, t, N_KV_HEADS, 2, HEAD_DIM)], axis=1)
                new_w.append(keys[:, n_buf + t - min(WINDOW, n_buf + t):])
            x = _linres(o, prm["attn_w_out"], x, gate, layer=a, rows_per_mod=rpm, tm=tm, tn=tiles["tn_out"])
        else:
            glu = _glu_proj(x, ng, shift, scale, prm["conv_w_pw1"], layer=a, rows_per_mod=rpm, tm=tm, tn=512)
            glu = glu.reshape(b, t, d)
            width = prm["conv_w_dw"].shape[1]
            halo = 32
            if caches is None:
                buf = jnp.zeros((b, width - 1, d), F32)
                t_conv = t
                cur = glu
            else:
                buf = caches[3][a]
                t_conv = halo
                cur = jnp.pad(glu, ((0, 0), (0, t_conv - t), (0, 0)))
            full = jnp.concatenate([buf, glu], axis=1)
            new_conv.append(full[:, full.shape[1] - (width - 1):])
            buf = jnp.pad(buf, ((0, 0), (halo - (width - 1), 0), (0, 0)))
            vec = lambda v: v.reshape(v.shape[0], 1, d)
            act = _dwconv(cur, buf, prm["conv_w_dw"], vec(prm["conv_b_dw"]), vec(prm["conv_ln_g"]),
                          vec(prm["conv_ln_b"]), layer=a, tt=min(tiles["tt"], t_conv))
            act = act[:, :t].reshape(m, d)
            x = _linres(act, prm["conv_w_pw2"], x, gate, layer=a, rows_per_mod=rpm, tm=tm, tn=tiles["tn_out"])
        x = ffn(x, i, 1, 6)
    y = _rms_norm(x, prm["final_norm_g"], tm).reshape(b, t, d)
    return y, jnp.stack(new_c), jnp.stack(new_s), jnp.stack(new_w), jnp.stack(new_conv)


PROMPT_TILES = dict(tm=1024, tm_ffn=1024, tf=512, tq=256, tk=512, tn_out=512, tt=128)
SAMPLE_TILES = dict(tm=32, tm_ffn=32, tf=512, tn_out=512, tt=32)


def kernel(x_prompt, x_sample, cache_cmp_kv, cache_sel_kv, cache_win_kv, state_conv, page_table, c_prompt, c_sample, w_mod, b_mod, norm_g, ffn_w_gate, ffn_w_up, ffn_w_down, attn_w_in, attn_w_out, cmp_pe, cmp_w1, cmp_w2, conv_w_pw1, conv_w_dw, conv_b_dw, conv_ln_g, conv_ln_b, conv_w_pw2, final_norm_g):
    mxu = lambda w: w.astype(MXU_DTYPE)
    prm = {"w_mod": w_mod, "norm_g": norm_g, "ffn_w_gate": mxu(ffn_w_gate), "ffn_w_up": mxu(ffn_w_up),
           "ffn_w_down": mxu(ffn_w_down), "attn_w_in": mxu(attn_w_in), "attn_gate_w": attn_w_in[..., -N_HEADS * 3:],
           "attn_w_out": mxu(attn_w_out), "cmp_pe": cmp_pe,
           "cmp_w1": cmp_w1, "cmp_w2": cmp_w2, "conv_w_pw1": mxu(conv_w_pw1), "conv_w_dw": conv_w_dw,
           "conv_b_dw": conv_b_dw, "conv_ln_g": conv_ln_g, "conv_ln_b": conv_ln_b, "conv_w_pw2": mxu(conv_w_pw2),
           "final_norm_g": final_norm_g}
    depth, d, _ = w_mod.shape
    bp, tp = x_prompt.shape[:2]
    bs, ts = x_sample.shape[:2]
    past = page_table.shape[1] * cache_cmp_kv.shape[2]

    n_req = bp + bs
    r_pad = -(-n_req // 8) * 8
    c_all = jnp.pad(jnp.concatenate([c_prompt, c_sample], axis=0), ((0, r_pad - n_req), (0, 0)))
    mod = _mod_vectors(c_all, w_mod, b_mod).reshape(depth, r_pad, N_MOD, d)

    pos_p = jnp.arange(tp, dtype=jnp.int32)
    pos_s = past + jnp.arange(ts, dtype=jnp.int32)
    y_p, p_cmp, p_sel, p_win, p_conv = _run_trunk(x_prompt, mod[:, :bp], pos_p, prm, None, PROMPT_TILES)
    y_s, s_cmp, s_sel, s_win, s_conv = _run_trunk(
        x_sample, mod[:, bp:n_req], pos_s, prm,
        (cache_cmp_kv, cache_sel_kv, cache_win_kv, state_conv, page_table), SAMPLE_TILES)
    return (y_p, y_s, p_cmp, p_sel, p_win, p_conv, s_cmp, s_sel, s_win, s_conv)
```

```python
import functools

import jax
import jax.numpy as jnp
from jax import lax
from jax.experimental import pallas as pl
from jax.experimental.pallas import tpu as pltpu

F32 = jnp.float32
MXU_DTYPE = jnp.bfloat16
VMEM_LIMIT_BYTES = 56 * 1024 * 1024
LANES = 128
SUBLANES = 8

N_HEADS = 16
HEAD_DIM = 128
N_KV_HEADS = 4
GROUP_SIZE = N_HEADS // N_KV_HEADS
CMP_BLOCK = 32
SEL_BLOCK = 64
SEL_RATIO = SEL_BLOCK // CMP_BLOCK
N_SELECT = 16
WINDOW = 512
ROPE_THETA = 10000.0
N_MOD = 9
RMS_EPS = 1e-6
LN_EPS = 1e-5
NEG_INF = -1e30
FORCED_SCORE = 1e9
KNOCKED_OUT = -3e38
ATTN_SCALE = HEAD_DIM ** -0.5
Q_WIDTH = N_HEADS * HEAD_DIM
KV_WIDTH = 2 * N_KV_HEADS * HEAD_DIM
GATE_COLS = GROUP_SIZE * 3
PAGES_PER_STEP = 16
DWCONV_COLS = 128
ROW_BLOCK = 32
EXP2_SCALE = ATTN_SCALE * 1.4426950408889634


def _params(*sem):
    return pltpu.CompilerParams(dimension_semantics=sem, vmem_limit_bytes=VMEM_LIMIT_BYTES)


def _mxu(a, b):
    return jnp.dot(a.astype(MXU_DTYPE), b.astype(MXU_DTYPE), preferred_element_type=F32)


def _mxu_nt(a, b):
    return lax.dot_general(a.astype(MXU_DTYPE), b.astype(MXU_DTYPE),
                           (((1,), (1,)), ((), ())), preferred_element_type=F32)


def _silu(x):
    return x * jax.nn.sigmoid(x)


def _softmax_rows(s):
    e = jnp.exp(s - jnp.max(s, axis=-1, keepdims=True))
    return e / jnp.sum(e, axis=-1, keepdims=True)


def _softmax_scaled(s):
    e = jnp.exp2((s - jnp.max(s, axis=-1, keepdims=True)) * EXP2_SCALE)
    return e / jnp.sum(e, axis=-1, keepdims=True)


def _mod_spec(tm, rows_per_mod, d, grid_rank):
    if rows_per_mod >= tm:
        assert rows_per_mod % tm == 0
        block, idx = (1, 1, d), (lambda i: (i * tm) // rows_per_mod)
    else:
        assert rows_per_mod == 1
        block, idx = (tm, 1, d), (lambda i: i)
    if grid_rank == 1:
        return pl.BlockSpec(block, lambda i: (idx(i), 0, 0))
    return pl.BlockSpec(block, lambda i, j: (idx(i), 0, 0))


def _modulated(x_ref, ng_ref, sh_ref, sc_ref):
    x = x_ref[...]
    y = x * lax.rsqrt(jnp.mean(x * x, axis=-1, keepdims=True) + RMS_EPS) * ng_ref[...]
    return y * (1.0 + sc_ref[:, 0, :]) + sh_ref[:, 0, :]


def _mod_kernel(c_ref, w_ref, b_ref, o_ref):
    o_ref[...] = _mxu(_silu(c_ref[...]), w_ref[...]) + b_ref[...]


def _mod_vectors(c_all, w_mod, b_mod, tn=1024):
    depth, d, n = w_mod.shape
    r = c_all.shape[0]
    return pl.pallas_call(
        _mod_kernel,
        grid=(depth, n // tn),
        in_specs=[pl.BlockSpec((r, d), lambda l, j: (0, 0)),
                  pl.BlockSpec((None, d, tn), lambda l, j: (l, 0, j)),
                  pl.BlockSpec((None, 1, tn), lambda l, j: (l, 0, j))],
        out_specs=pl.BlockSpec((None, r, tn), lambda l, j: (l, 0, j)),
        out_shape=jax.ShapeDtypeStruct((depth, r, n), F32),
        compiler_params=_params("arbitrary", "arbitrary"),
        name="mod_vectors",
    )(c_all, w_mod, b_mod.reshape(depth, 1, n))


def _ffn_kernel(x_ref, ng_ref, sh_ref, sc_ref, gt_ref, wg_ref, wu_ref, wd_ref, o_ref, h_scr):
    f = pl.program_id(1)

    @pl.when(f == 0)
    def _():
        h_scr[...] = _modulated(x_ref, ng_ref, sh_ref, sc_ref).astype(h_scr.dtype)

    h = h_scr[...]
    g = _mxu(h, wg_ref[...])
    u = _mxu(h, wu_ref[...])
    a = jnp.concatenate([(_silu(g[r0:r0 + ROW_BLOCK]) * u[r0:r0 + ROW_BLOCK]).astype(MXU_DTYPE)
                         for r0 in range(0, g.shape[0], ROW_BLOCK)], axis=0)
    y = _mxu(a, wd_ref[...])

    @pl.when(f == 0)
    def _():
        o_ref[...] = y

    @pl.when(f > 0)
    def _():
        o_ref[...] += y

    @pl.when(f == pl.num_programs(1) - 1)
    def _():
        o_ref[...] = x_ref[...] + (0.5 * gt_ref[:, 0, :]) * o_ref[...]


def _ffn(x, ng, shift, scale, gate, w_gate, w_up, w_down, *, layer, which, rows_per_mod, tm, tf):
    m, d = x.shape
    dff = w_gate.shape[-1]
    mod = _mod_spec(tm, rows_per_mod, d, 2)
    return pl.pallas_call(
        _ffn_kernel,
        grid=(m // tm, dff // tf),
        in_specs=[pl.BlockSpec((tm, d), lambda i, f: (i, 0), pipeline_mode=pl.Buffered(1)),
                  pl.BlockSpec((1, d), lambda i, f: (0, 0)),
                  mod, mod, mod,
                  pl.BlockSpec((None, None, d, tf), lambda i, f: (layer, which, 0, f)),
                  pl.BlockSpec((None, None, d, tf), lambda i, f: (layer, which, 0, f)),
                  pl.BlockSpec((None, None, tf, d), lambda i, f: (layer, which, f, 0))],
        out_specs=pl.BlockSpec((tm, d), lambda i, f: (i, 0)),
        out_shape=jax.ShapeDtypeStruct((m, d), F32),
        scratch_shapes=[pltpu.VMEM((tm, d), MXU_DTYPE)],
        compiler_params=_params("arbitrary", "arbitrary"),
        name="ffn",
    )(x, ng, shift, scale, gate, w_gate, w_up, w_down)


def _rope_chunk(z, cos, sin):
    return z * cos + pltpu.roll(z, HEAD_DIM // 2, 1) * sin


def _proj_kernel(x_ref, ng_ref, sh_ref, sc_ref, w_ref, cos_ref, sin_ref, *rest, mode):
    out_refs, h_scr = rest[:-1], rest[-1]

    @pl.when(pl.program_id(1) == 0)
    def _():
        h_scr[...] = _modulated(x_ref, ng_ref, sh_ref, sc_ref).astype(h_scr.dtype)

    z = _mxu(h_scr[...], w_ref[...])
    if mode == "sigmoid":
        out_refs[0][...] = jax.nn.sigmoid(z)
        return
    cos, sin = cos_ref[...], sin_ref[...]
    n_chunks = z.shape[1] // HEAD_DIM
    for c in range(n_chunks):
        zc = z[:, c * HEAD_DIM:(c + 1) * HEAD_DIM]
        if mode == "q" or c % 2 == 0:
            zc = _rope_chunk(zc, cos, sin)
        for k, o in enumerate(out_refs):
            if mode == "kv" and k == 0:
                o[pl.ds(c, zc.shape[0], stride=n_chunks), :] = zc
            else:
                o[:, c * HEAD_DIM:(c + 1) * HEAD_DIM] = zc.astype(o.dtype)


def _proj(x, ng, shift, scale, w, cos, sin, *, layer, mode, col0, n, out_dtypes, rows_per_mod, rows_per_seq,
          tm, tn):
    m, d = x.shape
    mod = _mod_spec(tm, rows_per_mod, d, 2)
    seq_tiles = rows_per_seq // tm
    assert col0 % tn == 0 and n % tn == 0 and rows_per_seq % tm == 0
    if mode == "kv":
        assert tn == KV_WIDTH and out_dtypes[0] == F32
        n_chunks = KV_WIDTH // HEAD_DIM
        out_specs = [pl.BlockSpec((None, tm * n_chunks, HEAD_DIM), lambda i, j: (j, i, 0))] + [
            pl.BlockSpec((None, tm, tn), lambda i, j: (j, i, 0)) for _ in out_dtypes[1:]]
        out_shape = [jax.ShapeDtypeStruct((n // KV_WIDTH, m * n_chunks, HEAD_DIM), F32)] + [
            jax.ShapeDtypeStruct((n // KV_WIDTH, m, KV_WIDTH), dt) for dt in out_dtypes[1:]]
    else:
        out_specs = [pl.BlockSpec((tm, tn), lambda i, j: (i, j)) for _ in out_dtypes]
        out_shape = [jax.ShapeDtypeStruct((m, n), dt) for dt in out_dtypes]
    return pl.pallas_call(
        functools.partial(_proj_kernel, mode=mode),
        grid=(m // tm, n // tn),
        in_specs=[pl.BlockSpec((tm, d), lambda i, j: (i, 0)),
                  pl.BlockSpec((1, d), lambda i, j: (0, 0)),
                  mod, mod,
                  pl.BlockSpec((None, d, tn), lambda i, j: (layer, 0, col0 // tn + j)),
                  pl.BlockSpec((tm, HEAD_DIM), lambda i, j: (i % seq_tiles, 0)),
                  pl.BlockSpec((tm, HEAD_DIM), lambda i, j: (i % seq_tiles, 0))],
        out_specs=out_specs,
        out_shape=out_shape,
        scratch_shapes=[pltpu.VMEM((tm, d), MXU_DTYPE)],
        compiler_params=_params("arbitrary", "arbitrary"),
        name="proj_" + mode,
    )(x, ng, shift, scale, w, cos, sin)


def _glu_kernel(x_ref, ng_ref, sh_ref, sc_ref, wa_ref, wb_ref, o_ref, h_scr):
    @pl.when(pl.program_id(1) == 0)
    def _():
        h_scr[...] = _modulated(x_ref, ng_ref, sh_ref, sc_ref).astype(h_scr.dtype)

    h = h_scr[...]
    o_ref[...] = _mxu(h, wa_ref[...]) * jax.nn.sigmoid(_mxu(h, wb_ref[...]))


def _glu_proj(x, ng, shift, scale, w_pw1, *, layer, rows_per_mod, tm, tn):
    m, d = x.shape
    dc = w_pw1.shape[-1] // 2
    mod = _mod_spec(tm, rows_per_mod, d, 2)
    return pl.pallas_call(
        _glu_kernel,
        grid=(m // tm, dc // tn),
        in_specs=[pl.BlockSpec((tm, d), lambda i, j: (i, 0)),
                  pl.BlockSpec((1, d), lambda i, j: (0, 0)),
                  mod, mod,
                  pl.BlockSpec((None, d, tn), lambda i, j: (layer, 0, j)),
                  pl.BlockSpec((None, d, tn), lambda i, j: (layer, 0, dc // tn + j))],
        out_specs=pl.BlockSpec((tm, tn), lambda i, j: (i, j)),
        out_shape=jax.ShapeDtypeStruct((m, dc), F32),
        scratch_shapes=[pltpu.VMEM((tm, d), MXU_DTYPE)],
        compiler_params=_params("arbitrary", "arbitrary"),
        name="glu_proj",
    )(x, ng, shift, scale, w_pw1, w_pw1)


def _linres_kernel(a_ref, w_ref, x_ref, gt_ref, o_ref):
    o_ref[...] = x_ref[...] + gt_ref[:, 0, :] * _mxu(a_ref[...], w_ref[...])


def _linres(a, w, x, gate, *, layer, rows_per_mod, tm, tn):
    m, k = a.shape
    d = x.shape[1]
    if rows_per_mod >= tm:
        gspec = pl.BlockSpec((1, 1, tn), lambda i, j: ((i * tm) // rows_per_mod, 0, j))
    else:
        gspec = pl.BlockSpec((tm, 1, tn), lambda i, j: (i, 0, j))
    return pl.pallas_call(
        _linres_kernel,
        grid=(m // tm, d // tn),
        in_specs=[pl.BlockSpec((tm, k), lambda i, j: (i, 0)),
                  pl.BlockSpec((None, k, tn), lambda i, j: (layer, 0, j)),
                  pl.BlockSpec((tm, tn), lambda i, j: (i, j)),
                  gspec],
        out_specs=pl.BlockSpec((tm, tn), lambda i, j: (i, j)),
        out_shape=jax.ShapeDtypeStruct((m, d), F32),
        compiler_params=_params("arbitrary", "arbitrary"),
        name="linres",
    )(a, w, x, gate)


def _rms_kernel(x_ref, g_ref, o_ref):
    x = x_ref[...]
    o_ref[...] = x * lax.rsqrt(jnp.mean(x * x, axis=-1, keepdims=True) + RMS_EPS) * g_ref[...]


def _rms_norm(x, g, tm):
    m, d = x.shape
    return pl.pallas_call(
        _rms_kernel,
        grid=(m // tm,),
        in_specs=[pl.BlockSpec((tm, d), lambda i: (i, 0)), pl.BlockSpec((1, d), lambda i: (0, 0))],
        out_specs=pl.BlockSpec((tm, d), lambda i: (i, 0)),
        out_shape=jax.ShapeDtypeStruct((m, d), F32),
        compiler_params=_params("arbitrary"),
        name="final_norm",
    )(x, g.reshape(1, d))


def _compress_kernel(x_ref, pe_ref, w1_ref, w2_ref, o_ref):
    n_chunks = KV_WIDTH // HEAD_DIM
    nb = x_ref.shape[0] // (CMP_BLOCK * n_chunks)
    for kv in range(2):
        acc = jnp.zeros((N_KV_HEADS * nb, HEAD_DIM), F32)
        for c in range(CMP_BLOCK):
            xc = jnp.concatenate([x_ref[pl.ds(c * n_chunks + 2 * g + kv, nb, stride=CMP_BLOCK * n_chunks), :]
                                  for g in range(N_KV_HEADS)], axis=0) + pe_ref[kv, c:c + 1, :]
            acc += _mxu(xc, w1_ref[kv, c * HEAD_DIM:(c + 1) * HEAD_DIM, :])
        y = _mxu(_silu(acc), w2_ref[kv])
        for g in range(N_KV_HEADS):
            o_ref[:, (2 * g + kv) * HEAD_DIM:(2 * g + kv + 1) * HEAD_DIM] = y[g * nb:(g + 1) * nb].astype(o_ref.dtype)


def _compress_rows(rows, pe, w1, w2, *, layer, n_seq):
    n_chunks = KV_WIDTH // HEAD_DIM
    n_blocks = rows.shape[0] // (CMP_BLOCK * n_chunks)
    nb = n_blocks // n_seq
    return pl.pallas_call(
        _compress_kernel,
        grid=(n_seq,),
        in_specs=[pl.BlockSpec((nb * CMP_BLOCK * n_chunks, HEAD_DIM), lambda i: (i, 0)),
                  pl.BlockSpec((None, 2, CMP_BLOCK, HEAD_DIM), lambda i: (layer, 0, 0, 0)),
                  pl.BlockSpec((None, 2, CMP_BLOCK * HEAD_DIM, HEAD_DIM), lambda i: (layer, 0, 0, 0)),
                  pl.BlockSpec((None, 2, HEAD_DIM, HEAD_DIM), lambda i: (layer, 0, 0, 0))],
        out_specs=pl.BlockSpec((nb, KV_WIDTH), lambda i: (i, 0)),
        out_shape=jax.ShapeDtypeStruct((n_blocks, KV_WIDTH), MXU_DTYPE),
        compiler_params=_params("arbitrary"),
        name="compress_rows",
    )(rows, pe, w1, w2)


def _compress_pages_kernel(pt_ref, *refs, blocks_per_page):
    del pt_ref
    pages = refs[:PAGES_PER_STEP]
    pe_ref, w1_ref, w2_ref, o_ref, t_scr = refs[PAGES_PER_STEP:]
    nb = PAGES_PER_STEP * blocks_per_page
    pages_per_group = SUBLANES // blocks_per_page
    n_groups = PAGES_PER_STEP // pages_per_group
    rows = SUBLANES * CMP_BLOCK
    out_row = lax.broadcasted_iota(jnp.int32, (rows, rows), 0)
    in_row = lax.broadcasted_iota(jnp.int32, (rows, rows), 1)
    perm = jnp.where(in_row == (out_row % SUBLANES) * CMP_BLOCK + out_row // SUBLANES, 1.0, 0.0)
    pe_rows = jnp.concatenate([pe_ref[kv] for _ in range(N_KV_HEADS) for kv in range(2)], axis=1)
    pe_rows = jnp.concatenate([pe_rows] * SUBLANES, axis=0)
    n_chunks = KV_WIDTH // HEAD_DIM
    page_rows = pages[0].shape[0] // n_chunks
    for gp in range(n_groups):
        x = jnp.concatenate(
            [jnp.concatenate([pages[gp * pages_per_group + k][pl.ds(ch, page_rows, stride=n_chunks), :]
                              for ch in range(n_chunks)], axis=1)
             for k in range(pages_per_group)], axis=0)
        t_scr[gp] = _mxu(perm, x + pe_rows)
    for kv in range(2):
        acc = jnp.zeros((N_KV_HEADS * nb, HEAD_DIM), F32)
        for c in range(CMP_BLOCK):
            xc = jnp.concatenate(
                [t_scr[:, c * SUBLANES:(c + 1) * SUBLANES,
                       (g * 2 + kv) * HEAD_DIM:(g * 2 + kv + 1) * HEAD_DIM].reshape(nb, HEAD_DIM)
                 for g in range(N_KV_HEADS)], axis=0)
            acc += _mxu(xc, w1_ref[kv, c * HEAD_DIM:(c + 1) * HEAD_DIM, :])
        y = _mxu(_silu(acc), w2_ref[kv])
        for g in range(N_KV_HEADS):
            o_ref[:, (g * 2 + kv) * HEAD_DIM:(g * 2 + kv + 1) * HEAD_DIM] = (
                y[g * nb:(g + 1) * nb].astype(o_ref.dtype))


def _compress_pages(pool, page_table, pe, w1, w2, *, layer):
    width = KV_WIDTH
    n_chunks = width // HEAD_DIM
    page = pool.shape[2] // n_chunks
    b, n_pages = page_table.shape
    bpp = page // CMP_BLOCK
    nb = PAGES_PER_STEP * bpp
    assert n_pages % PAGES_PER_STEP == 0 and page % CMP_BLOCK == 0 and SUBLANES % bpp == 0
    n_groups = nb // SUBLANES

    def page_spec(p):
        return pl.BlockSpec((None, None, page * n_chunks, HEAD_DIM),
                            lambda bi, gi, pt: (layer, pt[bi * n_pages + gi * PAGES_PER_STEP + p], 0, 0))

    grid_spec = pltpu.PrefetchScalarGridSpec(
        num_scalar_prefetch=1,
        grid=(b, n_pages // PAGES_PER_STEP),
        in_specs=[page_spec(p) for p in range(PAGES_PER_STEP)] + [
            pl.BlockSpec((None, 2, CMP_BLOCK, HEAD_DIM), lambda bi, gi, pt: (layer, 0, 0, 0)),
            pl.BlockSpec((None, 2, CMP_BLOCK * HEAD_DIM, HEAD_DIM), lambda bi, gi, pt: (layer, 0, 0, 0)),
            pl.BlockSpec((None, 2, HEAD_DIM, HEAD_DIM), lambda bi, gi, pt: (layer, 0, 0, 0))],
        out_specs=pl.BlockSpec((None, nb, width), lambda bi, gi, pt: (bi, gi, 0)),
        scratch_shapes=[pltpu.VMEM((n_groups, SUBLANES * CMP_BLOCK, width), F32)],
    )
    return pl.pallas_call(
        functools.partial(_compress_pages_kernel, blocks_per_page=bpp),
        grid_spec=grid_spec,
        out_shape=jax.ShapeDtypeStruct((b, n_pages * bpp, width), MXU_DTYPE),
        compiler_params=_params("arbitrary", "arbitrary"),
        name="compress_pages",
    )(page_table.reshape(-1), *([pool] * PAGES_PER_STEP), pe, w1, w2)


def _pair_sums(imp):
    out = []
    for k in range(imp.shape[1] // LANES):
        x = imp[:, k * LANES:(k + 1) * LANES]
        even = lax.broadcasted_iota(jnp.int32, x.shape, 1) % 2 == 0
        out.append(x + jnp.where(even, pltpu.roll(x, LANES - 1, 1), pltpu.roll(x, 1, 1)))
    return out[0] if len(out) == 1 else jnp.concatenate(out, axis=1)


def _block_scores(imp, q_pos, n_sel_blocks):
    lane = lax.broadcasted_iota(jnp.int32, imp.shape, 1)
    blk = lane // SEL_RATIO
    cur = q_pos // SEL_BLOCK
    forced = (blk == 0) | (blk == cur) | (blk == cur - 1)
    valid = blk * SEL_BLOCK <= q_pos
    score = jnp.where(forced, FORCED_SCORE, jnp.where(valid, _pair_sums(imp), -1.0))
    eligible = (lane % SEL_RATIO == 0) & (blk < n_sel_blocks)
    return jnp.where(eligible, score, KNOCKED_OUT)


def _take_top(work):
    lane = lax.broadcasted_iota(jnp.int32, work.shape, 1).astype(F32)
    top = jnp.max(work, axis=-1, keepdims=True)
    pick = jnp.min(jnp.where(work == top, lane, float(work.shape[1])), axis=-1, keepdims=True)
    return lane == pick, pick


def _with_ones(v):
    return jnp.concatenate([v, jnp.ones(v.shape, v.dtype)], axis=1)


def _attn_prompt_kernel(q_ref, ck_ref, cv_ref, ks_ref, vs_ref, kw_ref, vw_ref, g_ref, o_ref, *, tq, tk, seq):
    i = pl.program_id(2)
    n_cmp = ck_ref.shape[0]
    n_sel_blocks = -(-seq // SEL_BLOCK)
    n_take = min(N_SELECT, n_sel_blocks)
    assert SEL_RATIO == 2 and n_cmp * CMP_BLOCK == seq and n_cmp <= LANES
    q = q_ref[...]
    q4 = jnp.concatenate([q[:, h * HEAD_DIM:(h + 1) * HEAD_DIM] for h in range(GROUP_SIZE)], axis=0)
    q_pos = i * tq + lax.broadcasted_iota(jnp.int32, (tq, 1), 0)

    pad = jnp.zeros((LANES - n_cmp, HEAD_DIM), ck_ref.dtype)
    ck = jnp.concatenate([ck_ref[...], pad], axis=0) if n_cmp < LANES else ck_ref[...]
    cv = jnp.concatenate([cv_ref[...], pad], axis=0) if n_cmp < LANES else cv_ref[...]
    lane = lax.broadcasted_iota(jnp.int32, (1, LANES), 1)
    vis = ((lane + 1) * CMP_BLOCK - 1 <= q_pos) & (lane < n_cmp)
    s = _mxu_nt(q4, ck).reshape(GROUP_SIZE, tq, LANES)
    p = jnp.where(vis[None], _softmax_scaled(jnp.where(vis[None], s, NEG_INF)), 0.0)
    o_c = _mxu(p.reshape(GROUP_SIZE * tq, LANES), cv).reshape(GROUP_SIZE, tq, HEAD_DIM)
    imp = jnp.sum(p, axis=0)

    def top_blocks():
        work = _block_scores(imp, q_pos, n_sel_blocks)
        sel = jnp.zeros((tq, LANES), F32)
        for _ in range(n_take):
            hit, _ = _take_top(work)
            work = jnp.where(hit, KNOCKED_OUT, work)
            sel = jnp.where(hit, 1.0, sel)
        return sel

    def all_blocks():
        return jnp.where((lane % SEL_RATIO == 0) & (lane // SEL_RATIO < n_sel_blocks), 1.0,
                         jnp.zeros((tq, LANES), F32))

    sel = lax.cond(((i + 1) * tq - 1) // SEL_BLOCK + 1 <= n_take, all_blocks, top_blocks).astype(MXU_DTYPE)

    def sel_chunk(kc, carry):
        m_i, l_i, acc = carry
        start = pl.multiple_of(kc * tk, tk)
        k_pos = start + lax.broadcasted_iota(jnp.int32, (1, tk), 1)
        expand = lax.broadcasted_iota(jnp.int32, (LANES, 1), 0) == SEL_RATIO * (k_pos // SEL_BLOCK)
        picked = jnp.dot(sel, jnp.where(expand, 1.0, 0.0).astype(MXU_DTYPE), preferred_element_type=F32)
        bias = jnp.where((picked > 0.5) & (k_pos <= q_pos), 0.0, NEG_INF)[None]
        sc = _mxu_nt(q4, ks_ref[pl.ds(start, tk), :]).reshape(GROUP_SIZE, tq, tk) + bias
        m_new = jnp.maximum(m_i, jnp.max(sc, axis=-1, keepdims=True))
        alpha = jnp.exp2((m_i - m_new) * EXP2_SCALE)
        e = jnp.exp2((sc - m_new) * EXP2_SCALE)
        l_new = alpha * l_i + jnp.sum(e, axis=-1, keepdims=True)
        pv = _mxu(e.reshape(GROUP_SIZE * tq, tk), vs_ref[pl.ds(start, tk), :])
        return m_new, l_new, alpha * acc + pv.reshape(GROUP_SIZE, tq, HEAD_DIM)

    init = (jnp.full((GROUP_SIZE, tq, 1), NEG_INF, F32), jnp.zeros((GROUP_SIZE, tq, 1), F32),
            jnp.zeros((GROUP_SIZE, tq, HEAD_DIM), F32))
    _, l_s, acc_s = lax.fori_loop(0, ((i + 1) * tq + tk - 1) // tk, sel_chunk, init)
    o_s = acc_s / l_s

    span = tq + WINDOW
    w_start = pl.multiple_of(jnp.maximum(i * tq - WINDOW, 0), tq)
    dist = q_pos - (w_start + lax.broadcasted_iota(jnp.int32, (1, span), 1))
    bias = jnp.where((dist >= 0) & (dist < WINDOW), 0.0, NEG_INF)[None]
    sw = _mxu_nt(q4, kw_ref[pl.ds(w_start, span), :]).reshape(GROUP_SIZE, tq, span) + bias
    ew = jnp.exp2((sw - jnp.max(sw, axis=-1, keepdims=True)) * EXP2_SCALE)
    pv = _mxu(ew.reshape(GROUP_SIZE * tq, span), _with_ones(vw_ref[pl.ds(w_start, span), :]))
    o_w = (pv[:, :HEAD_DIM] / pv[:, HEAD_DIM:]).reshape(GROUP_SIZE, tq, HEAD_DIM)

    gates = g_ref[...]
    for h in range(GROUP_SIZE):
        o = (gates[:, 3 * h:3 * h + 1] * o_c[h] + gates[:, 3 * h + 1:3 * h + 2] * o_s[h]
             + gates[:, 3 * h + 2:3 * h + 3] * o_w[h])
        o_ref[:, h * HEAD_DIM:(h + 1) * HEAD_DIM] = o.astype(o_ref.dtype)


def _attn_prompt(q, ckv, kv, gates, *, tq, tk):
    b, t, _ = q.shape
    n_cmp = ckv.shape[1]
    assert t % tq == 0 and t % tk == 0 and t >= tq + WINDOW and WINDOW % tq == 0 and tq % ROW_BLOCK == 0
    gw = GROUP_SIZE * HEAD_DIM

    def kv_spec(branch, part):
        return pl.BlockSpec((None, None, t, HEAD_DIM), lambda bi, g, i: (branch, bi, 0, 2 * g + part))

    return pl.pallas_call(
        functools.partial(_attn_prompt_kernel, tq=tq, tk=tk, seq=t),
        grid=(b, N_KV_HEADS, t // tq),
        in_specs=[pl.BlockSpec((None, tq, gw), lambda bi, g, i: (bi, i, g)),
                  pl.BlockSpec((None, n_cmp, HEAD_DIM), lambda bi, g, i: (bi, 0, 2 * g)),
                  pl.BlockSpec((None, n_cmp, HEAD_DIM), lambda bi, g, i: (bi, 0, 2 * g + 1)),
                  kv_spec(1, 0), kv_spec(1, 1), kv_spec(2, 0), kv_spec(2, 1),
                  pl.BlockSpec((None, tq, LANES), lambda bi, g, i: (bi, i, g))],
        out_specs=pl.BlockSpec((None, tq, gw), lambda bi, g, i: (bi, i, g)),
        out_shape=jax.ShapeDtypeStruct((b, t, Q_WIDTH), MXU_DTYPE),
        compiler_params=_params("arbitrary", "arbitrary", "arbitrary"),
        name="attn_prompt",
    )(q, ckv, ckv, kv, kv, kv, kv, gates)


def _attn_cmp_sample_kernel(q_ref, ckv_ref, o_ref, idx_ref, *, past, n_total, width):
    t_rows = q_ref.shape[0]
    n_cmp = ckv_ref.shape[0]
    n_sel_blocks = -(-n_total // SEL_BLOCK)
    n_take = min(N_SELECT, n_sel_blocks)
    assert SEL_RATIO == 2 and n_take <= LANES and SEL_RATIO * n_sel_blocks <= width
    q_pos = past + lax.broadcasted_iota(jnp.int32, (t_rows, 1), 0)
    lane = lax.broadcasted_iota(jnp.int32, (1, width), 1)
    vis = ((lane + 1) * CMP_BLOCK - 1 <= q_pos) & (lane < n_cmp)
    out_lane = lax.broadcasted_iota(jnp.int32, (1, LANES), 1)
    pad = jnp.zeros((width - n_cmp, HEAD_DIM), ckv_ref.dtype)
    for g in range(N_KV_HEADS):
        ck = jnp.concatenate([ckv_ref[:, 2 * g * HEAD_DIM:(2 * g + 1) * HEAD_DIM], pad], axis=0)
        cv = jnp.concatenate([ckv_ref[:, (2 * g + 1) * HEAD_DIM:(2 * g + 2) * HEAD_DIM], pad], axis=0)
        imp = jnp.zeros((t_rows, width), F32)
        for h in range(GROUP_SIZE):
            cols = slice((g * GROUP_SIZE + h) * HEAD_DIM, (g * GROUP_SIZE + h + 1) * HEAD_DIM)
            s = _mxu_nt(q_ref[:, cols], ck) * ATTN_SCALE
            p = jnp.where(vis, _softmax_rows(jnp.where(vis, s, NEG_INF)), 0.0)
            o_ref[:, cols] = _mxu(p, cv)
            imp = imp + p
        work = _block_scores(imp, q_pos, n_sel_blocks)
        picks = jnp.zeros((t_rows, LANES), F32)
        for r in range(n_take):
            hit, pick = _take_top(work)
            work = jnp.where(hit, KNOCKED_OUT, work)
            picks = jnp.where(out_lane == r, pick, picks)
        idx_ref[g] = (picks * (1.0 / SEL_RATIO)).astype(jnp.int32)


def _attn_cmp_sample(q, ckv, *, past, n_total):
    b, t, qw = q.shape
    n_cmp = ckv.shape[1]
    width = -(-max(n_cmp, SEL_RATIO * -(-n_total // SEL_BLOCK)) // LANES) * LANES
    return pl.pallas_call(
        functools.partial(_attn_cmp_sample_kernel, past=past, n_total=n_total, width=width),
        grid=(b,),
        in_specs=[pl.BlockSpec((None, t, qw), lambda bi: (bi, 0, 0)),
                  pl.BlockSpec((None, n_cmp, KV_WIDTH), lambda bi: (bi, 0, 0))],
        out_specs=[pl.BlockSpec((None, t, qw), lambda bi: (bi, 0, 0)),
                   pl.BlockSpec((None, N_KV_HEADS, t, LANES), lambda bi: (bi, 0, 0, 0))],
        out_shape=[jax.ShapeDtypeStruct(q.shape, F32),
                   jax.ShapeDtypeStruct((b, N_KV_HEADS, t, LANES), jnp.int32)],
        compiler_params=_params("arbitrary"),
        name="attn_cmp_sample",
    )(q, ckv)


def _attn_win_sample_kernel(q_ref, buf_ref, new_ref, o_ref, *, t_new):
    t_rows = q_ref.shape[0]
    n_chunks = KV_WIDTH // HEAD_DIM
    n_buf = buf_ref.shape[0] // n_chunks
    t_idx = lax.broadcasted_iota(jnp.int32, (t_rows, 1), 0)
    dist_a = t_idx + n_buf - lax.broadcasted_iota(jnp.int32, (1, n_buf), 1)
    row_b = lax.broadcasted_iota(jnp.int32, (1, new_ref.shape[0]), 1)
    mask_a = (dist_a >= 0) & (dist_a < WINDOW)
    mask_b = (row_b <= t_idx) & (t_idx - row_b < WINDOW) & (row_b < t_new)
    for g in range(N_KV_HEADS):
        k_a = buf_ref[pl.ds(2 * g, n_buf, stride=n_chunks), :]
        v_a = buf_ref[pl.ds(2 * g + 1, n_buf, stride=n_chunks), :]
        k_b = new_ref[:, 2 * g * HEAD_DIM:(2 * g + 1) * HEAD_DIM]
        v_b = new_ref[:, (2 * g + 1) * HEAD_DIM:(2 * g + 2) * HEAD_DIM]
        for h in range(GROUP_SIZE):
            cols = slice((g * GROUP_SIZE + h) * HEAD_DIM, (g * GROUP_SIZE + h + 1) * HEAD_DIM)
            q = q_ref[:, cols]
            s_a = jnp.where(mask_a, _mxu_nt(q, k_a) * ATTN_SCALE, NEG_INF)
            s_b = jnp.where(mask_b, _mxu_nt(q, k_b) * ATTN_SCALE, NEG_INF)
            m = jnp.maximum(jnp.max(s_a, axis=-1, keepdims=True), jnp.max(s_b, axis=-1, keepdims=True))
            e_a, e_b = jnp.exp(s_a - m), jnp.exp(s_b - m)
            l = jnp.sum(e_a, axis=-1, keepdims=True) + jnp.sum(e_b, axis=-1, keepdims=True)
            o_ref[:, cols] = _mxu(e_a / l, v_a) + _mxu(e_b / l, v_b)


def _attn_win_sample(q, win_buf, kv_new, *, layer, t_new):
    b, t, qw = q.shape
    buf_rows = win_buf.shape[2]
    return pl.pallas_call(
        functools.partial(_attn_win_sample_kernel, t_new=t_new),
        grid=(b,),
        in_specs=[pl.BlockSpec((None, t, qw), lambda bi: (bi, 0, 0)),
                  pl.BlockSpec((None, None, buf_rows, HEAD_DIM), lambda bi: (layer, bi, 0, 0)),
                  pl.BlockSpec((None, kv_new.shape[1], KV_WIDTH), lambda bi: (bi, 0, 0))],
        out_specs=pl.BlockSpec((None, t, qw), lambda bi: (bi, 0, 0)),
        out_shape=jax.ShapeDtypeStruct(q.shape, F32),
        compiler_params=_params("arbitrary"),
        name="attn_win_sample",
    )(q, win_buf, kv_new)


def _attn_sel_sample_kernel(idx_ref, pt_ref, q_ref, *refs, past, t_new, n_slots, t_steps):
    del pt_ref
    blocks = refs[:n_slots]
    new_ref, oc_ref, ow_ref, g_ref, o_ref = refs[n_slots:]
    bi, t, gi = pl.program_id(0), pl.program_id(1), pl.program_id(2)
    base = ((bi * t_steps + t) * N_KV_HEADS + gi) * n_slots
    n_past_blocks = past // SEL_BLOCK
    q_pos = past + t
    q = q_ref[...]

    keys = jnp.concatenate([blk[:, 0, :].astype(MXU_DTYPE) for blk in blocks], axis=0)
    vals = jnp.concatenate([blk[:, 1, :].astype(MXU_DTYPE) for blk in blocks], axis=0)
    lane = lax.broadcasted_iota(jnp.int32, (1, n_slots * SEL_BLOCK), 1)
    pos = lane % SEL_BLOCK
    has_new = jnp.int32(0)
    for n in range(n_slots):
        blk = idx_ref[base + n]
        start = jnp.where(blk < n_past_blocks, blk * SEL_BLOCK, q_pos + 1)
        pos = pos + jnp.where(lane // SEL_BLOCK == n, start, 0)
        has_new = has_new | (blk == n_past_blocks).astype(jnp.int32)
    mask_a = pos <= q_pos
    new = new_ref[...]
    row_b = lax.broadcasted_iota(jnp.int32, (1, new.shape[0]), 1)
    mask_b = (row_b <= t) & (row_b < t_new) & (has_new > 0)
    s_a = jnp.where(mask_a, _mxu_nt(q, keys) * ATTN_SCALE, NEG_INF)
    s_b = jnp.where(mask_b, _mxu_nt(q, new[:, :HEAD_DIM]) * ATTN_SCALE, NEG_INF)
    m = jnp.maximum(jnp.max(s_a, axis=-1, keepdims=True), jnp.max(s_b, axis=-1, keepdims=True))
    e_a, e_b = jnp.exp(s_a - m), jnp.exp(s_b - m)
    l = jnp.sum(e_a, axis=-1, keepdims=True) + jnp.sum(e_b, axis=-1, keepdims=True)
    o_s = _mxu(e_a / l, vals) + _mxu(e_b / l, new[:, HEAD_DIM:])
    gates = g_ref[...]
    o_ref[...] = gates[:, 0:1] * oc_ref[...] + gates[:, 1:2] * o_s + gates[:, 2:3] * ow_ref[...]


def _attn_sel_sample(idx, page_table, q, pool, kv_new, o_c, o_w, gates, *, layer, past, t_new):
    b, t, g, hg, hd = q.shape
    n_slots = idx.shape[-1]
    n_pages = page_table.shape[1]
    page = past // n_pages
    per_page = page // SEL_BLOCK
    n_past_blocks = past // SEL_BLOCK

    def slot_spec(n):
        def index(bi, ti, gi, idx_s, pt_s):
            blk = jnp.minimum(idx_s[((bi * t + ti) * g + gi) * n_slots + n], n_past_blocks - 1)
            phys = pt_s[bi * n_pages + blk // per_page]
            return (layer, phys, blk % per_page, gi, 0, 0)
        return pl.BlockSpec((None, None, SEL_BLOCK, None, 2, HEAD_DIM), index)

    head_spec = pl.BlockSpec((None, None, None, hg, hd), lambda bi, ti, gi, idx_s, pt_s: (bi, ti, gi, 0, 0))
    grid_spec = pltpu.PrefetchScalarGridSpec(
        num_scalar_prefetch=2,
        grid=(b, t, g),
        in_specs=[head_spec] + [slot_spec(n) for n in range(n_slots)] + [
            pl.BlockSpec((None, kv_new.shape[1], 2 * HEAD_DIM), lambda bi, ti, gi, idx_s, pt_s: (bi, 0, gi)),
            head_spec, head_spec,
            pl.BlockSpec((None, None, None, hg, 3), lambda bi, ti, gi, idx_s, pt_s: (bi, ti, gi, 0, 0))],
        out_specs=head_spec,
    )
    return pl.pallas_call(
        functools.partial(_attn_sel_sample_kernel, past=past, t_new=t_new, n_slots=n_slots, t_steps=t),
        grid_spec=grid_spec,
        out_shape=jax.ShapeDtypeStruct(q.shape, F32),
        compiler_params=_params("arbitrary", "arbitrary", "arbitrary"),
        name="attn_sel_sample",
    )(idx.reshape(-1), page_table.reshape(-1), q, *([pool] * n_slots), kv_new, o_c, o_w, gates)


def _dwconv_kernel(cur_ref, prev_ref, buf_ref, w_ref, b_ref, lg_ref, lb_ref, o_ref, full_scr, y_scr, *,
                   tt, halo, width):
    i = pl.program_id(1)
    lead = halo - (width - 1)
    full_scr[0:halo, :] = prev_ref[...]

    @pl.when(i == 0)
    def _():
        full_scr[0:halo, :] = buf_ref[...]

    full_scr[halo:halo + tt, :] = cur_ref[...]
    for c0 in range(0, cur_ref.shape[1], DWCONV_COLS):
        cols = slice(c0, c0 + DWCONV_COLS)
        acc = jnp.zeros((tt, DWCONV_COLS), F32) + b_ref[:, cols]
        for r in range(SUBLANES):
            offsets = [o for o in range(r, lead + width, SUBLANES) if o >= lead]
            rows = tt + (SUBLANES if r else 0)
            group = None
            for o in offsets:
                term = full_scr[o - r:o - r + rows, cols] * w_ref[o - lead:o - lead + 1, cols]
                group = term if group is None else group + term
            if group is not None:
                acc = acc + group[r:r + tt]
        y_scr[:, cols] = acc
    y = y_scr[...]
    mu = jnp.mean(y, axis=-1, keepdims=True)
    var = jnp.mean(jnp.square(y - mu), axis=-1, keepdims=True)
    yn = (y - mu) * lax.rsqrt(var + LN_EPS) * lg_ref[...] + lb_ref[...]
    o_ref[...] = _silu(yn).astype(o_ref.dtype)


def _dwconv(glu, buf, w_dw, b_dw, ln_g, ln_b, *, layer, tt):
    b, t, d = glu.shape
    width = w_dw.shape[1]
    halo = buf.shape[1]
    assert tt % halo == 0 and t % tt == 0 and halo >= width - 1
    per = tt // halo

    def vec_spec():
        return pl.BlockSpec((None, 1, d), lambda bi, i: (layer, 0, 0))

    return pl.pallas_call(
        functools.partial(_dwconv_kernel, tt=tt, halo=halo, width=width),
        grid=(b, t // tt),
        in_specs=[pl.BlockSpec((None, tt, d), lambda bi, i: (bi, i, 0)),
                  pl.BlockSpec((None, halo, d), lambda bi, i: (bi, jnp.maximum(i * per - 1, 0), 0)),
                  pl.BlockSpec((None, halo, d), lambda bi, i: (bi, 0, 0)),
                  pl.BlockSpec((None, width, d), lambda bi, i: (layer, 0, 0)),
                  vec_spec(), vec_spec(), vec_spec()],
        out_specs=pl.BlockSpec((None, tt, d), lambda bi, i: (bi, i, 0)),
        out_shape=jax.ShapeDtypeStruct((b, t, d), MXU_DTYPE),
        scratch_shapes=[pltpu.VMEM((halo + tt, d), F32), pltpu.VMEM((tt, d), F32)],
        compiler_params=_params("arbitrary", "arbitrary"),
        name="dwconv",
    )(glu, glu, buf, w_dw, b_dw, ln_g, ln_b)


def _rope_tables(pos):
    half = HEAD_DIM // 2
    inv = ROPE_THETA ** (-jnp.arange(half, dtype=F32) / half)
    ang = pos.astype(F32)[:, None] * inv[None, :]
    cos, sin = jnp.cos(ang), jnp.sin(ang)
    return jnp.concatenate([cos, cos], axis=1), jnp.concatenate([-sin, sin], axis=1)


def _group_gate_weights(w_gate_cols):
    d = w_gate_cols.shape[0]
    wg = w_gate_cols.reshape(d, N_KV_HEADS, GATE_COLS)
    return jnp.pad(wg, ((0, 0), (0, 0), (0, LANES - GATE_COLS))).reshape(d, N_KV_HEADS * LANES)


def _run_trunk(x, mod, pos, prm, caches, tiles):
    b, t, d = x.shape
    m = b * t
    depth = prm["w_mod"].shape[0]
    tm = tiles["tm"]
    rpm = t if caches is None else 1
    cos, sin = _rope_tables(pos)
    if caches is not None:
        cos, sin = jnp.tile(cos, (b, 1)), jnp.tile(sin, (b, 1))
    rows_per_seq = t if caches is None else m
    x = x.reshape(m, d)
    new_c, new_s, new_w, new_conv = [], [], [], []

    def mod_rows(layer, k):
        v = mod[layer, :, k, :]
        if caches is not None:
            v = jnp.repeat(v, t, axis=0)
        return v[:, None, :]

    def ffn(x, layer, which, k0):
        return _ffn(x, prm["norm_g"][layer, which * 2].reshape(1, d), mod_rows(layer, k0), mod_rows(layer, k0 + 1),
                    mod_rows(layer, k0 + 2), prm["ffn_w_gate"], prm["ffn_w_up"], prm["ffn_w_down"],
                    layer=layer, which=which, rows_per_mod=rpm, tm=tiles["tm_ffn"], tf=tiles["tf"])

    for i in range(depth):
        x = ffn(x, i, 0, 0)
        ng = prm["norm_g"][i, 1].reshape(1, d)
        shift, scale, gate = mod_rows(i, 3), mod_rows(i, 4), mod_rows(i, 5)
        a = i // 2
        if i % 2 == 0:
            w_in = prm["attn_w_in"]
            common = dict(rows_per_mod=rpm, rows_per_seq=rows_per_seq, tm=tm)
            q_dtype = MXU_DTYPE if caches is None else F32
            q, = _proj(x, ng, shift, scale, w_in, cos, sin, layer=a, mode="q", col0=0, n=Q_WIDTH,
                       out_dtypes=[q_dtype], tn=512, **common)
            kv_outs = [F32, MXU_DTYPE] if caches is None else [F32]
            kv = _proj(x, ng, shift, scale, w_in, cos, sin, layer=a, mode="kv", col0=Q_WIDTH, n=3 * KV_WIDTH,
                       out_dtypes=kv_outs, tn=KV_WIDTH, **common)
            gates, = _proj(x, ng, shift, scale, _group_gate_weights(prm["attn_gate_w"][a])[None], cos, sin, layer=0,
                           mode="sigmoid", col0=0, n=N_KV_HEADS * LANES, out_dtypes=[F32],
                           tn=N_KV_HEADS * LANES, **common)
            kv32 = kv[0]
            new_c.append(kv32[0].reshape(b, t, N_KV_HEADS, 2, HEAD_DIM))
            new_s.append(kv32[1].reshape(b, t, N_KV_HEADS, 2, HEAD_DIM))
            if caches is None:
                ckv = _compress_rows(kv32[0], prm["cmp_pe"], prm["cmp_w1"], prm["cmp_w2"], layer=a, n_seq=b)
                o = _attn_prompt(q.reshape(b, t, Q_WIDTH), ckv.reshape(b, t // CMP_BLOCK, KV_WIDTH),
                                 kv[1].reshape(3, b, t, KV_WIDTH), gates.reshape(b, t, N_KV_HEADS * LANES),
                                 tq=tiles["tq"], tk=tiles["tk"]).reshape(m, Q_WIDTH)
                new_w.append(kv32[2].reshape(b, t, N_KV_HEADS, 2, HEAD_DIM)[:, t - min(WINDOW, t):])
            else:
                pool_c, pool_s, win_buf, _, page_table = caches
                n_layers, n_pool, page = pool_c.shape[:3]
                past = page_table.shape[1] * page
                assert t < CMP_BLOCK and page % SEL_BLOCK == 0
                ckv = _compress_pages(pool_c.reshape(n_layers, n_pool, page * (KV_WIDTH // HEAD_DIM), HEAD_DIM),
                                      page_table, prm["cmp_pe"], prm["cmp_w1"], prm["cmp_w2"], layer=a)
                q3 = q.reshape(b, t, Q_WIDTH)
                o_c, idx = _attn_cmp_sample(q3, ckv, past=past, n_total=past + t)
                t_pad = -(-t // SUBLANES) * SUBLANES
                pad_rows = lambda r: jnp.pad(r.reshape(b, t, KV_WIDTH), ((0, 0), (0, t_pad - t), (0, 0)))
                n_buf = win_buf.shape[2]
                o_w = _attn_win_sample(q3, win_buf.reshape(n_layers, b, n_buf * (KV_WIDTH // HEAD_DIM), HEAD_DIM),
                                       pad_rows(kv32[2]), layer=a, t_new=t)
                n_take = min(N_SELECT, -(-(past + t) // SEL_BLOCK))
                idx = jnp.transpose(idx[..., :n_take], (0, 2, 1, 3))
                heads = lambda v: v.reshape(b, t, N_KV_HEADS, GROUP_SIZE, HEAD_DIM)
                g5 = gates.reshape(b, t, N_KV_HEADS, LANES)[..., :GATE_COLS].reshape(b, t, N_KV_HEADS, GROUP_SIZE, 3)
                o = _attn_sel_sample(idx, page_table, heads(q3), pool_s, pad_rows(kv32[1]), heads(o_c), heads(o_w),
                                     g5, layer=a, past=past, t_new=t).reshape(m, Q_WIDTH)
                keys = jnp.concatenate([win_buf[a], kv32[2].reshape(b, t, N_KV_HEADS, 2, HEAD_DIM)], axis=1)
                new_w.append(keys[:, n_buf + t - min(WINDOW, n_buf + t):])
            x = _linres(o, prm["attn_w_out"], x, gate, layer=a, rows_per_mod=rpm, tm=tm, tn=tiles["tn_out"])
        else:
            glu = _glu_proj(x, ng, shift, scale, prm["conv_w_pw1"], layer=a, rows_per_mod=rpm, tm=tm, tn=512)
            glu = glu.reshape(b, t, d)
            width = prm["conv_w_dw"].shape[1]
            halo = 32
            if caches is None:
                buf = jnp.zeros((b, width - 1, d), F32)
                t_conv = t
                cur = glu
            else:
                buf = caches[3][a]
                t_conv = halo
                cur = jnp.pad(glu, ((0, 0), (0, t_conv - t), (0, 0)))
            if t >= width - 1:
                new_conv.append(glu[:, t - (width - 1):])
            else:
                new_conv.append(jnp.concatenate([buf[:, t:], glu], axis=1))
            buf = jnp.pad(buf, ((0, 0), (halo - (width - 1), 0), (0, 0)))
            vec = lambda v: v.reshape(v.shape[0], 1, d)
            act = _dwconv(cur, buf, prm["conv_w_dw"], vec(prm["conv_b_dw"]), vec(prm["conv_ln_g"]),
                          vec(prm["conv_ln_b"]), layer=a, tt=min(tiles["tt"], t_conv))
            act = act[:, :t].reshape(m, d)
            x = _linres(act, prm["conv_w_pw2"], x, gate, layer=a, rows_per_mod=rpm, tm=tm, tn=tiles["tn_out"])
        x = ffn(x, i, 1, 6)
    y = _rms_norm(x, prm["final_norm_g"], tm).reshape(b, t, d)
    return y, jnp.stack(new_c), jnp.stack(new_s), jnp.stack(new_w), jnp.stack(new_conv)


PROMPT_TILES = dict(tm=1024, tm_ffn=1024, tf=512, tq=256, tk=512, tn_out=512, tt=128)
SAMPLE_TILES = dict(tm=32, tm_ffn=32, tf=512, tn_out=512, tt=32)


def kernel(x_prompt, x_sample, cache_cmp_kv, cache_sel_kv, cache_win_kv, state_conv, page_table, c_prompt, c_sample, w_mod, b_mod, norm_g, ffn_w_gate, ffn_w_up, ffn_w_down, attn_w_in, attn_w_out, cmp_pe, cmp_w1, cmp_w2, conv_w_pw1, conv_w_dw, conv_b_dw, conv_ln_g, conv_ln_b, conv_w_pw2, final_norm_g):
    mxu = lambda w: w.astype(MXU_DTYPE)
    prm = {"w_mod": w_mod, "norm_g": norm_g, "ffn_w_gate": mxu(ffn_w_gate), "ffn_w_up": mxu(ffn_w_up),
           "ffn_w_down": mxu(ffn_w_down), "attn_w_in": mxu(attn_w_in), "attn_gate_w": attn_w_in[..., -N_HEADS * 3:],
           "attn_w_out": mxu(attn_w_out), "cmp_pe": cmp_pe,
           "cmp_w1": cmp_w1, "cmp_w2": cmp_w2, "conv_w_pw1": mxu(conv_w_pw1), "conv_w_dw": conv_w_dw,
           "conv_b_dw": conv_b_dw, "conv_ln_g": conv_ln_g, "conv_ln_b": conv_ln_b, "conv_w_pw2": mxu(conv_w_pw2),
           "final_norm_g": final_norm_g}
    depth, d, _ = w_mod.shape
    bp, tp = x_prompt.shape[:2]
    bs, ts = x_sample.shape[:2]
    past = page_table.shape[1] * cache_cmp_kv.shape[2]

    n_req = bp + bs
    r_pad = -(-n_req // 8) * 8
    c_all = jnp.pad(jnp.concatenate([c_prompt, c_sample], axis=0), ((0, r_pad - n_req), (0, 0)))
    mod = _mod_vectors(c_all, w_mod, b_mod).reshape(depth, r_pad, N_MOD, d)

    pos_p = jnp.arange(tp, dtype=jnp.int32)
    pos_s = past + jnp.arange(ts, dtype=jnp.int32)
    y_p, p_cmp, p_sel, p_win, p_conv = _run_trunk(x_prompt, mod[:, :bp], pos_p, prm, None, PROMPT_TILES)
    y_s, s_cmp, s_sel, s_win, s_conv = _run_trunk(
        x_sample, mod[:, bp:n_req], pos_s, prm,
        (cache_cmp_kv, cache_sel_kv, cache_win_kv, state_conv, page_table), SAMPLE_TILES)
    return (y_p, y_s, p_cmp, p_sel, p_win, p_conv, s_cmp, s_sel, s_win, s_conv)
```

```python
import functools

import jax
import jax.numpy as jnp
from jax import lax
from jax.experimental import pallas as pl
from jax.experimental.pallas import tpu as pltpu

F32 = jnp.float32
MXU_DTYPE = jnp.bfloat16
VMEM_LIMIT_BYTES = 56 * 1024 * 1024
LANES = 128
SUBLANES = 8

N_HEADS = 16
HEAD_DIM = 128
N_KV_HEADS = 4
GROUP_SIZE = N_HEADS // N_KV_HEADS
CMP_BLOCK = 32
SEL_BLOCK = 64
SEL_RATIO = SEL_BLOCK // CMP_BLOCK
N_SELECT = 16
WINDOW = 512
ROPE_THETA = 10000.0
N_MOD = 9
RMS_EPS = 1e-6
LN_EPS = 1e-5
NEG_INF = -1e30
FORCED_SCORE = 1e9
KNOCKED_OUT = -3e38
ATTN_SCALE = HEAD_DIM ** -0.5
Q_WIDTH = N_HEADS * HEAD_DIM
KV_WIDTH = 2 * N_KV_HEADS * HEAD_DIM
GATE_COLS = GROUP_SIZE * 3
PAGES_PER_STEP = 16
DWCONV_COLS = 128
ROW_BLOCK = 32
FFN_PANEL = 256
EXP2_SCALE = ATTN_SCALE * 1.4426950408889634


def _params(*sem):
    return pltpu.CompilerParams(dimension_semantics=sem, vmem_limit_bytes=VMEM_LIMIT_BYTES)


def _mxu(a, b):
    return jnp.dot(a.astype(MXU_DTYPE), b.astype(MXU_DTYPE), preferred_element_type=F32)


def _mxu_nt(a, b):
    return lax.dot_general(a.astype(MXU_DTYPE), b.astype(MXU_DTYPE),
                           (((1,), (1,)), ((), ())), preferred_element_type=F32)


def _silu(x):
    return x * jax.nn.sigmoid(x)


def _softmax_rows(s):
    e = jnp.exp(s - jnp.max(s, axis=-1, keepdims=True))
    return e / jnp.sum(e, axis=-1, keepdims=True)


def _softmax_scaled(s):
    e = jnp.exp2((s - jnp.max(s, axis=-1, keepdims=True)) * EXP2_SCALE)
    return e / jnp.sum(e, axis=-1, keepdims=True)


def _mod_spec(tm, rows_per_mod, d, grid_rank):
    if rows_per_mod >= tm:
        assert rows_per_mod % tm == 0
        block, idx = (1, 1, d), (lambda i: (i * tm) // rows_per_mod)
    else:
        assert rows_per_mod == 1
        block, idx = (tm, 1, d), (lambda i: i)
    if grid_rank == 1:
        return pl.BlockSpec(block, lambda i: (idx(i), 0, 0))
    return pl.BlockSpec(block, lambda i, j: (idx(i), 0, 0))


def _modulated(x_ref, ng_ref, sh_ref, sc_ref):
    x = x_ref[...]
    y = x * lax.rsqrt(jnp.mean(x * x, axis=-1, keepdims=True) + RMS_EPS) * ng_ref[...]
    return y * (1.0 + sc_ref[:, 0, :]) + sh_ref[:, 0, :]


def _mod_kernel(c_ref, w_ref, b_ref, o_ref):
    o_ref[...] = _mxu(_silu(c_ref[...]), w_ref[...]) + b_ref[...]


def _mod_vectors(c_all, w_mod, b_mod, tn=1024):
    depth, d, n = w_mod.shape
    r = c_all.shape[0]
    return pl.pallas_call(
        _mod_kernel,
        grid=(depth, n // tn),
        in_specs=[pl.BlockSpec((r, d), lambda l, j: (0, 0)),
                  pl.BlockSpec((None, d, tn), lambda l, j: (l, 0, j)),
                  pl.BlockSpec((None, 1, tn), lambda l, j: (l, 0, j))],
        out_specs=pl.BlockSpec((None, r, tn), lambda l, j: (l, 0, j)),
        out_shape=jax.ShapeDtypeStruct((depth, r, n), F32),
        compiler_params=_params("arbitrary", "arbitrary"),
        name="mod_vectors",
    )(c_all, w_mod, b_mod.reshape(depth, 1, n))


def _ffn_kernel(x_ref, ng_ref, sh_ref, sc_ref, gt_ref, wg_ref, wu_ref, wd_ref, o_ref, h_scr):
    f = pl.program_id(1)

    @pl.when(f == 0)
    def _():
        h_scr[...] = _modulated(x_ref, ng_ref, sh_ref, sc_ref).astype(h_scr.dtype)
        o_ref[...] = jnp.zeros_like(o_ref)

    panel = min(FFN_PANEL, h_scr.shape[0])
    for p0 in range(0, h_scr.shape[0], panel):
        rows = slice(p0, p0 + panel)
        h = h_scr[rows, :]
        g = _mxu(h, wg_ref[...])
        u = _mxu(h, wu_ref[...])
        a = jnp.concatenate([(_silu(g[r0:r0 + ROW_BLOCK]) * u[r0:r0 + ROW_BLOCK]).astype(MXU_DTYPE)
                             for r0 in range(0, panel, ROW_BLOCK)], axis=0)
        o_ref[rows, :] += _mxu(a, wd_ref[...])

    @pl.when(f == pl.num_programs(1) - 1)
    def _():
        o_ref[...] = x_ref[...] + (0.5 * gt_ref[:, 0, :]) * o_ref[...]


def _ffn(x, ng, shift, scale, gate, w_gate, w_up, w_down, *, layer, which, rows_per_mod, tm, tf):
    m, d = x.shape
    dff = w_gate.shape[-1]
    mod = _mod_spec(tm, rows_per_mod, d, 2)
    return pl.pallas_call(
        _ffn_kernel,
        grid=(m // tm, dff // tf),
        in_specs=[pl.BlockSpec((tm, d), lambda i, f: (i, 0), pipeline_mode=pl.Buffered(1)),
                  pl.BlockSpec((1, d), lambda i, f: (0, 0)),
                  mod, mod, mod,
                  pl.BlockSpec((None, None, d, tf), lambda i, f: (layer, which, 0, f)),
                  pl.BlockSpec((None, None, d, tf), lambda i, f: (layer, which, 0, f)),
                  pl.BlockSpec((None, None, tf, d), lambda i, f: (layer, which, f, 0))],
        out_specs=pl.BlockSpec((tm, d), lambda i, f: (i, 0)),
        out_shape=jax.ShapeDtypeStruct((m, d), F32),
        scratch_shapes=[pltpu.VMEM((tm, d), MXU_DTYPE)],
        compiler_params=_params("arbitrary", "arbitrary"),
        name="ffn",
    )(x, ng, shift, scale, gate, w_gate, w_up, w_down)


def _rope_chunk(z, cos, sin):
    return z * cos + pltpu.roll(z, HEAD_DIM // 2, 1) * sin


def _proj_kernel(x_ref, ng_ref, sh_ref, sc_ref, w_ref, cos_ref, sin_ref, *rest, mode):
    out_refs, h_scr = rest[:-1], rest[-1]

    @pl.when(pl.program_id(1) == 0)
    def _():
        h_scr[...] = _modulated(x_ref, ng_ref, sh_ref, sc_ref).astype(h_scr.dtype)

    z = _mxu(h_scr[...], w_ref[...])
    if mode == "sigmoid":
        out_refs[0][...] = jax.nn.sigmoid(z)
        return
    cos, sin = cos_ref[...], sin_ref[...]
    n_chunks = z.shape[1] // HEAD_DIM
    for c in range(n_chunks):
        zc = z[:, c * HEAD_DIM:(c + 1) * HEAD_DIM]
        if mode == "q" or c % 2 == 0:
            zc = _rope_chunk(zc, cos, sin)
        for k, o in enumerate(out_refs):
            if mode == "kv" and k == 0:
                o[pl.ds(c, zc.shape[0], stride=n_chunks), :] = zc
            else:
                o[:, c * HEAD_DIM:(c + 1) * HEAD_DIM] = zc.astype(o.dtype)


def _proj(x, ng, shift, scale, w, cos, sin, *, layer, mode, col0, n, out_dtypes, rows_per_mod, rows_per_seq,
          tm, tn):
    m, d = x.shape
    mod = _mod_spec(tm, rows_per_mod, d, 2)
    seq_tiles = rows_per_seq // tm
    assert col0 % tn == 0 and n % tn == 0 and rows_per_seq % tm == 0
    if mode == "kv":
        assert tn == KV_WIDTH and out_dtypes[0] == F32
        n_chunks = KV_WIDTH // HEAD_DIM
        out_specs = [pl.BlockSpec((None, tm * n_chunks, HEAD_DIM), lambda i, j: (j, i, 0))] + [
            pl.BlockSpec((None, tm, tn), lambda i, j: (j, i, 0)) for _ in out_dtypes[1:]]
        out_shape = [jax.ShapeDtypeStruct((n // KV_WIDTH, m * n_chunks, HEAD_DIM), F32)] + [
            jax.ShapeDtypeStruct((n // KV_WIDTH, m, KV_WIDTH), dt) for dt in out_dtypes[1:]]
    else:
        out_specs = [pl.BlockSpec((tm, tn), lambda i, j: (i, j)) for _ in out_dtypes]
        out_shape = [jax.ShapeDtypeStruct((m, n), dt) for dt in out_dtypes]
    return pl.pallas_call(
        functools.partial(_proj_kernel, mode=mode),
        grid=(m // tm, n // tn),
        in_specs=[pl.BlockSpec((tm, d), lambda i, j: (i, 0)),
                  pl.BlockSpec((1, d), lambda i, j: (0, 0)),
                  mod, mod,
                  pl.BlockSpec((None, d, tn), lambda i, j: (layer, 0, col0 // tn + j)),
                  pl.BlockSpec((tm, HEAD_DIM), lambda i, j: (i % seq_tiles, 0)),
                  pl.BlockSpec((tm, HEAD_DIM), lambda i, j: (i % seq_tiles, 0))],
        out_specs=out_specs,
        out_shape=out_shape,
        scratch_shapes=[pltpu.VMEM((tm, d), MXU_DTYPE)],
        compiler_params=_params("arbitrary", "arbitrary"),
        name="proj_" + mode,
    )(x, ng, shift, scale, w, cos, sin)


def _glu_kernel(x_ref, ng_ref, sh_ref, sc_ref, wa_ref, wb_ref, o_ref, h_scr):
    @pl.when(pl.program_id(1) == 0)
    def _():
        h_scr[...] = _modulated(x_ref, ng_ref, sh_ref, sc_ref).astype(h_scr.dtype)

    h = h_scr[...]
    o_ref[...] = _mxu(h, wa_ref[...]) * jax.nn.sigmoid(_mxu(h, wb_ref[...]))


def _glu_proj(x, ng, shift, scale, w_pw1, *, layer, rows_per_mod, tm, tn):
    m, d = x.shape
    dc = w_pw1.shape[-1] // 2
    mod = _mod_spec(tm, rows_per_mod, d, 2)
    return pl.pallas_call(
        _glu_kernel,
        grid=(m // tm, dc // tn),
        in_specs=[pl.BlockSpec((tm, d), lambda i, j: (i, 0)),
                  pl.BlockSpec((1, d), lambda i, j: (0, 0)),
                  mod, mod,
                  pl.BlockSpec((None, d, tn), lambda i, j: (layer, 0, j)),
                  pl.BlockSpec((None, d, tn), lambda i, j: (layer, 0, dc // tn + j))],
        out_specs=pl.BlockSpec((tm, tn), lambda i, j: (i, j)),
        out_shape=jax.ShapeDtypeStruct((m, dc), F32),
        scratch_shapes=[pltpu.VMEM((tm, d), MXU_DTYPE)],
        compiler_params=_params("arbitrary", "arbitrary"),
        name="glu_proj",
    )(x, ng, shift, scale, w_pw1, w_pw1)


def _linres_kernel(a_ref, w_ref, x_ref, gt_ref, o_ref):
    o_ref[...] = x_ref[...] + gt_ref[:, 0, :] * _mxu(a_ref[...], w_ref[...])


def _linres(a, w, x, gate, *, layer, rows_per_mod, tm, tn):
    m, k = a.shape
    d = x.shape[1]
    if rows_per_mod >= tm:
        gspec = pl.BlockSpec((1, 1, tn), lambda i, j: ((i * tm) // rows_per_mod, 0, j))
    else:
        gspec = pl.BlockSpec((tm, 1, tn), lambda i, j: (i, 0, j))
    return pl.pallas_call(
        _linres_kernel,
        grid=(m // tm, d // tn),
        in_specs=[pl.BlockSpec((tm, k), lambda i, j: (i, 0)),
                  pl.BlockSpec((None, k, tn), lambda i, j: (layer, 0, j)),
                  pl.BlockSpec((tm, tn), lambda i, j: (i, j)),
                  gspec],
        out_specs=pl.BlockSpec((tm, tn), lambda i, j: (i, j)),
        out_shape=jax.ShapeDtypeStruct((m, d), F32),
        compiler_params=_params("arbitrary", "arbitrary"),
        name="linres",
    )(a, w, x, gate)


def _rms_kernel(x_ref, g_ref, o_ref):
    x = x_ref[...]
    o_ref[...] = x * lax.rsqrt(jnp.mean(x * x, axis=-1, keepdims=True) + RMS_EPS) * g_ref[...]


def _rms_norm(x, g, tm):
    m, d = x.shape
    return pl.pallas_call(
        _rms_kernel,
        grid=(m // tm,),
        in_specs=[pl.BlockSpec((tm, d), lambda i: (i, 0)), pl.BlockSpec((1, d), lambda i: (0, 0))],
        out_specs=pl.BlockSpec((tm, d), lambda i: (i, 0)),
        out_shape=jax.ShapeDtypeStruct((m, d), F32),
        compiler_params=_params("arbitrary"),
        name="final_norm",
    )(x, g.reshape(1, d))


def _compress_kernel(x_ref, pe_ref, w1_ref, w2_ref, o_ref):
    n_chunks = KV_WIDTH // HEAD_DIM
    nb = x_ref.shape[0] // (CMP_BLOCK * n_chunks)
    for kv in range(2):
        acc = jnp.zeros((N_KV_HEADS * nb, HEAD_DIM), F32)
        for c in range(CMP_BLOCK):
            xc = jnp.concatenate([x_ref[pl.ds(c * n_chunks + 2 * g + kv, nb, stride=CMP_BLOCK * n_chunks), :]
                                  for g in range(N_KV_HEADS)], axis=0) + pe_ref[kv, c:c + 1, :]
            acc += _mxu(xc, w1_ref[kv, c * HEAD_DIM:(c + 1) * HEAD_DIM, :])
        y = _mxu(_silu(acc), w2_ref[kv])
        for g in range(N_KV_HEADS):
            o_ref[:, (2 * g + kv) * HEAD_DIM:(2 * g + kv + 1) * HEAD_DIM] = y[g * nb:(g + 1) * nb].astype(o_ref.dtype)


def _compress_rows(rows, pe, w1, w2, *, layer, n_seq):
    n_chunks = KV_WIDTH // HEAD_DIM
    n_blocks = rows.shape[0] // (CMP_BLOCK * n_chunks)
    nb = n_blocks // n_seq
    return pl.pallas_call(
        _compress_kernel,
        grid=(n_seq,),
        in_specs=[pl.BlockSpec((nb * CMP_BLOCK * n_chunks, HEAD_DIM), lambda i: (i, 0)),
                  pl.BlockSpec((None, 2, CMP_BLOCK, HEAD_DIM), lambda i: (layer, 0, 0, 0)),
                  pl.BlockSpec((None, 2, CMP_BLOCK * HEAD_DIM, HEAD_DIM), lambda i: (layer, 0, 0, 0)),
                  pl.BlockSpec((None, 2, HEAD_DIM, HEAD_DIM), lambda i: (layer, 0, 0, 0))],
        out_specs=pl.BlockSpec((nb, KV_WIDTH), lambda i: (i, 0)),
        out_shape=jax.ShapeDtypeStruct((n_blocks, KV_WIDTH), MXU_DTYPE),
        compiler_params=_params("arbitrary"),
        name="compress_rows",
    )(rows, pe, w1, w2)


def _compress_pages_kernel(pt_ref, *refs, blocks_per_page):
    del pt_ref
    pages = refs[:PAGES_PER_STEP]
    pe_ref, w1_ref, w2_ref, o_ref, t_scr = refs[PAGES_PER_STEP:]
    nb = PAGES_PER_STEP * blocks_per_page
    pages_per_group = SUBLANES // blocks_per_page
    n_groups = PAGES_PER_STEP // pages_per_group
    rows = SUBLANES * CMP_BLOCK
    out_row = lax.broadcasted_iota(jnp.int32, (rows, rows), 0)
    in_row = lax.broadcasted_iota(jnp.int32, (rows, rows), 1)
    perm = jnp.where(in_row == (out_row % SUBLANES) * CMP_BLOCK + out_row // SUBLANES, 1.0, 0.0)
    pe_rows = jnp.concatenate([pe_ref[kv] for _ in range(N_KV_HEADS) for kv in range(2)], axis=1)
    pe_rows = jnp.concatenate([pe_rows] * SUBLANES, axis=0)
    n_chunks = KV_WIDTH // HEAD_DIM
    page_rows = pages[0].shape[0] // n_chunks
    for gp in range(n_groups):
        x = jnp.concatenate(
            [jnp.concatenate([pages[gp * pages_per_group + k][pl.ds(ch, page_rows, stride=n_chunks), :]
                              for ch in range(n_chunks)], axis=1)
             for k in range(pages_per_group)], axis=0)
        t_scr[gp] = _mxu(perm, x + pe_rows)
    def block_rows(c, kv):
        return jnp.concatenate(
            [t_scr[:, c * SUBLANES:(c + 1) * SUBLANES,
                   (g * 2 + kv) * HEAD_DIM:(g * 2 + kv + 1) * HEAD_DIM].reshape(nb, HEAD_DIM)
             for g in range(N_KV_HEADS)], axis=0)

    for kv in range(2):
        acc = jnp.zeros((N_KV_HEADS * nb, HEAD_DIM), F32)
        for c in range(0, CMP_BLOCK, 2):
            xc = jnp.concatenate([block_rows(c, kv), block_rows(c + 1, kv)], axis=1)
            acc += _mxu(xc, w1_ref[kv, c * HEAD_DIM:(c + 2) * HEAD_DIM, :])
        y = _mxu(_silu(acc), w2_ref[kv])
        for g in range(N_KV_HEADS):
            o_ref[:, (g * 2 + kv) * HEAD_DIM:(g * 2 + kv + 1) * HEAD_DIM] = (
                y[g * nb:(g + 1) * nb].astype(o_ref.dtype))


def _compress_pages(pool, page_table, pe, w1, w2, *, layer):
    width = KV_WIDTH
    n_chunks = width // HEAD_DIM
    page = pool.shape[2] // n_chunks
    b, n_pages = page_table.shape
    bpp = page // CMP_BLOCK
    nb = PAGES_PER_STEP * bpp
    assert n_pages % PAGES_PER_STEP == 0 and page % CMP_BLOCK == 0 and SUBLANES % bpp == 0
    n_groups = nb // SUBLANES

    def page_spec(p):
        return pl.BlockSpec((None, None, page * n_chunks, HEAD_DIM),
                            lambda bi, gi, pt: (layer, pt[bi * n_pages + gi * PAGES_PER_STEP + p], 0, 0))

    grid_spec = pltpu.PrefetchScalarGridSpec(
        num_scalar_prefetch=1,
        grid=(b, n_pages // PAGES_PER_STEP),
        in_specs=[page_spec(p) for p in range(PAGES_PER_STEP)] + [
            pl.BlockSpec((None, 2, CMP_BLOCK, HEAD_DIM), lambda bi, gi, pt: (layer, 0, 0, 0)),
            pl.BlockSpec((None, 2, CMP_BLOCK * HEAD_DIM, HEAD_DIM), lambda bi, gi, pt: (layer, 0, 0, 0)),
            pl.BlockSpec((None, 2, HEAD_DIM, HEAD_DIM), lambda bi, gi, pt: (layer, 0, 0, 0))],
        out_specs=pl.BlockSpec((None, nb, width), lambda bi, gi, pt: (bi, gi, 0)),
        scratch_shapes=[pltpu.VMEM((n_groups, SUBLANES * CMP_BLOCK, width), F32)],
    )
    return pl.pallas_call(
        functools.partial(_compress_pages_kernel, blocks_per_page=bpp),
        grid_spec=grid_spec,
        out_shape=jax.ShapeDtypeStruct((b, n_pages * bpp, width), MXU_DTYPE),
        compiler_params=_params("arbitrary", "arbitrary"),
        name="compress_pages",
    )(page_table.reshape(-1), *([pool] * PAGES_PER_STEP), pe, w1, w2)


def _pair_sums(imp):
    out = []
    for k in range(imp.shape[1] // LANES):
        x = imp[:, k * LANES:(k + 1) * LANES]
        even = lax.broadcasted_iota(jnp.int32, x.shape, 1) % 2 == 0
        out.append(x + jnp.where(even, pltpu.roll(x, LANES - 1, 1), pltpu.roll(x, 1, 1)))
    return out[0] if len(out) == 1 else jnp.concatenate(out, axis=1)


def _block_scores(imp, q_pos, n_sel_blocks):
    lane = lax.broadcasted_iota(jnp.int32, imp.shape, 1)
    blk = lane // SEL_RATIO
    cur = q_pos // SEL_BLOCK
    forced = (blk == 0) | (blk == cur) | (blk == cur - 1)
    valid = blk * SEL_BLOCK <= q_pos
    score = jnp.where(forced, FORCED_SCORE, jnp.where(valid, _pair_sums(imp), -1.0))
    eligible = (lane % SEL_RATIO == 0) & (blk < n_sel_blocks)
    return jnp.where(eligible, score, KNOCKED_OUT)


def _take_top(work):
    lane = lax.broadcasted_iota(jnp.int32, work.shape, 1).astype(F32)
    top = jnp.max(work, axis=-1, keepdims=True)
    pick = jnp.min(jnp.where(work == top, lane, float(work.shape[1])), axis=-1, keepdims=True)
    return lane == pick, pick


def _with_ones(v):
    return jnp.concatenate([v, jnp.ones(v.shape, v.dtype)], axis=1)


def _attn_prompt_kernel(q_ref, ck_ref, cv_ref, ks_ref, vs_ref, kw_ref, vw_ref, g_ref, o_ref, *, tq, tk, seq):
    i = pl.program_id(2)
    n_cmp = ck_ref.shape[0]
    n_sel_blocks = -(-seq // SEL_BLOCK)
    n_take = min(N_SELECT, n_sel_blocks)
    assert SEL_RATIO == 2 and n_cmp * CMP_BLOCK == seq and n_cmp <= LANES
    q = q_ref[...]
    q4 = jnp.concatenate([q[:, h * HEAD_DIM:(h + 1) * HEAD_DIM] for h in range(GROUP_SIZE)], axis=0)
    q_pos = i * tq + lax.broadcasted_iota(jnp.int32, (tq, 1), 0)

    pad = jnp.zeros((LANES - n_cmp, HEAD_DIM), ck_ref.dtype)
    ck = jnp.concatenate([ck_ref[...], pad], axis=0) if n_cmp < LANES else ck_ref[...]
    cv = jnp.concatenate([cv_ref[...], pad], axis=0) if n_cmp < LANES else cv_ref[...]
    lane = lax.broadcasted_iota(jnp.int32, (1, LANES), 1)
    vis = ((lane + 1) * CMP_BLOCK - 1 <= q_pos) & (lane < n_cmp)
    s = _mxu_nt(q4, ck).reshape(GROUP_SIZE, tq, LANES)
    p = jnp.where(vis[None], _softmax_scaled(jnp.where(vis[None], s, NEG_INF)), 0.0)
    o_c = _mxu(p.reshape(GROUP_SIZE * tq, LANES), cv).reshape(GROUP_SIZE, tq, HEAD_DIM)

    def top_blocks():
        q_pos_t = i * tq + lax.broadcasted_iota(jnp.int32, (1, tq), 1)
        row = lax.broadcasted_iota(jnp.int32, (n_cmp, 1), 0)
        vis_t = (row + 1) * CMP_BLOCK - 1 <= q_pos_t
        imp_t = jnp.zeros((n_cmp, tq), F32)
        for h in range(GROUP_SIZE):
            s_t = jnp.where(vis_t, _mxu_nt(ck_ref[...], q[:, h * HEAD_DIM:(h + 1) * HEAD_DIM]), NEG_INF)
            e = jnp.exp2((s_t - jnp.max(s_t, axis=0, keepdims=True)) * EXP2_SCALE)
            imp_t = imp_t + jnp.where(vis_t, e / jnp.sum(e, axis=0, keepdims=True), 0.0)
        pair = imp_t + pltpu.roll(imp_t, n_cmp - 1, 0)
        blk = row // SEL_RATIO
        cur = q_pos_t // SEL_BLOCK
        forced = (blk == 0) | (blk == cur) | (blk == cur - 1)
        score = jnp.where(forced, FORCED_SCORE, jnp.where(blk * SEL_BLOCK <= q_pos_t, pair, -1.0))
        work = jnp.where((row % SEL_RATIO == 0) & (blk < n_sel_blocks), score, KNOCKED_OUT)
        row_f = row.astype(F32)
        sel_t = jnp.zeros((n_cmp, tq), F32)
        for _ in range(n_take):
            top = jnp.max(work, axis=0, keepdims=True)
            pick = jnp.min(jnp.where(work == top, row_f, float(n_cmp)), axis=0, keepdims=True)
            hit = row_f == pick
            work = jnp.where(hit, KNOCKED_OUT, work)
            sel_t = jnp.where(hit, 1.0, sel_t)
        if n_cmp < LANES:
            sel_t = jnp.concatenate([sel_t, jnp.zeros((LANES - n_cmp, tq), F32)], axis=0)
        return sel_t.T

    def all_blocks():
        return jnp.where((lane % SEL_RATIO == 0) & (lane // SEL_RATIO < n_sel_blocks), 1.0,
                         jnp.zeros((tq, LANES), F32))

    sel = lax.cond(((i + 1) * tq - 1) // SEL_BLOCK + 1 <= n_take, all_blocks, top_blocks).astype(MXU_DTYPE)

    def sel_chunk(kc, carry):
        m_i, l_i, acc = carry
        start = pl.multiple_of(kc * tk, tk)
        k_pos = start + lax.broadcasted_iota(jnp.int32, (1, tk), 1)
        expand = lax.broadcasted_iota(jnp.int32, (LANES, 1), 0) == SEL_RATIO * (k_pos // SEL_BLOCK)
        picked = jnp.dot(sel, jnp.where(expand, 1.0, 0.0).astype(MXU_DTYPE), preferred_element_type=F32)
        bias = jnp.where((picked > 0.5) & (k_pos <= q_pos), 0.0, NEG_INF)[None]
        sc = _mxu_nt(q4, ks_ref[pl.ds(start, tk), :]).reshape(GROUP_SIZE, tq, tk) + bias
        m_new = jnp.maximum(m_i, jnp.max(sc, axis=-1, keepdims=True))
        alpha = jnp.exp2((m_i - m_new) * EXP2_SCALE)
        e = jnp.exp2((sc - m_new) * EXP2_SCALE)
        l_new = alpha * l_i + jnp.sum(e, axis=-1, keepdims=True)
        pv = _mxu(e.reshape(GROUP_SIZE * tq, tk), vs_ref[pl.ds(start, tk), :])
        return m_new, l_new, alpha * acc + pv.reshape(GROUP_SIZE, tq, HEAD_DIM)

    init = (jnp.full((GROUP_SIZE, tq, 1), NEG_INF, F32), jnp.zeros((GROUP_SIZE, tq, 1), F32),
            jnp.zeros((GROUP_SIZE, tq, HEAD_DIM), F32))
    _, l_s, acc_s = lax.fori_loop(0, ((i + 1) * tq + tk - 1) // tk, sel_chunk, init)
    o_s = acc_s / l_s

    span = tq + WINDOW
    w_start = pl.multiple_of(jnp.maximum(i * tq - WINDOW, 0), tq)
    dist = q_pos - (w_start + lax.broadcasted_iota(jnp.int32, (1, span), 1))
    bias = jnp.where((dist >= 0) & (dist < WINDOW), 0.0, NEG_INF)[None]
    sw = _mxu_nt(q4, kw_ref[pl.ds(w_start, span), :]).reshape(GROUP_SIZE, tq, span) + bias
    ew = jnp.exp2((sw - jnp.max(sw, axis=-1, keepdims=True)) * EXP2_SCALE)
    pv = _mxu(ew.reshape(GROUP_SIZE * tq, span), _with_ones(vw_ref[pl.ds(w_start, span), :]))
    o_w = (pv[:, :HEAD_DIM] / pv[:, HEAD_DIM:]).reshape(GROUP_SIZE, tq, HEAD_DIM)

    gates = g_ref[...]
    for h in range(GROUP_SIZE):
        o = (gates[:, 3 * h:3 * h + 1] * o_c[h] + gates[:, 3 * h + 1:3 * h + 2] * o_s[h]
             + gates[:, 3 * h + 2:3 * h + 3] * o_w[h])
        o_ref[:, h * HEAD_DIM:(h + 1) * HEAD_DIM] = o.astype(o_ref.dtype)


def _attn_prompt(q, ckv, kv, gates, *, tq, tk):
    b, t, _ = q.shape
    n_cmp = ckv.shape[1]
    assert t % tq == 0 and t % tk == 0 and t >= tq + WINDOW and WINDOW % tq == 0 and tq % ROW_BLOCK == 0
    gw = GROUP_SIZE * HEAD_DIM

    def kv_spec(branch, part):
        return pl.BlockSpec((None, None, t, HEAD_DIM), lambda bi, g, i: (branch, bi, 0, 2 * g + part))

    return pl.pallas_call(
        functools.partial(_attn_prompt_kernel, tq=tq, tk=tk, seq=t),
        grid=(b, N_KV_HEADS, t // tq),
        in_specs=[pl.BlockSpec((None, tq, gw), lambda bi, g, i: (bi, i, g)),
                  pl.BlockSpec((None, n_cmp, HEAD_DIM), lambda bi, g, i: (bi, 0, 2 * g)),
                  pl.BlockSpec((None, n_cmp, HEAD_DIM), lambda bi, g, i: (bi, 0, 2 * g + 1)),
                  kv_spec(1, 0), kv_spec(1, 1), kv_spec(2, 0), kv_spec(2, 1),
                  pl.BlockSpec((None, tq, LANES), lambda bi, g, i: (bi, i, g))],
        out_specs=pl.BlockSpec((None, tq, gw), lambda bi, g, i: (bi, i, g)),
        out_shape=jax.ShapeDtypeStruct((b, t, Q_WIDTH), MXU_DTYPE),
        compiler_params=_params("arbitrary", "arbitrary", "arbitrary"),
        name="attn_prompt",
    )(q, ckv, ckv, kv, kv, kv, kv, gates)


def _attn_cmp_sample_kernel(q_ref, ckv_ref, o_ref, idx_ref, *, past, n_total, width):
    t_rows = q_ref.shape[0]
    n_cmp = ckv_ref.shape[0]
    n_sel_blocks = -(-n_total // SEL_BLOCK)
    n_take = min(N_SELECT, n_sel_blocks)
    assert SEL_RATIO == 2 and n_take <= LANES and SEL_RATIO * n_sel_blocks <= width
    q_pos = past + lax.broadcasted_iota(jnp.int32, (t_rows, 1), 0)
    lane = lax.broadcasted_iota(jnp.int32, (1, width), 1)
    vis = ((lane + 1) * CMP_BLOCK - 1 <= q_pos) & (lane < n_cmp)
    out_lane = lax.broadcasted_iota(jnp.int32, (1, LANES), 1)
    pad = jnp.zeros((width - n_cmp, HEAD_DIM), ckv_ref.dtype)
    for g in range(N_KV_HEADS):
        ck = jnp.concatenate([ckv_ref[:, 2 * g * HEAD_DIM:(2 * g + 1) * HEAD_DIM], pad], axis=0)
        cv = jnp.concatenate([ckv_ref[:, (2 * g + 1) * HEAD_DIM:(2 * g + 2) * HEAD_DIM], pad], axis=0)
        imp = jnp.zeros((t_rows, width), F32)
        for h in range(GROUP_SIZE):
            cols = slice((g * GROUP_SIZE + h) * HEAD_DIM, (g * GROUP_SIZE + h + 1) * HEAD_DIM)
            s = _mxu_nt(q_ref[:, cols], ck) * ATTN_SCALE
            p = jnp.where(vis, _softmax_rows(jnp.where(vis, s, NEG_INF)), 0.0)
            o_ref[:, cols] = _mxu(p, cv)
            imp = imp + p
        work = _block_scores(imp, q_pos, n_sel_blocks)
        picks = jnp.zeros((t_rows, LANES), F32)
        for r in range(n_take):
            hit, pick = _take_top(work)
            work = jnp.where(hit, KNOCKED_OUT, work)
            picks = jnp.where(out_lane == r, pick, picks)
        idx_ref[g] = (picks * (1.0 / SEL_RATIO)).astype(jnp.int32)


def _attn_cmp_sample(q, ckv, *, past, n_total):
    b, t, qw = q.shape
    n_cmp = ckv.shape[1]
    width = -(-max(n_cmp, SEL_RATIO * -(-n_total // SEL_BLOCK)) // LANES) * LANES
    return pl.pallas_call(
        functools.partial(_attn_cmp_sample_kernel, past=past, n_total=n_total, width=width),
        grid=(b,),
        in_specs=[pl.BlockSpec((None, t, qw), lambda bi: (bi, 0, 0)),
                  pl.BlockSpec((None, n_cmp, KV_WIDTH), lambda bi: (bi, 0, 0))],
        out_specs=[pl.BlockSpec((None, t, qw), lambda bi: (bi, 0, 0)),
                   pl.BlockSpec((None, N_KV_HEADS, t, LANES), lambda bi: (bi, 0, 0, 0))],
        out_shape=[jax.ShapeDtypeStruct(q.shape, F32),
                   jax.ShapeDtypeStruct((b, N_KV_HEADS, t, LANES), jnp.int32)],
        compiler_params=_params("arbitrary"),
        name="attn_cmp_sample",
    )(q, ckv)


def _attn_win_sample_kernel(q_ref, buf_ref, new_ref, o_ref, *, t_new):
    t_rows = q_ref.shape[0]
    n_chunks = KV_WIDTH // HEAD_DIM
    n_buf = buf_ref.shape[0] // n_chunks
    t_idx = lax.broadcasted_iota(jnp.int32, (t_rows, 1), 0)
    dist_a = t_idx + n_buf - lax.broadcasted_iota(jnp.int32, (1, n_buf), 1)
    row_b = lax.broadcasted_iota(jnp.int32, (1, new_ref.shape[0]), 1)
    mask_a = (dist_a >= 0) & (dist_a < WINDOW)
    mask_b = (row_b <= t_idx) & (t_idx - row_b < WINDOW) & (row_b < t_new)
    for g in range(N_KV_HEADS):
        k_a = buf_ref[pl.ds(2 * g, n_buf, stride=n_chunks), :]
        v_a = buf_ref[pl.ds(2 * g + 1, n_buf, stride=n_chunks), :]
        k_b = new_ref[:, 2 * g * HEAD_DIM:(2 * g + 1) * HEAD_DIM]
        v_b = new_ref[:, (2 * g + 1) * HEAD_DIM:(2 * g + 2) * HEAD_DIM]
        for h in range(GROUP_SIZE):
            cols = slice((g * GROUP_SIZE + h) * HEAD_DIM, (g * GROUP_SIZE + h + 1) * HEAD_DIM)
            q = q_ref[:, cols]
            s_a = jnp.where(mask_a, _mxu_nt(q, k_a) * ATTN_SCALE, NEG_INF)
            s_b = jnp.where(mask_b, _mxu_nt(q, k_b) * ATTN_SCALE, NEG_INF)
            m = jnp.maximum(jnp.max(s_a, axis=-1, keepdims=True), jnp.max(s_b, axis=-1, keepdims=True))
            e_a, e_b = jnp.exp(s_a - m), jnp.exp(s_b - m)
            l = jnp.sum(e_a, axis=-1, keepdims=True) + jnp.sum(e_b, axis=-1, keepdims=True)
            o_ref[:, cols] = _mxu(e_a / l, v_a) + _mxu(e_b / l, v_b)


def _attn_win_sample(q, win_buf, kv_new, *, layer, t_new):
    b, t, qw = q.shape
    buf_rows = win_buf.shape[2]
    return pl.pallas_call(
        functools.partial(_attn_win_sample_kernel, t_new=t_new),
        grid=(b,),
        in_specs=[pl.BlockSpec((None, t, qw), lambda bi: (bi, 0, 0)),
                  pl.BlockSpec((None, None, buf_rows, HEAD_DIM), lambda bi: (layer, bi, 0, 0)),
                  pl.BlockSpec((None, kv_new.shape[1], KV_WIDTH), lambda bi: (bi, 0, 0))],
        out_specs=pl.BlockSpec((None, t, qw), lambda bi: (bi, 0, 0)),
        out_shape=jax.ShapeDtypeStruct(q.shape, F32),
        compiler_params=_params("arbitrary"),
        name="attn_win_sample",
    )(q, win_buf, kv_new)


def _attn_sel_sample_kernel(idx_ref, pt_ref, q_ref, *refs, past, t_new, n_slots, t_steps):
    del pt_ref
    blocks = refs[:n_slots]
    new_ref, oc_ref, ow_ref, g_ref, o_ref = refs[n_slots:]
    bi, t, gi = pl.program_id(0), pl.program_id(1), pl.program_id(2)
    base = ((bi * t_steps + t) * N_KV_HEADS + gi) * n_slots
    n_past_blocks = past // SEL_BLOCK
    q_pos = past + t
    q = q_ref[...]

    keys = jnp.concatenate([blk[:, 0, :].astype(MXU_DTYPE) for blk in blocks], axis=0)
    vals = jnp.concatenate([blk[:, 1, :].astype(MXU_DTYPE) for blk in blocks], axis=0)
    lane = lax.broadcasted_iota(jnp.int32, (1, n_slots * SEL_BLOCK), 1)
    pos = lane % SEL_BLOCK
    has_new = jnp.int32(0)
    for n in range(n_slots):
        blk = idx_ref[base + n]
        start = jnp.where(blk < n_past_blocks, blk * SEL_BLOCK, q_pos + 1)
        pos = pos + jnp.where(lane // SEL_BLOCK == n, start, 0)
        has_new = has_new | (blk == n_past_blocks).astype(jnp.int32)
    mask_a = pos <= q_pos
    new = new_ref[...]
    row_b = lax.broadcasted_iota(jnp.int32, (1, new.shape[0]), 1)
    mask_b = (row_b <= t) & (row_b < t_new) & (has_new > 0)
    s_a = jnp.where(mask_a, _mxu_nt(q, keys) * ATTN_SCALE, NEG_INF)
    s_b = jnp.where(mask_b, _mxu_nt(q, new[:, :HEAD_DIM]) * ATTN_SCALE, NEG_INF)
    m = jnp.maximum(jnp.max(s_a, axis=-1, keepdims=True), jnp.max(s_b, axis=-1, keepdims=True))
    e_a, e_b = jnp.exp(s_a - m), jnp.exp(s_b - m)
    l = jnp.sum(e_a, axis=-1, keepdims=True) + jnp.sum(e_b, axis=-1, keepdims=True)
    o_s = _mxu(e_a / l, vals) + _mxu(e_b / l, new[:, HEAD_DIM:])
    gates = g_ref[...]
    o_ref[...] = gates[:, 0:1] * oc_ref[...] + gates[:, 1:2] * o_s + gates[:, 2:3] * ow_ref[...]


def _attn_sel_sample(idx, page_table, q, pool, kv_new, o_c, o_w, gates, *, layer, past, t_new):
    b, t, g, hg, hd = q.shape
    n_slots = idx.shape[-1]
    n_pages = page_table.shape[1]
    page = past // n_pages
    per_page = page // SEL_BLOCK
    n_past_blocks = past // SEL_BLOCK

    def slot_spec(n):
        def index(bi, ti, gi, idx_s, pt_s):
            blk = jnp.minimum(idx_s[((bi * t + ti) * g + gi) * n_slots + n], n_past_blocks - 1)
            phys = pt_s[bi * n_pages + blk // per_page]
            return (layer, phys, blk % per_page, gi, 0, 0)
        return pl.BlockSpec((None, None, SEL_BLOCK, None, 2, HEAD_DIM), index)

    head_spec = pl.BlockSpec((None, None, None, hg, hd), lambda bi, ti, gi, idx_s, pt_s: (bi, ti, gi, 0, 0))
    grid_spec = pltpu.PrefetchScalarGridSpec(
        num_scalar_prefetch=2,
        grid=(b, t, g),
        in_specs=[head_spec] + [slot_spec(n) for n in range(n_slots)] + [
            pl.BlockSpec((None, kv_new.shape[1], 2 * HEAD_DIM), lambda bi, ti, gi, idx_s, pt_s: (bi, 0, gi)),
            head_spec, head_spec,
            pl.BlockSpec((None, None, None, hg, 3), lambda bi, ti, gi, idx_s, pt_s: (bi, ti, gi, 0, 0))],
        out_specs=head_spec,
    )
    return pl.pallas_call(
        functools.partial(_attn_sel_sample_kernel, past=past, t_new=t_new, n_slots=n_slots, t_steps=t),
        grid_spec=grid_spec,
        out_shape=jax.ShapeDtypeStruct(q.shape, F32),
        compiler_params=_params("arbitrary", "arbitrary", "arbitrary"),
        name="attn_sel_sample",
    )(idx.reshape(-1), page_table.reshape(-1), q, *([pool] * n_slots), kv_new, o_c, o_w, gates)


def _dwconv_kernel(cur_ref, prev_ref, buf_ref, w_ref, b_ref, lg_ref, lb_ref, o_ref, full_scr, y_scr, *,
                   tt, halo, width):
    i = pl.program_id(1)
    lead = halo - (width - 1)
    full_scr[0:halo, :] = prev_ref[...]

    @pl.when(i == 0)
    def _():
        full_scr[0:halo, :] = buf_ref[...]

    full_scr[halo:halo + tt, :] = cur_ref[...]
    for c0 in range(0, cur_ref.shape[1], DWCONV_COLS):
        cols = slice(c0, c0 + DWCONV_COLS)
        acc = jnp.zeros((tt, DWCONV_COLS), F32) + b_ref[:, cols]
        for r in range(SUBLANES):
            offsets = [o for o in range(r, lead + width, SUBLANES) if o >= lead]
            rows = tt + (SUBLANES if r else 0)
            group = None
            for o in offsets:
                term = full_scr[o - r:o - r + rows, cols] * w_ref[o - lead:o - lead + 1, cols]
                group = term if group is None else group + term
            if group is not None:
                acc = acc + group[r:r + tt]
        y_scr[:, cols] = acc
    y = y_scr[...]
    mu = jnp.mean(y, axis=-1, keepdims=True)
    var = jnp.mean(jnp.square(y - mu), axis=-1, keepdims=True)
    yn = (y - mu) * lax.rsqrt(var + LN_EPS) * lg_ref[...] + lb_ref[...]
    o_ref[...] = _silu(yn).astype(o_ref.dtype)


def _dwconv(glu, buf, w_dw, b_dw, ln_g, ln_b, *, layer, tt):
    b, t, d = glu.shape
    width = w_dw.shape[1]
    halo = buf.shape[1]
    assert tt % halo == 0 and t % tt == 0 and halo >= width - 1
    per = tt // halo

    def vec_spec():
        return pl.BlockSpec((None, 1, d), lambda bi, i: (layer, 0, 0))

    return pl.pallas_call(
        functools.partial(_dwconv_kernel, tt=tt, halo=halo, width=width),
        grid=(b, t // tt),
        in_specs=[pl.BlockSpec((None, tt, d), lambda bi, i: (bi, i, 0)),
                  pl.BlockSpec((None, halo, d), lambda bi, i: (bi, jnp.maximum(i * per - 1, 0), 0)),
                  pl.BlockSpec((None, halo, d), lambda bi, i: (bi, 0, 0)),
                  pl.BlockSpec((None, width, d), lambda bi, i: (layer, 0, 0)),
                  vec_spec(), vec_spec(), vec_spec()],
        out_specs=pl.BlockSpec((None, tt, d), lambda bi, i: (bi, i, 0)),
        out_shape=jax.ShapeDtypeStruct((b, t, d), MXU_DTYPE),
        scratch_shapes=[pltpu.VMEM((halo + tt, d), F32), pltpu.VMEM((tt, d), F32)],
        compiler_params=_params("arbitrary", "arbitrary"),
        name="dwconv",
    )(glu, glu, buf, w_dw, b_dw, ln_g, ln_b)


def _rope_tables(pos):
    half = HEAD_DIM // 2
    inv = ROPE_THETA ** (-jnp.arange(half, dtype=F32) / half)
    ang = pos.astype(F32)[:, None] * inv[None, :]
    cos, sin = jnp.cos(ang), jnp.sin(ang)
    return jnp.concatenate([cos, cos], axis=1), jnp.concatenate([-sin, sin], axis=1)


def _group_gate_weights(w_gate_cols):
    d = w_gate_cols.shape[0]
    wg = w_gate_cols.reshape(d, N_KV_HEADS, GATE_COLS)
    return jnp.pad(wg, ((0, 0), (0, 0), (0, LANES - GATE_COLS))).reshape(d, N_KV_HEADS * LANES)


def _run_trunk(x, mod, pos, prm, caches, tiles):
    b, t, d = x.shape
    m = b * t
    depth = prm["w_mod"].shape[0]
    tm = tiles["tm"]
    rpm = t if caches is None else 1
    cos, sin = _rope_tables(pos)
    if caches is not None:
        cos, sin = jnp.tile(cos, (b, 1)), jnp.tile(sin, (b, 1))
    rows_per_seq = t if caches is None else m
    x = x.reshape(m, d)
    new_c, new_s, new_w, new_conv = [], [], [], []

    def mod_rows(layer, k):
        v = mod[layer, :, k, :]
        if caches is not None:
            v = jnp.repeat(v, t, axis=0)
        return v[:, None, :]

    def ffn(x, layer, which, k0):
        return _ffn(x, prm["norm_g"][layer, which * 2].reshape(1, d), mod_rows(layer, k0), mod_rows(layer, k0 + 1),
                    mod_rows(layer, k0 + 2), prm["ffn_w_gate"], prm["ffn_w_up"], prm["ffn_w_down"],
                    layer=layer, which=which, rows_per_mod=rpm, tm=tiles["tm_ffn"], tf=tiles["tf"])

    for i in range(depth):
        x = ffn(x, i, 0, 0)
        ng = prm["norm_g"][i, 1].reshape(1, d)
        shift, scale, gate = mod_rows(i, 3), mod_rows(i, 4), mod_rows(i, 5)
        a = i // 2
        if i % 2 == 0:
            w_in = prm["attn_w_in"]
            common = dict(rows_per_mod=rpm, rows_per_seq=rows_per_seq, tm=tm)
            q_dtype = MXU_DTYPE if caches is None else F32
            q, = _proj(x, ng, shift, scale, w_in, cos, sin, layer=a, mode="q", col0=0, n=Q_WIDTH,
                       out_dtypes=[q_dtype], tn=512, **common)
            kv_outs = [F32, MXU_DTYPE] if caches is None else [F32]
            kv = _proj(x, ng, shift, scale, w_in, cos, sin, layer=a, mode="kv", col0=Q_WIDTH, n=3 * KV_WIDTH,
                       out_dtypes=kv_outs, tn=KV_WIDTH, **common)
            gates, = _proj(x, ng, shift, scale, _group_gate_weights(prm["attn_gate_w"][a])[None], cos, sin, layer=0,
                           mode="sigmoid", col0=0, n=N_KV_HEADS * LANES, out_dtypes=[F32],
                           tn=N_KV_HEADS * LANES, **common)
            kv32 = kv[0]
            new_c.append(kv32[0].reshape(b, t, N_KV_HEADS, 2, HEAD_DIM))
            new_s.append(kv32[1].reshape(b, t, N_KV_HEADS, 2, HEAD_DIM))
            if caches is None:
                ckv = _compress_rows(kv32[0], prm["cmp_pe"], prm["cmp_w1"], prm["cmp_w2"], layer=a, n_seq=b)
                o = _attn_prompt(q.reshape(b, t, Q_WIDTH), ckv.reshape(b, t // CMP_BLOCK, KV_WIDTH),
                                 kv[1].reshape(3, b, t, KV_WIDTH), gates.reshape(b, t, N_KV_HEADS * LANES),
                                 tq=tiles["tq"], tk=tiles["tk"]).reshape(m, Q_WIDTH)
                new_w.append(kv32[2].reshape(b, t, N_KV_HEADS, 2, HEAD_DIM)[:, t - min(WINDOW, t):])
            else:
                pool_c, pool_s, win_buf, _, page_table = caches
                n_layers, n_pool, page = pool_c.shape[:3]
                past = page_table.shape[1] * page
                assert t < CMP_BLOCK and page % SEL_BLOCK == 0
                ckv = _compress_pages(pool_c.reshape(n_layers, n_pool, page * (KV_WIDTH // HEAD_DIM), HEAD_DIM),
                                      page_table, prm["cmp_pe"], prm["cmp_w1"], prm["cmp_w2"], layer=a)
                q3 = q.reshape(b, t, Q_WIDTH)
                o_c, idx = _attn_cmp_sample(q3, ckv, past=past, n_total=past + t)
                t_pad = -(-t // SUBLANES) * SUBLANES
                pad_rows = lambda r: jnp.pad(r.reshape(b, t, KV_WIDTH), ((0, 0), (0, t_pad - t), (0, 0)))
                n_buf = win_buf.shape[2]
                o_w = _attn_win_sample(q3, win_buf.reshape(n_layers, b, n_buf * (KV_WIDTH // HEAD_DIM), HEAD_DIM),
                                       pad_rows(kv32[2]), layer=a, t_new=t)
                n_take = min(N_SELECT, -(-(past + t) // SEL_BLOCK))
                idx = jnp.transpose(idx[..., :n_take], (0, 2, 1, 3))
                heads = lambda v: v.reshape(b, t, N_KV_HEADS, GROUP_SIZE, HEAD_DIM)
                g5 = gates.reshape(b, t, N_KV_HEADS, LANES)[..., :GATE_COLS].reshape(b, t, N_KV_HEADS, GROUP_SIZE, 3)
                o = _attn_sel_sample(idx, page_table, heads(q3), pool_s, pad_rows(kv32[1]), heads(o_c), heads(o_w),
                                     g5, layer=a, past=past, t_new=t).reshape(m, Q_WIDTH)
                new_w.append(kv32[2].reshape(b, t, N_KV_HEADS, 2, HEAD_DIM))
            x = _linres(o, prm["attn_w_out"], x, gate, layer=a, rows_per_mod=rpm, tm=tm, tn=tiles["tn_out"])
        else:
            glu = _glu_proj(x, ng, shift, scale, prm["conv_w_pw1"], layer=a, rows_per_mod=rpm, tm=tm, tn=512)
            glu = glu.reshape(b, t, d)
            width = prm["conv_w_dw"].shape[1]
            halo = 32
            if caches is None:
                buf = jnp.zeros((b, width - 1, d), F32)
                t_conv = t
                cur = glu
            else:
                buf = caches[3][a]
                t_conv = halo
                cur = jnp.pad(glu, ((0, 0), (0, t_conv - t), (0, 0)))
            if t >= width - 1:
                new_conv.append(glu[:, t - (width - 1):])
            else:
                new_conv.append(jnp.concatenate([buf[:, t:], glu], axis=1))
            buf = jnp.pad(buf, ((0, 0), (halo - (width - 1), 0), (0, 0)))
            vec = lambda v: v.reshape(v.shape[0], 1, d)
            act = _dwconv(cur, buf, prm["conv_w_dw"], vec(prm["conv_b_dw"]), vec(prm["conv_ln_g"]),
                          vec(prm["conv_ln_b"]), layer=a, tt=min(tiles["tt"], t_conv))
            act = act[:, :t].reshape(m, d)
            x = _linres(act, prm["conv_w_pw2"], x, gate, layer=a, rows_per_mod=rpm, tm=tm, tn=tiles["tn_out"])
        x = ffn(x, i, 1, 6)
    y = _rms_norm(x, prm["final_norm_g"], tm).reshape(b, t, d)
    new_w = jnp.stack(new_w)
    if caches is not None:
        keep = min(WINDOW, caches[2].shape[2] + t) - t
        new_w = jnp.concatenate([caches[2][:, :, caches[2].shape[2] - keep:], new_w], axis=2)
    return y, jnp.stack(new_c), jnp.stack(new_s), new_w, jnp.stack(new_conv)


PROMPT_TILES = dict(tm=1024, tm_ffn=1024, tf=512, tq=256, tk=512, tn_out=512, tt=128)
SAMPLE_TILES = dict(tm=32, tm_ffn=32, tf=512, tn_out=512, tt=32)


def kernel(x_prompt, x_sample, cache_cmp_kv, cache_sel_kv, cache_win_kv, state_conv, page_table, c_prompt, c_sample, w_mod, b_mod, norm_g, ffn_w_gate, ffn_w_up, ffn_w_down, attn_w_in, attn_w_out, cmp_pe, cmp_w1, cmp_w2, conv_w_pw1, conv_w_dw, conv_b_dw, conv_ln_g, conv_ln_b, conv_w_pw2, final_norm_g):
    mxu = lambda w: w.astype(MXU_DTYPE)
    prm = {"w_mod": w_mod, "norm_g": norm_g, "ffn_w_gate": mxu(ffn_w_gate), "ffn_w_up": mxu(ffn_w_up),
           "ffn_w_down": mxu(ffn_w_down), "attn_w_in": mxu(attn_w_in), "attn_gate_w": attn_w_in[..., -N_HEADS * 3:],
           "attn_w_out": mxu(attn_w_out), "cmp_pe": cmp_pe,
           "cmp_w1": cmp_w1, "cmp_w2": cmp_w2, "conv_w_pw1": mxu(conv_w_pw1), "conv_w_dw": conv_w_dw,
           "conv_b_dw": conv_b_dw, "conv_ln_g": conv_ln_g, "conv_ln_b": conv_ln_b, "conv_w_pw2": mxu(conv_w_pw2),
           "final_norm_g": final_norm_g}
    depth, d, _ = w_mod.shape
    bp, tp = x_prompt.shape[:2]
    bs, ts = x_sample.shape[:2]
    past = page_table.shape[1] * cache_cmp_kv.shape[2]

    n_req = bp + bs
    r_pad = -(-n_req // 8) * 8
    c_all = jnp.pad(jnp.concatenate([c_prompt, c_sample], axis=0), ((0, r_pad - n_req), (0, 0)))
    mod = _mod_vectors(c_all, w_mod, b_mod).reshape(depth, r_pad, N_MOD, d)

    pos_p = jnp.arange(tp, dtype=jnp.int32)
    pos_s = past + jnp.arange(ts, dtype=jnp.int32)
    y_p, p_cmp, p_sel, p_win, p_conv = _run_trunk(x_prompt, mod[:, :bp], pos_p, prm, None, PROMPT_TILES)
    y_s, s_cmp, s_sel, s_win, s_conv = _run_trunk(
        x_sample, mod[:, bp:n_req], pos_s, prm,
        (cache_cmp_kv, cache_sel_kv, cache_win_kv, state_conv, page_table), SAMPLE_TILES)
    return (y_p, y_s, p_cmp, p_sel, p_win, p_conv, s_cmp, s_sel, s_win, s_conv)
```

```python
import functools

import jax
import jax.numpy as jnp
from jax import lax
from jax.experimental import pallas as pl
from jax.experimental.pallas import tpu as pltpu

F32 = jnp.float32
MXU_DTYPE = jnp.bfloat16
VMEM_LIMIT_BYTES = 60 * 1024 * 1024
LANES = 128
SUBLANES = 8

N_HEADS = 16
HEAD_DIM = 128
N_KV_HEADS = 4
GROUP_SIZE = N_HEADS // N_KV_HEADS
CMP_BLOCK = 32
SEL_BLOCK = 64
SEL_RATIO = SEL_BLOCK // CMP_BLOCK
N_SELECT = 16
WINDOW = 512
ROPE_THETA = 10000.0
N_MOD = 9
RMS_EPS = 1e-6
LN_EPS = 1e-5
NEG_INF = -1e30
FORCED_SCORE = 1e9
KNOCKED_OUT = -3e38
ATTN_SCALE = HEAD_DIM ** -0.5
Q_WIDTH = N_HEADS * HEAD_DIM
KV_WIDTH = 2 * N_KV_HEADS * HEAD_DIM
GATE_COLS = GROUP_SIZE * 3
PAGES_PER_STEP = 16
DWCONV_COLS = 128
ROW_BLOCK = 32
FFN_PANEL = 256
EXP2_SCALE = ATTN_SCALE * 1.4426950408889634


def _params(*sem):
    return pltpu.CompilerParams(dimension_semantics=sem, vmem_limit_bytes=VMEM_LIMIT_BYTES)


def _mxu(a, b):
    return jnp.dot(a.astype(MXU_DTYPE), b.astype(MXU_DTYPE), preferred_element_type=F32)


def _mxu_nt(a, b):
    return lax.dot_general(a.astype(MXU_DTYPE), b.astype(MXU_DTYPE),
                           (((1,), (1,)), ((), ())), preferred_element_type=F32)


def _silu(x):
    return x * jax.nn.sigmoid(x)


def _softmax_rows(s):
    e = jnp.exp(s - jnp.max(s, axis=-1, keepdims=True))
    return e / jnp.sum(e, axis=-1, keepdims=True)


def _softmax_scaled(s):
    e = jnp.exp2((s - jnp.max(s, axis=-1, keepdims=True)) * EXP2_SCALE)
    return e / jnp.sum(e, axis=-1, keepdims=True)


def _mod_spec(tm, rows_per_mod, d, grid_rank):
    if rows_per_mod >= tm:
        assert rows_per_mod % tm == 0
        block, idx = (1, 1, d), (lambda i: (i * tm) // rows_per_mod)
    else:
        assert rows_per_mod == 1
        block, idx = (tm, 1, d), (lambda i: i)
    if grid_rank == 1:
        return pl.BlockSpec(block, lambda i: (idx(i), 0, 0))
    return pl.BlockSpec(block, lambda i, j: (idx(i), 0, 0))


def _modulated(x_ref, ng_ref, sh_ref, sc_ref):
    x = x_ref[...]
    y = x * lax.rsqrt(jnp.mean(x * x, axis=-1, keepdims=True) + RMS_EPS) * ng_ref[...]
    return y * (1.0 + sc_ref[:, 0, :]) + sh_ref[:, 0, :]


def _mod_kernel(c_ref, w_ref, b_ref, o_ref):
    o_ref[...] = _mxu(_silu(c_ref[...]), w_ref[...]) + b_ref[...]


def _mod_vectors(c_all, w_mod, b_mod, tn=1024):
    depth, d, n = w_mod.shape
    r = c_all.shape[0]
    return pl.pallas_call(
        _mod_kernel,
        grid=(depth, n // tn),
        in_specs=[pl.BlockSpec((r, d), lambda l, j: (0, 0)),
                  pl.BlockSpec((None, d, tn), lambda l, j: (l, 0, j)),
                  pl.BlockSpec((None, 1, tn), lambda l, j: (l, 0, j))],
        out_specs=pl.BlockSpec((None, r, tn), lambda l, j: (l, 0, j)),
        out_shape=jax.ShapeDtypeStruct((depth, r, n), F32),
        compiler_params=_params("arbitrary", "arbitrary"),
        name="mod_vectors",
    )(c_all, w_mod, b_mod.reshape(depth, 1, n))


def _ffn_kernel(x_ref, ng_ref, sh_ref, sc_ref, gt_ref, fg_ref, wg_ref, wu_ref, wd_ref, o_ref, h_scr, *, final_norm):
    f = pl.program_id(1)

    @pl.when(f == 0)
    def _():
        h_scr[...] = _modulated(x_ref, ng_ref, sh_ref, sc_ref).astype(h_scr.dtype)
        o_ref[...] = jnp.zeros_like(o_ref)

    panel = min(FFN_PANEL, h_scr.shape[0])
    for p0 in range(0, h_scr.shape[0], panel):
        rows = slice(p0, p0 + panel)
        h = h_scr[rows, :]
        g = _mxu(h, wg_ref[...])
        u = _mxu(h, wu_ref[...])
        a = jnp.concatenate([(_silu(g[r0:r0 + ROW_BLOCK]) * u[r0:r0 + ROW_BLOCK]).astype(MXU_DTYPE)
                             for r0 in range(0, panel, ROW_BLOCK)], axis=0)
        o_ref[rows, :] += _mxu(a, wd_ref[...])

    @pl.when(f == pl.num_programs(1) - 1)
    def _():
        y = x_ref[...] + (0.5 * gt_ref[:, 0, :]) * o_ref[...]
        if final_norm:
            y = y * lax.rsqrt(jnp.mean(y * y, axis=-1, keepdims=True) + RMS_EPS) * fg_ref[...]
        o_ref[...] = y


def _ffn(x, ng, shift, scale, gate, final_g, w_gate, w_up, w_down, *, layer, which, rows_per_mod, tm, tf,
         final_norm):
    m, d = x.shape
    dff = w_gate.shape[-1]
    mod = _mod_spec(tm, rows_per_mod, d, 2)
    return pl.pallas_call(
        functools.partial(_ffn_kernel, final_norm=final_norm),
        grid=(m // tm, dff // tf),
        in_specs=[pl.BlockSpec((tm, d), lambda i, f: (i, 0)),
                  pl.BlockSpec((1, d), lambda i, f: (0, 0)),
                  mod, mod, mod,
                  pl.BlockSpec((1, d), lambda i, f: (0, 0)),
                  pl.BlockSpec((None, None, d, tf), lambda i, f: (layer, which, 0, f)),
                  pl.BlockSpec((None, None, d, tf), lambda i, f: (layer, which, 0, f)),
                  pl.BlockSpec((None, None, tf, d), lambda i, f: (layer, which, f, 0))],
        out_specs=pl.BlockSpec((tm, d), lambda i, f: (i, 0)),
        out_shape=jax.ShapeDtypeStruct((m, d), F32),
        scratch_shapes=[pltpu.VMEM((tm, d), MXU_DTYPE)],
        compiler_params=_params("arbitrary", "arbitrary"),
        name="ffn",
    )(x, ng, shift, scale, gate, final_g, w_gate, w_up, w_down)


def _rope_chunk(z, cos, sin):
    return z * cos + pltpu.roll(z, HEAD_DIM // 2, 1) * sin


def _proj_kernel(x_ref, ng_ref, sh_ref, sc_ref, w_ref, cos_ref, sin_ref, *rest, mode):
    out_refs, h_scr = rest[:-1], rest[-1]

    @pl.when(pl.program_id(1) == 0)
    def _():
        h_scr[...] = _modulated(x_ref, ng_ref, sh_ref, sc_ref).astype(h_scr.dtype)

    z = _mxu(h_scr[...], w_ref[...])
    if mode == "sigmoid":
        out_refs[0][...] = jax.nn.sigmoid(z)
        return
    cos, sin = cos_ref[...], sin_ref[...]
    n_chunks = z.shape[1] // HEAD_DIM
    for c in range(n_chunks):
        zc = z[:, c * HEAD_DIM:(c + 1) * HEAD_DIM]
        if mode == "q" or c % 2 == 0:
            zc = _rope_chunk(zc, cos, sin)
        for k, o in enumerate(out_refs):
            if mode == "kv" and k == 0:
                o[pl.ds(c, zc.shape[0], stride=n_chunks), :] = zc
            else:
                o[:, c * HEAD_DIM:(c + 1) * HEAD_DIM] = zc.astype(o.dtype)


def _proj(x, ng, shift, scale, w, cos, sin, *, layer, mode, col0, n, out_dtypes, rows_per_mod, rows_per_seq,
          tm, tn):
    m, d = x.shape
    mod = _mod_spec(tm, rows_per_mod, d, 2)
    seq_tiles = rows_per_seq // tm
    assert col0 % tn == 0 and n % tn == 0 and rows_per_seq % tm == 0
    if mode == "kv":
        assert tn == KV_WIDTH and out_dtypes[0] == F32
        n_chunks = KV_WIDTH // HEAD_DIM
        out_specs = [pl.BlockSpec((None, tm * n_chunks, HEAD_DIM), lambda i, j: (j, i, 0))] + [
            pl.BlockSpec((None, tm, tn), lambda i, j: (j, i, 0)) for _ in out_dtypes[1:]]
        out_shape = [jax.ShapeDtypeStruct((n // KV_WIDTH, m * n_chunks, HEAD_DIM), F32)] + [
            jax.ShapeDtypeStruct((n // KV_WIDTH, m, KV_WIDTH), dt) for dt in out_dtypes[1:]]
    else:
        out_specs = [pl.BlockSpec((tm, tn), lambda i, j: (i, j)) for _ in out_dtypes]
        out_shape = [jax.ShapeDtypeStruct((m, n), dt) for dt in out_dtypes]
    return pl.pallas_call(
        functools.partial(_proj_kernel, mode=mode),
        grid=(m // tm, n // tn),
        in_specs=[pl.BlockSpec((tm, d), lambda i, j: (i, 0)),
                  pl.BlockSpec((1, d), lambda i, j: (0, 0)),
                  mod, mod,
                  pl.BlockSpec((None, d, tn), lambda i, j: (layer, 0, col0 // tn + j)),
                  pl.BlockSpec((tm, HEAD_DIM), lambda i, j: (i % seq_tiles, 0)),
                  pl.BlockSpec((tm, HEAD_DIM), lambda i, j: (i % seq_tiles, 0))],
        out_specs=out_specs,
        out_shape=out_shape,
        scratch_shapes=[pltpu.VMEM((tm, d), MXU_DTYPE)],
        compiler_params=_params("arbitrary", "arbitrary"),
        name="proj_" + mode,
    )(x, ng, shift, scale, w, cos, sin)


def _glu_kernel(x_ref, ng_ref, sh_ref, sc_ref, wa_ref, wb_ref, o_ref, h_scr):
    @pl.when(pl.program_id(1) == 0)
    def _():
        h_scr[...] = _modulated(x_ref, ng_ref, sh_ref, sc_ref).astype(h_scr.dtype)

    h = h_scr[...]
    o_ref[...] = _mxu(h, wa_ref[...]) * jax.nn.sigmoid(_mxu(h, wb_ref[...]))


def _glu_proj(x, ng, shift, scale, w_pw1, *, layer, rows_per_mod, tm, tn):
    m, d = x.shape
    dc = w_pw1.shape[-1] // 2
    mod = _mod_spec(tm, rows_per_mod, d, 2)
    return pl.pallas_call(
        _glu_kernel,
        grid=(m // tm, dc // tn),
        in_specs=[pl.BlockSpec((tm, d), lambda i, j: (i, 0)),
                  pl.BlockSpec((1, d), lambda i, j: (0, 0)),
                  mod, mod,
                  pl.BlockSpec((None, d, tn), lambda i, j: (layer, 0, j)),
                  pl.BlockSpec((None, d, tn), lambda i, j: (layer, 0, dc // tn + j))],
        out_specs=pl.BlockSpec((tm, tn), lambda i, j: (i, j)),
        out_shape=jax.ShapeDtypeStruct((m, dc), F32),
        scratch_shapes=[pltpu.VMEM((tm, d), MXU_DTYPE)],
        compiler_params=_params("arbitrary", "arbitrary"),
        name="glu_proj",
    )(x, ng, shift, scale, w_pw1, w_pw1)


def _linres_kernel(a_ref, w_ref, x_ref, gt_ref, o_ref):
    o_ref[...] = x_ref[...] + gt_ref[:, 0, :] * _mxu(a_ref[...], w_ref[...])


def _linres(a, w, x, gate, *, layer, rows_per_mod, tm, tn):
    m, k = a.shape
    d = x.shape[1]
    if rows_per_mod >= tm:
        gspec = pl.BlockSpec((1, 1, tn), lambda i, j: ((i * tm) // rows_per_mod, 0, j))
    else:
        gspec = pl.BlockSpec((tm, 1, tn), lambda i, j: (i, 0, j))
    return pl.pallas_call(
        _linres_kernel,
        grid=(m // tm, d // tn),
        in_specs=[pl.BlockSpec((tm, k), lambda i, j: (i, 0)),
                  pl.BlockSpec((None, k, tn), lambda i, j: (layer, 0, j)),
                  pl.BlockSpec((tm, tn), lambda i, j: (i, j)),
                  gspec],
        out_specs=pl.BlockSpec((tm, tn), lambda i, j: (i, j)),
        out_shape=jax.ShapeDtypeStruct((m, d), F32),
        compiler_params=_params("arbitrary", "arbitrary"),
        name="linres",
    )(a, w, x, gate)


def _compress_kernel(x_ref, pe_ref, w1_ref, w2_ref, o_ref):
    n_chunks = KV_WIDTH // HEAD_DIM
    nb = x_ref.shape[0] // (CMP_BLOCK * n_chunks)
    for kv in range(2):
        acc = jnp.zeros((N_KV_HEADS * nb, HEAD_DIM), F32)
        for c in range(CMP_BLOCK):
            xc = jnp.concatenate([x_ref[pl.ds(c * n_chunks + 2 * g + kv, nb, stride=CMP_BLOCK * n_chunks), :]
                                  for g in range(N_KV_HEADS)], axis=0) + pe_ref[kv, c:c + 1, :]
            acc += _mxu(xc, w1_ref[kv, c * HEAD_DIM:(c + 1) * HEAD_DIM, :])
        y = _mxu(_silu(acc), w2_ref[kv])
        for g in range(N_KV_HEADS):
            o_ref[:, (2 * g + kv) * HEAD_DIM:(2 * g + kv + 1) * HEAD_DIM] = y[g * nb:(g + 1) * nb].astype(o_ref.dtype)


def _compress_rows(rows, pe, w1, w2, *, layer, n_seq):
    n_chunks = KV_WIDTH // HEAD_DIM
    n_blocks = rows.shape[0] // (CMP_BLOCK * n_chunks)
    nb = n_blocks // n_seq
    return pl.pallas_call(
        _compress_kernel,
        grid=(n_seq,),
        in_specs=[pl.BlockSpec((nb * CMP_BLOCK * n_chunks, HEAD_DIM), lambda i: (i, 0)),
                  pl.BlockSpec((None, 2, CMP_BLOCK, HEAD_DIM), lambda i: (layer, 0, 0, 0)),
                  pl.BlockSpec((None, 2, CMP_BLOCK * HEAD_DIM, HEAD_DIM), lambda i: (layer, 0, 0, 0)),
                  pl.BlockSpec((None, 2, HEAD_DIM, HEAD_DIM), lambda i: (layer, 0, 0, 0))],
        out_specs=pl.BlockSpec((nb, KV_WIDTH), lambda i: (i, 0)),
        out_shape=jax.ShapeDtypeStruct((n_blocks, KV_WIDTH), MXU_DTYPE),
        compiler_params=_params("arbitrary"),
        name="compress_rows",
    )(rows, pe, w1, w2)


def _compress_pages_kernel(pt_ref, *refs, blocks_per_page):
    del pt_ref
    pages = refs[:PAGES_PER_STEP]
    pe_ref, w1_ref, w2_ref, o_ref, t_scr = refs[PAGES_PER_STEP:]
    nb = PAGES_PER_STEP * blocks_per_page
    pages_per_group = SUBLANES // blocks_per_page
    n_groups = PAGES_PER_STEP // pages_per_group
    rows = SUBLANES * CMP_BLOCK
    out_row = lax.broadcasted_iota(jnp.int32, (rows, rows), 0)
    in_row = lax.broadcasted_iota(jnp.int32, (rows, rows), 1)
    perm = jnp.where(in_row == (out_row % SUBLANES) * CMP_BLOCK + out_row // SUBLANES, 1.0, 0.0)
    pe_rows = jnp.concatenate([pe_ref[kv] for _ in range(N_KV_HEADS) for kv in range(2)], axis=1)
    pe_rows = jnp.concatenate([pe_rows] * SUBLANES, axis=0)
    n_chunks = KV_WIDTH // HEAD_DIM
    page_rows = pages[0].shape[0] // n_chunks
    for gp in range(n_groups):
        x = jnp.concatenate(
            [jnp.concatenate([pages[gp * pages_per_group + k][pl.ds(ch, page_rows, stride=n_chunks), :]
                              for ch in range(n_chunks)], axis=1)
             for k in range(pages_per_group)], axis=0)
        t_scr[gp] = _mxu(perm, x + pe_rows)
    def block_rows(c, kv):
        return jnp.concatenate(
            [t_scr[:, c * SUBLANES:(c + 1) * SUBLANES,
                   (g * 2 + kv) * HEAD_DIM:(g * 2 + kv + 1) * HEAD_DIM].reshape(nb, HEAD_DIM)
             for g in range(N_KV_HEADS)], axis=0)

    for kv in range(2):
        acc = jnp.zeros((N_KV_HEADS * nb, HEAD_DIM), F32)
        for c in range(0, CMP_BLOCK, 2):
            xc = jnp.concatenate([block_rows(c, kv), block_rows(c + 1, kv)], axis=1)
            acc += _mxu(xc, w1_ref[kv, c * HEAD_DIM:(c + 2) * HEAD_DIM, :])
        y = _mxu(_silu(acc), w2_ref[kv])
        for g in range(N_KV_HEADS):
            o_ref[:, (g * 2 + kv) * HEAD_DIM:(g * 2 + kv + 1) * HEAD_DIM] = (
                y[g * nb:(g + 1) * nb].astype(o_ref.dtype))


def _compress_pages(pool, page_table, pe, w1, w2, *, layer):
    width = KV_WIDTH
    n_chunks = width // HEAD_DIM
    page = pool.shape[2] // n_chunks
    b, n_pages = page_table.shape
    bpp = page // CMP_BLOCK
    nb = PAGES_PER_STEP * bpp
    assert n_pages % PAGES_PER_STEP == 0 and page % CMP_BLOCK == 0 and SUBLANES % bpp == 0
    n_groups = nb // SUBLANES

    def page_spec(p):
        return pl.BlockSpec((None, None, page * n_chunks, HEAD_DIM),
                            lambda bi, gi, pt: (layer, pt[bi * n_pages + gi * PAGES_PER_STEP + p], 0, 0))

    grid_spec = pltpu.PrefetchScalarGridSpec(
        num_scalar_prefetch=1,
        grid=(b, n_pages // PAGES_PER_STEP),
        in_specs=[page_spec(p) for p in range(PAGES_PER_STEP)] + [
            pl.BlockSpec((None, 2, CMP_BLOCK, HEAD_DIM), lambda bi, gi, pt: (layer, 0, 0, 0)),
            pl.BlockSpec((None, 2, CMP_BLOCK * HEAD_DIM, HEAD_DIM), lambda bi, gi, pt: (layer, 0, 0, 0)),
            pl.BlockSpec((None, 2, HEAD_DIM, HEAD_DIM), lambda bi, gi, pt: (layer, 0, 0, 0))],
        out_specs=pl.BlockSpec((None, nb, width), lambda bi, gi, pt: (bi, gi, 0)),
        scratch_shapes=[pltpu.VMEM((n_groups, SUBLANES * CMP_BLOCK, width), F32)],
    )
    return pl.pallas_call(
        functools.partial(_compress_pages_kernel, blocks_per_page=bpp),
        grid_spec=grid_spec,
        out_shape=jax.ShapeDtypeStruct((b, n_pages * bpp, width), MXU_DTYPE),
        compiler_params=_params("arbitrary", "arbitrary"),
        name="compress_pages",
    )(page_table.reshape(-1), *([pool] * PAGES_PER_STEP), pe, w1, w2)


def _pair_sums(imp):
    out = []
    for k in range(imp.shape[1] // LANES):
        x = imp[:, k * LANES:(k + 1) * LANES]
        even = lax.broadcasted_iota(jnp.int32, x.shape, 1) % 2 == 0
        out.append(x + jnp.where(even, pltpu.roll(x, LANES - 1, 1), pltpu.roll(x, 1, 1)))
    return out[0] if len(out) == 1 else jnp.concatenate(out, axis=1)


def _block_scores(imp, q_pos, n_sel_blocks):
    lane = lax.broadcasted_iota(jnp.int32, imp.shape, 1)
    blk = lane // SEL_RATIO
    cur = q_pos // SEL_BLOCK
    forced = (blk == 0) | (blk == cur) | (blk == cur - 1)
    valid = blk * SEL_BLOCK <= q_pos
    score = jnp.where(forced, FORCED_SCORE, jnp.where(valid, _pair_sums(imp), -1.0))
    eligible = (lane % SEL_RATIO == 0) & (blk < n_sel_blocks)
    return jnp.where(eligible, score, KNOCKED_OUT)


def _take_top(work):
    lane = lax.broadcasted_iota(jnp.int32, work.shape, 1).astype(F32)
    top = jnp.max(work, axis=-1, keepdims=True)
    pick = jnp.min(jnp.where(work == top, lane, float(work.shape[1])), axis=-1, keepdims=True)
    return lane == pick, pick


def _with_ones(v):
    return jnp.concatenate([v, jnp.ones(v.shape, v.dtype)], axis=1)


def _attn_prompt_kernel(q_ref, ck_ref, cv_ref, ks_ref, vs_ref, kw_ref, vw_ref, g_ref, o_ref, *, tq, tk, seq):
    i = pl.program_id(2)
    n_cmp = ck_ref.shape[0]
    n_sel_blocks = -(-seq // SEL_BLOCK)
    n_take = min(N_SELECT, n_sel_blocks)
    assert SEL_RATIO == 2 and n_cmp * CMP_BLOCK == seq and n_cmp <= LANES
    q = q_ref[...]
    q4 = jnp.concatenate([q[:, h * HEAD_DIM:(h + 1) * HEAD_DIM] for h in range(GROUP_SIZE)], axis=0)
    q_pos = i * tq + lax.broadcasted_iota(jnp.int32, (tq, 1), 0)

    pad = jnp.zeros((LANES - n_cmp, HEAD_DIM), ck_ref.dtype)
    ck = jnp.concatenate([ck_ref[...], pad], axis=0) if n_cmp < LANES else ck_ref[...]
    cv = jnp.concatenate([cv_ref[...], pad], axis=0) if n_cmp < LANES else cv_ref[...]
    lane = lax.broadcasted_iota(jnp.int32, (1, LANES), 1)
    vis = ((lane + 1) * CMP_BLOCK - 1 <= q_pos) & (lane < n_cmp)
    s = _mxu_nt(q4, ck).reshape(GROUP_SIZE, tq, LANES)
    p = jnp.where(vis[None], _softmax_scaled(jnp.where(vis[None], s, NEG_INF)), 0.0)
    o_c = _mxu(p.reshape(GROUP_SIZE * tq, LANES), cv).reshape(GROUP_SIZE, tq, HEAD_DIM)

    def top_blocks():
        q_pos_t = i * tq + lax.broadcasted_iota(jnp.int32, (1, tq), 1)
        row = lax.broadcasted_iota(jnp.int32, (n_cmp, 1), 0)
        vis_t = (row + 1) * CMP_BLOCK - 1 <= q_pos_t
        imp_t = jnp.zeros((n_cmp, tq), F32)
        for h in range(GROUP_SIZE):
            s_t = jnp.where(vis_t, _mxu_nt(ck_ref[...], q[:, h * HEAD_DIM:(h + 1) * HEAD_DIM]), NEG_INF)
            e = jnp.exp2((s_t - jnp.max(s_t, axis=0, keepdims=True)) * EXP2_SCALE)
            imp_t = imp_t + jnp.where(vis_t, e / jnp.sum(e, axis=0, keepdims=True), 0.0)
        pair = imp_t + pltpu.roll(imp_t, n_cmp - 1, 0)
        blk = row // SEL_RATIO
        cur = q_pos_t // SEL_BLOCK
        forced = (blk == 0) | (blk == cur) | (blk == cur - 1)
        score = jnp.where(forced, FORCED_SCORE, jnp.where(blk * SEL_BLOCK <= q_pos_t, pair, -1.0))
        work = jnp.where((row % SEL_RATIO == 0) & (blk < n_sel_blocks), score, KNOCKED_OUT)
        row_f = row.astype(F32)
        sel_t = jnp.zeros((n_cmp, tq), F32)
        for _ in range(n_take):
            top = jnp.max(work, axis=0, keepdims=True)
            pick = jnp.min(jnp.where(work == top, row_f, float(n_cmp)), axis=0, keepdims=True)
            hit = row_f == pick
            work = jnp.where(hit, KNOCKED_OUT, work)
            sel_t = jnp.where(hit, 1.0, sel_t)
        if n_cmp < LANES:
            sel_t = jnp.concatenate([sel_t, jnp.zeros((LANES - n_cmp, tq), F32)], axis=0)
        return sel_t.T

    def all_blocks():
        return jnp.where((lane % SEL_RATIO == 0) & (lane // SEL_RATIO < n_sel_blocks), 1.0,
                         jnp.zeros((tq, LANES), F32))

    sel = lax.cond(((i + 1) * tq - 1) // SEL_BLOCK + 1 <= n_take, all_blocks, top_blocks).astype(MXU_DTYPE)

    def sel_chunk(kc, carry):
        m_i, l_i, acc = carry
        start = pl.multiple_of(kc * tk, tk)
        k_pos = start + lax.broadcasted_iota(jnp.int32, (1, tk), 1)
        expand = lax.broadcasted_iota(jnp.int32, (LANES, 1), 0) == SEL_RATIO * (k_pos // SEL_BLOCK)
        picked = jnp.dot(sel, jnp.where(expand, 1.0, 0.0).astype(MXU_DTYPE), preferred_element_type=F32)
        bias = jnp.where((picked > 0.5) & (k_pos <= q_pos), 0.0, NEG_INF)[None]
        sc = _mxu_nt(q4, ks_ref[pl.ds(start, tk), :]).reshape(GROUP_SIZE, tq, tk) + bias
        m_new = jnp.maximum(m_i, jnp.max(sc, axis=-1, keepdims=True))
        alpha = jnp.exp2((m_i - m_new) * EXP2_SCALE)
        e = jnp.exp2((sc - m_new) * EXP2_SCALE)
        l_new = alpha * l_i + jnp.sum(e, axis=-1, keepdims=True)
        pv = _mxu(e.reshape(GROUP_SIZE * tq, tk), vs_ref[pl.ds(start, tk), :])
        return m_new, l_new, alpha * acc + pv.reshape(GROUP_SIZE, tq, HEAD_DIM)

    init = (jnp.full((GROUP_SIZE, tq, 1), NEG_INF, F32), jnp.zeros((GROUP_SIZE, tq, 1), F32),
            jnp.zeros((GROUP_SIZE, tq, HEAD_DIM), F32))
    _, l_s, acc_s = lax.fori_loop(0, ((i + 1) * tq + tk - 1) // tk, sel_chunk, init)
    o_s = acc_s / l_s

    span = tq + WINDOW
    w_start = pl.multiple_of(jnp.maximum(i * tq - WINDOW, 0), tq)
    dist = q_pos - (w_start + lax.broadcasted_iota(jnp.int32, (1, span), 1))
    bias = jnp.where((dist >= 0) & (dist < WINDOW), 0.0, NEG_INF)[None]
    sw = _mxu_nt(q4, kw_ref[pl.ds(w_start, span), :]).reshape(GROUP_SIZE, tq, span) + bias
    ew = jnp.exp2((sw - jnp.max(sw, axis=-1, keepdims=True)) * EXP2_SCALE)
    pv = _mxu(ew.reshape(GROUP_SIZE * tq, span), _with_ones(vw_ref[pl.ds(w_start, span), :]))
    o_w = (pv[:, :HEAD_DIM] / pv[:, HEAD_DIM:]).reshape(GROUP_SIZE, tq, HEAD_DIM)

    gates = g_ref[...]
    for h in range(GROUP_SIZE):
        o = (gates[:, 3 * h:3 * h + 1] * o_c[h] + gates[:, 3 * h + 1:3 * h + 2] * o_s[h]
             + gates[:, 3 * h + 2:3 * h + 3] * o_w[h])
        o_ref[:, h * HEAD_DIM:(h + 1) * HEAD_DIM] = o.astype(o_ref.dtype)


def _attn_prompt(q, ckv, kv, gates, *, tq, tk):
    b, t, _ = q.shape
    n_cmp = ckv.shape[1]
    assert t % tq == 0 and t % tk == 0 and t >= tq + WINDOW and WINDOW % tq == 0 and tq % ROW_BLOCK == 0
    gw = GROUP_SIZE * HEAD_DIM

    def kv_spec(branch, part):
        return pl.BlockSpec((None, None, t, HEAD_DIM), lambda bi, g, i: (branch, bi, 0, 2 * g + part))

    return pl.pallas_call(
        functools.partial(_attn_prompt_kernel, tq=tq, tk=tk, seq=t),
        grid=(b, N_KV_HEADS, t // tq),
        in_specs=[pl.BlockSpec((None, tq, gw), lambda bi, g, i: (bi, i, g)),
                  pl.BlockSpec((None, n_cmp, HEAD_DIM), lambda bi, g, i: (bi, 0, 2 * g)),
                  pl.BlockSpec((None, n_cmp, HEAD_DIM), lambda bi, g, i: (bi, 0, 2 * g + 1)),
                  kv_spec(1, 0), kv_spec(1, 1), kv_spec(2, 0), kv_spec(2, 1),
                  pl.BlockSpec((None, tq, LANES), lambda bi, g, i: (bi, i, g))],
        out_specs=pl.BlockSpec((None, tq, gw), lambda bi, g, i: (bi, i, g)),
        out_shape=jax.ShapeDtypeStruct((b, t, Q_WIDTH), MXU_DTYPE),
        compiler_params=_params("arbitrary", "arbitrary", "arbitrary"),
        name="attn_prompt",
    )(q, ckv, ckv, kv, kv, kv, kv, gates)


def _attn_cmp_sample_kernel(q_ref, ckv_ref, o_ref, idx_ref, *, past, n_total, width):
    t_rows = q_ref.shape[0]
    n_cmp = ckv_ref.shape[0]
    n_sel_blocks = -(-n_total // SEL_BLOCK)
    n_take = min(N_SELECT, n_sel_blocks)
    assert SEL_RATIO == 2 and n_take <= LANES and SEL_RATIO * n_sel_blocks <= width
    q_pos = past + lax.broadcasted_iota(jnp.int32, (t_rows, 1), 0)
    lane = lax.broadcasted_iota(jnp.int32, (1, width), 1)
    vis = ((lane + 1) * CMP_BLOCK - 1 <= q_pos) & (lane < n_cmp)
    out_lane = lax.broadcasted_iota(jnp.int32, (1, LANES), 1)
    pad = jnp.zeros((width - n_cmp, HEAD_DIM), ckv_ref.dtype)
    imps = []
    for g in range(N_KV_HEADS):
        ck = jnp.concatenate([ckv_ref[:, 2 * g * HEAD_DIM:(2 * g + 1) * HEAD_DIM], pad], axis=0)
        cv = jnp.concatenate([ckv_ref[:, (2 * g + 1) * HEAD_DIM:(2 * g + 2) * HEAD_DIM], pad], axis=0)
        imp = jnp.zeros((t_rows, width), F32)
        for h in range(GROUP_SIZE):
            cols = slice((g * GROUP_SIZE + h) * HEAD_DIM, (g * GROUP_SIZE + h + 1) * HEAD_DIM)
            s = _mxu_nt(q_ref[:, cols], ck) * ATTN_SCALE
            p = jnp.where(vis, _softmax_rows(jnp.where(vis, s, NEG_INF)), 0.0)
            o_ref[:, cols] = _mxu(p, cv)
            imp = imp + p
        imps.append(imp)
    work = _block_scores(jnp.concatenate(imps, axis=0), jnp.concatenate([q_pos] * N_KV_HEADS, axis=0), n_sel_blocks)
    picks = jnp.zeros((N_KV_HEADS * t_rows, LANES), F32)
    for r in range(n_take):
        hit, pick = _take_top(work)
        work = jnp.where(hit, KNOCKED_OUT, work)
        picks = jnp.where(out_lane == r, pick, picks)
    idx = (picks * (1.0 / SEL_RATIO)).astype(jnp.int32)
    for g in range(N_KV_HEADS):
        idx_ref[g] = idx[g * t_rows:(g + 1) * t_rows]


def _attn_cmp_sample(q, ckv, *, past, n_total):
    b, t, qw = q.shape
    n_cmp = ckv.shape[1]
    width = -(-max(n_cmp, SEL_RATIO * -(-n_total // SEL_BLOCK)) // LANES) * LANES
    return pl.pallas_call(
        functools.partial(_attn_cmp_sample_kernel, past=past, n_total=n_total, width=width),
        grid=(b,),
        in_specs=[pl.BlockSpec((None, t, qw), lambda bi: (bi, 0, 0)),
                  pl.BlockSpec((None, n_cmp, KV_WIDTH), lambda bi: (bi, 0, 0))],
        out_specs=[pl.BlockSpec((None, t, qw), lambda bi: (bi, 0, 0)),
                   pl.BlockSpec((None, N_KV_HEADS, t, LANES), lambda bi: (bi, 0, 0, 0))],
        out_shape=[jax.ShapeDtypeStruct(q.shape, F32),
                   jax.ShapeDtypeStruct((b, N_KV_HEADS, t, LANES), jnp.int32)],
        compiler_params=_params("arbitrary"),
        name="attn_cmp_sample",
    )(q, ckv)


def _attn_win_sample_kernel(q_ref, buf_ref, new_ref, o_ref, *, t_new):
    t_rows = q_ref.shape[0]
    n_chunks = KV_WIDTH // HEAD_DIM
    n_buf = buf_ref.shape[0] // n_chunks
    t_idx = lax.broadcasted_iota(jnp.int32, (GROUP_SIZE * t_rows, 1), 0) % t_rows
    dist_a = t_idx + n_buf - lax.broadcasted_iota(jnp.int32, (1, n_buf), 1)
    row_b = lax.broadcasted_iota(jnp.int32, (1, new_ref.shape[0]), 1)
    mask_a = (dist_a >= 0) & (dist_a < WINDOW)
    mask_b = (row_b <= t_idx) & (t_idx - row_b < WINDOW) & (row_b < t_new)
    for g in range(N_KV_HEADS):
        k_a = buf_ref[pl.ds(2 * g, n_buf, stride=n_chunks), :]
        v_a = buf_ref[pl.ds(2 * g + 1, n_buf, stride=n_chunks), :]
        k_b = new_ref[:, 2 * g * HEAD_DIM:(2 * g + 1) * HEAD_DIM]
        v_b = new_ref[:, (2 * g + 1) * HEAD_DIM:(2 * g + 2) * HEAD_DIM]
        head_cols = [slice((g * GROUP_SIZE + h) * HEAD_DIM, (g * GROUP_SIZE + h + 1) * HEAD_DIM)
                     for h in range(GROUP_SIZE)]
        q = jnp.concatenate([q_ref[:, cols] for cols in head_cols], axis=0)
        s_a = jnp.where(mask_a, _mxu_nt(q, k_a) * ATTN_SCALE, NEG_INF)
        s_b = jnp.where(mask_b, _mxu_nt(q, k_b) * ATTN_SCALE, NEG_INF)
        m = jnp.maximum(jnp.max(s_a, axis=-1, keepdims=True), jnp.max(s_b, axis=-1, keepdims=True))
        e_a, e_b = jnp.exp(s_a - m), jnp.exp(s_b - m)
        l = jnp.sum(e_a, axis=-1, keepdims=True) + jnp.sum(e_b, axis=-1, keepdims=True)
        o = _mxu(e_a / l, v_a) + _mxu(e_b / l, v_b)
        for h, cols in enumerate(head_cols):
            o_ref[:, cols] = o[h * t_rows:(h + 1) * t_rows]


def _attn_win_sample(q, win_buf, kv_new, *, layer, t_new):
    b, t, qw = q.shape
    buf_rows = win_buf.shape[2]
    return pl.pallas_call(
        functools.partial(_attn_win_sample_kernel, t_new=t_new),
        grid=(b,),
        in_specs=[pl.BlockSpec((None, t, qw), lambda bi: (bi, 0, 0)),
                  pl.BlockSpec((None, None, buf_rows, HEAD_DIM), lambda bi: (layer, bi, 0, 0)),
                  pl.BlockSpec((None, kv_new.shape[1], KV_WIDTH), lambda bi: (bi, 0, 0))],
        out_specs=pl.BlockSpec((None, t, qw), lambda bi: (bi, 0, 0)),
        out_shape=jax.ShapeDtypeStruct(q.shape, F32),
        compiler_params=_params("arbitrary"),
        name="attn_win_sample",
    )(q, win_buf, kv_new)


def _attn_sel_sample_kernel(idx_ref, pt_ref, q_ref, *refs, past, t_new, n_slots, t_steps):
    del pt_ref
    blocks = refs[:n_slots]
    new_ref, oc_ref, ow_ref, g_ref, o_ref = refs[n_slots:]
    bi, t, gi = pl.program_id(0), pl.program_id(1), pl.program_id(2)
    base = ((bi * t_steps + t) * N_KV_HEADS + gi) * n_slots
    n_past_blocks = past // SEL_BLOCK
    q_pos = past + t
    q = q_ref[...]
    n_chunks = KV_WIDTH // HEAD_DIM
    lane = lax.broadcasted_iota(jnp.int32, (1, n_slots * SEL_BLOCK), 1)
    pos = lane % SEL_BLOCK
    has_new = jnp.int32(0)
    for n in range(n_slots):
        blk = idx_ref[base + n]
        start = jnp.where(blk < n_past_blocks, blk * SEL_BLOCK, q_pos + 1)
        pos = pos + jnp.where(lane // SEL_BLOCK == n, start, 0)
        has_new = has_new | (blk == n_past_blocks).astype(jnp.int32)
    mask_a = pos <= q_pos
    new = new_ref[...]
    row_b = lax.broadcasted_iota(jnp.int32, (1, new.shape[0]), 1)
    mask_b = (row_b <= t) & (row_b < t_new) & (has_new > 0)
    s_b = jnp.where(mask_b, _mxu_nt(q, new[:, :HEAD_DIM]) * ATTN_SCALE, NEG_INF)
    gates = g_ref[...]

    for g in range(N_KV_HEADS):
        @pl.when(gi == g)
        def _(g=g):
            keys = jnp.concatenate([blk[pl.ds(2 * g, SEL_BLOCK, stride=n_chunks), :] for blk in blocks], axis=0)
            vals = jnp.concatenate([blk[pl.ds(2 * g + 1, SEL_BLOCK, stride=n_chunks), :] for blk in blocks], axis=0)
            s_a = jnp.where(mask_a, _mxu_nt(q, keys) * ATTN_SCALE, NEG_INF)
            m = jnp.maximum(jnp.max(s_a, axis=-1, keepdims=True), jnp.max(s_b, axis=-1, keepdims=True))
            e_a, e_b = jnp.exp(s_a - m), jnp.exp(s_b - m)
            l = jnp.sum(e_a, axis=-1, keepdims=True) + jnp.sum(e_b, axis=-1, keepdims=True)
            o_s = _mxu(e_a / l, vals) + _mxu(e_b / l, new[:, HEAD_DIM:])
            o_ref[...] = gates[:, 0:1] * oc_ref[...] + gates[:, 1:2] * o_s + gates[:, 2:3] * ow_ref[...]


def _attn_sel_sample(idx, page_table, q, pool, kv_new, o_c, o_w, gates, *, layer, past, t_new):
    b, t, g, hg, hd = q.shape
    n_slots = idx.shape[-1]
    n_pages = page_table.shape[1]
    page = past // n_pages
    per_page = page // SEL_BLOCK
    n_past_blocks = past // SEL_BLOCK

    def slot_spec(n):
        def index(bi, ti, gi, idx_s, pt_s):
            blk = jnp.minimum(idx_s[((bi * t + ti) * g + gi) * n_slots + n], n_past_blocks - 1)
            phys = pt_s[bi * n_pages + blk // per_page]
            return (layer, phys, blk % per_page, 0)
        return pl.BlockSpec((None, None, SEL_BLOCK * (KV_WIDTH // HEAD_DIM), HEAD_DIM), index)

    head_spec = pl.BlockSpec((None, None, None, hg, hd), lambda bi, ti, gi, idx_s, pt_s: (bi, ti, gi, 0, 0))
    grid_spec = pltpu.PrefetchScalarGridSpec(
        num_scalar_prefetch=2,
        grid=(b, t, g),
        in_specs=[head_spec] + [slot_spec(n) for n in range(n_slots)] + [
            pl.BlockSpec((None, kv_new.shape[1], 2 * HEAD_DIM), lambda bi, ti, gi, idx_s, pt_s: (bi, 0, gi)),
            head_spec, head_spec,
            pl.BlockSpec((None, None, None, hg, 3), lambda bi, ti, gi, idx_s, pt_s: (bi, ti, gi, 0, 0))],
        out_specs=head_spec,
    )
    return pl.pallas_call(
        functools.partial(_attn_sel_sample_kernel, past=past, t_new=t_new, n_slots=n_slots, t_steps=t),
        grid_spec=grid_spec,
        out_shape=jax.ShapeDtypeStruct(q.shape, F32),
        compiler_params=_params("arbitrary", "arbitrary", "arbitrary"),
        name="attn_sel_sample",
    )(idx.reshape(-1), page_table.reshape(-1), q, *([pool] * n_slots), kv_new, o_c, o_w, gates)


def _dwconv_kernel(cur_ref, prev_ref, buf_ref, w_ref, b_ref, lg_ref, lb_ref, o_ref, full_scr, y_scr, *,
                   tt, halo, width):
    i = pl.program_id(1)
    lead = halo - (width - 1)
    full_scr[0:halo, :] = prev_ref[...]

    @pl.when(i == 0)
    def _():
        full_scr[0:halo, :] = buf_ref[...]

    full_scr[halo:halo + tt, :] = cur_ref[...]
    for c0 in range(0, cur_ref.shape[1], DWCONV_COLS):
        cols = slice(c0, c0 + DWCONV_COLS)
        acc = jnp.zeros((tt, DWCONV_COLS), F32) + b_ref[:, cols]
        for r in range(SUBLANES):
            offsets = [o for o in range(r, lead + width, SUBLANES) if o >= lead]
            rows = tt + (SUBLANES if r else 0)
            group = None
            for o in offsets:
                term = full_scr[o - r:o - r + rows, cols] * w_ref[o - lead:o - lead + 1, cols]
                group = term if group is None else group + term
            if group is not None:
                acc = acc + group[r:r + tt]
        y_scr[:, cols] = acc
    y = y_scr[...]
    mu = jnp.mean(y, axis=-1, keepdims=True)
    var = jnp.mean(jnp.square(y - mu), axis=-1, keepdims=True)
    yn = (y - mu) * lax.rsqrt(var + LN_EPS) * lg_ref[...] + lb_ref[...]
    o_ref[...] = _silu(yn).astype(o_ref.dtype)


def _dwconv(glu, buf, w_dw, b_dw, ln_g, ln_b, *, layer, tt):
    b, t, d = glu.shape
    width = w_dw.shape[1]
    halo = buf.shape[1]
    assert tt % halo == 0 and t % tt == 0 and halo >= width - 1
    per = tt // halo

    def vec_spec():
        return pl.BlockSpec((None, 1, d), lambda bi, i: (layer, 0, 0))

    return pl.pallas_call(
        functools.partial(_dwconv_kernel, tt=tt, halo=halo, width=width),
        grid=(b, t // tt),
        in_specs=[pl.BlockSpec((None, tt, d), lambda bi, i: (bi, i, 0)),
                  pl.BlockSpec((None, halo, d), lambda bi, i: (bi, jnp.maximum(i * per - 1, 0), 0)),
                  pl.BlockSpec((None, halo, d), lambda bi, i: (bi, 0, 0)),
                  pl.BlockSpec((None, width, d), lambda bi, i: (layer, 0, 0)),
                  vec_spec(), vec_spec(), vec_spec()],
        out_specs=pl.BlockSpec((None, tt, d), lambda bi, i: (bi, i, 0)),
        out_shape=jax.ShapeDtypeStruct((b, t, d), MXU_DTYPE),
        scratch_shapes=[pltpu.VMEM((halo + tt, d), F32), pltpu.VMEM((tt, d), F32)],
        compiler_params=_params("arbitrary", "arbitrary"),
        name="dwconv",
    )(glu, glu, buf, w_dw, b_dw, ln_g, ln_b)


def _rope_tables(pos):
    half = HEAD_DIM // 2
    inv = ROPE_THETA ** (-jnp.arange(half, dtype=F32) / half)
    ang = pos.astype(F32)[:, None] * inv[None, :]
    cos, sin = jnp.cos(ang), jnp.sin(ang)
    return jnp.concatenate([cos, cos], axis=1), jnp.concatenate([-sin, sin], axis=1)


def _group_gate_weights(w_gate_cols):
    d = w_gate_cols.shape[0]
    wg = w_gate_cols.reshape(d, N_KV_HEADS, GATE_COLS)
    return jnp.pad(wg, ((0, 0), (0, 0), (0, LANES - GATE_COLS))).reshape(d, N_KV_HEADS * LANES)


def _run_trunk(x, mod, pos, prm, caches, tiles):
    b, t, d = x.shape
    m = b * t
    depth = prm["w_mod"].shape[0]
    tm = tiles["tm"]
    rpm = t if caches is None else 1
    cos, sin = _rope_tables(pos)
    if caches is not None:
        cos, sin = jnp.tile(cos, (b, 1)), jnp.tile(sin, (b, 1))
    rows_per_seq = t if caches is None else m
    x = x.reshape(m, d)
    new_c, new_s, new_w, new_conv = [], [], [], []

    def mod_rows(layer, k):
        v = mod[layer, :, k, :]
        if caches is not None:
            v = jnp.repeat(v, t, axis=0)
        return v[:, None, :]

    def ffn(x, layer, which, k0):
        return _ffn(x, prm["norm_g"][layer, which * 2].reshape(1, d), mod_rows(layer, k0), mod_rows(layer, k0 + 1),
                    mod_rows(layer, k0 + 2), prm["final_norm_g"].reshape(1, d), prm["ffn_w_gate"], prm["ffn_w_up"],
                    prm["ffn_w_down"], layer=layer, which=which, rows_per_mod=rpm, tm=tiles["tm_ffn"],
                    tf=tiles["tf"], final_norm=(layer == depth - 1 and which == 1))

    for i in range(depth):
        x = ffn(x, i, 0, 0)
        ng = prm["norm_g"][i, 1].reshape(1, d)
        shift, scale, gate = mod_rows(i, 3), mod_rows(i, 4), mod_rows(i, 5)
        a = i // 2
        if i % 2 == 0:
            w_in = prm["attn_w_in"]
            common = dict(rows_per_mod=rpm, rows_per_seq=rows_per_seq, tm=tm)
            q_dtype = MXU_DTYPE if caches is None else F32
            q, = _proj(x, ng, shift, scale, w_in, cos, sin, layer=a, mode="q", col0=0, n=Q_WIDTH,
                       out_dtypes=[q_dtype], tn=512, **common)
            kv_outs = [F32, MXU_DTYPE] if caches is None else [F32]
            kv = _proj(x, ng, shift, scale, w_in, cos, sin, layer=a, mode="kv", col0=Q_WIDTH, n=3 * KV_WIDTH,
                       out_dtypes=kv_outs, tn=KV_WIDTH, **common)
            gates, = _proj(x, ng, shift, scale, _group_gate_weights(prm["attn_gate_w"][a])[None], cos, sin, layer=0,
                           mode="sigmoid", col0=0, n=N_KV_HEADS * LANES, out_dtypes=[F32],
                           tn=N_KV_HEADS * LANES, **common)
            kv32 = kv[0]
            new_c.append(kv32[0].reshape(b, t, N_KV_HEADS, 2, HEAD_DIM))
            new_s.append(kv32[1].reshape(b, t, N_KV_HEADS, 2, HEAD_DIM))
            if caches is None:
                ckv = _compress_rows(kv32[0], prm["cmp_pe"], prm["cmp_w1"], prm["cmp_w2"], layer=a, n_seq=b)
                o = _attn_prompt(q.reshape(b, t, Q_WIDTH), ckv.reshape(b, t // CMP_BLOCK, KV_WIDTH),
                                 kv[1].reshape(3, b, t, KV_WIDTH), gates.reshape(b, t, N_KV_HEADS * LANES),
                                 tq=tiles["tq"], tk=tiles["tk"]).reshape(m, Q_WIDTH)
                new_w.append(kv32[2].reshape(b, t, N_KV_HEADS, 2, HEAD_DIM)[:, t - min(WINDOW, t):])
            else:
                pool_c, pool_s, win_buf, _, page_table = caches
                n_layers, n_pool, page = pool_c.shape[:3]
                past = page_table.shape[1] * page
                assert t < CMP_BLOCK and page % SEL_BLOCK == 0
                ckv = _compress_pages(pool_c.reshape(n_layers, n_pool, page * (KV_WIDTH // HEAD_DIM), HEAD_DIM),
                                      page_table, prm["cmp_pe"], prm["cmp_w1"], prm["cmp_w2"], layer=a)
                q3 = q.reshape(b, t, Q_WIDTH)
                o_c, idx = _attn_cmp_sample(q3, ckv, past=past, n_total=past + t)
                t_pad = -(-t // SUBLANES) * SUBLANES
                pad_rows = lambda r: jnp.pad(r.reshape(b, t, KV_WIDTH), ((0, 0), (0, t_pad - t), (0, 0)))
                n_buf = win_buf.shape[2]
                o_w = _attn_win_sample(q3, win_buf.reshape(n_layers, b, n_buf * (KV_WIDTH // HEAD_DIM), HEAD_DIM),
                                       pad_rows(kv32[2]), layer=a, t_new=t)
                n_take = min(N_SELECT, -(-(past + t) // SEL_BLOCK))
                idx = jnp.transpose(idx[..., :n_take], (0, 2, 1, 3))
                heads = lambda v: v.reshape(b, t, N_KV_HEADS, GROUP_SIZE, HEAD_DIM)
                g5 = gates.reshape(b, t, N_KV_HEADS, LANES)[..., :GATE_COLS].reshape(b, t, N_KV_HEADS, GROUP_SIZE, 3)
                o = _attn_sel_sample(idx, page_table, heads(q3),
                                     pool_s.reshape(n_layers, n_pool, page * (KV_WIDTH // HEAD_DIM), HEAD_DIM),
                                     pad_rows(kv32[1]), heads(o_c), heads(o_w), g5, layer=a, past=past,
                                     t_new=t).reshape(m, Q_WIDTH)
                new_w.append(kv32[2].reshape(b, t, N_KV_HEADS, 2, HEAD_DIM))
            x = _linres(o, prm["attn_w_out"], x, gate, layer=a, rows_per_mod=rpm, tm=tm, tn=tiles["tn_out"])
        else:
            glu = _glu_proj(x, ng, shift, scale, prm["conv_w_pw1"], layer=a, rows_per_mod=rpm, tm=tm, tn=512)
            glu = glu.reshape(b, t, d)
            width = prm["conv_w_dw"].shape[1]
            halo = 32
            if caches is None:
                buf = jnp.zeros((b, width - 1, d), F32)
                t_conv = t
                cur = glu
            else:
                buf = caches[3][a]
                t_conv = halo
                cur = jnp.pad(glu, ((0, 0), (0, t_conv - t), (0, 0)))
            if t >= width - 1:
                new_conv.append(glu[:, t - (width - 1):])
            else:
                new_conv.append(jnp.concatenate([buf[:, t:], glu], axis=1))
            buf = jnp.pad(buf, ((0, 0), (halo - (width - 1), 0), (0, 0)))
            vec = lambda v: v.reshape(v.shape[0], 1, d)
            act = _dwconv(cur, buf, prm["conv_w_dw"], vec(prm["conv_b_dw"]), vec(prm["conv_ln_g"]),
                          vec(prm["conv_ln_b"]), layer=a, tt=min(tiles["tt"], t_conv))
            act = act[:, :t].reshape(m, d)
            x = _linres(act, prm["conv_w_pw2"], x, gate, layer=a, rows_per_mod=rpm, tm=tm, tn=tiles["tn_out"])
        x = ffn(x, i, 1, 6)
    y = x.reshape(b, t, d)
    new_w = jnp.stack(new_w)
    if caches is not None:
        keep = min(WINDOW, caches[2].shape[2] + t) - t
        new_w = jnp.concatenate([caches[2][:, :, caches[2].shape[2] - keep:], new_w], axis=2)
    return y, jnp.stack(new_c), jnp.stack(new_s), new_w, jnp.stack(new_conv)


PROMPT_TILES = dict(tm=1024, tm_ffn=1024, tf=512, tq=256, tk=512, tn_out=512, tt=128)
SAMPLE_TILES = dict(tm=32, tm_ffn=32, tf=512, tn_out=512, tt=32)


def kernel(x_prompt, x_sample, cache_cmp_kv, cache_sel_kv, cache_win_kv, state_conv, page_table, c_prompt, c_sample, w_mod, b_mod, norm_g, ffn_w_gate, ffn_w_up, ffn_w_down, attn_w_in, attn_w_out, cmp_pe, cmp_w1, cmp_w2, conv_w_pw1, conv_w_dw, conv_b_dw, conv_ln_g, conv_ln_b, conv_w_pw2, final_norm_g):
    mxu = lambda w: w.astype(MXU_DTYPE)
    prm = {"w_mod": w_mod, "norm_g": norm_g, "ffn_w_gate": mxu(ffn_w_gate), "ffn_w_up": mxu(ffn_w_up),
           "ffn_w_down": mxu(ffn_w_down), "attn_w_in": mxu(attn_w_in), "attn_gate_w": attn_w_in[..., -N_HEADS * 3:],
           "attn_w_out": mxu(attn_w_out), "cmp_pe": cmp_pe,
           "cmp_w1": cmp_w1, "cmp_w2": cmp_w2, "conv_w_pw1": mxu(conv_w_pw1), "conv_w_dw": conv_w_dw,
           "conv_b_dw": conv_b_dw, "conv_ln_g": conv_ln_g, "conv_ln_b": conv_ln_b, "conv_w_pw2": mxu(conv_w_pw2),
           "final_norm_g": final_norm_g}
    depth, d, _ = w_mod.shape
    bp, tp = x_prompt.shape[:2]
    bs, ts = x_sample.shape[:2]
    past = page_table.shape[1] * cache_cmp_kv.shape[2]

    n_req = bp + bs
    r_pad = -(-n_req // 8) * 8
    c_all = jnp.pad(jnp.concatenate([c_prompt, c_sample], axis=0), ((0, r_pad - n_req), (0, 0)))
    mod = _mod_vectors(c_all, w_mod, b_mod).reshape(depth, r_pad, N_MOD, d)

    pos_p = jnp.arange(tp, dtype=jnp.int32)
    pos_s = past + jnp.arange(ts, dtype=jnp.int32)
    y_p, p_cmp, p_sel, p_win, p_conv = _run_trunk(x_prompt, mod[:, :bp], pos_p, prm, None, PROMPT_TILES)
    y_s, s_cmp, s_sel, s_win, s_conv = _run_trunk(
        x_sample, mod[:, bp:n_req], pos_s, prm,
        (cache_cmp_kv, cache_sel_kv, cache_win_kv, state_conv, page_table), SAMPLE_TILES)
    return (y_p, y_s, p_cmp, p_sel, p_win, p_conv, s_cmp, s_sel, s_win, s_conv)
```

```python
import functools

import jax
import jax.numpy as jnp
from jax import lax
from jax.experimental import pallas as pl
from jax.experimental.pallas import tpu as pltpu

F32 = jnp.float32
MXU_DTYPE = jnp.bfloat16
VMEM_LIMIT_BYTES = 60 * 1024 * 1024
LANES = 128
SUBLANES = 8

N_HEADS = 16
HEAD_DIM = 128
N_KV_HEADS = 4
GROUP_SIZE = N_HEADS // N_KV_HEADS
CMP_BLOCK = 32
SEL_BLOCK = 64
SEL_RATIO = SEL_BLOCK // CMP_BLOCK
N_SELECT = 16
WINDOW = 512
ROPE_THETA = 10000.0
N_MOD = 9
RMS_EPS = 1e-6
LN_EPS = 1e-5
NEG_INF = -1e30
FORCED_SCORE = 1e9
KNOCKED_OUT = -3e38
ATTN_SCALE = HEAD_DIM ** -0.5
Q_WIDTH = N_HEADS * HEAD_DIM
KV_WIDTH = 2 * N_KV_HEADS * HEAD_DIM
GATE_COLS = GROUP_SIZE * 3
PAGES_PER_STEP = 16
DWCONV_COLS = 128
ROW_BLOCK = 32
FFN_PANEL = 256
PROJ_PANEL = 256
EXP2_SCALE = ATTN_SCALE * 1.4426950408889634


def _params(*sem):
    return pltpu.CompilerParams(dimension_semantics=sem, vmem_limit_bytes=VMEM_LIMIT_BYTES)


def _mxu(a, b):
    return jnp.dot(a.astype(MXU_DTYPE), b.astype(MXU_DTYPE), preferred_element_type=F32)


def _mxu_nt(a, b):
    return lax.dot_general(a.astype(MXU_DTYPE), b.astype(MXU_DTYPE),
                           (((1,), (1,)), ((), ())), preferred_element_type=F32)


def _silu(x):
    return x * jax.nn.sigmoid(x)


def _softmax_rows(s):
    e = jnp.exp(s - jnp.max(s, axis=-1, keepdims=True))
    return e / jnp.sum(e, axis=-1, keepdims=True)


def _softmax_scaled(s):
    e = jnp.exp2((s - jnp.max(s, axis=-1, keepdims=True)) * EXP2_SCALE)
    return e / jnp.sum(e, axis=-1, keepdims=True)


def _mod_spec(tm, rows_per_mod, d, grid_rank):
    if rows_per_mod >= tm:
        assert rows_per_mod % tm == 0
        block, idx = (1, 1, d), (lambda i: (i * tm) // rows_per_mod)
    else:
        assert rows_per_mod == 1
        block, idx = (tm, 1, d), (lambda i: i)
    if grid_rank == 1:
        return pl.BlockSpec(block, lambda i: (idx(i), 0, 0))
    return pl.BlockSpec(block, lambda i, j: (idx(i), 0, 0))


def _modulated(x_ref, ng_ref, sh_ref, sc_ref):
    x = x_ref[...]
    gain = ng_ref[...] * (1.0 + sc_ref[:, 0, :])
    return x * lax.rsqrt(jnp.mean(x * x, axis=-1, keepdims=True) + RMS_EPS) * gain + sh_ref[:, 0, :]


def _mod_kernel(c_ref, w_ref, b_ref, o_ref):
    o_ref[...] = _mxu(_silu(c_ref[...]), w_ref[...]) + b_ref[...]


def _mod_vectors(c_all, w_mod, b_mod, tn=1024):
    depth, d, n = w_mod.shape
    r = c_all.shape[0]
    return pl.pallas_call(
        _mod_kernel,
        grid=(depth, n // tn),
        in_specs=[pl.BlockSpec((r, d), lambda l, j: (0, 0)),
                  pl.BlockSpec((None, d, tn), lambda l, j: (l, 0, j)),
                  pl.BlockSpec((None, 1, tn), lambda l, j: (l, 0, j))],
        out_specs=pl.BlockSpec((None, r, tn), lambda l, j: (l, 0, j)),
        out_shape=jax.ShapeDtypeStruct((depth, r, n), F32),
        compiler_params=_params("arbitrary", "arbitrary"),
        name="mod_vectors",
    )(c_all, w_mod, b_mod.reshape(depth, 1, n))


def _ffn_kernel(x_ref, ng_ref, sh_ref, sc_ref, gt_ref, fg_ref, wg_ref, wu_ref, wd_ref, o_ref, h_scr, *, final_norm):
    f = pl.program_id(1)

    @pl.when(f == 0)
    def _():
        h_scr[...] = _modulated(x_ref, ng_ref, sh_ref, sc_ref).astype(h_scr.dtype)
        o_ref[...] = jnp.zeros_like(o_ref)

    panel = min(FFN_PANEL, h_scr.shape[0])
    for p0 in range(0, h_scr.shape[0], panel):
        rows = slice(p0, p0 + panel)
        h = h_scr[rows, :]
        g = _mxu(h, wg_ref[...])
        u = _mxu(h, wu_ref[...])
        a = jnp.concatenate([(_silu(g[r0:r0 + ROW_BLOCK]) * u[r0:r0 + ROW_BLOCK]).astype(MXU_DTYPE)
                             for r0 in range(0, panel, ROW_BLOCK)], axis=0)
        o_ref[rows, :] += _mxu(a, wd_ref[...])

    @pl.when(f == pl.num_programs(1) - 1)
    def _():
        y = x_ref[...] + (0.5 * gt_ref[:, 0, :]) * o_ref[...]
        if final_norm:
            y = y * lax.rsqrt(jnp.mean(y * y, axis=-1, keepdims=True) + RMS_EPS) * fg_ref[...]
        o_ref[...] = y


def _ffn(x, ng, shift, scale, gate, final_g, w_gate, w_up, w_down, *, layer, which, rows_per_mod, tm, tf,
         final_norm):
    m, d = x.shape
    dff = w_gate.shape[-1]
    mod = _mod_spec(tm, rows_per_mod, d, 2)
    return pl.pallas_call(
        functools.partial(_ffn_kernel, final_norm=final_norm),
        grid=(m // tm, dff // tf),
        in_specs=[pl.BlockSpec((tm, d), lambda i, f: (i, 0)),
                  pl.BlockSpec((1, d), lambda i, f: (0, 0)),
                  mod, mod, mod,
                  pl.BlockSpec((1, d), lambda i, f: (0, 0)),
                  pl.BlockSpec((None, None, d, tf), lambda i, f: (layer, which, 0, f)),
                  pl.BlockSpec((None, None, d, tf), lambda i, f: (layer, which, 0, f)),
                  pl.BlockSpec((None, None, tf, d), lambda i, f: (layer, which, f, 0))],
        out_specs=pl.BlockSpec((tm, d), lambda i, f: (i, 0)),
        out_shape=jax.ShapeDtypeStruct((m, d), F32),
        scratch_shapes=[pltpu.VMEM((tm, d), MXU_DTYPE)],
        compiler_params=_params("arbitrary", "arbitrary"),
        name="ffn",
    )(x, ng, shift, scale, gate, final_g, w_gate, w_up, w_down)


def _rope_chunk(z, cos, sin):
    return z * cos + pltpu.roll(z, HEAD_DIM // 2, 1) * sin


def _proj_kernel(x_ref, ng_ref, sh_ref, sc_ref, w_ref, cos_ref, sin_ref, *rest, mode):
    out_refs, h_scr = rest[:-1], rest[-1]

    @pl.when(pl.program_id(1) == 0)
    def _():
        h_scr[...] = _modulated(x_ref, ng_ref, sh_ref, sc_ref).astype(h_scr.dtype)

    tm = h_scr.shape[0]
    panel = min(PROJ_PANEL, tm)
    n_chunks = w_ref.shape[1] // HEAD_DIM
    for p0 in range(0, tm, panel):
        rows = slice(p0, p0 + panel)
        z = _mxu(h_scr[rows, :], w_ref[...])
        if mode == "sigmoid":
            out_refs[0][rows, :] = jax.nn.sigmoid(z)
            continue
        cos, sin = cos_ref[rows, :], sin_ref[rows, :]
        for c in range(n_chunks):
            zc = z[:, c * HEAD_DIM:(c + 1) * HEAD_DIM]
            if mode == "q" or c % 2 == 0:
                zc = _rope_chunk(zc, cos, sin)
            for k, o in enumerate(out_refs):
                if mode == "kv" and k == 0:
                    o[pl.ds(p0 * n_chunks + c, panel, stride=n_chunks), :] = zc
                else:
                    o[rows, c * HEAD_DIM:(c + 1) * HEAD_DIM] = zc.astype(o.dtype)


def _proj(x, ng, shift, scale, w, cos, sin, *, layer, mode, col0, n, out_dtypes, rows_per_mod, rows_per_seq,
          tm, tn):
    m, d = x.shape
    mod = _mod_spec(tm, rows_per_mod, d, 2)
    seq_tiles = rows_per_seq // tm
    assert col0 % tn == 0 and n % tn == 0 and rows_per_seq % tm == 0
    if mode == "kv":
        assert tn == KV_WIDTH and out_dtypes[0] == F32
        n_chunks = KV_WIDTH // HEAD_DIM
        out_specs = [pl.BlockSpec((None, tm * n_chunks, HEAD_DIM), lambda i, j: (j, i, 0))] + [
            pl.BlockSpec((None, tm, tn), lambda i, j: (j, i, 0)) for _ in out_dtypes[1:]]
        out_shape = [jax.ShapeDtypeStruct((n // KV_WIDTH, m * n_chunks, HEAD_DIM), F32)] + [
            jax.ShapeDtypeStruct((n // KV_WIDTH, m, KV_WIDTH), dt) for dt in out_dtypes[1:]]
    else:
        out_specs = [pl.BlockSpec((tm, tn), lambda i, j: (i, j)) for _ in out_dtypes]
        out_shape = [jax.ShapeDtypeStruct((m, n), dt) for dt in out_dtypes]
    return pl.pallas_call(
        functools.partial(_proj_kernel, mode=mode),
        grid=(m // tm, n // tn),
        in_specs=[pl.BlockSpec((tm, d), lambda i, j: (i, 0)),
                  pl.BlockSpec((1, d), lambda i, j: (0, 0)),
                  mod, mod,
                  pl.BlockSpec((None, d, tn), lambda i, j: (layer, 0, col0 // tn + j)),
                  pl.BlockSpec((tm, HEAD_DIM), lambda i, j: (i % seq_tiles, 0)),
                  pl.BlockSpec((tm, HEAD_DIM), lambda i, j: (i % seq_tiles, 0))],
        out_specs=out_specs,
        out_shape=out_shape,
        scratch_shapes=[pltpu.VMEM((tm, d), MXU_DTYPE)],
        compiler_params=_params("arbitrary", "arbitrary"),
        name="proj_" + mode,
    )(x, ng, shift, scale, w, cos, sin)


def _glu_kernel(x_ref, ng_ref, sh_ref, sc_ref, wa_ref, wb_ref, o_ref, h_scr):
    @pl.when(pl.program_id(1) == 0)
    def _():
        h_scr[...] = _modulated(x_ref, ng_ref, sh_ref, sc_ref).astype(h_scr.dtype)

    panel = min(PROJ_PANEL, h_scr.shape[0])
    for p0 in range(0, h_scr.shape[0], panel):
        h = h_scr[p0:p0 + panel, :]
        o_ref[p0:p0 + panel, :] = _mxu(h, wa_ref[...]) * jax.nn.sigmoid(_mxu(h, wb_ref[...]))


def _glu_proj(x, ng, shift, scale, w_pw1, *, layer, rows_per_mod, tm, tn):
    m, d = x.shape
    dc = w_pw1.shape[-1] // 2
    mod = _mod_spec(tm, rows_per_mod, d, 2)
    return pl.pallas_call(
        _glu_kernel,
        grid=(m // tm, dc // tn),
        in_specs=[pl.BlockSpec((tm, d), lambda i, j: (i, 0)),
                  pl.BlockSpec((1, d), lambda i, j: (0, 0)),
                  mod, mod,
                  pl.BlockSpec((None, d, tn), lambda i, j: (layer, 0, j)),
                  pl.BlockSpec((None, d, tn), lambda i, j: (layer, 0, dc // tn + j))],
        out_specs=pl.BlockSpec((tm, tn), lambda i, j: (i, j)),
        out_shape=jax.ShapeDtypeStruct((m, dc), F32),
        scratch_shapes=[pltpu.VMEM((tm, d), MXU_DTYPE)],
        compiler_params=_params("arbitrary", "arbitrary"),
        name="glu_proj",
    )(x, ng, shift, scale, w_pw1, w_pw1)


def _linres_kernel(a_ref, w_ref, x_ref, gt_ref, o_ref):
    tm = a_ref.shape[0]
    panel = min(PROJ_PANEL, tm)
    for p0 in range(0, tm, panel):
        rows = slice(p0, p0 + panel)
        gate = gt_ref[:, 0, :] if gt_ref.shape[0] == 1 else gt_ref[rows, 0, :]
        o_ref[rows, :] = x_ref[rows, :] + gate * _mxu(a_ref[rows, :], w_ref[...])


def _linres(a, w, x, gate, *, layer, rows_per_mod, tm, tn):
    m, k = a.shape
    d = x.shape[1]
    if rows_per_mod >= tm:
        gspec = pl.BlockSpec((1, 1, tn), lambda i, j: ((i * tm) // rows_per_mod, 0, j))
    else:
        gspec = pl.BlockSpec((tm, 1, tn), lambda i, j: (i, 0, j))
    return pl.pallas_call(
        _linres_kernel,
        grid=(m // tm, d // tn),
        in_specs=[pl.BlockSpec((tm, k), lambda i, j: (i, 0)),
                  pl.BlockSpec((None, k, tn), lambda i, j: (layer, 0, j)),
                  pl.BlockSpec((tm, tn), lambda i, j: (i, j)),
                  gspec],
        out_specs=pl.BlockSpec((tm, tn), lambda i, j: (i, j)),
        out_shape=jax.ShapeDtypeStruct((m, d), F32),
        compiler_params=_params("arbitrary", "arbitrary"),
        name="linres",
    )(a, w, x, gate)


def _compress_kernel(x_ref, pe_ref, w1_ref, w2_ref, o_ref):
    n_chunks = KV_WIDTH // HEAD_DIM
    nb = x_ref.shape[0] // (CMP_BLOCK * n_chunks)
    for kv in range(2):
        acc = jnp.zeros((N_KV_HEADS * nb, HEAD_DIM), F32)
        for c in range(CMP_BLOCK):
            xc = jnp.concatenate([x_ref[pl.ds(c * n_chunks + 2 * g + kv, nb, stride=CMP_BLOCK * n_chunks), :]
                                  for g in range(N_KV_HEADS)], axis=0) + pe_ref[kv, c:c + 1, :]
            acc += _mxu(xc, w1_ref[kv, c * HEAD_DIM:(c + 1) * HEAD_DIM, :])
        y = _mxu(_silu(acc), w2_ref[kv])
        for g in range(N_KV_HEADS):
            o_ref[:, (2 * g + kv) * HEAD_DIM:(2 * g + kv + 1) * HEAD_DIM] = y[g * nb:(g + 1) * nb].astype(o_ref.dtype)


def _compress_rows(rows, pe, w1, w2, *, layer, n_seq):
    n_chunks = KV_WIDTH // HEAD_DIM
    n_blocks = rows.shape[0] // (CMP_BLOCK * n_chunks)
    nb = n_blocks // n_seq
    return pl.pallas_call(
        _compress_kernel,
        grid=(n_seq,),
        in_specs=[pl.BlockSpec((nb * CMP_BLOCK * n_chunks, HEAD_DIM), lambda i: (i, 0)),
                  pl.BlockSpec((None, 2, CMP_BLOCK, HEAD_DIM), lambda i: (layer, 0, 0, 0)),
                  pl.BlockSpec((None, 2, CMP_BLOCK * HEAD_DIM, HEAD_DIM), lambda i: (layer, 0, 0, 0)),
                  pl.BlockSpec((None, 2, HEAD_DIM, HEAD_DIM), lambda i: (layer, 0, 0, 0))],
        out_specs=pl.BlockSpec((nb, KV_WIDTH), lambda i: (i, 0)),
        out_shape=jax.ShapeDtypeStruct((n_blocks, KV_WIDTH), MXU_DTYPE),
        compiler_params=_params("arbitrary"),
        name="compress_rows",
    )(rows, pe, w1, w2)


def _compress_pages_kernel(pt_ref, *refs, blocks_per_page):
    del pt_ref
    pages = refs[:PAGES_PER_STEP]
    pe_ref, w1_ref, w2_ref, o_ref, t_scr = refs[PAGES_PER_STEP:]
    nb = PAGES_PER_STEP * blocks_per_page
    pages_per_group = SUBLANES // blocks_per_page
    n_groups = PAGES_PER_STEP // pages_per_group
    rows = SUBLANES * CMP_BLOCK
    out_row = lax.broadcasted_iota(jnp.int32, (rows, rows), 0)
    in_row = lax.broadcasted_iota(jnp.int32, (rows, rows), 1)
    perm = jnp.where(in_row == (out_row % SUBLANES) * CMP_BLOCK + out_row // SUBLANES, 1.0, 0.0)
    pe_rows = jnp.concatenate([pe_ref[kv] for _ in range(N_KV_HEADS) for kv in range(2)], axis=1)
    pe_rows = jnp.concatenate([pe_rows] * SUBLANES, axis=0)
    n_chunks = KV_WIDTH // HEAD_DIM
    page_rows = pages[0].shape[0] // n_chunks
    for gp in range(n_groups):
        x = jnp.concatenate(
            [jnp.concatenate([pages[gp * pages_per_group + k][pl.ds(ch, page_rows, stride=n_chunks), :]
                              for ch in range(n_chunks)], axis=1)
             for k in range(pages_per_group)], axis=0)
        t_scr[gp] = _mxu(perm, x + pe_rows)
    def block_rows(c, kv):
        return jnp.concatenate(
            [t_scr[:, c * SUBLANES:(c + 1) * SUBLANES,
                   (g * 2 + kv) * HEAD_DIM:(g * 2 + kv + 1) * HEAD_DIM].reshape(nb, HEAD_DIM)
             for g in range(N_KV_HEADS)], axis=0)

    for kv in range(2):
        acc = jnp.zeros((N_KV_HEADS * nb, HEAD_DIM), F32)
        for c in range(0, CMP_BLOCK, 2):
            xc = jnp.concatenate([block_rows(c, kv), block_rows(c + 1, kv)], axis=1)
            acc += _mxu(xc, w1_ref[kv, c * HEAD_DIM:(c + 2) * HEAD_DIM, :])
        y = _mxu(_silu(acc), w2_ref[kv])
        for g in range(N_KV_HEADS):
            o_ref[:, (g * 2 + kv) * HEAD_DIM:(g * 2 + kv + 1) * HEAD_DIM] = (
                y[g * nb:(g + 1) * nb].astype(o_ref.dtype))


def _compress_pages(pool, page_table, pe, w1, w2, *, layer):
    width = KV_WIDTH
    n_chunks = width // HEAD_DIM
    page = pool.shape[2] // n_chunks
    b, n_pages = page_table.shape
    bpp = page // CMP_BLOCK
    nb = PAGES_PER_STEP * bpp
    assert n_pages % PAGES_PER_STEP == 0 and page % CMP_BLOCK == 0 and SUBLANES % bpp == 0
    n_groups = nb // SUBLANES

    def page_spec(p):
        return pl.BlockSpec((None, None, page * n_chunks, HEAD_DIM),
                            lambda bi, gi, pt: (layer, pt[bi * n_pages + gi * PAGES_PER_STEP + p], 0, 0))

    grid_spec = pltpu.PrefetchScalarGridSpec(
        num_scalar_prefetch=1,
        grid=(b, n_pages // PAGES_PER_STEP),
        in_specs=[page_spec(p) for p in range(PAGES_PER_STEP)] + [
            pl.BlockSpec((None, 2, CMP_BLOCK, HEAD_DIM), lambda bi, gi, pt: (layer, 0, 0, 0)),
            pl.BlockSpec((None, 2, CMP_BLOCK * HEAD_DIM, HEAD_DIM), lambda bi, gi, pt: (layer, 0, 0, 0)),
            pl.BlockSpec((None, 2, HEAD_DIM, HEAD_DIM), lambda bi, gi, pt: (layer, 0, 0, 0))],
        out_specs=pl.BlockSpec((None, nb, width), lambda bi, gi, pt: (bi, gi, 0)),
        scratch_shapes=[pltpu.VMEM((n_groups, SUBLANES * CMP_BLOCK, width), F32)],
    )
    return pl.pallas_call(
        functools.partial(_compress_pages_kernel, blocks_per_page=bpp),
        grid_spec=grid_spec,
        out_shape=jax.ShapeDtypeStruct((b, n_pages * bpp, width), MXU_DTYPE),
        compiler_params=_params("arbitrary", "arbitrary"),
        name="compress_pages",
    )(page_table.reshape(-1), *([pool] * PAGES_PER_STEP), pe, w1, w2)


def _pair_sums(imp):
    out = []
    for k in range(imp.shape[1] // LANES):
        x = imp[:, k * LANES:(k + 1) * LANES]
        even = lax.broadcasted_iota(jnp.int32, x.shape, 1) % 2 == 0
        out.append(x + jnp.where(even, pltpu.roll(x, LANES - 1, 1), pltpu.roll(x, 1, 1)))
    return out[0] if len(out) == 1 else jnp.concatenate(out, axis=1)


def _block_scores(imp, q_pos, n_sel_blocks):
    lane = lax.broadcasted_iota(jnp.int32, imp.shape, 1)
    blk = lane // SEL_RATIO
    cur = q_pos // SEL_BLOCK
    forced = (blk == 0) | (blk == cur) | (blk == cur - 1)
    valid = blk * SEL_BLOCK <= q_pos
    score = jnp.where(forced, FORCED_SCORE, jnp.where(valid, _pair_sums(imp), -1.0))
    eligible = (lane % SEL_RATIO == 0) & (blk < n_sel_blocks)
    return jnp.where(eligible, score, KNOCKED_OUT)


def _take_top(work):
    lane = lax.broadcasted_iota(jnp.int32, work.shape, 1).astype(F32)
    top = jnp.max(work, axis=-1, keepdims=True)
    pick = jnp.min(jnp.where(work == top, lane, float(work.shape[1])), axis=-1, keepdims=True)
    return lane == pick, pick


def _with_ones(v):
    return jnp.concatenate([v, jnp.ones(v.shape, v.dtype)], axis=1)


def _attn_prompt_kernel(q_ref, ck_ref, cv_ref, ks_ref, vs_ref, kw_ref, vw_ref, g_ref, o_ref, *, tq, tk, seq):
    i = pl.program_id(2)
    n_cmp = ck_ref.shape[0]
    n_sel_blocks = -(-seq // SEL_BLOCK)
    n_take = min(N_SELECT, n_sel_blocks)
    assert SEL_RATIO == 2 and n_cmp * CMP_BLOCK == seq and n_cmp <= LANES
    q = q_ref[...]
    q4 = jnp.concatenate([q[:, h * HEAD_DIM:(h + 1) * HEAD_DIM] for h in range(GROUP_SIZE)], axis=0)
    q_pos = i * tq + lax.broadcasted_iota(jnp.int32, (tq, 1), 0)

    pad = jnp.zeros((LANES - n_cmp, HEAD_DIM), ck_ref.dtype)
    ck = jnp.concatenate([ck_ref[...], pad], axis=0) if n_cmp < LANES else ck_ref[...]
    cv = jnp.concatenate([cv_ref[...], pad], axis=0) if n_cmp < LANES else cv_ref[...]
    lane = lax.broadcasted_iota(jnp.int32, (1, LANES), 1)
    vis = ((lane + 1) * CMP_BLOCK - 1 <= q_pos) & (lane < n_cmp)
    s = _mxu_nt(q4, ck).reshape(GROUP_SIZE, tq, LANES)
    p = jnp.where(vis[None], _softmax_scaled(jnp.where(vis[None], s, NEG_INF)), 0.0)
    o_c = _mxu(p.reshape(GROUP_SIZE * tq, LANES), cv).reshape(GROUP_SIZE, tq, HEAD_DIM)

    def top_blocks():
        q_pos_t = i * tq + lax.broadcasted_iota(jnp.int32, (1, tq), 1)
        row = lax.broadcasted_iota(jnp.int32, (n_cmp, 1), 0)
        vis_t = (row + 1) * CMP_BLOCK - 1 <= q_pos_t
        imp_t = jnp.zeros((n_cmp, tq), F32)
        for h in range(GROUP_SIZE):
            s_t = jnp.where(vis_t, _mxu_nt(ck_ref[...], q[:, h * HEAD_DIM:(h + 1) * HEAD_DIM]), NEG_INF)
            e = jnp.exp2((s_t - jnp.max(s_t, axis=0, keepdims=True)) * EXP2_SCALE)
            imp_t = imp_t + jnp.where(vis_t, e / jnp.sum(e, axis=0, keepdims=True), 0.0)
        pair = imp_t + pltpu.roll(imp_t, n_cmp - 1, 0)
        blk = row // SEL_RATIO
        cur = q_pos_t // SEL_BLOCK
        forced = (blk == 0) | (blk == cur) | (blk == cur - 1)
        score = jnp.where(forced, FORCED_SCORE, jnp.where(blk * SEL_BLOCK <= q_pos_t, pair, -1.0))
        work = jnp.where((row % SEL_RATIO == 0) & (blk < n_sel_blocks), score, KNOCKED_OUT)
        row_f = row.astype(F32)
        sel_t = jnp.zeros((n_cmp, tq), F32)
        for _ in range(n_take):
            top = jnp.max(work, axis=0, keepdims=True)
            pick = jnp.min(jnp.where(work == top, row_f, float(n_cmp)), axis=0, keepdims=True)
            hit = row_f == pick
            work = jnp.where(hit, KNOCKED_OUT, work)
            sel_t = jnp.where(hit, 1.0, sel_t)
        if n_cmp < LANES:
            sel_t = jnp.concatenate([sel_t, jnp.zeros((LANES - n_cmp, tq), F32)], axis=0)
        return sel_t.T

    def all_blocks():
        return jnp.where((lane % SEL_RATIO == 0) & (lane // SEL_RATIO < n_sel_blocks), 1.0,
                         jnp.zeros((tq, LANES), F32))

    sel = lax.cond(((i + 1) * tq - 1) // SEL_BLOCK + 1 <= n_take, all_blocks, top_blocks).astype(MXU_DTYPE)

    def sel_chunk(kc, carry):
        m_i, l_i, acc = carry
        start = pl.multiple_of(kc * tk, tk)
        k_pos = start + lax.broadcasted_iota(jnp.int32, (1, tk), 1)
        expand = lax.broadcasted_iota(jnp.int32, (LANES, 1), 0) == SEL_RATIO * (k_pos // SEL_BLOCK)
        picked = jnp.dot(sel, jnp.where(expand, 1.0, 0.0).astype(MXU_DTYPE), preferred_element_type=F32)
        bias = jnp.where((picked > 0.5) & (k_pos <= q_pos), 0.0, NEG_INF)[None]
        sc = _mxu_nt(q4, ks_ref[pl.ds(start, tk), :]).reshape(GROUP_SIZE, tq, tk) + bias
        m_new = jnp.maximum(m_i, jnp.max(sc, axis=-1, keepdims=True))
        alpha = jnp.exp2((m_i - m_new) * EXP2_SCALE)
        e = jnp.exp2((sc - m_new) * EXP2_SCALE)
        l_new = alpha * l_i + jnp.sum(e, axis=-1, keepdims=True)
        pv = _mxu(e.reshape(GROUP_SIZE * tq, tk), vs_ref[pl.ds(start, tk), :])
        return m_new, l_new, alpha * acc + pv.reshape(GROUP_SIZE, tq, HEAD_DIM)

    init = (jnp.full((GROUP_SIZE, tq, 1), NEG_INF, F32), jnp.zeros((GROUP_SIZE, tq, 1), F32),
            jnp.zeros((GROUP_SIZE, tq, HEAD_DIM), F32))
    _, l_s, acc_s = lax.fori_loop(0, ((i + 1) * tq + tk - 1) // tk, sel_chunk, init)
    o_s = acc_s / l_s

    span = tq + WINDOW
    w_start = pl.multiple_of(jnp.maximum(i * tq - WINDOW, 0), tq)
    dist = q_pos - (w_start + lax.broadcasted_iota(jnp.int32, (1, span), 1))
    bias = jnp.where((dist >= 0) & (dist < WINDOW), 0.0, NEG_INF)[None]
    sw = _mxu_nt(q4, kw_ref[pl.ds(w_start, span), :]).reshape(GROUP_SIZE, tq, span) + bias
    ew = jnp.exp2((sw - jnp.max(sw, axis=-1, keepdims=True)) * EXP2_SCALE)
    pv = _mxu(ew.reshape(GROUP_SIZE * tq, span), _with_ones(vw_ref[pl.ds(w_start, span), :]))
    o_w = (pv[:, :HEAD_DIM] / pv[:, HEAD_DIM:]).reshape(GROUP_SIZE, tq, HEAD_DIM)

    gates = g_ref[...]
    for h in range(GROUP_SIZE):
        o = (gates[:, 3 * h:3 * h + 1] * o_c[h] + gates[:, 3 * h + 1:3 * h + 2] * o_s[h]
             + gates[:, 3 * h + 2:3 * h + 3] * o_w[h])
        o_ref[:, h * HEAD_DIM:(h + 1) * HEAD_DIM] = o.astype(o_ref.dtype)


def _attn_prompt(q, ckv, kv, gates, *, tq, tk):
    b, t, _ = q.shape
    n_cmp = ckv.shape[1]
    assert t % tq == 0 and t % tk == 0 and t >= tq + WINDOW and WINDOW % tq == 0 and tq % ROW_BLOCK == 0
    gw = GROUP_SIZE * HEAD_DIM

    def kv_spec(branch, part):
        return pl.BlockSpec((None, None, t, HEAD_DIM), lambda bi, g, i: (branch, bi, 0, 2 * g + part))

    return pl.pallas_call(
        functools.partial(_attn_prompt_kernel, tq=tq, tk=tk, seq=t),
        grid=(b, N_KV_HEADS, t // tq),
        in_specs=[pl.BlockSpec((None, tq, gw), lambda bi, g, i: (bi, i, g)),
                  pl.BlockSpec((None, n_cmp, HEAD_DIM), lambda bi, g, i: (bi, 0, 2 * g)),
                  pl.BlockSpec((None, n_cmp, HEAD_DIM), lambda bi, g, i: (bi, 0, 2 * g + 1)),
                  kv_spec(1, 0), kv_spec(1, 1), kv_spec(2, 0), kv_spec(2, 1),
                  pl.BlockSpec((None, tq, LANES), lambda bi, g, i: (bi, i, g))],
        out_specs=pl.BlockSpec((None, tq, gw), lambda bi, g, i: (bi, i, g)),
        out_shape=jax.ShapeDtypeStruct((b, t, Q_WIDTH), MXU_DTYPE),
        compiler_params=_params("arbitrary", "arbitrary", "arbitrary"),
        name="attn_prompt",
    )(q, ckv, ckv, kv, kv, kv, kv, gates)


def _attn_cmp_sample_kernel(q_ref, ckv_ref, o_ref, idx_ref, *, past, n_total, width):
    t_rows = q_ref.shape[0]
    n_cmp = ckv_ref.shape[0]
    n_sel_blocks = -(-n_total // SEL_BLOCK)
    n_take = min(N_SELECT, n_sel_blocks)
    assert SEL_RATIO == 2 and n_take <= LANES and SEL_RATIO * n_sel_blocks <= width
    q_pos = past + lax.broadcasted_iota(jnp.int32, (t_rows, 1), 0)
    lane = lax.broadcasted_iota(jnp.int32, (1, width), 1)
    vis = ((lane + 1) * CMP_BLOCK - 1 <= q_pos) & (lane < n_cmp)
    out_lane = lax.broadcasted_iota(jnp.int32, (1, LANES), 1)
    pad = jnp.zeros((width - n_cmp, HEAD_DIM), ckv_ref.dtype)
    imps = []
    for g in range(N_KV_HEADS):
        ck = jnp.concatenate([ckv_ref[:, 2 * g * HEAD_DIM:(2 * g + 1) * HEAD_DIM], pad], axis=0)
        cv = jnp.concatenate([ckv_ref[:, (2 * g + 1) * HEAD_DIM:(2 * g + 2) * HEAD_DIM], pad], axis=0)
        imp = jnp.zeros((t_rows, width), F32)
        for h in range(GROUP_SIZE):
            cols = slice((g * GROUP_SIZE + h) * HEAD_DIM, (g * GROUP_SIZE + h + 1) * HEAD_DIM)
            s = _mxu_nt(q_ref[:, cols], ck) * ATTN_SCALE
            p = jnp.where(vis, _softmax_rows(jnp.where(vis, s, NEG_INF)), 0.0)
            o_ref[:, cols] = _mxu(p, cv)
            imp = imp + p
        imps.append(imp)
    work = _block_scores(jnp.concatenate(imps, axis=0), jnp.concatenate([q_pos] * N_KV_HEADS, axis=0), n_sel_blocks)
    picks = jnp.zeros((N_KV_HEADS * t_rows, LANES), F32)
    for r in range(n_take):
        hit, pick = _take_top(work)
        work = jnp.where(hit, KNOCKED_OUT, work)
        picks = jnp.where(out_lane == r, pick, picks)
    idx = (picks * (1.0 / SEL_RATIO)).astype(jnp.int32)
    for g in range(N_KV_HEADS):
        idx_ref[g] = idx[g * t_rows:(g + 1) * t_rows]


def _attn_cmp_sample(q, ckv, *, past, n_total):
    b, t, qw = q.shape
    n_cmp = ckv.shape[1]
    width = -(-max(n_cmp, SEL_RATIO * -(-n_total // SEL_BLOCK)) // LANES) * LANES
    return pl.pallas_call(
        functools.partial(_attn_cmp_sample_kernel, past=past, n_total=n_total, width=width),
        grid=(b,),
        in_specs=[pl.BlockSpec((None, t, qw), lambda bi: (bi, 0, 0)),
                  pl.BlockSpec((None, n_cmp, KV_WIDTH), lambda bi: (bi, 0, 0))],
        out_specs=[pl.BlockSpec((None, t, qw), lambda bi: (bi, 0, 0)),
                   pl.BlockSpec((None, N_KV_HEADS, t, LANES), lambda bi: (bi, 0, 0, 0))],
        out_shape=[jax.ShapeDtypeStruct(q.shape, F32),
                   jax.ShapeDtypeStruct((b, N_KV_HEADS, t, LANES), jnp.int32)],
        compiler_params=_params("arbitrary"),
        name="attn_cmp_sample",
    )(q, ckv)


def _attn_win_sample_kernel(q_ref, buf_ref, new_ref, o_ref, *, t_new):
    t_rows = q_ref.shape[0]
    n_chunks = KV_WIDTH // HEAD_DIM
    n_buf = buf_ref.shape[0] // n_chunks
    t_idx = lax.broadcasted_iota(jnp.int32, (GROUP_SIZE * t_rows, 1), 0) % t_rows
    dist_a = t_idx + n_buf - lax.broadcasted_iota(jnp.int32, (1, n_buf), 1)
    row_b = lax.broadcasted_iota(jnp.int32, (1, new_ref.shape[0]), 1)
    mask_a = (dist_a >= 0) & (dist_a < WINDOW)
    mask_b = (row_b <= t_idx) & (t_idx - row_b < WINDOW) & (row_b < t_new)
    for g in range(N_KV_HEADS):
        k_a = buf_ref[pl.ds(2 * g, n_buf, stride=n_chunks), :]
        v_a = buf_ref[pl.ds(2 * g + 1, n_buf, stride=n_chunks), :]
        k_b = new_ref[:, 2 * g * HEAD_DIM:(2 * g + 1) * HEAD_DIM]
        v_b = new_ref[:, (2 * g + 1) * HEAD_DIM:(2 * g + 2) * HEAD_DIM]
        head_cols = [slice((g * GROUP_SIZE + h) * HEAD_DIM, (g * GROUP_SIZE + h + 1) * HEAD_DIM)
                     for h in range(GROUP_SIZE)]
        q = jnp.concatenate([q_ref[:, cols] for cols in head_cols], axis=0)
        s_a = jnp.where(mask_a, _mxu_nt(q, k_a) * ATTN_SCALE, NEG_INF)
        s_b = jnp.where(mask_b, _mxu_nt(q, k_b) * ATTN_SCALE, NEG_INF)
        m = jnp.maximum(jnp.max(s_a, axis=-1, keepdims=True), jnp.max(s_b, axis=-1, keepdims=True))
        e_a, e_b = jnp.exp(s_a - m), jnp.exp(s_b - m)
        l = jnp.sum(e_a, axis=-1, keepdims=True) + jnp.sum(e_b, axis=-1, keepdims=True)
        o = _mxu(e_a / l, v_a) + _mxu(e_b / l, v_b)
        for h, cols in enumerate(head_cols):
            o_ref[:, cols] = o[h * t_rows:(h + 1) * t_rows]


def _attn_win_sample(q, win_buf, kv_new, *, layer, t_new):
    b, t, qw = q.shape
    buf_rows = win_buf.shape[2]
    return pl.pallas_call(
        functools.partial(_attn_win_sample_kernel, t_new=t_new),
        grid=(b,),
        in_specs=[pl.BlockSpec((None, t, qw), lambda bi: (bi, 0, 0)),
                  pl.BlockSpec((None, None, buf_rows, HEAD_DIM), lambda bi: (layer, bi, 0, 0)),
                  pl.BlockSpec((None, kv_new.shape[1], KV_WIDTH), lambda bi: (bi, 0, 0))],
        out_specs=pl.BlockSpec((None, t, qw), lambda bi: (bi, 0, 0)),
        out_shape=jax.ShapeDtypeStruct(q.shape, F32),
        compiler_params=_params("arbitrary"),
        name="attn_win_sample",
    )(q, win_buf, kv_new)


def _attn_sel_sample_kernel(idx_ref, pt_ref, q_ref, *refs, past, t_new, n_slots, t_steps):
    del pt_ref
    blocks = refs[:n_slots]
    new_ref, oc_ref, ow_ref, g_ref, o_ref = refs[n_slots:]
    bi, t, gi = pl.program_id(0), pl.program_id(1), pl.program_id(2)
    base = ((bi * t_steps + t) * N_KV_HEADS + gi) * n_slots
    n_past_blocks = past // SEL_BLOCK
    q_pos = past + t
    q = q_ref[...]
    n_chunks = KV_WIDTH // HEAD_DIM
    lane = lax.broadcasted_iota(jnp.int32, (1, n_slots * SEL_BLOCK), 1)
    pos = lane % SEL_BLOCK
    has_new = jnp.int32(0)
    for n in range(n_slots):
        blk = idx_ref[base + n]
        start = jnp.where(blk < n_past_blocks, blk * SEL_BLOCK, q_pos + 1)
        pos = pos + jnp.where(lane // SEL_BLOCK == n, start, 0)
        has_new = has_new | (blk == n_past_blocks).astype(jnp.int32)
    mask_a = pos <= q_pos
    new = new_ref[...]
    row_b = lax.broadcasted_iota(jnp.int32, (1, new.shape[0]), 1)
    mask_b = (row_b <= t) & (row_b < t_new) & (has_new > 0)
    s_b = jnp.where(mask_b, _mxu_nt(q, new[:, :HEAD_DIM]) * ATTN_SCALE, NEG_INF)
    gates = g_ref[...]

    for g in range(N_KV_HEADS):
        @pl.when(gi == g)
        def _(g=g):
            keys = jnp.concatenate([blk[pl.ds(2 * g, SEL_BLOCK, stride=n_chunks), :] for blk in blocks], axis=0)
            vals = jnp.concatenate([blk[pl.ds(2 * g + 1, SEL_BLOCK, stride=n_chunks), :] for blk in blocks], axis=0)
            s_a = jnp.where(mask_a, _mxu_nt(q, keys) * ATTN_SCALE, NEG_INF)
            m = jnp.maximum(jnp.max(s_a, axis=-1, keepdims=True), jnp.max(s_b, axis=-1, keepdims=True))
            e_a, e_b = jnp.exp(s_a - m), jnp.exp(s_b - m)
            l = jnp.sum(e_a, axis=-1, keepdims=True) + jnp.sum(e_b, axis=-1, keepdims=True)
            o_s = _mxu(e_a / l, vals) + _mxu(e_b / l, new[:, HEAD_DIM:])
            o_ref[...] = gates[:, 0:1] * oc_ref[...] + gates[:, 1:2] * o_s + gates[:, 2:3] * ow_ref[...]


def _attn_sel_sample(idx, page_table, q, pool, kv_new, o_c, o_w, gates, *, layer, past, t_new):
    b, t, g, hg, hd = q.shape
    n_slots = idx.shape[-1]
    n_pages = page_table.shape[1]
    page = past // n_pages
    per_page = page // SEL_BLOCK
    n_past_blocks = past // SEL_BLOCK

    def slot_spec(n):
        def index(bi, ti, gi, idx_s, pt_s):
            blk = jnp.minimum(idx_s[((bi * t + ti) * g + gi) * n_slots + n], n_past_blocks - 1)
            phys = pt_s[bi * n_pages + blk // per_page]
            return (layer, phys, blk % per_page, 0)
        return pl.BlockSpec((None, None, SEL_BLOCK * (KV_WIDTH // HEAD_DIM), HEAD_DIM), index)

    head_spec = pl.BlockSpec((None, None, None, hg, hd), lambda bi, ti, gi, idx_s, pt_s: (bi, ti, gi, 0, 0))
    grid_spec = pltpu.PrefetchScalarGridSpec(
        num_scalar_prefetch=2,
        grid=(b, t, g),
        in_specs=[head_spec] + [slot_spec(n) for n in range(n_slots)] + [
            pl.BlockSpec((None, kv_new.shape[1], 2 * HEAD_DIM), lambda bi, ti, gi, idx_s, pt_s: (bi, 0, gi)),
            head_spec, head_spec,
            pl.BlockSpec((None, None, None, hg, 3), lambda bi, ti, gi, idx_s, pt_s: (bi, ti, gi, 0, 0))],
        out_specs=head_spec,
    )
    return pl.pallas_call(
        functools.partial(_attn_sel_sample_kernel, past=past, t_new=t_new, n_slots=n_slots, t_steps=t),
        grid_spec=grid_spec,
        out_shape=jax.ShapeDtypeStruct(q.shape, F32),
        compiler_params=_params("arbitrary", "arbitrary", "arbitrary"),
        name="attn_sel_sample",
    )(idx.reshape(-1), page_table.reshape(-1), q, *([pool] * n_slots), kv_new, o_c, o_w, gates)


def _dwconv_kernel(cur_ref, prev_ref, buf_ref, w_ref, b_ref, lg_ref, lb_ref, o_ref, full_scr, y_scr, *,
                   tt, halo, width):
    i = pl.program_id(1)
    lead = halo - (width - 1)
    full_scr[0:halo, :] = prev_ref[...]

    @pl.when(i == 0)
    def _():
        full_scr[0:halo, :] = buf_ref[...]

    full_scr[halo:halo + tt, :] = cur_ref[...]
    for c0 in range(0, cur_ref.shape[1], DWCONV_COLS):
        cols = slice(c0, c0 + DWCONV_COLS)
        acc = jnp.zeros((tt, DWCONV_COLS), F32) + b_ref[:, cols]
        for r in range(SUBLANES):
            offsets = [o for o in range(r, lead + width, SUBLANES) if o >= lead]
            rows = tt + (SUBLANES if r else 0)
            group = None
            for o in offsets:
                term = full_scr[o - r:o - r + rows, cols] * w_ref[o - lead:o - lead + 1, cols]
                group = term if group is None else group + term
            if group is not None:
                acc = acc + group[r:r + tt]
        y_scr[:, cols] = acc
    y = y_scr[...]
    mu = jnp.mean(y, axis=-1, keepdims=True)
    var = jnp.mean(jnp.square(y - mu), axis=-1, keepdims=True)
    yn = (y - mu) * lax.rsqrt(var + LN_EPS) * lg_ref[...] + lb_ref[...]
    o_ref[...] = _silu(yn).astype(o_ref.dtype)


def _dwconv(glu, buf, w_dw, b_dw, ln_g, ln_b, *, layer, tt):
    b, t, d = glu.shape
    width = w_dw.shape[1]
    halo = buf.shape[1]
    assert tt % halo == 0 and t % tt == 0 and halo >= width - 1
    per = tt // halo

    def vec_spec():
        return pl.BlockSpec((None, 1, d), lambda bi, i: (layer, 0, 0))

    return pl.pallas_call(
        functools.partial(_dwconv_kernel, tt=tt, halo=halo, width=width),
        grid=(b, t // tt),
        in_specs=[pl.BlockSpec((None, tt, d), lambda bi, i: (bi, i, 0)),
                  pl.BlockSpec((None, halo, d), lambda bi, i: (bi, jnp.maximum(i * per - 1, 0), 0)),
                  pl.BlockSpec((None, halo, d), lambda bi, i: (bi, 0, 0)),
                  pl.BlockSpec((None, width, d), lambda bi, i: (layer, 0, 0)),
                  vec_spec(), vec_spec(), vec_spec()],
        out_specs=pl.BlockSpec((None, tt, d), lambda bi, i: (bi, i, 0)),
        out_shape=jax.ShapeDtypeStruct((b, t, d), MXU_DTYPE),
        scratch_shapes=[pltpu.VMEM((halo + tt, d), F32), pltpu.VMEM((tt, d), F32)],
        compiler_params=_params("arbitrary", "arbitrary"),
        name="dwconv",
    )(glu, glu, buf, w_dw, b_dw, ln_g, ln_b)


def _rope_tables(pos):
    half = HEAD_DIM // 2
    inv = ROPE_THETA ** (-jnp.arange(half, dtype=F32) / half)
    ang = pos.astype(F32)[:, None] * inv[None, :]
    cos, sin = jnp.cos(ang), jnp.sin(ang)
    return jnp.concatenate([cos, cos], axis=1), jnp.concatenate([-sin, sin], axis=1)


def _group_gate_weights(w_gate_cols):
    d = w_gate_cols.shape[0]
    wg = w_gate_cols.reshape(d, N_KV_HEADS, GATE_COLS)
    return jnp.pad(wg, ((0, 0), (0, 0), (0, LANES - GATE_COLS))).reshape(d, N_KV_HEADS * LANES)


def _run_trunk(x, mod, pos, prm, caches, tiles):
    b, t, d = x.shape
    m = b * t
    depth = prm["w_mod"].shape[0]
    tm = tiles["tm"]
    rpm = t if caches is None else 1
    cos, sin = _rope_tables(pos)
    if caches is not None:
        cos, sin = jnp.tile(cos, (b, 1)), jnp.tile(sin, (b, 1))
    rows_per_seq = t if caches is None else m
    x = x.reshape(m, d)
    new_c, new_s, new_w, new_conv = [], [], [], []

    def mod_rows(layer, k):
        v = mod[layer, :, k, :]
        if caches is not None:
            v = jnp.repeat(v, t, axis=0)
        return v[:, None, :]

    def ffn(x, layer, which, k0):
        return _ffn(x, prm["norm_g"][layer, which * 2].reshape(1, d), mod_rows(layer, k0), mod_rows(layer, k0 + 1),
                    mod_rows(layer, k0 + 2), prm["final_norm_g"].reshape(1, d), prm["ffn_w_gate"], prm["ffn_w_up"],
                    prm["ffn_w_down"], layer=layer, which=which, rows_per_mod=rpm, tm=tiles["tm_ffn"],
                    tf=tiles["tf"], final_norm=(layer == depth - 1 and which == 1))

    for i in range(depth):
        x = ffn(x, i, 0, 0)
        ng = prm["norm_g"][i, 1].reshape(1, d)
        shift, scale, gate = mod_rows(i, 3), mod_rows(i, 4), mod_rows(i, 5)
        a = i // 2
        if i % 2 == 0:
            w_in = prm["attn_w_in"]
            common = dict(rows_per_mod=rpm, rows_per_seq=rows_per_seq, tm=tm)
            q_dtype = MXU_DTYPE if caches is None else F32
            q, = _proj(x, ng, shift, scale, w_in, cos, sin, layer=a, mode="q", col0=0, n=Q_WIDTH,
                       out_dtypes=[q_dtype], tn=512, **common)
            kv_outs = [F32, MXU_DTYPE] if caches is None else [F32]
            kv = _proj(x, ng, shift, scale, w_in, cos, sin, layer=a, mode="kv", col0=Q_WIDTH, n=3 * KV_WIDTH,
                       out_dtypes=kv_outs, tn=KV_WIDTH, **common)
            gates, = _proj(x, ng, shift, scale, _group_gate_weights(prm["attn_gate_w"][a])[None], cos, sin, layer=0,
                           mode="sigmoid", col0=0, n=N_KV_HEADS * LANES, out_dtypes=[F32],
                           tn=N_KV_HEADS * LANES, **common)
            kv32 = kv[0]
            new_c.append(kv32[0].reshape(b, t, N_KV_HEADS, 2, HEAD_DIM))
            new_s.append(kv32[1].reshape(b, t, N_KV_HEADS, 2, HEAD_DIM))
            if caches is None:
                ckv = _compress_rows(kv32[0], prm["cmp_pe"], prm["cmp_w1"], prm["cmp_w2"], layer=a, n_seq=b)
                o = _attn_prompt(q.reshape(b, t, Q_WIDTH), ckv.reshape(b, t // CMP_BLOCK, KV_WIDTH),
                                 kv[1].reshape(3, b, t, KV_WIDTH), gates.reshape(b, t, N_KV_HEADS * LANES),
                                 tq=tiles["tq"], tk=tiles["tk"]).reshape(m, Q_WIDTH)
                new_w.append(kv32[2].reshape(b, t, N_KV_HEADS, 2, HEAD_DIM)[:, t - min(WINDOW, t):])
            else:
                pool_c, pool_s, win_buf, _, page_table = caches
                n_layers, n_pool, page = pool_c.shape[:3]
                past = page_table.shape[1] * page
                assert t < CMP_BLOCK and page % SEL_BLOCK == 0
                ckv = _compress_pages(pool_c.reshape(n_layers, n_pool, page * (KV_WIDTH // HEAD_DIM), HEAD_DIM),
                                      page_table, prm["cmp_pe"], prm["cmp_w1"], prm["cmp_w2"], layer=a)
                q3 = q.reshape(b, t, Q_WIDTH)
                o_c, idx = _attn_cmp_sample(q3, ckv, past=past, n_total=past + t)
                t_pad = -(-t // SUBLANES) * SUBLANES
                pad_rows = lambda r: jnp.pad(r.reshape(b, t, KV_WIDTH), ((0, 0), (0, t_pad - t), (0, 0)))
                n_buf = win_buf.shape[2]
                o_w = _attn_win_sample(q3, win_buf.reshape(n_layers, b, n_buf * (KV_WIDTH // HEAD_DIM), HEAD_DIM),
                                       pad_rows(kv32[2]), layer=a, t_new=t)
                n_take = min(N_SELECT, -(-(past + t) // SEL_BLOCK))
                idx = jnp.transpose(idx[..., :n_take], (0, 2, 1, 3))
                heads = lambda v: v.reshape(b, t, N_KV_HEADS, GROUP_SIZE, HEAD_DIM)
                g5 = gates.reshape(b, t, N_KV_HEADS, LANES)[..., :GATE_COLS].reshape(b, t, N_KV_HEADS, GROUP_SIZE, 3)
                o = _attn_sel_sample(idx, page_table, heads(q3),
                                     pool_s.reshape(n_layers, n_pool, page * (KV_WIDTH // HEAD_DIM), HEAD_DIM),
                                     pad_rows(kv32[1]), heads(o_c), heads(o_w), g5, layer=a, past=past,
                                     t_new=t).reshape(m, Q_WIDTH)
                new_w.append(kv32[2].reshape(b, t, N_KV_HEADS, 2, HEAD_DIM))
            x = _linres(o, prm["attn_w_out"], x, gate, layer=a, rows_per_mod=rpm, tm=tm, tn=tiles["tn_out"])
        else:
            glu = _glu_proj(x, ng, shift, scale, prm["conv_w_pw1"], layer=a, rows_per_mod=rpm, tm=tm, tn=512)
            glu = glu.reshape(b, t, d)
            width = prm["conv_w_dw"].shape[1]
            halo = 32
            if caches is None:
                buf = jnp.zeros((b, width - 1, d), F32)
                t_conv = t
                cur = glu
            else:
                buf = caches[3][a]
                t_conv = halo
                cur = jnp.pad(glu, ((0, 0), (0, t_conv - t), (0, 0)))
            if t >= width - 1:
                new_conv.append(glu[:, t - (width - 1):])
            else:
                new_conv.append(jnp.concatenate([buf[:, t:], glu], axis=1))
            buf = jnp.pad(buf, ((0, 0), (halo - (width - 1), 0), (0, 0)))
            vec = lambda v: v.reshape(v.shape[0], 1, d)
            act = _dwconv(cur, buf, prm["conv_w_dw"], vec(prm["conv_b_dw"]), vec(prm["conv_ln_g"]),
                          vec(prm["conv_ln_b"]), layer=a, tt=min(tiles["tt"], t_conv))
            act = act[:, :t].reshape(m, d)
            x = _linres(act, prm["conv_w_pw2"], x, gate, layer=a, rows_per_mod=rpm, tm=tm, tn=tiles["tn_out"])
        x = ffn(x, i, 1, 6)
    y = x.reshape(b, t, d)
    new_w = jnp.stack(new_w)
    if caches is not None:
        keep = min(WINDOW, caches[2].shape[2] + t) - t
        new_w = jnp.concatenate([caches[2][:, :, caches[2].shape[2] - keep:], new_w], axis=2)
    return y, jnp.stack(new_c), jnp.stack(new_s), new_w, jnp.stack(new_conv)


PROMPT_TILES = dict(tm=1024, tm_ffn=1024, tf=512, tq=256, tk=512, tn_out=512, tt=128)
SAMPLE_TILES = dict(tm=32, tm_ffn=32, tf=512, tn_out=512, tt=32)


def kernel(x_prompt, x_sample, cache_cmp_kv, cache_sel_kv, cache_win_kv, state_conv, page_table, c_prompt, c_sample, w_mod, b_mod, norm_g, ffn_w_gate, ffn_w_up, ffn_w_down, attn_w_in, attn_w_out, cmp_pe, cmp_w1, cmp_w2, conv_w_pw1, conv_w_dw, conv_b_dw, conv_ln_g, conv_ln_b, conv_w_pw2, final_norm_g):
    mxu = lambda w: w.astype(MXU_DTYPE)
    prm = {"w_mod": w_mod, "norm_g": norm_g, "ffn_w_gate": mxu(ffn_w_gate), "ffn_w_up": mxu(ffn_w_up),
           "ffn_w_down": mxu(ffn_w_down), "attn_w_in": mxu(attn_w_in), "attn_gate_w": attn_w_in[..., -N_HEADS * 3:],
           "attn_w_out": mxu(attn_w_out), "cmp_pe": cmp_pe,
           "cmp_w1": cmp_w1, "cmp_w2": cmp_w2, "conv_w_pw1": mxu(conv_w_pw1), "conv_w_dw": conv_w_dw,
           "conv_b_dw": conv_b_dw, "conv_ln_g": conv_ln_g, "conv_ln_b": conv_ln_b, "conv_w_pw2": mxu(conv_w_pw2),
           "final_norm_g": final_norm_g}
    depth, d, _ = w_mod.shape
    bp, tp = x_prompt.shape[:2]
    bs, ts = x_sample.shape[:2]
    past = page_table.shape[1] * cache_cmp_kv.shape[2]

    n_req = bp + bs
    r_pad = -(-n_req // 8) * 8
    c_all = jnp.pad(jnp.concatenate([c_prompt, c_sample], axis=0), ((0, r_pad - n_req), (0, 0)))
    mod = _mod_vectors(c_all, w_mod, b_mod).reshape(depth, r_pad, N_MOD, d)

    pos_p = jnp.arange(tp, dtype=jnp.int32)
    pos_s = past + jnp.arange(ts, dtype=jnp.int32)
    y_p, p_cmp, p_sel, p_win, p_conv = _run_trunk(x_prompt, mod[:, :bp], pos_p, prm, None, PROMPT_TILES)
    y_s, s_cmp, s_sel, s_win, s_conv = _run_trunk(
        x_sample, mod[:, bp:n_req], pos_s, prm,
        (cache_cmp_kv, cache_sel_kv, cache_win_kv, state_conv, page_table), SAMPLE_TILES)
    return (y_p, y_s, p_cmp, p_sel, p_win, p_conv, s_cmp, s_sel, s_win, s_conv)
```

```python
import functools

import jax
import jax.numpy as jnp
from jax import lax
from jax.experimental import pallas as pl
from jax.experimental.pallas import tpu as pltpu

F32 = jnp.float32
MXU_DTYPE = jnp.bfloat16
VMEM_LIMIT_BYTES = 60 * 1024 * 1024
LANES = 128
SUBLANES = 8

N_HEADS = 16
HEAD_DIM = 128
N_KV_HEADS = 4
GROUP_SIZE = N_HEADS // N_KV_HEADS
CMP_BLOCK = 32
SEL_BLOCK = 64
SEL_RATIO = SEL_BLOCK // CMP_BLOCK
N_SELECT = 16
WINDOW = 512
ROPE_THETA = 10000.0
N_MOD = 9
RMS_EPS = 1e-6
LN_EPS = 1e-5
NEG_INF = -1e30
FORCED_SCORE = 1e9
KNOCKED_OUT = -3e38
ATTN_SCALE = HEAD_DIM ** -0.5
Q_WIDTH = N_HEADS * HEAD_DIM
KV_WIDTH = 2 * N_KV_HEADS * HEAD_DIM
GATE_COLS = GROUP_SIZE * 3
PAGES_PER_STEP = 16
DWCONV_COLS = 128
ROW_BLOCK = 32
FFN_PANEL = 256
PROJ_PANEL = 256
EXP2_SCALE = ATTN_SCALE * 1.4426950408889634


def _params(*sem):
    return pltpu.CompilerParams(dimension_semantics=sem, vmem_limit_bytes=VMEM_LIMIT_BYTES)


def _mxu(a, b):
    return jnp.dot(a.astype(MXU_DTYPE), b.astype(MXU_DTYPE), preferred_element_type=F32)


def _mxu_nt(a, b):
    return lax.dot_general(a.astype(MXU_DTYPE), b.astype(MXU_DTYPE),
                           (((1,), (1,)), ((), ())), preferred_element_type=F32)


def _silu(x):
    return x * jax.nn.sigmoid(x)


def _softmax_rows(s):
    e = jnp.exp(s - jnp.max(s, axis=-1, keepdims=True))
    return e / jnp.sum(e, axis=-1, keepdims=True)


def _softmax_scaled(s):
    e = jnp.exp2((s - jnp.max(s, axis=-1, keepdims=True)) * EXP2_SCALE)
    return e / jnp.sum(e, axis=-1, keepdims=True)


def _mod_spec(tm, rows_per_mod, width, slot, by_column=False):
    if rows_per_mod >= tm:
        assert rows_per_mod % tm == 0
        block_rows, idx = 1, (lambda i: (i * tm) // rows_per_mod)
    else:
        assert rows_per_mod == 1
        block_rows, idx = tm, (lambda i: i)
    return pl.BlockSpec((None, block_rows, 1, width), lambda i, j: (slot, idx(i), 0, j if by_column else 0))


def _gain_spec(d, slot):
    return pl.BlockSpec((None, 1, d), lambda i, j: (slot, 0, 0))


def _modulated(x_ref, ng_ref, sh_ref, sc_ref):
    x = x_ref[...]
    gain = ng_ref[...] * (1.0 + sc_ref[:, 0, :])
    return x * lax.rsqrt(jnp.mean(x * x, axis=-1, keepdims=True) + RMS_EPS) * gain + sh_ref[:, 0, :]


def _mod_kernel(c_ref, w_ref, b_ref, o_ref):
    o_ref[...] = _mxu(_silu(c_ref[...]), w_ref[...]) + b_ref[...]


def _mod_vectors(c_all, w_mod, b_mod, tn=1024):
    depth, d, n = w_mod.shape
    r = c_all.shape[0]
    return pl.pallas_call(
        _mod_kernel,
        grid=(depth, n // tn),
        in_specs=[pl.BlockSpec((r, d), lambda l, j: (0, 0)),
                  pl.BlockSpec((None, d, tn), lambda l, j: (l, 0, j)),
                  pl.BlockSpec((None, 1, tn), lambda l, j: (l, 0, j))],
        out_specs=pl.BlockSpec((None, r, tn), lambda l, j: (l, 0, j)),
        out_shape=jax.ShapeDtypeStruct((depth, r, n), F32),
        compiler_params=_params("arbitrary", "arbitrary"),
        name="mod_vectors",
    )(c_all, w_mod, b_mod.reshape(depth, 1, n))


def _ffn_kernel(x_ref, ng_ref, sh_ref, sc_ref, gt_ref, fg_ref, wg_ref, wu_ref, wd_ref, o_ref, h_scr, *, final_norm):
    f = pl.program_id(1)

    def panels(first):
        panel = min(FFN_PANEL, h_scr.shape[0])
        for p0 in range(0, h_scr.shape[0], panel):
            rows = slice(p0, p0 + panel)
            h = h_scr[rows, :]
            g = _mxu(h, wg_ref[...])
            u = _mxu(h, wu_ref[...])
            a = jnp.concatenate([(_silu(g[r0:r0 + ROW_BLOCK]) * u[r0:r0 + ROW_BLOCK]).astype(MXU_DTYPE)
                                 for r0 in range(0, panel, ROW_BLOCK)], axis=0)
            y = _mxu(a, wd_ref[...])
            if first:
                o_ref[rows, :] = y
            else:
                o_ref[rows, :] += y

    @pl.when(f == 0)
    def _():
        h_scr[...] = _modulated(x_ref, ng_ref, sh_ref, sc_ref).astype(h_scr.dtype)
        panels(True)

    @pl.when(f > 0)
    def _():
        panels(False)

    @pl.when(f == pl.num_programs(1) - 1)
    def _():
        y = x_ref[...] + (0.5 * gt_ref[:, 0, :]) * o_ref[...]
        if final_norm:
            y = y * lax.rsqrt(jnp.mean(y * y, axis=-1, keepdims=True) + RMS_EPS) * fg_ref[...]
        o_ref[...] = y


def _ffn(x, norm_tab, mod_tab, final_g, w_gate, w_up, w_down, *, norm_slot, mod_slot, layer, which, rows_per_mod,
         tm, tf, final_norm):
    m, d = x.shape
    dff = w_gate.shape[-1]
    return pl.pallas_call(
        functools.partial(_ffn_kernel, final_norm=final_norm),
        grid=(m // tm, dff // tf),
        in_specs=[pl.BlockSpec((tm, d), lambda i, f: (i, 0)),
                  _gain_spec(d, norm_slot),
                  _mod_spec(tm, rows_per_mod, d, mod_slot), _mod_spec(tm, rows_per_mod, d, mod_slot + 1),
                  _mod_spec(tm, rows_per_mod, d, mod_slot + 2),
                  pl.BlockSpec((1, d), lambda i, f: (0, 0)),
                  pl.BlockSpec((None, None, d, tf), lambda i, f: (layer, which, 0, f)),
                  pl.BlockSpec((None, None, d, tf), lambda i, f: (layer, which, 0, f)),
                  pl.BlockSpec((None, None, tf, d), lambda i, f: (layer, which, f, 0))],
        out_specs=pl.BlockSpec((tm, d), lambda i, f: (i, 0)),
        out_shape=jax.ShapeDtypeStruct((m, d), F32),
        scratch_shapes=[pltpu.VMEM((tm, d), MXU_DTYPE)],
        compiler_params=_params("arbitrary", "arbitrary"),
        name="ffn",
    )(x, norm_tab, mod_tab, mod_tab, mod_tab, final_g, w_gate, w_up, w_down)


def _rope_chunk(z, cos, sin):
    return z * cos + pltpu.roll(z, HEAD_DIM // 2, 1) * sin


def _proj_kernel(x_ref, ng_ref, sh_ref, sc_ref, w_ref, cos_ref, sin_ref, *rest, mode):
    out_refs, h_scr = rest[:-1], rest[-1]

    @pl.when(pl.program_id(1) == 0)
    def _():
        h_scr[...] = _modulated(x_ref, ng_ref, sh_ref, sc_ref).astype(h_scr.dtype)

    tm = h_scr.shape[0]
    panel = min(PROJ_PANEL, tm)
    n_chunks = w_ref.shape[1] // HEAD_DIM
    for p0 in range(0, tm, panel):
        rows = slice(p0, p0 + panel)
        z = _mxu(h_scr[rows, :], w_ref[...])
        if mode == "sigmoid":
            out_refs[0][rows, :] = jax.nn.sigmoid(z)
            continue
        cos, sin = cos_ref[rows, :], sin_ref[rows, :]
        for c in range(n_chunks):
            zc = z[:, c * HEAD_DIM:(c + 1) * HEAD_DIM]
            if mode == "q" or c % 2 == 0:
                zc = _rope_chunk(zc, cos, sin)
            for k, o in enumerate(out_refs):
                if mode == "kv" and k == 0:
                    o[pl.ds(p0 * n_chunks + c, panel, stride=n_chunks), :] = zc
                else:
                    o[rows, c * HEAD_DIM:(c + 1) * HEAD_DIM] = zc.astype(o.dtype)


def _proj(x, norm_tab, mod_tab, w, cos, sin, *, norm_slot, mod_slot, layer, mode, col0, n, out_dtypes, rows_per_mod,
          rows_per_seq, tm, tn):
    m, d = x.shape
    seq_tiles = rows_per_seq // tm
    assert col0 % tn == 0 and n % tn == 0 and rows_per_seq % tm == 0
    if mode == "kv":
        assert tn == KV_WIDTH and out_dtypes[0] == F32
        n_chunks = KV_WIDTH // HEAD_DIM
        out_specs = [pl.BlockSpec((None, tm * n_chunks, HEAD_DIM), lambda i, j: (j, i, 0))] + [
            pl.BlockSpec((None, tm, tn), lambda i, j: (j, i, 0)) for _ in out_dtypes[1:]]
        out_shape = [jax.ShapeDtypeStruct((n // KV_WIDTH, m * n_chunks, HEAD_DIM), F32)] + [
            jax.ShapeDtypeStruct((n // KV_WIDTH, m, KV_WIDTH), dt) for dt in out_dtypes[1:]]
    else:
        out_specs = [pl.BlockSpec((tm, tn), lambda i, j: (i, j)) for _ in out_dtypes]
        out_shape = [jax.ShapeDtypeStruct((m, n), dt) for dt in out_dtypes]
    return pl.pallas_call(
        functools.partial(_proj_kernel, mode=mode),
        grid=(m // tm, n // tn),
        in_specs=[pl.BlockSpec((tm, d), lambda i, j: (i, 0)),
                  _gain_spec(d, norm_slot),
                  _mod_spec(tm, rows_per_mod, d, mod_slot), _mod_spec(tm, rows_per_mod, d, mod_slot + 1),
                  pl.BlockSpec((None, d, tn), lambda i, j: (layer, 0, col0 // tn + j)),
                  pl.BlockSpec((tm, HEAD_DIM), lambda i, j: (i % seq_tiles, 0)),
                  pl.BlockSpec((tm, HEAD_DIM), lambda i, j: (i % seq_tiles, 0))],
        out_specs=out_specs,
        out_shape=out_shape,
        scratch_shapes=[pltpu.VMEM((tm, d), MXU_DTYPE)],
        compiler_params=_params("arbitrary", "arbitrary"),
        name="proj_" + mode,
    )(x, norm_tab, mod_tab, mod_tab, w, cos, sin)


def _glu_kernel(x_ref, ng_ref, sh_ref, sc_ref, wa_ref, wb_ref, o_ref, h_scr):
    @pl.when(pl.program_id(1) == 0)
    def _():
        h_scr[...] = _modulated(x_ref, ng_ref, sh_ref, sc_ref).astype(h_scr.dtype)

    panel = min(PROJ_PANEL, h_scr.shape[0])
    for p0 in range(0, h_scr.shape[0], panel):
        h = h_scr[p0:p0 + panel, :]
        o_ref[p0:p0 + panel, :] = _mxu(h, wa_ref[...]) * jax.nn.sigmoid(_mxu(h, wb_ref[...]))


def _glu_proj(x, norm_tab, mod_tab, w_pw1, *, norm_slot, mod_slot, layer, rows_per_mod, tm, tn):
    m, d = x.shape
    dc = w_pw1.shape[-1] // 2
    return pl.pallas_call(
        _glu_kernel,
        grid=(m // tm, dc // tn),
        in_specs=[pl.BlockSpec((tm, d), lambda i, j: (i, 0)),
                  _gain_spec(d, norm_slot),
                  _mod_spec(tm, rows_per_mod, d, mod_slot), _mod_spec(tm, rows_per_mod, d, mod_slot + 1),
                  pl.BlockSpec((None, d, tn), lambda i, j: (layer, 0, j)),
                  pl.BlockSpec((None, d, tn), lambda i, j: (layer, 0, dc // tn + j))],
        out_specs=pl.BlockSpec((tm, tn), lambda i, j: (i, j)),
        out_shape=jax.ShapeDtypeStruct((m, dc), F32),
        scratch_shapes=[pltpu.VMEM((tm, d), MXU_DTYPE)],
        compiler_params=_params("arbitrary", "arbitrary"),
        name="glu_proj",
    )(x, norm_tab, mod_tab, mod_tab, w_pw1, w_pw1)


def _linres_kernel(a_ref, w_ref, x_ref, gt_ref, o_ref):
    tm = a_ref.shape[0]
    panel = min(PROJ_PANEL, tm)
    for p0 in range(0, tm, panel):
        rows = slice(p0, p0 + panel)
        gate = gt_ref[:, 0, :] if gt_ref.shape[0] == 1 else gt_ref[rows, 0, :]
        o_ref[rows, :] = x_ref[rows, :] + gate * _mxu(a_ref[rows, :], w_ref[...])


def _linres(a, w, x, mod_tab, *, gate_slot, layer, rows_per_mod, tm, tn):
    m, k = a.shape
    d = x.shape[1]
    gspec = _mod_spec(tm, rows_per_mod, tn, gate_slot, by_column=True)
    return pl.pallas_call(
        _linres_kernel,
        grid=(m // tm, d // tn),
        in_specs=[pl.BlockSpec((tm, k), lambda i, j: (i, 0)),
                  pl.BlockSpec((None, k, tn), lambda i, j: (layer, 0, j)),
                  pl.BlockSpec((tm, tn), lambda i, j: (i, j)),
                  gspec],
        out_specs=pl.BlockSpec((tm, tn), lambda i, j: (i, j)),
        out_shape=jax.ShapeDtypeStruct((m, d), F32),
        compiler_params=_params("arbitrary", "arbitrary"),
        name="linres",
    )(a, w, x, mod_tab)


def _compress_kernel(x_ref, pe_ref, w1_ref, w2_ref, o_ref):
    n_chunks = KV_WIDTH // HEAD_DIM
    nb = x_ref.shape[0] // (CMP_BLOCK * n_chunks)
    for kv in range(2):
        acc = jnp.zeros((N_KV_HEADS * nb, HEAD_DIM), F32)
        for c in range(CMP_BLOCK):
            xc = jnp.concatenate([x_ref[pl.ds(c * n_chunks + 2 * g + kv, nb, stride=CMP_BLOCK * n_chunks), :]
                                  for g in range(N_KV_HEADS)], axis=0) + pe_ref[kv, c:c + 1, :]
            acc += _mxu(xc, w1_ref[kv, c * HEAD_DIM:(c + 1) * HEAD_DIM, :])
        y = _mxu(_silu(acc), w2_ref[kv])
        for g in range(N_KV_HEADS):
            o_ref[:, (2 * g + kv) * HEAD_DIM:(2 * g + kv + 1) * HEAD_DIM] = y[g * nb:(g + 1) * nb].astype(o_ref.dtype)


def _compress_rows(rows, pe, w1, w2, *, layer, n_seq):
    n_chunks = KV_WIDTH // HEAD_DIM
    n_blocks = rows.shape[0] // (CMP_BLOCK * n_chunks)
    nb = n_blocks // n_seq
    return pl.pallas_call(
        _compress_kernel,
        grid=(n_seq,),
        in_specs=[pl.BlockSpec((nb * CMP_BLOCK * n_chunks, HEAD_DIM), lambda i: (i, 0)),
                  pl.BlockSpec((None, 2, CMP_BLOCK, HEAD_DIM), lambda i: (layer, 0, 0, 0)),
                  pl.BlockSpec((None, 2, CMP_BLOCK * HEAD_DIM, HEAD_DIM), lambda i: (layer, 0, 0, 0)),
                  pl.BlockSpec((None, 2, HEAD_DIM, HEAD_DIM), lambda i: (layer, 0, 0, 0))],
        out_specs=pl.BlockSpec((nb, KV_WIDTH), lambda i: (i, 0)),
        out_shape=jax.ShapeDtypeStruct((n_blocks, KV_WIDTH), MXU_DTYPE),
        compiler_params=_params("arbitrary"),
        name="compress_rows",
    )(rows, pe, w1, w2)


def _compress_pages_kernel(pt_ref, *refs, blocks_per_page):
    del pt_ref
    pages = refs[:PAGES_PER_STEP]
    pe_ref, w1_ref, w2_ref, o_ref, t_scr = refs[PAGES_PER_STEP:]
    nb = PAGES_PER_STEP * blocks_per_page
    pages_per_group = SUBLANES // blocks_per_page
    n_groups = PAGES_PER_STEP // pages_per_group
    rows = SUBLANES * CMP_BLOCK
    out_row = lax.broadcasted_iota(jnp.int32, (rows, rows), 0)
    in_row = lax.broadcasted_iota(jnp.int32, (rows, rows), 1)
    perm = jnp.where(in_row == (out_row % SUBLANES) * CMP_BLOCK + out_row // SUBLANES, 1.0, 0.0)
    pe_rows = jnp.concatenate([pe_ref[kv] for _ in range(N_KV_HEADS) for kv in range(2)], axis=1)
    pe_rows = jnp.concatenate([pe_rows] * SUBLANES, axis=0)
    n_chunks = KV_WIDTH // HEAD_DIM
    page_rows = pages[0].shape[0] // n_chunks
    for gp in range(n_groups):
        x = jnp.concatenate(
            [jnp.concatenate([pages[gp * pages_per_group + k][pl.ds(ch, page_rows, stride=n_chunks), :]
                              for ch in range(n_chunks)], axis=1)
             for k in range(pages_per_group)], axis=0)
        t_scr[gp] = _mxu(perm, x + pe_rows)
    def block_rows(c, kv):
        return jnp.concatenate(
            [t_scr[:, c * SUBLANES:(c + 1) * SUBLANES,
                   (g * 2 + kv) * HEAD_DIM:(g * 2 + kv + 1) * HEAD_DIM].reshape(nb, HEAD_DIM)
             for g in range(N_KV_HEADS)], axis=0)

    for kv in range(2):
        acc = jnp.zeros((N_KV_HEADS * nb, HEAD_DIM), F32)
        for c in range(0, CMP_BLOCK, 2):
            xc = jnp.concatenate([block_rows(c, kv), block_rows(c + 1, kv)], axis=1)
            acc += _mxu(xc, w1_ref[kv, c * HEAD_DIM:(c + 2) * HEAD_DIM, :])
        y = _mxu(_silu(acc), w2_ref[kv])
        for g in range(N_KV_HEADS):
            o_ref[:, (g * 2 + kv) * HEAD_DIM:(g * 2 + kv + 1) * HEAD_DIM] = (
                y[g * nb:(g + 1) * nb].astype(o_ref.dtype))


def _compress_pages(pool, page_table, pe, w1, w2, *, layer):
    width = KV_WIDTH
    n_chunks = width // HEAD_DIM
    page = pool.shape[2] // n_chunks
    b, n_pages = page_table.shape
    bpp = page // CMP_BLOCK
    nb = PAGES_PER_STEP * bpp
    assert n_pages % PAGES_PER_STEP == 0 and page % CMP_BLOCK == 0 and SUBLANES % bpp == 0
    n_groups = nb // SUBLANES

    def page_spec(p):
        return pl.BlockSpec((None, None, page * n_chunks, HEAD_DIM),
                            lambda bi, gi, pt: (layer, pt[bi * n_pages + gi * PAGES_PER_STEP + p], 0, 0))

    grid_spec = pltpu.PrefetchScalarGridSpec(
        num_scalar_prefetch=1,
        grid=(b, n_pages // PAGES_PER_STEP),
        in_specs=[page_spec(p) for p in range(PAGES_PER_STEP)] + [
            pl.BlockSpec((None, 2, CMP_BLOCK, HEAD_DIM), lambda bi, gi, pt: (layer, 0, 0, 0)),
            pl.BlockSpec((None, 2, CMP_BLOCK * HEAD_DIM, HEAD_DIM), lambda bi, gi, pt: (layer, 0, 0, 0)),
            pl.BlockSpec((None, 2, HEAD_DIM, HEAD_DIM), lambda bi, gi, pt: (layer, 0, 0, 0))],
        out_specs=pl.BlockSpec((None, nb, width), lambda bi, gi, pt: (bi, gi, 0)),
        scratch_shapes=[pltpu.VMEM((n_groups, SUBLANES * CMP_BLOCK, width), F32)],
    )
    return pl.pallas_call(
        functools.partial(_compress_pages_kernel, blocks_per_page=bpp),
        grid_spec=grid_spec,
        out_shape=jax.ShapeDtypeStruct((b, n_pages * bpp, width), MXU_DTYPE),
        compiler_params=_params("arbitrary", "arbitrary"),
        name="compress_pages",
    )(page_table.reshape(-1), *([pool] * PAGES_PER_STEP), pe, w1, w2)


def _pair_sums(imp):
    out = []
    for k in range(imp.shape[1] // LANES):
        x = imp[:, k * LANES:(k + 1) * LANES]
        even = lax.broadcasted_iota(jnp.int32, x.shape, 1) % 2 == 0
        out.append(x + jnp.where(even, pltpu.roll(x, LANES - 1, 1), pltpu.roll(x, 1, 1)))
    return out[0] if len(out) == 1 else jnp.concatenate(out, axis=1)


def _block_scores(imp, q_pos, n_sel_blocks):
    lane = lax.broadcasted_iota(jnp.int32, imp.shape, 1)
    blk = lane // SEL_RATIO
    cur = q_pos // SEL_BLOCK
    forced = (blk == 0) | (blk == cur) | (blk == cur - 1)
    valid = blk * SEL_BLOCK <= q_pos
    score = jnp.where(forced, FORCED_SCORE, jnp.where(valid, _pair_sums(imp), -1.0))
    eligible = (lane % SEL_RATIO == 0) & (blk < n_sel_blocks)
    return jnp.where(eligible, score, KNOCKED_OUT)


def _take_top(work):
    lane = lax.broadcasted_iota(jnp.int32, work.shape, 1).astype(F32)
    top = jnp.max(work, axis=-1, keepdims=True)
    pick = jnp.min(jnp.where(work == top, lane, float(work.shape[1])), axis=-1, keepdims=True)
    return lane == pick, pick


def _with_ones(v):
    return jnp.concatenate([v, jnp.ones(v.shape, v.dtype)], axis=1)


def _attn_prompt_kernel(q_ref, ck_ref, cv_ref, ks_ref, vs_ref, kw_ref, vw_ref, g_ref, o_ref, *, tq, tk, seq):
    i = pl.program_id(2)
    n_cmp = ck_ref.shape[0]
    n_sel_blocks = -(-seq // SEL_BLOCK)
    n_take = min(N_SELECT, n_sel_blocks)
    assert SEL_RATIO == 2 and n_cmp * CMP_BLOCK == seq and n_cmp <= LANES
    q = q_ref[...]
    q4 = jnp.concatenate([q[:, h * HEAD_DIM:(h + 1) * HEAD_DIM] for h in range(GROUP_SIZE)], axis=0)
    q_pos = i * tq + lax.broadcasted_iota(jnp.int32, (tq, 1), 0)

    pad = jnp.zeros((LANES - n_cmp, HEAD_DIM), ck_ref.dtype)
    ck = jnp.concatenate([ck_ref[...], pad], axis=0) if n_cmp < LANES else ck_ref[...]
    cv = jnp.concatenate([cv_ref[...], pad], axis=0) if n_cmp < LANES else cv_ref[...]
    lane = lax.broadcasted_iota(jnp.int32, (1, LANES), 1)
    vis = ((lane + 1) * CMP_BLOCK - 1 <= q_pos) & (lane < n_cmp)
    s = _mxu_nt(q4, ck).reshape(GROUP_SIZE, tq, LANES)
    p = jnp.where(vis[None], _softmax_scaled(jnp.where(vis[None], s, NEG_INF)), 0.0)
    o_c = _mxu(p.reshape(GROUP_SIZE * tq, LANES), cv).reshape(GROUP_SIZE, tq, HEAD_DIM)

    def top_blocks():
        q_pos_t = i * tq + lax.broadcasted_iota(jnp.int32, (1, tq), 1)
        row = lax.broadcasted_iota(jnp.int32, (n_cmp, 1), 0)
        vis_t = (row + 1) * CMP_BLOCK - 1 <= q_pos_t
        imp_t = jnp.zeros((n_cmp, tq), F32)
        for h in range(GROUP_SIZE):
            s_t = jnp.where(vis_t, _mxu_nt(ck_ref[...], q[:, h * HEAD_DIM:(h + 1) * HEAD_DIM]), NEG_INF)
            e = jnp.exp2((s_t - jnp.max(s_t, axis=0, keepdims=True)) * EXP2_SCALE)
            imp_t = imp_t + jnp.where(vis_t, e / jnp.sum(e, axis=0, keepdims=True), 0.0)
        pair = imp_t + pltpu.roll(imp_t, n_cmp - 1, 0)
        blk = row // SEL_RATIO
        cur = q_pos_t // SEL_BLOCK
        forced = (blk == 0) | (blk == cur) | (blk == cur - 1)
        score = jnp.where(forced, FORCED_SCORE, jnp.where(blk * SEL_BLOCK <= q_pos_t, pair, -1.0))
        work = jnp.where((row % SEL_RATIO == 0) & (blk < n_sel_blocks), score, KNOCKED_OUT)
        row_f = row.astype(F32)
        sel_t = jnp.zeros((n_cmp, tq), F32)
        for _ in range(n_take):
            top = jnp.max(work, axis=0, keepdims=True)
            pick = jnp.min(jnp.where(work == top, row_f, float(n_cmp)), axis=0, keepdims=True)
            hit = row_f == pick
            work = jnp.where(hit, KNOCKED_OUT, work)
            sel_t = jnp.where(hit, 1.0, sel_t)
        if n_cmp < LANES:
            sel_t = jnp.concatenate([sel_t, jnp.zeros((LANES - n_cmp, tq), F32)], axis=0)
        return sel_t.T

    def all_blocks():
        return jnp.where((lane % SEL_RATIO == 0) & (lane // SEL_RATIO < n_sel_blocks), 1.0,
                         jnp.zeros((tq, LANES), F32))

    sel = lax.cond(((i + 1) * tq - 1) // SEL_BLOCK + 1 <= n_take, all_blocks, top_blocks).astype(MXU_DTYPE)

    def sel_chunk(kc, carry):
        m_i, l_i, acc = carry
        start = pl.multiple_of(kc * tk, tk)
        k_pos = start + lax.broadcasted_iota(jnp.int32, (1, tk), 1)
        expand = lax.broadcasted_iota(jnp.int32, (LANES, 1), 0) == SEL_RATIO * (k_pos // SEL_BLOCK)
        picked = jnp.dot(sel, jnp.where(expand, 1.0, 0.0).astype(MXU_DTYPE), preferred_element_type=F32)
        bias = jnp.where((picked > 0.5) & (k_pos <= q_pos), 0.0, NEG_INF)[None]
        sc = _mxu_nt(q4, ks_ref[pl.ds(start, tk), :]).reshape(GROUP_SIZE, tq, tk) + bias
        m_new = jnp.maximum(m_i, jnp.max(sc, axis=-1, keepdims=True))
        alpha = jnp.exp2((m_i - m_new) * EXP2_SCALE)
        e = jnp.exp2((sc - m_new) * EXP2_SCALE)
        l_new = alpha * l_i + jnp.sum(e, axis=-1, keepdims=True)
        pv = _mxu(e.reshape(GROUP_SIZE * tq, tk), vs_ref[pl.ds(start, tk), :])
        return m_new, l_new, alpha * acc + pv.reshape(GROUP_SIZE, tq, HEAD_DIM)

    init = (jnp.full((GROUP_SIZE, tq, 1), NEG_INF, F32), jnp.zeros((GROUP_SIZE, tq, 1), F32),
            jnp.zeros((GROUP_SIZE, tq, HEAD_DIM), F32))
    _, l_s, acc_s = lax.fori_loop(0, ((i + 1) * tq + tk - 1) // tk, sel_chunk, init)
    o_s = acc_s / l_s

    span = tq + WINDOW
    w_start = pl.multiple_of(jnp.maximum(i * tq - WINDOW, 0), tq)
    dist = q_pos - (w_start + lax.broadcasted_iota(jnp.int32, (1, span), 1))
    bias = jnp.where((dist >= 0) & (dist < WINDOW), 0.0, NEG_INF)[None]
    sw = _mxu_nt(q4, kw_ref[pl.ds(w_start, span), :]).reshape(GROUP_SIZE, tq, span) + bias
    ew = jnp.exp2((sw - jnp.max(sw, axis=-1, keepdims=True)) * EXP2_SCALE)
    pv = _mxu(ew.reshape(GROUP_SIZE * tq, span), _with_ones(vw_ref[pl.ds(w_start, span), :]))
    o_w = (pv[:, :HEAD_DIM] / pv[:, HEAD_DIM:]).reshape(GROUP_SIZE, tq, HEAD_DIM)

    gates = g_ref[...]
    for h in range(GROUP_SIZE):
        o = (gates[:, 3 * h:3 * h + 1] * o_c[h] + gates[:, 3 * h + 1:3 * h + 2] * o_s[h]
             + gates[:, 3 * h + 2:3 * h + 3] * o_w[h])
        o_ref[:, h * HEAD_DIM:(h + 1) * HEAD_DIM] = o.astype(o_ref.dtype)


def _attn_prompt(q, ckv, kv, gates, *, tq, tk):
    b, t, _ = q.shape
    n_cmp = ckv.shape[1]
    assert t % tq == 0 and t % tk == 0 and t >= tq + WINDOW and WINDOW % tq == 0 and tq % ROW_BLOCK == 0
    gw = GROUP_SIZE * HEAD_DIM

    def kv_spec(branch, part):
        return pl.BlockSpec((None, None, t, HEAD_DIM), lambda bi, g, i: (branch, bi, 0, 2 * g + part))

    return pl.pallas_call(
        functools.partial(_attn_prompt_kernel, tq=tq, tk=tk, seq=t),
        grid=(b, N_KV_HEADS, t // tq),
        in_specs=[pl.BlockSpec((None, tq, gw), lambda bi, g, i: (bi, i, g)),
                  pl.BlockSpec((None, n_cmp, HEAD_DIM), lambda bi, g, i: (bi, 0, 2 * g)),
                  pl.BlockSpec((None, n_cmp, HEAD_DIM), lambda bi, g, i: (bi, 0, 2 * g + 1)),
                  kv_spec(1, 0), kv_spec(1, 1), kv_spec(2, 0), kv_spec(2, 1),
                  pl.BlockSpec((None, tq, LANES), lambda bi, g, i: (bi, i, g))],
        out_specs=pl.BlockSpec((None, tq, gw), lambda bi, g, i: (bi, i, g)),
        out_shape=jax.ShapeDtypeStruct((b, t, Q_WIDTH), MXU_DTYPE),
        compiler_params=_params("arbitrary", "arbitrary", "arbitrary"),
        name="attn_prompt",
    )(q, ckv, ckv, kv, kv, kv, kv, gates)


def _attn_cmp_sample_kernel(q_ref, ckv_ref, o_ref, idx_ref, *, past, n_total, width):
    t_rows = q_ref.shape[0]
    n_cmp = ckv_ref.shape[0]
    n_sel_blocks = -(-n_total // SEL_BLOCK)
    n_take = min(N_SELECT, n_sel_blocks)
    assert SEL_RATIO == 2 and n_take <= LANES and SEL_RATIO * n_sel_blocks <= width
    q_pos = past + lax.broadcasted_iota(jnp.int32, (t_rows, 1), 0)
    lane = lax.broadcasted_iota(jnp.int32, (1, width), 1)
    vis = ((lane + 1) * CMP_BLOCK - 1 <= q_pos) & (lane < n_cmp)
    out_lane = lax.broadcasted_iota(jnp.int32, (1, LANES), 1)
    pad = jnp.zeros((width - n_cmp, HEAD_DIM), ckv_ref.dtype)
    imps = []
    for g in range(N_KV_HEADS):
        ck = jnp.concatenate([ckv_ref[:, 2 * g * HEAD_DIM:(2 * g + 1) * HEAD_DIM], pad], axis=0)
        cv = jnp.concatenate([ckv_ref[:, (2 * g + 1) * HEAD_DIM:(2 * g + 2) * HEAD_DIM], pad], axis=0)
        imp = jnp.zeros((t_rows, width), F32)
        for h in range(GROUP_SIZE):
            cols = slice((g * GROUP_SIZE + h) * HEAD_DIM, (g * GROUP_SIZE + h + 1) * HEAD_DIM)
            s = _mxu_nt(q_ref[:, cols], ck) * ATTN_SCALE
            p = jnp.where(vis, _softmax_rows(jnp.where(vis, s, NEG_INF)), 0.0)
            o_ref[:, cols] = _mxu(p, cv)
            imp = imp + p
        imps.append(imp)
    work = _block_scores(jnp.concatenate(imps, axis=0), jnp.concatenate([q_pos] * N_KV_HEADS, axis=0), n_sel_blocks)
    picks = jnp.zeros((N_KV_HEADS * t_rows, LANES), F32)
    for r in range(n_take):
        hit, pick = _take_top(work)
        work = jnp.where(hit, KNOCKED_OUT, work)
        picks = jnp.where(out_lane == r, pick, picks)
    idx = (picks * (1.0 / SEL_RATIO)).astype(jnp.int32)
    for g in range(N_KV_HEADS):
        idx_ref[g] = idx[g * t_rows:(g + 1) * t_rows]


def _attn_cmp_sample(q, ckv, *, past, n_total):
    b, t, qw = q.shape
    n_cmp = ckv.shape[1]
    width = -(-max(n_cmp, SEL_RATIO * -(-n_total // SEL_BLOCK)) // LANES) * LANES
    return pl.pallas_call(
        functools.partial(_attn_cmp_sample_kernel, past=past, n_total=n_total, width=width),
        grid=(b,),
        in_specs=[pl.BlockSpec((None, t, qw), lambda bi: (bi, 0, 0)),
                  pl.BlockSpec((None, n_cmp, KV_WIDTH), lambda bi: (bi, 0, 0))],
        out_specs=[pl.BlockSpec((None, t, qw), lambda bi: (bi, 0, 0)),
                   pl.BlockSpec((None, N_KV_HEADS, t, LANES), lambda bi: (bi, 0, 0, 0))],
        out_shape=[jax.ShapeDtypeStruct(q.shape, F32),
                   jax.ShapeDtypeStruct((b, N_KV_HEADS, t, LANES), jnp.int32)],
        compiler_params=_params("arbitrary"),
        name="attn_cmp_sample",
    )(q, ckv)


def _attn_win_sample_kernel(q_ref, buf_ref, new_ref, o_ref, *, t_new):
    t_rows = q_ref.shape[0]
    n_chunks = KV_WIDTH // HEAD_DIM
    n_buf = buf_ref.shape[0] // n_chunks
    t_idx = lax.broadcasted_iota(jnp.int32, (GROUP_SIZE * t_rows, 1), 0) % t_rows
    dist_a = t_idx + n_buf - lax.broadcasted_iota(jnp.int32, (1, n_buf), 1)
    row_b = lax.broadcasted_iota(jnp.int32, (1, new_ref.shape[0]), 1)
    mask_a = (dist_a >= 0) & (dist_a < WINDOW)
    mask_b = (row_b <= t_idx) & (t_idx - row_b < WINDOW) & (row_b < t_new)
    for g in range(N_KV_HEADS):
        k_a = buf_ref[pl.ds(2 * g, n_buf, stride=n_chunks), :]
        v_a = buf_ref[pl.ds(2 * g + 1, n_buf, stride=n_chunks), :]
        k_b = new_ref[:, 2 * g * HEAD_DIM:(2 * g + 1) * HEAD_DIM]
        v_b = new_ref[:, (2 * g + 1) * HEAD_DIM:(2 * g + 2) * HEAD_DIM]
        head_cols = [slice((g * GROUP_SIZE + h) * HEAD_DIM, (g * GROUP_SIZE + h + 1) * HEAD_DIM)
                     for h in range(GROUP_SIZE)]
        q = jnp.concatenate([q_ref[:, cols] for cols in head_cols], axis=0)
        s_a = jnp.where(mask_a, _mxu_nt(q, k_a) * ATTN_SCALE, NEG_INF)
        s_b = jnp.where(mask_b, _mxu_nt(q, k_b) * ATTN_SCALE, NEG_INF)
        m = jnp.maximum(jnp.max(s_a, axis=-1, keepdims=True), jnp.max(s_b, axis=-1, keepdims=True))
        e_a, e_b = jnp.exp(s_a - m), jnp.exp(s_b - m)
        l = jnp.sum(e_a, axis=-1, keepdims=True) + jnp.sum(e_b, axis=-1, keepdims=True)
        o = _mxu(e_a / l, v_a) + _mxu(e_b / l, v_b)
        for h, cols in enumerate(head_cols):
            o_ref[:, cols] = o[h * t_rows:(h + 1) * t_rows]


def _attn_win_sample(q, win_buf, kv_new, *, layer, t_new):
    b, t, qw = q.shape
    buf_rows = win_buf.shape[2]
    return pl.pallas_call(
        functools.partial(_attn_win_sample_kernel, t_new=t_new),
        grid=(b,),
        in_specs=[pl.BlockSpec((None, t, qw), lambda bi: (bi, 0, 0)),
                  pl.BlockSpec((None, None, buf_rows, HEAD_DIM), lambda bi: (layer, bi, 0, 0)),
                  pl.BlockSpec((None, kv_new.shape[1], KV_WIDTH), lambda bi: (bi, 0, 0))],
        out_specs=pl.BlockSpec((None, t, qw), lambda bi: (bi, 0, 0)),
        out_shape=jax.ShapeDtypeStruct(q.shape, F32),
        compiler_params=_params("arbitrary"),
        name="attn_win_sample",
    )(q, win_buf, kv_new)


def _attn_sel_sample_kernel(idx_ref, pt_ref, q_ref, *refs, past, t_new, n_slots, t_steps):
    del pt_ref
    blocks = refs[:n_slots]
    new_ref, oc_ref, ow_ref, g_ref, o_ref = refs[n_slots:]
    bi, t, gi = pl.program_id(0), pl.program_id(1), pl.program_id(2)
    base = ((bi * t_steps + t) * N_KV_HEADS + gi) * n_slots
    n_past_blocks = past // SEL_BLOCK
    q_pos = past + t
    q = q_ref[...]
    n_chunks = KV_WIDTH // HEAD_DIM
    lane = lax.broadcasted_iota(jnp.int32, (1, n_slots * SEL_BLOCK), 1)
    pos = lane % SEL_BLOCK
    has_new = jnp.int32(0)
    for n in range(n_slots):
        blk = idx_ref[base + n]
        start = jnp.where(blk < n_past_blocks, blk * SEL_BLOCK, q_pos + 1)
        pos = pos + jnp.where(lane // SEL_BLOCK == n, start, 0)
        has_new = has_new | (blk == n_past_blocks).astype(jnp.int32)
    mask_a = pos <= q_pos
    new = new_ref[...]
    row_b = lax.broadcasted_iota(jnp.int32, (1, new.shape[0]), 1)
    mask_b = (row_b <= t) & (row_b < t_new) & (has_new > 0)
    s_b = jnp.where(mask_b, _mxu_nt(q, new[:, :HEAD_DIM]) * ATTN_SCALE, NEG_INF)
    gates = g_ref[...]

    for g in range(N_KV_HEADS):
        @pl.when(gi == g)
        def _(g=g):
            keys = jnp.concatenate([blk[pl.ds(2 * g, SEL_BLOCK, stride=n_chunks), :] for blk in blocks], axis=0)
            vals = jnp.concatenate([blk[pl.ds(2 * g + 1, SEL_BLOCK, stride=n_chunks), :] for blk in blocks], axis=0)
            s_a = jnp.where(mask_a, _mxu_nt(q, keys) * ATTN_SCALE, NEG_INF)
            m = jnp.maximum(jnp.max(s_a, axis=-1, keepdims=True), jnp.max(s_b, axis=-1, keepdims=True))
            e_a, e_b = jnp.exp(s_a - m), jnp.exp(s_b - m)
            l = jnp.sum(e_a, axis=-1, keepdims=True) + jnp.sum(e_b, axis=-1, keepdims=True)
            o_s = _mxu(e_a / l, vals) + _mxu(e_b / l, new[:, HEAD_DIM:])
            o_ref[...] = gates[:, 0:1] * oc_ref[...] + gates[:, 1:2] * o_s + gates[:, 2:3] * ow_ref[...]


def _attn_sel_sample(idx, page_table, q, pool, kv_new, o_c, o_w, gates, *, layer, past, t_new):
    b, t, g, hg, hd = q.shape
    n_slots = idx.shape[-1]
    n_pages = page_table.shape[1]
    page = past // n_pages
    per_page = page // SEL_BLOCK
    n_past_blocks = past // SEL_BLOCK

    def slot_spec(n):
        def index(bi, ti, gi, idx_s, pt_s):
            blk = jnp.minimum(idx_s[((bi * t + ti) * g + gi) * n_slots + n], n_past_blocks - 1)
            phys = pt_s[bi * n_pages + blk // per_page]
            return (layer, phys, blk % per_page, 0)
        return pl.BlockSpec((None, None, SEL_BLOCK * (KV_WIDTH // HEAD_DIM), HEAD_DIM), index)

    head_spec = pl.BlockSpec((None, None, None, hg, hd), lambda bi, ti, gi, idx_s, pt_s: (bi, ti, gi, 0, 0))
    grid_spec = pltpu.PrefetchScalarGridSpec(
        num_scalar_prefetch=2,
        grid=(b, t, g),
        in_specs=[head_spec] + [slot_spec(n) for n in range(n_slots)] + [
            pl.BlockSpec((None, kv_new.shape[1], 2 * HEAD_DIM), lambda bi, ti, gi, idx_s, pt_s: (bi, 0, gi)),
            head_spec, head_spec,
            pl.BlockSpec((None, None, None, hg, 3), lambda bi, ti, gi, idx_s, pt_s: (bi, ti, gi, 0, 0))],
        out_specs=head_spec,
    )
    return pl.pallas_call(
        functools.partial(_attn_sel_sample_kernel, past=past, t_new=t_new, n_slots=n_slots, t_steps=t),
        grid_spec=grid_spec,
        out_shape=jax.ShapeDtypeStruct(q.shape, F32),
        compiler_params=_params("arbitrary", "arbitrary", "arbitrary"),
        name="attn_sel_sample",
    )(idx.reshape(-1), page_table.reshape(-1), q, *([pool] * n_slots), kv_new, o_c, o_w, gates)


def _dwconv_kernel(cur_ref, prev_ref, buf_ref, w_ref, b_ref, lg_ref, lb_ref, o_ref, full_scr, y_scr, *,
                   tt, halo, width):
    i = pl.program_id(1)
    lead = halo - (width - 1)
    full_scr[0:halo, :] = prev_ref[...]

    @pl.when(i == 0)
    def _():
        full_scr[0:halo, :] = buf_ref[...]

    full_scr[halo:halo + tt, :] = cur_ref[...]
    for c0 in range(0, cur_ref.shape[1], DWCONV_COLS):
        cols = slice(c0, c0 + DWCONV_COLS)
        acc = jnp.zeros((tt, DWCONV_COLS), F32) + b_ref[:, cols]
        for r in range(SUBLANES):
            offsets = [o for o in range(r, lead + width, SUBLANES) if o >= lead]
            rows = tt + (SUBLANES if r else 0)
            group = None
            for o in offsets:
                term = full_scr[o - r:o - r + rows, cols] * w_ref[o - lead:o - lead + 1, cols]
                group = term if group is None else group + term
            if group is not None:
                acc = acc + group[r:r + tt]
        y_scr[:, cols] = acc
    y = y_scr[...]
    mu = jnp.mean(y, axis=-1, keepdims=True)
    var = jnp.mean(jnp.square(y - mu), axis=-1, keepdims=True)
    yn = (y - mu) * lax.rsqrt(var + LN_EPS) * lg_ref[...] + lb_ref[...]
    o_ref[...] = _silu(yn).astype(o_ref.dtype)


def _dwconv(glu, buf, w_dw, b_dw, ln_g, ln_b, *, layer, tt):
    b, t, d = glu.shape
    width = w_dw.shape[1]
    halo = buf.shape[1]
    assert tt % halo == 0 and t % tt == 0 and halo >= width - 1
    per = tt // halo

    def vec_spec():
        return pl.BlockSpec((None, 1, d), lambda bi, i: (layer, 0, 0))

    return pl.pallas_call(
        functools.partial(_dwconv_kernel, tt=tt, halo=halo, width=width),
        grid=(b, t // tt),
        in_specs=[pl.BlockSpec((None, tt, d), lambda bi, i: (bi, i, 0)),
                  pl.BlockSpec((None, halo, d), lambda bi, i: (bi, jnp.maximum(i * per - 1, 0), 0)),
                  pl.BlockSpec((None, halo, d), lambda bi, i: (bi, 0, 0)),
                  pl.BlockSpec((None, width, d), lambda bi, i: (layer, 0, 0)),
                  vec_spec(), vec_spec(), vec_spec()],
        out_specs=pl.BlockSpec((None, tt, d), lambda bi, i: (bi, i, 0)),
        out_shape=jax.ShapeDtypeStruct((b, t, d), MXU_DTYPE),
        scratch_shapes=[pltpu.VMEM((halo + tt, d), F32), pltpu.VMEM((tt, d), F32)],
        compiler_params=_params("arbitrary", "arbitrary"),
        name="dwconv",
    )(glu, glu, buf, w_dw, b_dw, ln_g, ln_b)


def _rope_tables(pos):
    half = HEAD_DIM // 2
    inv = ROPE_THETA ** (-jnp.arange(half, dtype=F32) / half)
    ang = pos.astype(F32)[:, None] * inv[None, :]
    cos, sin = jnp.cos(ang), jnp.sin(ang)
    return jnp.concatenate([cos, cos], axis=1), jnp.concatenate([-sin, sin], axis=1)


def _group_gate_weights(w_gate_cols):
    d = w_gate_cols.shape[0]
    wg = w_gate_cols.reshape(d, N_KV_HEADS, GATE_COLS)
    return jnp.pad(wg, ((0, 0), (0, 0), (0, LANES - GATE_COLS))).reshape(d, N_KV_HEADS * LANES)


def _run_trunk(x, mod, pos, prm, caches, tiles):
    b, t, d = x.shape
    m = b * t
    depth = prm["w_mod"].shape[0]
    tm = tiles["tm"]
    rpm = t if caches is None else 1
    cos, sin = _rope_tables(pos)
    if caches is not None:
        cos, sin = jnp.tile(cos, (b, 1)), jnp.tile(sin, (b, 1))
    rows_per_seq = t if caches is None else m
    x = x.reshape(m, d)
    new_c, new_s, new_w, new_conv = [], [], [], []

    mod_tab = jnp.transpose(mod, (0, 2, 1, 3)).reshape(depth * N_MOD, b, d)
    if caches is not None:
        mod_tab = jnp.repeat(mod_tab, t, axis=1)
    mod_tab = mod_tab[:, :, None, :]
    norm_tab = prm["norm_g"].reshape(depth * 3, 1, d)

    def ffn(x, layer, which, k0):
        return _ffn(x, norm_tab, mod_tab, prm["final_norm_g"].reshape(1, d), prm["ffn_w_gate"], prm["ffn_w_up"],
                    prm["ffn_w_down"], norm_slot=layer * 3 + which * 2, mod_slot=layer * N_MOD + k0, layer=layer,
                    which=which, rows_per_mod=rpm, tm=tiles["tm_ffn"], tf=tiles["tf"],
                    final_norm=(layer == depth - 1 and which == 1))

    for i in range(depth):
        x = ffn(x, i, 0, 0)
        a = i // 2
        mixer = dict(norm_slot=i * 3 + 1, mod_slot=i * N_MOD + 3, rows_per_mod=rpm, tm=tm)
        if i % 2 == 0:
            w_in = prm["attn_w_in"]
            common = dict(rows_per_seq=rows_per_seq, **mixer)
            q_dtype = MXU_DTYPE if caches is None else F32
            q, = _proj(x, norm_tab, mod_tab, w_in, cos, sin, layer=a, mode="q", col0=0, n=Q_WIDTH,
                       out_dtypes=[q_dtype], tn=512, **common)
            kv_outs = [F32, MXU_DTYPE] if caches is None else [F32]
            kv = _proj(x, norm_tab, mod_tab, w_in, cos, sin, layer=a, mode="kv", col0=Q_WIDTH, n=3 * KV_WIDTH,
                       out_dtypes=kv_outs, tn=KV_WIDTH, **common)
            gates, = _proj(x, norm_tab, mod_tab, _group_gate_weights(prm["attn_gate_w"][a])[None], cos, sin, layer=0,
                           mode="sigmoid", col0=0, n=N_KV_HEADS * LANES, out_dtypes=[F32],
                           tn=N_KV_HEADS * LANES, **common)
            kv32 = kv[0]
            new_c.append(kv32[0].reshape(b, t, N_KV_HEADS, 2, HEAD_DIM))
            new_s.append(kv32[1].reshape(b, t, N_KV_HEADS, 2, HEAD_DIM))
            if caches is None:
                ckv = _compress_rows(kv32[0], prm["cmp_pe"], prm["cmp_w1"], prm["cmp_w2"], layer=a, n_seq=b)
                o = _attn_prompt(q.reshape(b, t, Q_WIDTH), ckv.reshape(b, t // CMP_BLOCK, KV_WIDTH),
                                 kv[1].reshape(3, b, t, KV_WIDTH), gates.reshape(b, t, N_KV_HEADS * LANES),
                                 tq=tiles["tq"], tk=tiles["tk"]).reshape(m, Q_WIDTH)
                new_w.append(kv32[2].reshape(b, t, N_KV_HEADS, 2, HEAD_DIM)[:, t - min(WINDOW, t):])
            else:
                pool_c, pool_s, win_buf, _, page_table = caches
                n_layers, n_pool, page = pool_c.shape[:3]
                past = page_table.shape[1] * page
                assert t < CMP_BLOCK and page % SEL_BLOCK == 0
                ckv = _compress_pages(pool_c.reshape(n_layers, n_pool, page * (KV_WIDTH // HEAD_DIM), HEAD_DIM),
                                      page_table, prm["cmp_pe"], prm["cmp_w1"], prm["cmp_w2"], layer=a)
                q3 = q.reshape(b, t, Q_WIDTH)
                o_c, idx = _attn_cmp_sample(q3, ckv, past=past, n_total=past + t)
                t_pad = -(-t // SUBLANES) * SUBLANES
                pad_rows = lambda r: jnp.pad(r.reshape(b, t, KV_WIDTH), ((0, 0), (0, t_pad - t), (0, 0)))
                n_buf = win_buf.shape[2]
                o_w = _attn_win_sample(q3, win_buf.reshape(n_layers, b, n_buf * (KV_WIDTH // HEAD_DIM), HEAD_DIM),
                                       pad_rows(kv32[2]), layer=a, t_new=t)
                n_take = min(N_SELECT, -(-(past + t) // SEL_BLOCK))
                idx = jnp.transpose(idx[..., :n_take], (0, 2, 1, 3))
                heads = lambda v: v.reshape(b, t, N_KV_HEADS, GROUP_SIZE, HEAD_DIM)
                g5 = gates.reshape(b, t, N_KV_HEADS, LANES)[..., :GATE_COLS].reshape(b, t, N_KV_HEADS, GROUP_SIZE, 3)
                o = _attn_sel_sample(idx, page_table, heads(q3),
                                     pool_s.reshape(n_layers, n_pool, page * (KV_WIDTH // HEAD_DIM), HEAD_DIM),
                                     pad_rows(kv32[1]), heads(o_c), heads(o_w), g5, layer=a, past=past,
                                     t_new=t).reshape(m, Q_WIDTH)
                new_w.append(kv32[2].reshape(b, t, N_KV_HEADS, 2, HEAD_DIM))
            x = _linres(o, prm["attn_w_out"], x, mod_tab, gate_slot=i * N_MOD + 5, layer=a, rows_per_mod=rpm, tm=tm,
                        tn=tiles["tn_out"])
        else:
            glu = _glu_proj(x, norm_tab, mod_tab, prm["conv_w_pw1"], layer=a, tn=512, **mixer)
            glu = glu.reshape(b, t, d)
            width = prm["conv_w_dw"].shape[1]
            halo = 32
            if caches is None:
                buf = jnp.zeros((b, width - 1, d), F32)
                t_conv = t
                cur = glu
            else:
                buf = caches[3][a]
                t_conv = halo
                cur = jnp.pad(glu, ((0, 0), (0, t_conv - t), (0, 0)))
            if t >= width - 1:
                new_conv.append(glu[:, t - (width - 1):])
            else:
                new_conv.append(jnp.concatenate([buf[:, t:], glu], axis=1))
            buf = jnp.pad(buf, ((0, 0), (halo - (width - 1), 0), (0, 0)))
            vec = lambda v: v.reshape(v.shape[0], 1, d)
            act = _dwconv(cur, buf, prm["conv_w_dw"], vec(prm["conv_b_dw"]), vec(prm["conv_ln_g"]),
                          vec(prm["conv_ln_b"]), layer=a, tt=min(tiles["tt"], t_conv))
            act = act[:, :t].reshape(m, d)
            x = _linres(act, prm["conv_w_pw2"], x, mod_tab, gate_slot=i * N_MOD + 5, layer=a, rows_per_mod=rpm, tm=tm,
                        tn=tiles["tn_out"])
        x = ffn(x, i, 1, 6)
    y = x.reshape(b, t, d)
    new_w = jnp.stack(new_w)
    if caches is not None:
        keep = min(WINDOW, caches[2].shape[2] + t) - t
        new_w = jnp.concatenate([caches[2][:, :, caches[2].shape[2] - keep:], new_w], axis=2)
    return y, jnp.stack(new_c), jnp.stack(new_s), new_w, jnp.stack(new_conv)


PROMPT_TILES = dict(tm=1024, tm_ffn=1024, tf=512, tq=256, tk=512, tn_out=512, tt=128)
SAMPLE_TILES = dict(tm=32, tm_ffn=32, tf=512, tn_out=512, tt=32)


def kernel(x_prompt, x_sample, cache_cmp_kv, cache_sel_kv, cache_win_kv, state_conv, page_table, c_prompt, c_sample, w_mod, b_mod, norm_g, ffn_w_gate, ffn_w_up, ffn_w_down, attn_w_in, attn_w_out, cmp_pe, cmp_w1, cmp_w2, conv_w_pw1, conv_w_dw, conv_b_dw, conv_ln_g, conv_ln_b, conv_w_pw2, final_norm_g):
    mxu = lambda w: w.astype(MXU_DTYPE)
    prm = {"w_mod": w_mod, "norm_g": norm_g, "ffn_w_gate": mxu(ffn_w_gate), "ffn_w_up": mxu(ffn_w_up),
           "ffn_w_down": mxu(ffn_w_down), "attn_w_in": mxu(attn_w_in), "attn_gate_w": attn_w_in[..., -N_HEADS * 3:],
           "attn_w_out": mxu(attn_w_out), "cmp_pe": cmp_pe,
           "cmp_w1": cmp_w1, "cmp_w2": cmp_w2, "conv_w_pw1": mxu(conv_w_pw1), "conv_w_dw": conv_w_dw,
           "conv_b_dw": conv_b_dw, "conv_ln_g": conv_ln_g, "conv_ln_b": conv_ln_b, "conv_w_pw2": mxu(conv_w_pw2),
           "final_norm_g": final_norm_g}
    depth, d, _ = w_mod.shape
    bp, tp = x_prompt.shape[:2]
    bs, ts = x_sample.shape[:2]
    past = page_table.shape[1] * cache_cmp_kv.shape[2]

    n_req = bp + bs
    r_pad = -(-n_req // 8) * 8
    c_all = jnp.pad(jnp.concatenate([c_prompt, c_sample], axis=0), ((0, r_pad - n_req), (0, 0)))
    mod = _mod_vectors(c_all, w_mod, b_mod).reshape(depth, r_pad, N_MOD, d)

    pos_p = jnp.arange(tp, dtype=jnp.int32)
    pos_s = past + jnp.arange(ts, dtype=jnp.int32)
    y_p, p_cmp, p_sel, p_win, p_conv = _run_trunk(x_prompt, mod[:, :bp], pos_p, prm, None, PROMPT_TILES)
    y_s, s_cmp, s_sel, s_win, s_conv = _run_trunk(
        x_sample, mod[:, bp:n_req], pos_s, prm,
        (cache_cmp_kv, cache_sel_kv, cache_win_kv, state_conv, page_table), SAMPLE_TILES)
    return (y_p, y_s, p_cmp, p_sel, p_win, p_conv, s_cmp, s_sel, s_win, s_conv)
```

```python
import functools

import jax
import jax.numpy as jnp
from jax import lax
from jax.experimental import pallas as pl
from jax.experimental.pallas import tpu as pltpu

F32 = jnp.float32
MXU_DTYPE = jnp.bfloat16
VMEM_LIMIT_BYTES = 60 * 1024 * 1024
LANES = 128
SUBLANES = 8

N_HEADS = 16
HEAD_DIM = 128
N_KV_HEADS = 4
GROUP_SIZE = N_HEADS // N_KV_HEADS
CMP_BLOCK = 32
SEL_BLOCK = 64
SEL_RATIO = SEL_BLOCK // CMP_BLOCK
N_SELECT = 16
WINDOW = 512
ROPE_THETA = 10000.0
N_MOD = 9
RMS_EPS = 1e-6
LN_EPS = 1e-5
NEG_INF = -1e30
FORCED_SCORE = 1e9
KNOCKED_OUT = -3e38
ATTN_SCALE = HEAD_DIM ** -0.5
Q_WIDTH = N_HEADS * HEAD_DIM
KV_WIDTH = 2 * N_KV_HEADS * HEAD_DIM
GATE_COLS = GROUP_SIZE * 3
PAGES_PER_STEP = 16
DWCONV_COLS = 128
ROW_BLOCK = 32
FFN_PANEL = 512
PROJ_PANEL = 256
EXP2_SCALE = ATTN_SCALE * 1.4426950408889634


def _params(*sem):
    return pltpu.CompilerParams(dimension_semantics=sem, vmem_limit_bytes=VMEM_LIMIT_BYTES)


def _mxu(a, b):
    return jnp.dot(a.astype(MXU_DTYPE), b.astype(MXU_DTYPE), preferred_element_type=F32)


def _mxu_nt(a, b):
    return lax.dot_general(a.astype(MXU_DTYPE), b.astype(MXU_DTYPE),
                           (((1,), (1,)), ((), ())), preferred_element_type=F32)


def _silu(x):
    return x * jax.nn.sigmoid(x)


def _softmax_rows(s):
    e = jnp.exp(s - jnp.max(s, axis=-1, keepdims=True))
    return e / jnp.sum(e, axis=-1, keepdims=True)


def _softmax_scaled(s):
    e = jnp.exp2((s - jnp.max(s, axis=-1, keepdims=True)) * EXP2_SCALE)
    return e / jnp.sum(e, axis=-1, keepdims=True)


def _mod_spec(tm, rows_per_mod, width, slot, by_column=False):
    if rows_per_mod >= tm:
        assert rows_per_mod % tm == 0
        block_rows, idx = 1, (lambda i: (i * tm) // rows_per_mod)
    else:
        assert rows_per_mod == 1
        block_rows, idx = tm, (lambda i: i)
    return pl.BlockSpec((None, block_rows, 1, width), lambda i, j: (slot, idx(i), 0, j if by_column else 0))


def _gain_spec(d, slot):
    return pl.BlockSpec((None, 1, d), lambda i, j: (slot, 0, 0))


def _modulated(x_ref, ng_ref, sh_ref, sc_ref):
    x = x_ref[...]
    gain = ng_ref[...] * (1.0 + sc_ref[:, 0, :])
    return x * lax.rsqrt(jnp.mean(x * x, axis=-1, keepdims=True) + RMS_EPS) * gain + sh_ref[:, 0, :]


def _mod_kernel(c_ref, w_ref, b_ref, o_ref):
    o_ref[...] = _mxu(_silu(c_ref[...]), w_ref[...]) + b_ref[...]


def _mod_vectors(c_all, w_mod, b_mod, tn=1024):
    depth, d, n = w_mod.shape
    r = c_all.shape[0]
    return pl.pallas_call(
        _mod_kernel,
        grid=(depth, n // tn),
        in_specs=[pl.BlockSpec((r, d), lambda l, j: (0, 0)),
                  pl.BlockSpec((None, d, tn), lambda l, j: (l, 0, j)),
                  pl.BlockSpec((None, 1, tn), lambda l, j: (l, 0, j))],
        out_specs=pl.BlockSpec((None, r, tn), lambda l, j: (l, 0, j)),
        out_shape=jax.ShapeDtypeStruct((depth, r, n), F32),
        compiler_params=_params("arbitrary", "arbitrary"),
        name="mod_vectors",
    )(c_all, w_mod, b_mod.reshape(depth, 1, n))


def _ffn_kernel(x_ref, ng_ref, sh_ref, sc_ref, gt_ref, fg_ref, wg_ref, wu_ref, wd_ref, o_ref, h_scr, *, final_norm):
    f = pl.program_id(1)

    def panels(first):
        panel = min(FFN_PANEL, h_scr.shape[0])
        for p0 in range(0, h_scr.shape[0], panel):
            rows = slice(p0, p0 + panel)
            h = h_scr[rows, :]
            g = _mxu(h, wg_ref[...])
            u = _mxu(h, wu_ref[...])
            a = jnp.concatenate([(_silu(g[r0:r0 + ROW_BLOCK]) * u[r0:r0 + ROW_BLOCK]).astype(MXU_DTYPE)
                                 for r0 in range(0, panel, ROW_BLOCK)], axis=0)
            y = _mxu(a, wd_ref[...])
            if first:
                o_ref[rows, :] = y
            else:
                o_ref[rows, :] += y

    @pl.when(f == 0)
    def _():
        h_scr[...] = _modulated(x_ref, ng_ref, sh_ref, sc_ref).astype(h_scr.dtype)
        panels(True)

    @pl.when(f > 0)
    def _():
        panels(False)

    @pl.when(f == pl.num_programs(1) - 1)
    def _():
        y = x_ref[...] + (0.5 * gt_ref[:, 0, :]) * o_ref[...]
        if final_norm:
            y = y * lax.rsqrt(jnp.mean(y * y, axis=-1, keepdims=True) + RMS_EPS) * fg_ref[...]
        o_ref[...] = y


def _ffn(x, norm_tab, mod_tab, final_g, w_gate, w_up, w_down, *, norm_slot, mod_slot, layer, which, rows_per_mod,
         tm, tf, final_norm):
    m, d = x.shape
    dff = w_gate.shape[-1]
    return pl.pallas_call(
        functools.partial(_ffn_kernel, final_norm=final_norm),
        grid=(m // tm, dff // tf),
        in_specs=[pl.BlockSpec((tm, d), lambda i, f: (i, 0)),
                  _gain_spec(d, norm_slot),
                  _mod_spec(tm, rows_per_mod, d, mod_slot), _mod_spec(tm, rows_per_mod, d, mod_slot + 1),
                  _mod_spec(tm, rows_per_mod, d, mod_slot + 2),
                  pl.BlockSpec((1, d), lambda i, f: (0, 0)),
                  pl.BlockSpec((None, None, d, tf), lambda i, f: (layer, which, 0, f)),
                  pl.BlockSpec((None, None, d, tf), lambda i, f: (layer, which, 0, f)),
                  pl.BlockSpec((None, None, tf, d), lambda i, f: (layer, which, f, 0))],
        out_specs=pl.BlockSpec((tm, d), lambda i, f: (i, 0)),
        out_shape=jax.ShapeDtypeStruct((m, d), F32),
        scratch_shapes=[pltpu.VMEM((tm, d), MXU_DTYPE)],
        compiler_params=_params("arbitrary", "arbitrary"),
        name="ffn",
    )(x, norm_tab, mod_tab, mod_tab, mod_tab, final_g, w_gate, w_up, w_down)


def _rope_chunk(z, cos, sin):
    return z * cos + pltpu.roll(z, HEAD_DIM // 2, 1) * sin


def _proj_kernel(x_ref, ng_ref, sh_ref, sc_ref, w_ref, cos_ref, sin_ref, *rest, mode):
    out_refs, h_scr = rest[:-1], rest[-1]

    @pl.when(pl.program_id(1) == 0)
    def _():
        h_scr[...] = _modulated(x_ref, ng_ref, sh_ref, sc_ref).astype(h_scr.dtype)

    tm = h_scr.shape[0]
    panel = min(PROJ_PANEL, tm)
    n_chunks = w_ref.shape[1] // HEAD_DIM
    for p0 in range(0, tm, panel):
        rows = slice(p0, p0 + panel)
        z = _mxu(h_scr[rows, :], w_ref[...])
        if mode == "sigmoid":
            out_refs[0][rows, :] = jax.nn.sigmoid(z)
            continue
        cos, sin = cos_ref[rows, :], sin_ref[rows, :]
        for c in range(n_chunks):
            zc = z[:, c * HEAD_DIM:(c + 1) * HEAD_DIM]
            if mode == "q" or c % 2 == 0:
                zc = _rope_chunk(zc, cos, sin)
            for k, o in enumerate(out_refs):
                if mode == "kv" and k == 0:
                    o[pl.ds(p0 * n_chunks + c, panel, stride=n_chunks), :] = zc
                else:
                    o[rows, c * HEAD_DIM:(c + 1) * HEAD_DIM] = zc.astype(o.dtype)


def _proj(x, norm_tab, mod_tab, w, cos, sin, *, norm_slot, mod_slot, layer, mode, col0, n, out_dtypes, rows_per_mod,
          rows_per_seq, tm, tn):
    m, d = x.shape
    seq_tiles = rows_per_seq // tm
    assert col0 % tn == 0 and n % tn == 0 and rows_per_seq % tm == 0
    if mode == "kv":
        assert tn == KV_WIDTH and out_dtypes[0] == F32
        n_chunks = KV_WIDTH // HEAD_DIM
        out_specs = [pl.BlockSpec((None, tm * n_chunks, HEAD_DIM), lambda i, j: (j, i, 0))] + [
            pl.BlockSpec((None, tm, tn), lambda i, j: (j, i, 0)) for _ in out_dtypes[1:]]
        out_shape = [jax.ShapeDtypeStruct((n // KV_WIDTH, m * n_chunks, HEAD_DIM), F32)] + [
            jax.ShapeDtypeStruct((n // KV_WIDTH, m, KV_WIDTH), dt) for dt in out_dtypes[1:]]
    else:
        out_specs = [pl.BlockSpec((tm, tn), lambda i, j: (i, j)) for _ in out_dtypes]
        out_shape = [jax.ShapeDtypeStruct((m, n), dt) for dt in out_dtypes]
    return pl.pallas_call(
        functools.partial(_proj_kernel, mode=mode),
        grid=(m // tm, n // tn),
        in_specs=[pl.BlockSpec((tm, d), lambda i, j: (i, 0)),
                  _gain_spec(d, norm_slot),
                  _mod_spec(tm, rows_per_mod, d, mod_slot), _mod_spec(tm, rows_per_mod, d, mod_slot + 1),
                  pl.BlockSpec((None, d, tn), lambda i, j: (layer, 0, col0 // tn + j)),
                  pl.BlockSpec((tm, HEAD_DIM), lambda i, j: (i % seq_tiles, 0)),
                  pl.BlockSpec((tm, HEAD_DIM), lambda i, j: (i % seq_tiles, 0))],
        out_specs=out_specs,
        out_shape=out_shape,
        scratch_shapes=[pltpu.VMEM((tm, d), MXU_DTYPE)],
        compiler_params=_params("arbitrary", "arbitrary"),
        name="proj_" + mode,
    )(x, norm_tab, mod_tab, mod_tab, w, cos, sin)


def _glu_kernel(x_ref, ng_ref, sh_ref, sc_ref, wa_ref, wb_ref, o_ref, h_scr):
    @pl.when(pl.program_id(1) == 0)
    def _():
        h_scr[...] = _modulated(x_ref, ng_ref, sh_ref, sc_ref).astype(h_scr.dtype)

    panel = min(PROJ_PANEL, h_scr.shape[0])
    for p0 in range(0, h_scr.shape[0], panel):
        h = h_scr[p0:p0 + panel, :]
        o_ref[p0:p0 + panel, :] = _mxu(h, wa_ref[...]) * jax.nn.sigmoid(_mxu(h, wb_ref[...]))


def _glu_proj(x, norm_tab, mod_tab, w_pw1, *, norm_slot, mod_slot, layer, rows_per_mod, tm, tn):
    m, d = x.shape
    dc = w_pw1.shape[-1] // 2
    return pl.pallas_call(
        _glu_kernel,
        grid=(m // tm, dc // tn),
        in_specs=[pl.BlockSpec((tm, d), lambda i, j: (i, 0)),
                  _gain_spec(d, norm_slot),
                  _mod_spec(tm, rows_per_mod, d, mod_slot), _mod_spec(tm, rows_per_mod, d, mod_slot + 1),
                  pl.BlockSpec((None, d, tn), lambda i, j: (layer, 0, j)),
                  pl.BlockSpec((None, d, tn), lambda i, j: (layer, 0, dc // tn + j))],
        out_specs=pl.BlockSpec((tm, tn), lambda i, j: (i, j)),
        out_shape=jax.ShapeDtypeStruct((m, dc), F32),
        scratch_shapes=[pltpu.VMEM((tm, d), MXU_DTYPE)],
        compiler_params=_params("arbitrary", "arbitrary"),
        name="glu_proj",
    )(x, norm_tab, mod_tab, mod_tab, w_pw1, w_pw1)


def _linres_kernel(a_ref, w_ref, x_ref, gt_ref, o_ref):
    tm = a_ref.shape[0]
    panel = min(PROJ_PANEL, tm)
    for p0 in range(0, tm, panel):
        rows = slice(p0, p0 + panel)
        gate = gt_ref[:, 0, :] if gt_ref.shape[0] == 1 else gt_ref[rows, 0, :]
        o_ref[rows, :] = x_ref[rows, :] + gate * _mxu(a_ref[rows, :], w_ref[...])


def _linres(a, w, x, mod_tab, *, gate_slot, layer, rows_per_mod, tm, tn):
    m, k = a.shape
    d = x.shape[1]
    gspec = _mod_spec(tm, rows_per_mod, tn, gate_slot, by_column=True)
    return pl.pallas_call(
        _linres_kernel,
        grid=(m // tm, d // tn),
        in_specs=[pl.BlockSpec((tm, k), lambda i, j: (i, 0)),
                  pl.BlockSpec((None, k, tn), lambda i, j: (layer, 0, j)),
                  pl.BlockSpec((tm, tn), lambda i, j: (i, j)),
                  gspec],
        out_specs=pl.BlockSpec((tm, tn), lambda i, j: (i, j)),
        out_shape=jax.ShapeDtypeStruct((m, d), F32),
        compiler_params=_params("arbitrary", "arbitrary"),
        name="linres",
    )(a, w, x, mod_tab)


def _compress_kernel(x_ref, pe_ref, w1_ref, w2_ref, o_ref):
    n_chunks = KV_WIDTH // HEAD_DIM
    nb = x_ref.shape[0] // (CMP_BLOCK * n_chunks)
    for kv in range(2):
        acc = jnp.zeros((N_KV_HEADS * nb, HEAD_DIM), F32)
        for c in range(CMP_BLOCK):
            xc = jnp.concatenate([x_ref[pl.ds(c * n_chunks + 2 * g + kv, nb, stride=CMP_BLOCK * n_chunks), :]
                                  for g in range(N_KV_HEADS)], axis=0) + pe_ref[kv, c:c + 1, :]
            acc += _mxu(xc, w1_ref[kv, c * HEAD_DIM:(c + 1) * HEAD_DIM, :])
        y = _mxu(_silu(acc), w2_ref[kv])
        for g in range(N_KV_HEADS):
            o_ref[:, (2 * g + kv) * HEAD_DIM:(2 * g + kv + 1) * HEAD_DIM] = y[g * nb:(g + 1) * nb].astype(o_ref.dtype)


def _compress_rows(rows, pe, w1, w2, *, layer, n_seq):
    n_chunks = KV_WIDTH // HEAD_DIM
    n_blocks = rows.shape[0] // (CMP_BLOCK * n_chunks)
    nb = n_blocks // n_seq
    return pl.pallas_call(
        _compress_kernel,
        grid=(n_seq,),
        in_specs=[pl.BlockSpec((nb * CMP_BLOCK * n_chunks, HEAD_DIM), lambda i: (i, 0)),
                  pl.BlockSpec((None, 2, CMP_BLOCK, HEAD_DIM), lambda i: (layer, 0, 0, 0)),
                  pl.BlockSpec((None, 2, CMP_BLOCK * HEAD_DIM, HEAD_DIM), lambda i: (layer, 0, 0, 0)),
                  pl.BlockSpec((None, 2, HEAD_DIM, HEAD_DIM), lambda i: (layer, 0, 0, 0))],
        out_specs=pl.BlockSpec((nb, KV_WIDTH), lambda i: (i, 0)),
        out_shape=jax.ShapeDtypeStruct((n_blocks, KV_WIDTH), MXU_DTYPE),
        compiler_params=_params("arbitrary"),
        name="compress_rows",
    )(rows, pe, w1, w2)


def _compress_pages_kernel(pt_ref, *refs, blocks_per_page):
    del pt_ref
    pages = refs[:PAGES_PER_STEP]
    pe_ref, w1_ref, w2_ref, o_ref, t_scr = refs[PAGES_PER_STEP:]
    nb = PAGES_PER_STEP * blocks_per_page
    pages_per_group = SUBLANES // blocks_per_page
    n_groups = PAGES_PER_STEP // pages_per_group
    rows = SUBLANES * CMP_BLOCK
    out_row = lax.broadcasted_iota(jnp.int32, (rows, rows), 0)
    in_row = lax.broadcasted_iota(jnp.int32, (rows, rows), 1)
    perm = jnp.where(in_row == (out_row % SUBLANES) * CMP_BLOCK + out_row // SUBLANES, 1.0, 0.0)
    pe_rows = jnp.concatenate([pe_ref[kv] for _ in range(N_KV_HEADS) for kv in range(2)], axis=1)
    pe_rows = jnp.concatenate([pe_rows] * SUBLANES, axis=0)
    n_chunks = KV_WIDTH // HEAD_DIM
    page_rows = pages[0].shape[0] // n_chunks
    for gp in range(n_groups):
        x = jnp.concatenate(
            [jnp.concatenate([pages[gp * pages_per_group + k][pl.ds(ch, page_rows, stride=n_chunks), :]
                              for ch in range(n_chunks)], axis=1)
             for k in range(pages_per_group)], axis=0)
        t_scr[gp] = _mxu(perm, x + pe_rows)
    def block_rows(c, kv):
        return jnp.concatenate(
            [t_scr[:, c * SUBLANES:(c + 1) * SUBLANES,
                   (g * 2 + kv) * HEAD_DIM:(g * 2 + kv + 1) * HEAD_DIM].reshape(nb, HEAD_DIM)
             for g in range(N_KV_HEADS)], axis=0)

    for kv in range(2):
        acc = jnp.zeros((N_KV_HEADS * nb, HEAD_DIM), F32)
        for c in range(0, CMP_BLOCK, 2):
            xc = jnp.concatenate([block_rows(c, kv), block_rows(c + 1, kv)], axis=1)
            acc += _mxu(xc, w1_ref[kv, c * HEAD_DIM:(c + 2) * HEAD_DIM, :])
        y = _mxu(_silu(acc), w2_ref[kv])
        for g in range(N_KV_HEADS):
            o_ref[:, (g * 2 + kv) * HEAD_DIM:(g * 2 + kv + 1) * HEAD_DIM] = (
                y[g * nb:(g + 1) * nb].astype(o_ref.dtype))


def _compress_pages(pool, page_table, pe, w1, w2, *, layer):
    width = KV_WIDTH
    n_chunks = width // HEAD_DIM
    page = pool.shape[2] // n_chunks
    b, n_pages = page_table.shape
    bpp = page // CMP_BLOCK
    nb = PAGES_PER_STEP * bpp
    assert n_pages % PAGES_PER_STEP == 0 and page % CMP_BLOCK == 0 and SUBLANES % bpp == 0
    n_groups = nb // SUBLANES

    def page_spec(p):
        return pl.BlockSpec((None, None, page * n_chunks, HEAD_DIM),
                            lambda bi, gi, pt: (layer, pt[bi * n_pages + gi * PAGES_PER_STEP + p], 0, 0))

    grid_spec = pltpu.PrefetchScalarGridSpec(
        num_scalar_prefetch=1,
        grid=(b, n_pages // PAGES_PER_STEP),
        in_specs=[page_spec(p) for p in range(PAGES_PER_STEP)] + [
            pl.BlockSpec((None, 2, CMP_BLOCK, HEAD_DIM), lambda bi, gi, pt: (layer, 0, 0, 0)),
            pl.BlockSpec((None, 2, CMP_BLOCK * HEAD_DIM, HEAD_DIM), lambda bi, gi, pt: (layer, 0, 0, 0)),
            pl.BlockSpec((None, 2, HEAD_DIM, HEAD_DIM), lambda bi, gi, pt: (layer, 0, 0, 0))],
        out_specs=pl.BlockSpec((None, nb, width), lambda bi, gi, pt: (bi, gi, 0)),
        scratch_shapes=[pltpu.VMEM((n_groups, SUBLANES * CMP_BLOCK, width), F32)],
    )
    return pl.pallas_call(
        functools.partial(_compress_pages_kernel, blocks_per_page=bpp),
        grid_spec=grid_spec,
        out_shape=jax.ShapeDtypeStruct((b, n_pages * bpp, width), MXU_DTYPE),
        compiler_params=_params("arbitrary", "arbitrary"),
        name="compress_pages",
    )(page_table.reshape(-1), *([pool] * PAGES_PER_STEP), pe, w1, w2)


def _pair_sums(imp):
    out = []
    for k in range(imp.shape[1] // LANES):
        x = imp[:, k * LANES:(k + 1) * LANES]
        even = lax.broadcasted_iota(jnp.int32, x.shape, 1) % 2 == 0
        out.append(x + jnp.where(even, pltpu.roll(x, LANES - 1, 1), pltpu.roll(x, 1, 1)))
    return out[0] if len(out) == 1 else jnp.concatenate(out, axis=1)


def _block_scores(imp, q_pos, n_sel_blocks):
    lane = lax.broadcasted_iota(jnp.int32, imp.shape, 1)
    blk = lane // SEL_RATIO
    cur = q_pos // SEL_BLOCK
    forced = (blk == 0) | (blk == cur) | (blk == cur - 1)
    valid = blk * SEL_BLOCK <= q_pos
    score = jnp.where(forced, FORCED_SCORE, jnp.where(valid, _pair_sums(imp), -1.0))
    eligible = (lane % SEL_RATIO == 0) & (blk < n_sel_blocks)
    return jnp.where(eligible, score, KNOCKED_OUT)


def _take_top(work):
    lane = lax.broadcasted_iota(jnp.int32, work.shape, 1).astype(F32)
    top = jnp.max(work, axis=-1, keepdims=True)
    pick = jnp.min(jnp.where(work == top, lane, float(work.shape[1])), axis=-1, keepdims=True)
    return lane == pick, pick


def _with_ones(v):
    return jnp.concatenate([v, jnp.ones(v.shape, v.dtype)], axis=1)


def _attn_prompt_kernel(q_ref, ck_ref, cv_ref, ks_ref, vs_ref, kw_ref, vw_ref, g_ref, o_ref, *, tq, tk, seq):
    i = pl.program_id(2)
    n_cmp = ck_ref.shape[0]
    n_sel_blocks = -(-seq // SEL_BLOCK)
    n_take = min(N_SELECT, n_sel_blocks)
    assert SEL_RATIO == 2 and n_cmp * CMP_BLOCK == seq and n_cmp <= LANES
    q = q_ref[...]
    q4 = jnp.concatenate([q[:, h * HEAD_DIM:(h + 1) * HEAD_DIM] for h in range(GROUP_SIZE)], axis=0)
    q_pos = i * tq + lax.broadcasted_iota(jnp.int32, (tq, 1), 0)

    pad = jnp.zeros((LANES - n_cmp, HEAD_DIM), ck_ref.dtype)
    ck = jnp.concatenate([ck_ref[...], pad], axis=0) if n_cmp < LANES else ck_ref[...]
    cv = jnp.concatenate([cv_ref[...], pad], axis=0) if n_cmp < LANES else cv_ref[...]
    lane = lax.broadcasted_iota(jnp.int32, (1, LANES), 1)
    vis = ((lane + 1) * CMP_BLOCK - 1 <= q_pos) & (lane < n_cmp)
    s = _mxu_nt(q4, ck).reshape(GROUP_SIZE, tq, LANES)
    p = jnp.where(vis[None], _softmax_scaled(jnp.where(vis[None], s, NEG_INF)), 0.0)
    o_c = _mxu(p.reshape(GROUP_SIZE * tq, LANES), cv).reshape(GROUP_SIZE, tq, HEAD_DIM)

    def top_blocks():
        q_pos_t = i * tq + lax.broadcasted_iota(jnp.int32, (1, tq), 1)
        row = lax.broadcasted_iota(jnp.int32, (n_cmp, 1), 0)
        vis_t = (row + 1) * CMP_BLOCK - 1 <= q_pos_t
        imp_t = jnp.zeros((n_cmp, tq), F32)
        for h in range(GROUP_SIZE):
            s_t = jnp.where(vis_t, _mxu_nt(ck_ref[...], q[:, h * HEAD_DIM:(h + 1) * HEAD_DIM]), NEG_INF)
            e = jnp.exp2((s_t - jnp.max(s_t, axis=0, keepdims=True)) * EXP2_SCALE)
            imp_t = imp_t + jnp.where(vis_t, e / jnp.sum(e, axis=0, keepdims=True), 0.0)
        pair = imp_t + pltpu.roll(imp_t, n_cmp - 1, 0)
        blk = row // SEL_RATIO
        cur = q_pos_t // SEL_BLOCK
        forced = (blk == 0) | (blk == cur) | (blk == cur - 1)
        score = jnp.where(forced, FORCED_SCORE, jnp.where(blk * SEL_BLOCK <= q_pos_t, pair, -1.0))
        work = jnp.where((row % SEL_RATIO == 0) & (blk < n_sel_blocks), score, KNOCKED_OUT)
        row_f = row.astype(F32)
        sel_t = jnp.zeros((n_cmp, tq), F32)
        for _ in range(n_take):
            top = jnp.max(work, axis=0, keepdims=True)
            pick = jnp.min(jnp.where(work == top, row_f, float(n_cmp)), axis=0, keepdims=True)
            hit = row_f == pick
            work = jnp.where(hit, KNOCKED_OUT, work)
            sel_t = jnp.where(hit, 1.0, sel_t)
        if n_cmp < LANES:
            sel_t = jnp.concatenate([sel_t, jnp.zeros((LANES - n_cmp, tq), F32)], axis=0)
        return sel_t.T

    def all_blocks():
        return jnp.where((lane % SEL_RATIO == 0) & (lane // SEL_RATIO < n_sel_blocks), 1.0,
                         jnp.zeros((tq, LANES), F32))

    sel = lax.cond(((i + 1) * tq - 1) // SEL_BLOCK + 1 <= n_take, all_blocks, top_blocks).astype(MXU_DTYPE)

    def sel_chunk(kc, carry):
        m_i, l_i, acc = carry
        start = pl.multiple_of(kc * tk, tk)
        k_pos = start + lax.broadcasted_iota(jnp.int32, (1, tk), 1)
        expand = lax.broadcasted_iota(jnp.int32, (LANES, 1), 0) == SEL_RATIO * (k_pos // SEL_BLOCK)
        picked = jnp.dot(sel, jnp.where(expand, 1.0, 0.0).astype(MXU_DTYPE), preferred_element_type=F32)
        bias = jnp.where((picked > 0.5) & (k_pos <= q_pos), 0.0, NEG_INF)[None]
        sc = _mxu_nt(q4, ks_ref[pl.ds(start, tk), :]).reshape(GROUP_SIZE, tq, tk) + bias
        m_new = jnp.maximum(m_i, jnp.max(sc, axis=-1, keepdims=True))
        alpha = jnp.exp2((m_i - m_new) * EXP2_SCALE)
        e = jnp.exp2((sc - m_new) * EXP2_SCALE)
        l_new = alpha * l_i + jnp.sum(e, axis=-1, keepdims=True)
        pv = _mxu(e.reshape(GROUP_SIZE * tq, tk), vs_ref[pl.ds(start, tk), :])
        return m_new, l_new, alpha * acc + pv.reshape(GROUP_SIZE, tq, HEAD_DIM)

    init = (jnp.full((GROUP_SIZE, tq, 1), NEG_INF, F32), jnp.zeros((GROUP_SIZE, tq, 1), F32),
            jnp.zeros((GROUP_SIZE, tq, HEAD_DIM), F32))
    _, l_s, acc_s = lax.fori_loop(0, ((i + 1) * tq + tk - 1) // tk, sel_chunk, init)
    o_s = acc_s / l_s

    span = tq + WINDOW
    w_start = pl.multiple_of(jnp.maximum(i * tq - WINDOW, 0), tq)
    dist = q_pos - (w_start + lax.broadcasted_iota(jnp.int32, (1, span), 1))
    bias = jnp.where((dist >= 0) & (dist < WINDOW), 0.0, NEG_INF)[None]
    sw = _mxu_nt(q4, kw_ref[pl.ds(w_start, span), :]).reshape(GROUP_SIZE, tq, span) + bias
    ew = jnp.exp2((sw - jnp.max(sw, axis=-1, keepdims=True)) * EXP2_SCALE)
    pv = _mxu(ew.reshape(GROUP_SIZE * tq, span), _with_ones(vw_ref[pl.ds(w_start, span), :]))
    o_w = (pv[:, :HEAD_DIM] / pv[:, HEAD_DIM:]).reshape(GROUP_SIZE, tq, HEAD_DIM)

    gates = g_ref[...]
    for h in range(GROUP_SIZE):
        o = (gates[:, 3 * h:3 * h + 1] * o_c[h] + gates[:, 3 * h + 1:3 * h + 2] * o_s[h]
             + gates[:, 3 * h + 2:3 * h + 3] * o_w[h])
        o_ref[:, h * HEAD_DIM:(h + 1) * HEAD_DIM] = o.astype(o_ref.dtype)


def _attn_prompt(q, ckv, kv, gates, *, tq, tk):
    b, t, _ = q.shape
    n_cmp = ckv.shape[1]
    assert t % tq == 0 and t % tk == 0 and t >= tq + WINDOW and WINDOW % tq == 0 and tq % ROW_BLOCK == 0
    gw = GROUP_SIZE * HEAD_DIM

    def kv_spec(branch, part):
        return pl.BlockSpec((None, None, t, HEAD_DIM), lambda bi, g, i: (branch, bi, 0, 2 * g + part))

    return pl.pallas_call(
        functools.partial(_attn_prompt_kernel, tq=tq, tk=tk, seq=t),
        grid=(b, N_KV_HEADS, t // tq),
        in_specs=[pl.BlockSpec((None, tq, gw), lambda bi, g, i: (bi, i, g)),
                  pl.BlockSpec((None, n_cmp, HEAD_DIM), lambda bi, g, i: (bi, 0, 2 * g)),
                  pl.BlockSpec((None, n_cmp, HEAD_DIM), lambda bi, g, i: (bi, 0, 2 * g + 1)),
                  kv_spec(1, 0), kv_spec(1, 1), kv_spec(2, 0), kv_spec(2, 1),
                  pl.BlockSpec((None, tq, LANES), lambda bi, g, i: (bi, i, g))],
        out_specs=pl.BlockSpec((None, tq, gw), lambda bi, g, i: (bi, i, g)),
        out_shape=jax.ShapeDtypeStruct((b, t, Q_WIDTH), MXU_DTYPE),
        compiler_params=_params("arbitrary", "arbitrary", "arbitrary"),
        name="attn_prompt",
    )(q, ckv, ckv, kv, kv, kv, kv, gates)


def _attn_cmp_sample_kernel(q_ref, ckv_ref, o_ref, idx_ref, *, past, n_total, width):
    t_rows = q_ref.shape[0]
    n_cmp = ckv_ref.shape[0]
    n_sel_blocks = -(-n_total // SEL_BLOCK)
    n_take = min(N_SELECT, n_sel_blocks)
    assert SEL_RATIO == 2 and n_take <= LANES and SEL_RATIO * n_sel_blocks <= width
    q_pos = past + lax.broadcasted_iota(jnp.int32, (t_rows, 1), 0)
    lane = lax.broadcasted_iota(jnp.int32, (1, width), 1)
    vis = ((lane + 1) * CMP_BLOCK - 1 <= q_pos) & (lane < n_cmp)
    out_lane = lax.broadcasted_iota(jnp.int32, (1, LANES), 1)
    pad = jnp.zeros((width - n_cmp, HEAD_DIM), ckv_ref.dtype)
    imps = []
    for g in range(N_KV_HEADS):
        ck = jnp.concatenate([ckv_ref[:, 2 * g * HEAD_DIM:(2 * g + 1) * HEAD_DIM], pad], axis=0)
        cv = jnp.concatenate([ckv_ref[:, (2 * g + 1) * HEAD_DIM:(2 * g + 2) * HEAD_DIM], pad], axis=0)
        imp = jnp.zeros((t_rows, width), F32)
        for h in range(GROUP_SIZE):
            cols = slice((g * GROUP_SIZE + h) * HEAD_DIM, (g * GROUP_SIZE + h + 1) * HEAD_DIM)
            s = _mxu_nt(q_ref[:, cols], ck) * ATTN_SCALE
            p = jnp.where(vis, _softmax_rows(jnp.where(vis, s, NEG_INF)), 0.0)
            o_ref[:, cols] = _mxu(p, cv)
            imp = imp + p
        imps.append(imp)
    work = _block_scores(jnp.concatenate(imps, axis=0), jnp.concatenate([q_pos] * N_KV_HEADS, axis=0), n_sel_blocks)
    picks = jnp.zeros((N_KV_HEADS * t_rows, LANES), F32)
    for r in range(n_take):
        hit, pick = _take_top(work)
        work = jnp.where(hit, KNOCKED_OUT, work)
        picks = jnp.where(out_lane == r, pick, picks)
    idx = (picks * (1.0 / SEL_RATIO)).astype(jnp.int32)
    for g in range(N_KV_HEADS):
        idx_ref[g] = idx[g * t_rows:(g + 1) * t_rows]


def _attn_cmp_sample(q, ckv, *, past, n_total):
    b, t, qw = q.shape
    n_cmp = ckv.shape[1]
    width = -(-max(n_cmp, SEL_RATIO * -(-n_total // SEL_BLOCK)) // LANES) * LANES
    return pl.pallas_call(
        functools.partial(_attn_cmp_sample_kernel, past=past, n_total=n_total, width=width),
        grid=(b,),
        in_specs=[pl.BlockSpec((None, t, qw), lambda bi: (bi, 0, 0)),
                  pl.BlockSpec((None, n_cmp, KV_WIDTH), lambda bi: (bi, 0, 0))],
        out_specs=[pl.BlockSpec((None, t, qw), lambda bi: (bi, 0, 0)),
                   pl.BlockSpec((None, N_KV_HEADS, t, LANES), lambda bi: (bi, 0, 0, 0))],
        out_shape=[jax.ShapeDtypeStruct(q.shape, F32),
                   jax.ShapeDtypeStruct((b, N_KV_HEADS, t, LANES), jnp.int32)],
        compiler_params=_params("arbitrary"),
        name="attn_cmp_sample",
    )(q, ckv)


def _attn_win_sample_kernel(q_ref, buf_ref, new_ref, o_ref, *, t_new):
    t_rows = q_ref.shape[0]
    n_chunks = KV_WIDTH // HEAD_DIM
    n_buf = buf_ref.shape[0] // n_chunks
    t_idx = lax.broadcasted_iota(jnp.int32, (GROUP_SIZE * t_rows, 1), 0) % t_rows
    dist_a = t_idx + n_buf - lax.broadcasted_iota(jnp.int32, (1, n_buf), 1)
    row_b = lax.broadcasted_iota(jnp.int32, (1, new_ref.shape[0]), 1)
    mask_a = (dist_a >= 0) & (dist_a < WINDOW)
    mask_b = (row_b <= t_idx) & (t_idx - row_b < WINDOW) & (row_b < t_new)
    for g in range(N_KV_HEADS):
        k_a = buf_ref[pl.ds(2 * g, n_buf, stride=n_chunks), :]
        v_a = buf_ref[pl.ds(2 * g + 1, n_buf, stride=n_chunks), :]
        k_b = new_ref[:, 2 * g * HEAD_DIM:(2 * g + 1) * HEAD_DIM]
        v_b = new_ref[:, (2 * g + 1) * HEAD_DIM:(2 * g + 2) * HEAD_DIM]
        head_cols = [slice((g * GROUP_SIZE + h) * HEAD_DIM, (g * GROUP_SIZE + h + 1) * HEAD_DIM)
                     for h in range(GROUP_SIZE)]
        q = jnp.concatenate([q_ref[:, cols] for cols in head_cols], axis=0)
        s_a = jnp.where(mask_a, _mxu_nt(q, k_a) * ATTN_SCALE, NEG_INF)
        s_b = jnp.where(mask_b, _mxu_nt(q, k_b) * ATTN_SCALE, NEG_INF)
        m = jnp.maximum(jnp.max(s_a, axis=-1, keepdims=True), jnp.max(s_b, axis=-1, keepdims=True))
        e_a, e_b = jnp.exp(s_a - m), jnp.exp(s_b - m)
        l = jnp.sum(e_a, axis=-1, keepdims=True) + jnp.sum(e_b, axis=-1, keepdims=True)
        o = _mxu(e_a / l, v_a) + _mxu(e_b / l, v_b)
        for h, cols in enumerate(head_cols):
            o_ref[:, cols] = o[h * t_rows:(h + 1) * t_rows]


def _attn_win_sample(q, win_buf, kv_new, *, layer, t_new):
    b, t, qw = q.shape
    buf_rows = win_buf.shape[2]
    return pl.pallas_call(
        functools.partial(_attn_win_sample_kernel, t_new=t_new),
        grid=(b,),
        in_specs=[pl.BlockSpec((None, t, qw), lambda bi: (bi, 0, 0)),
                  pl.BlockSpec((None, None, buf_rows, HEAD_DIM), lambda bi: (layer, bi, 0, 0)),
                  pl.BlockSpec((None, kv_new.shape[1], KV_WIDTH), lambda bi: (bi, 0, 0))],
        out_specs=pl.BlockSpec((None, t, qw), lambda bi: (bi, 0, 0)),
        out_shape=jax.ShapeDtypeStruct(q.shape, F32),
        compiler_params=_params("arbitrary"),
        name="attn_win_sample",
    )(q, win_buf, kv_new)


def _attn_sel_sample_kernel(idx_ref, pt_ref, q_ref, *refs, past, t_new, n_slots, t_steps):
    del pt_ref
    blocks = refs[:n_slots]
    new_ref, oc_ref, ow_ref, g_ref, o_ref = refs[n_slots:]
    bi, t, gi = pl.program_id(0), pl.program_id(1), pl.program_id(2)
    base = ((bi * t_steps + t) * N_KV_HEADS + gi) * n_slots
    n_past_blocks = past // SEL_BLOCK
    q_pos = past + t
    q = q_ref[...]
    n_chunks = KV_WIDTH // HEAD_DIM
    lane = lax.broadcasted_iota(jnp.int32, (1, n_slots * SEL_BLOCK), 1)
    pos = lane % SEL_BLOCK
    has_new = jnp.int32(0)
    for n in range(n_slots):
        blk = idx_ref[base + n]
        start = jnp.where(blk < n_past_blocks, blk * SEL_BLOCK, q_pos + 1)
        pos = pos + jnp.where(lane // SEL_BLOCK == n, start, 0)
        has_new = has_new | (blk == n_past_blocks).astype(jnp.int32)
    mask_a = pos <= q_pos
    new = new_ref[...]
    row_b = lax.broadcasted_iota(jnp.int32, (1, new.shape[0]), 1)
    mask_b = (row_b <= t) & (row_b < t_new) & (has_new > 0)
    s_b = jnp.where(mask_b, _mxu_nt(q, new[:, :HEAD_DIM]) * ATTN_SCALE, NEG_INF)
    gates = g_ref[...]

    for g in range(N_KV_HEADS):
        @pl.when(gi == g)
        def _(g=g):
            keys = jnp.concatenate([blk[pl.ds(2 * g, SEL_BLOCK, stride=n_chunks), :] for blk in blocks], axis=0)
            vals = jnp.concatenate([blk[pl.ds(2 * g + 1, SEL_BLOCK, stride=n_chunks), :] for blk in blocks], axis=0)
            s_a = jnp.where(mask_a, _mxu_nt(q, keys) * ATTN_SCALE, NEG_INF)
            m = jnp.maximum(jnp.max(s_a, axis=-1, keepdims=True), jnp.max(s_b, axis=-1, keepdims=True))
            e_a, e_b = jnp.exp(s_a - m), jnp.exp(s_b - m)
            l = jnp.sum(e_a, axis=-1, keepdims=True) + jnp.sum(e_b, axis=-1, keepdims=True)
            o_s = _mxu(e_a / l, vals) + _mxu(e_b / l, new[:, HEAD_DIM:])
            o_ref[...] = gates[:, 0:1] * oc_ref[...] + gates[:, 1:2] * o_s + gates[:, 2:3] * ow_ref[...]


def _attn_sel_sample(idx, page_table, q, pool, kv_new, o_c, o_w, gates, *, layer, past, t_new):
    b, t, g, hg, hd = q.shape
    n_slots = idx.shape[-1]
    n_pages = page_table.shape[1]
    page = past // n_pages
    per_page = page // SEL_BLOCK
    n_past_blocks = past // SEL_BLOCK

    def slot_spec(n):
        def index(bi, ti, gi, idx_s, pt_s):
            blk = jnp.minimum(idx_s[((bi * t + ti) * g + gi) * n_slots + n], n_past_blocks - 1)
            phys = pt_s[bi * n_pages + blk // per_page]
            return (layer, phys, blk % per_page, 0)
        return pl.BlockSpec((None, None, SEL_BLOCK * (KV_WIDTH // HEAD_DIM), HEAD_DIM), index)

    head_spec = pl.BlockSpec((None, None, None, hg, hd), lambda bi, ti, gi, idx_s, pt_s: (bi, ti, gi, 0, 0))
    grid_spec = pltpu.PrefetchScalarGridSpec(
        num_scalar_prefetch=2,
        grid=(b, t, g),
        in_specs=[head_spec] + [slot_spec(n) for n in range(n_slots)] + [
            pl.BlockSpec((None, kv_new.shape[1], 2 * HEAD_DIM), lambda bi, ti, gi, idx_s, pt_s: (bi, 0, gi)),
            head_spec, head_spec,
            pl.BlockSpec((None, None, None, hg, 3), lambda bi, ti, gi, idx_s, pt_s: (bi, ti, gi, 0, 0))],
        out_specs=head_spec,
    )
    return pl.pallas_call(
        functools.partial(_attn_sel_sample_kernel, past=past, t_new=t_new, n_slots=n_slots, t_steps=t),
        grid_spec=grid_spec,
        out_shape=jax.ShapeDtypeStruct(q.shape, F32),
        compiler_params=_params("arbitrary", "arbitrary", "arbitrary"),
        name="attn_sel_sample",
    )(idx.reshape(-1), page_table.reshape(-1), q, *([pool] * n_slots), kv_new, o_c, o_w, gates)


def _dwconv_kernel(cur_ref, prev_ref, buf_ref, w_ref, b_ref, lg_ref, lb_ref, o_ref, full_scr, y_scr, *,
                   tt, halo, width):
    i = pl.program_id(1)
    lead = halo - (width - 1)
    full_scr[0:halo, :] = prev_ref[...]

    @pl.when(i == 0)
    def _():
        full_scr[0:halo, :] = buf_ref[...]

    full_scr[halo:halo + tt, :] = cur_ref[...]
    for c0 in range(0, cur_ref.shape[1], DWCONV_COLS):
        cols = slice(c0, c0 + DWCONV_COLS)
        acc = jnp.zeros((tt, DWCONV_COLS), F32) + b_ref[:, cols]
        for r in range(SUBLANES):
            offsets = [o for o in range(r, lead + width, SUBLANES) if o >= lead]
            rows = tt + (SUBLANES if r else 0)
            group = None
            for o in offsets:
                term = full_scr[o - r:o - r + rows, cols] * w_ref[o - lead:o - lead + 1, cols]
                group = term if group is None else group + term
            if group is not None:
                acc = acc + group[r:r + tt]
        y_scr[:, cols] = acc
    y = y_scr[...]
    mu = jnp.mean(y, axis=-1, keepdims=True)
    var = jnp.mean(jnp.square(y - mu), axis=-1, keepdims=True)
    yn = (y - mu) * lax.rsqrt(var + LN_EPS) * lg_ref[...] + lb_ref[...]
    o_ref[...] = _silu(yn).astype(o_ref.dtype)


def _dwconv(glu, buf, w_dw, b_dw, ln_g, ln_b, *, layer, tt):
    b, t, d = glu.shape
    width = w_dw.shape[1]
    halo = buf.shape[1]
    assert tt % halo == 0 and t % tt == 0 and halo >= width - 1
    per = tt // halo

    def vec_spec():
        return pl.BlockSpec((None, 1, d), lambda bi, i: (layer, 0, 0))

    return pl.pallas_call(
        functools.partial(_dwconv_kernel, tt=tt, halo=halo, width=width),
        grid=(b, t // tt),
        in_specs=[pl.BlockSpec((None, tt, d), lambda bi, i: (bi, i, 0)),
                  pl.BlockSpec((None, halo, d), lambda bi, i: (bi, jnp.maximum(i * per - 1, 0), 0)),
                  pl.BlockSpec((None, halo, d), lambda bi, i: (bi, 0, 0)),
                  pl.BlockSpec((None, width, d), lambda bi, i: (layer, 0, 0)),
                  vec_spec(), vec_spec(), vec_spec()],
        out_specs=pl.BlockSpec((None, tt, d), lambda bi, i: (bi, i, 0)),
        out_shape=jax.ShapeDtypeStruct((b, t, d), MXU_DTYPE),
        scratch_shapes=[pltpu.VMEM((halo + tt, d), F32), pltpu.VMEM((tt, d), F32)],
        compiler_params=_params("arbitrary", "arbitrary"),
        name="dwconv",
    )(glu, glu, buf, w_dw, b_dw, ln_g, ln_b)


def _rope_tables(pos):
    half = HEAD_DIM // 2
    inv = ROPE_THETA ** (-jnp.arange(half, dtype=F32) / half)
    ang = pos.astype(F32)[:, None] * inv[None, :]
    cos, sin = jnp.cos(ang), jnp.sin(ang)
    return jnp.concatenate([cos, cos], axis=1), jnp.concatenate([-sin, sin], axis=1)


def _group_gate_weights(w_gate_cols):
    d = w_gate_cols.shape[0]
    wg = w_gate_cols.reshape(d, N_KV_HEADS, GATE_COLS)
    return jnp.pad(wg, ((0, 0), (0, 0), (0, LANES - GATE_COLS))).reshape(d, N_KV_HEADS * LANES)


def _run_trunk(x, mod, pos, prm, caches, tiles):
    b, t, d = x.shape
    m = b * t
    depth = prm["w_mod"].shape[0]
    tm = tiles["tm"]
    rpm = t if caches is None else 1
    cos, sin = _rope_tables(pos)
    if caches is not None:
        cos, sin = jnp.tile(cos, (b, 1)), jnp.tile(sin, (b, 1))
    rows_per_seq = t if caches is None else m
    x = x.reshape(m, d)
    new_c, new_s, new_w, new_conv = [], [], [], []

    mod_tab = jnp.transpose(mod, (0, 2, 1, 3)).reshape(depth * N_MOD, b, d)
    if caches is not None:
        mod_tab = jnp.repeat(mod_tab, t, axis=1)
    mod_tab = mod_tab[:, :, None, :]
    norm_tab = prm["norm_g"].reshape(depth * 3, 1, d)

    def ffn(x, layer, which, k0):
        return _ffn(x, norm_tab, mod_tab, prm["final_norm_g"].reshape(1, d), prm["ffn_w_gate"], prm["ffn_w_up"],
                    prm["ffn_w_down"], norm_slot=layer * 3 + which * 2, mod_slot=layer * N_MOD + k0, layer=layer,
                    which=which, rows_per_mod=rpm, tm=tiles["tm_ffn"], tf=tiles["tf"],
                    final_norm=(layer == depth - 1 and which == 1))

    for i in range(depth):
        x = ffn(x, i, 0, 0)
        a = i // 2
        mixer = dict(norm_slot=i * 3 + 1, mod_slot=i * N_MOD + 3, rows_per_mod=rpm, tm=tm)
        if i % 2 == 0:
            w_in = prm["attn_w_in"]
            common = dict(rows_per_seq=rows_per_seq, **mixer)
            q_dtype = MXU_DTYPE if caches is None else F32
            q, = _proj(x, norm_tab, mod_tab, w_in, cos, sin, layer=a, mode="q", col0=0, n=Q_WIDTH,
                       out_dtypes=[q_dtype], tn=tiles["tn_out"], **common)
            kv_outs = [F32, MXU_DTYPE] if caches is None else [F32]
            kv = _proj(x, norm_tab, mod_tab, w_in, cos, sin, layer=a, mode="kv", col0=Q_WIDTH, n=3 * KV_WIDTH,
                       out_dtypes=kv_outs, tn=KV_WIDTH, **common)
            gates, = _proj(x, norm_tab, mod_tab, _group_gate_weights(prm["attn_gate_w"][a])[None], cos, sin, layer=0,
                           mode="sigmoid", col0=0, n=N_KV_HEADS * LANES, out_dtypes=[F32],
                           tn=N_KV_HEADS * LANES, **common)
            kv32 = kv[0]
            new_c.append(kv32[0].reshape(b, t, N_KV_HEADS, 2, HEAD_DIM))
            new_s.append(kv32[1].reshape(b, t, N_KV_HEADS, 2, HEAD_DIM))
            if caches is None:
                ckv = _compress_rows(kv32[0], prm["cmp_pe"], prm["cmp_w1"], prm["cmp_w2"], layer=a, n_seq=b)
                o = _attn_prompt(q.reshape(b, t, Q_WIDTH), ckv.reshape(b, t // CMP_BLOCK, KV_WIDTH),
                                 kv[1].reshape(3, b, t, KV_WIDTH), gates.reshape(b, t, N_KV_HEADS * LANES),
                                 tq=tiles["tq"], tk=tiles["tk"]).reshape(m, Q_WIDTH)
                new_w.append(kv32[2].reshape(b, t, N_KV_HEADS, 2, HEAD_DIM)[:, t - min(WINDOW, t):])
            else:
                pool_c, pool_s, win_buf, _, page_table = caches
                n_layers, n_pool, page = pool_c.shape[:3]
                past = page_table.shape[1] * page
                assert t < CMP_BLOCK and page % SEL_BLOCK == 0
                ckv = _compress_pages(pool_c.reshape(n_layers, n_pool, page * (KV_WIDTH // HEAD_DIM), HEAD_DIM),
                                      page_table, prm["cmp_pe"], prm["cmp_w1"], prm["cmp_w2"], layer=a)
                q3 = q.reshape(b, t, Q_WIDTH)
                o_c, idx = _attn_cmp_sample(q3, ckv, past=past, n_total=past + t)
                t_pad = -(-t // SUBLANES) * SUBLANES
                pad_rows = lambda r: jnp.pad(r.reshape(b, t, KV_WIDTH), ((0, 0), (0, t_pad - t), (0, 0)))
                n_buf = win_buf.shape[2]
                o_w = _attn_win_sample(q3, win_buf.reshape(n_layers, b, n_buf * (KV_WIDTH // HEAD_DIM), HEAD_DIM),
                                       pad_rows(kv32[2]), layer=a, t_new=t)
                n_take = min(N_SELECT, -(-(past + t) // SEL_BLOCK))
                idx = jnp.transpose(idx[..., :n_take], (0, 2, 1, 3))
                heads = lambda v: v.reshape(b, t, N_KV_HEADS, GROUP_SIZE, HEAD_DIM)
                g5 = gates.reshape(b, t, N_KV_HEADS, LANES)[..., :GATE_COLS].reshape(b, t, N_KV_HEADS, GROUP_SIZE, 3)
                o = _attn_sel_sample(idx, page_table, heads(q3),
                                     pool_s.reshape(n_layers, n_pool, page * (KV_WIDTH // HEAD_DIM), HEAD_DIM),
                                     pad_rows(kv32[1]), heads(o_c), heads(o_w), g5, layer=a, past=past,
                                     t_new=t).reshape(m, Q_WIDTH)
                new_w.append(kv32[2].reshape(b, t, N_KV_HEADS, 2, HEAD_DIM))
            x = _linres(o, prm["attn_w_out"], x, mod_tab, gate_slot=i * N_MOD + 5, layer=a, rows_per_mod=rpm, tm=tm,
                        tn=tiles["tn_out"])
        else:
            glu = _glu_proj(x, norm_tab, mod_tab, prm["conv_w_pw1"], layer=a, tn=tiles["tn_out"], **mixer)
            glu = glu.reshape(b, t, d)
            width = prm["conv_w_dw"].shape[1]
            halo = 32
            if caches is None:
                buf = jnp.zeros((b, width - 1, d), F32)
                t_conv = t
                cur = glu
            else:
                buf = caches[3][a]
                t_conv = halo
                cur = jnp.pad(glu, ((0, 0), (0, t_conv - t), (0, 0)))
            if t >= width - 1:
                new_conv.append(glu[:, t - (width - 1):])
            else:
                new_conv.append(jnp.concatenate([buf[:, t:], glu], axis=1))
            buf = jnp.pad(buf, ((0, 0), (halo - (width - 1), 0), (0, 0)))
            vec = lambda v: v.reshape(v.shape[0], 1, d)
            act = _dwconv(cur, buf, prm["conv_w_dw"], vec(prm["conv_b_dw"]), vec(prm["conv_ln_g"]),
                          vec(prm["conv_ln_b"]), layer=a, tt=min(tiles["tt"], t_conv))
            act = act[:, :t].reshape(m, d)
            x = _linres(act, prm["conv_w_pw2"], x, mod_tab, gate_slot=i * N_MOD + 5, layer=a, rows_per_mod=rpm, tm=tm,
                        tn=tiles["tn_out"])
        x = ffn(x, i, 1, 6)
    y = x.reshape(b, t, d)
    new_w = jnp.stack(new_w)
    if caches is not None:
        keep = min(WINDOW, caches[2].shape[2] + t) - t
        new_w = jnp.concatenate([caches[2][:, :, caches[2].shape[2] - keep:], new_w], axis=2)
    return y, jnp.stack(new_c), jnp.stack(new_s), new_w, jnp.stack(new_conv)


PROMPT_TILES = dict(tm=1024, tm_ffn=1024, tf=512, tq=256, tk=512, tn_out=1024, tt=128)
SAMPLE_TILES = dict(tm=32, tm_ffn=32, tf=512, tn_out=1024, tt=32)


def kernel(x_prompt, x_sample, cache_cmp_kv, cache_sel_kv, cache_win_kv, state_conv, page_table, c_prompt, c_sample, w_mod, b_mod, norm_g, ffn_w_gate, ffn_w_up, ffn_w_down, attn_w_in, attn_w_out, cmp_pe, cmp_w1, cmp_w2, conv_w_pw1, conv_w_dw, conv_b_dw, conv_ln_g, conv_ln_b, conv_w_pw2, final_norm_g):
    mxu = lambda w: w.astype(MXU_DTYPE)
    prm = {"w_mod": w_mod, "norm_g": norm_g, "ffn_w_gate": mxu(ffn_w_gate), "ffn_w_up": mxu(ffn_w_up),
           "ffn_w_down": mxu(ffn_w_down), "attn_w_in": mxu(attn_w_in), "attn_gate_w": attn_w_in[..., -N_HEADS * 3:],
           "attn_w_out": mxu(attn_w_out), "cmp_pe": cmp_pe,
           "cmp_w1": cmp_w1, "cmp_w2": cmp_w2, "conv_w_pw1": mxu(conv_w_pw1), "conv_w_dw": conv_w_dw,
           "conv_b_dw": conv_b_dw, "conv_ln_g": conv_ln_g, "conv_ln_b": conv_ln_b, "conv_w_pw2": mxu(conv_w_pw2),
           "final_norm_g": final_norm_g}
    depth, d, _ = w_mod.shape
    bp, tp = x_prompt.shape[:2]
    bs, ts = x_sample.shape[:2]
    past = page_table.shape[1] * cache_cmp_kv.shape[2]

    n_req = bp + bs
    r_pad = -(-n_req // 8) * 8
    c_all = jnp.pad(jnp.concatenate([c_prompt, c_sample], axis=0), ((0, r_pad - n_req), (0, 0)))
    mod = _mod_vectors(c_all, w_mod, b_mod).reshape(depth, r_pad, N_MOD, d)

    pos_p = jnp.arange(tp, dtype=jnp.int32)
    pos_s = past + jnp.arange(ts, dtype=jnp.int32)
    y_p, p_cmp, p_sel, p_win, p_conv = _run_trunk(x_prompt, mod[:, :bp], pos_p, prm, None, PROMPT_TILES)
    y_s, s_cmp, s_sel, s_win, s_conv = _run_trunk(
        x_sample, mod[:, bp:n_req], pos_s, prm,
        (cache_cmp_kv, cache_sel_kv, cache_win_kv, state_conv, page_table), SAMPLE_TILES)
    return (y_p, y_s, p_cmp, p_sel, p_win, p_conv, s_cmp, s_sel, s_win, s_conv)
```

```python
import functools

import jax
import jax.numpy as jnp
from jax import lax
from jax.experimental import pallas as pl
from jax.experimental.pallas import tpu as pltpu

F32 = jnp.float32
MXU_DTYPE = jnp.bfloat16
VMEM_LIMIT_BYTES = 60 * 1024 * 1024
LANES = 128
SUBLANES = 8

N_HEADS = 16
HEAD_DIM = 128
N_KV_HEADS = 4
GROUP_SIZE = N_HEADS // N_KV_HEADS
CMP_BLOCK = 32
SEL_BLOCK = 64
SEL_RATIO = SEL_BLOCK // CMP_BLOCK
N_SELECT = 16
WINDOW = 512
ROPE_THETA = 10000.0
N_MOD = 9
RMS_EPS = 1e-6
LN_EPS = 1e-5
NEG_INF = -1e30
FORCED_SCORE = 1e9
KNOCKED_OUT = -3e38
ATTN_SCALE = HEAD_DIM ** -0.5
Q_WIDTH = N_HEADS * HEAD_DIM
KV_WIDTH = 2 * N_KV_HEADS * HEAD_DIM
GATE_COLS = GROUP_SIZE * 3
PAGES_PER_STEP = 16
DWCONV_COLS = 128
ROW_BLOCK = 32
FFN_PANEL = 512
PROJ_PANEL = 512
DWCONV_HALO = 32
EXP2_SCALE = ATTN_SCALE * 1.4426950408889634


def _params(*sem):
    return pltpu.CompilerParams(dimension_semantics=sem, vmem_limit_bytes=VMEM_LIMIT_BYTES)


def _mxu(a, b):
    return jnp.dot(a.astype(MXU_DTYPE), b.astype(MXU_DTYPE), preferred_element_type=F32)


def _mxu_nt(a, b):
    return lax.dot_general(a.astype(MXU_DTYPE), b.astype(MXU_DTYPE),
                           (((1,), (1,)), ((), ())), preferred_element_type=F32)


def _silu(x):
    return x * jax.nn.sigmoid(x)


def _softmax_rows(s):
    e = jnp.exp(s - jnp.max(s, axis=-1, keepdims=True))
    return e / jnp.sum(e, axis=-1, keepdims=True)


def _softmax_scaled(s):
    e = jnp.exp2((s - jnp.max(s, axis=-1, keepdims=True)) * EXP2_SCALE)
    return e / jnp.sum(e, axis=-1, keepdims=True)


def _mod_spec(tm, rows_per_mod, width, slot, by_column=False):
    if rows_per_mod >= tm:
        assert rows_per_mod % tm == 0
        block_rows, idx = 1, (lambda i: (i * tm) // rows_per_mod)
    else:
        assert rows_per_mod == 1
        block_rows, idx = tm, (lambda i: i)
    return pl.BlockSpec((None, block_rows, 1, width), lambda i, j: (slot, idx(i), 0, j if by_column else 0))


def _gain_spec(d, slot):
    return pl.BlockSpec((None, 1, d), lambda i, j: (slot, 0, 0))


def _modulated(x_ref, ng_ref, sh_ref, sc_ref):
    x = x_ref[...]
    gain = ng_ref[...] * (1.0 + sc_ref[:, 0, :])
    return x * lax.rsqrt(jnp.mean(x * x, axis=-1, keepdims=True) + RMS_EPS) * gain + sh_ref[:, 0, :]


def _mod_kernel(c_ref, w_ref, b_ref, o_ref):
    o_ref[...] = _mxu(_silu(c_ref[...]), w_ref[...]) + b_ref[...]


def _mod_vectors(c_all, w_mod, b_mod, tn=1024):
    depth, d, n = w_mod.shape
    r = c_all.shape[0]
    return pl.pallas_call(
        _mod_kernel,
        grid=(depth, n // tn),
        in_specs=[pl.BlockSpec((r, d), lambda l, j: (0, 0)),
                  pl.BlockSpec((None, d, tn), lambda l, j: (l, 0, j)),
                  pl.BlockSpec((None, 1, tn), lambda l, j: (l, 0, j))],
        out_specs=pl.BlockSpec((None, r, tn), lambda l, j: (l, 0, j)),
        out_shape=jax.ShapeDtypeStruct((depth, r, n), F32),
        compiler_params=_params("arbitrary", "arbitrary"),
        name="mod_vectors",
    )(c_all, w_mod, b_mod.reshape(depth, 1, n))


def _ffn_kernel(x_ref, ng_ref, sh_ref, sc_ref, gt_ref, fg_ref, wg_ref, wu_ref, wd_ref, o_ref, h_scr, *, final_norm):
    f = pl.program_id(1)

    def panels(first):
        panel = min(FFN_PANEL, h_scr.shape[0])
        for p0 in range(0, h_scr.shape[0], panel):
            rows = slice(p0, p0 + panel)
            h = h_scr[rows, :]
            g = _mxu(h, wg_ref[...])
            u = _mxu(h, wu_ref[...])
            a = jnp.concatenate([(_silu(g[r0:r0 + ROW_BLOCK]) * u[r0:r0 + ROW_BLOCK]).astype(MXU_DTYPE)
                                 for r0 in range(0, panel, ROW_BLOCK)], axis=0)
            y = _mxu(a, wd_ref[...])
            if first:
                o_ref[rows, :] = y
            else:
                o_ref[rows, :] += y

    @pl.when(f == 0)
    def _():
        h_scr[...] = _modulated(x_ref, ng_ref, sh_ref, sc_ref).astype(h_scr.dtype)
        panels(True)

    @pl.when(f > 0)
    def _():
        panels(False)

    @pl.when(f == pl.num_programs(1) - 1)
    def _():
        y = x_ref[...] + (0.5 * gt_ref[:, 0, :]) * o_ref[...]
        if final_norm:
            y = y * lax.rsqrt(jnp.mean(y * y, axis=-1, keepdims=True) + RMS_EPS) * fg_ref[...]
        o_ref[...] = y


def _ffn(x, norm_tab, mod_tab, final_g, w_gate, w_up, w_down, *, norm_slot, mod_slot, layer, which, rows_per_mod,
         tm, tf, final_norm):
    m, d = x.shape
    dff = w_gate.shape[-1]
    return pl.pallas_call(
        functools.partial(_ffn_kernel, final_norm=final_norm),
        grid=(m // tm, dff // tf),
        in_specs=[pl.BlockSpec((tm, d), lambda i, f: (i, 0)),
                  _gain_spec(d, norm_slot),
                  _mod_spec(tm, rows_per_mod, d, mod_slot), _mod_spec(tm, rows_per_mod, d, mod_slot + 1),
                  _mod_spec(tm, rows_per_mod, d, mod_slot + 2),
                  pl.BlockSpec((1, d), lambda i, f: (0, 0)),
                  pl.BlockSpec((None, None, d, tf), lambda i, f: (layer, which, 0, f)),
                  pl.BlockSpec((None, None, d, tf), lambda i, f: (layer, which, 0, f)),
                  pl.BlockSpec((None, None, tf, d), lambda i, f: (layer, which, f, 0))],
        out_specs=pl.BlockSpec((tm, d), lambda i, f: (i, 0)),
        out_shape=jax.ShapeDtypeStruct((m, d), F32),
        scratch_shapes=[pltpu.VMEM((tm, d), MXU_DTYPE)],
        compiler_params=_params("arbitrary", "arbitrary"),
        name="ffn",
    )(x, norm_tab, mod_tab, mod_tab, mod_tab, final_g, w_gate, w_up, w_down)


def _rope_chunk(z, cos, sin):
    return z * cos + pltpu.roll(z, HEAD_DIM // 2, 1) * sin


def _proj_kernel(x_ref, ng_ref, sh_ref, sc_ref, w_ref, cos_ref, sin_ref, *rest, mode):
    out_refs, h_scr = rest[:-1], rest[-1]

    @pl.when(pl.program_id(1) == 0)
    def _():
        h_scr[...] = _modulated(x_ref, ng_ref, sh_ref, sc_ref).astype(h_scr.dtype)

    tm = h_scr.shape[0]
    panel = min(PROJ_PANEL, tm)
    n_chunks = w_ref.shape[1] // HEAD_DIM
    for p0 in range(0, tm, panel):
        rows = slice(p0, p0 + panel)
        z = _mxu(h_scr[rows, :], w_ref[...])
        if mode == "sigmoid":
            out_refs[0][rows, :] = jax.nn.sigmoid(z)
            continue
        cos, sin = cos_ref[rows, :], sin_ref[rows, :]
        for c in range(n_chunks):
            zc = z[:, c * HEAD_DIM:(c + 1) * HEAD_DIM]
            if mode == "q" or c % 2 == 0:
                zc = _rope_chunk(zc, cos, sin)
            for k, o in enumerate(out_refs):
                if mode == "kv" and k == 0:
                    o[pl.ds(p0 * n_chunks + c, panel, stride=n_chunks), :] = zc
                else:
                    o[rows, c * HEAD_DIM:(c + 1) * HEAD_DIM] = zc.astype(o.dtype)


def _proj(x, norm_tab, mod_tab, w, cos, sin, *, norm_slot, mod_slot, layer, mode, col0, n, out_dtypes, rows_per_mod,
          rows_per_seq, tm, tn):
    m, d = x.shape
    seq_tiles = rows_per_seq // tm
    assert col0 % tn == 0 and n % tn == 0 and rows_per_seq % tm == 0
    if mode == "kv":
        assert tn == KV_WIDTH and out_dtypes[0] == F32
        n_chunks = KV_WIDTH // HEAD_DIM
        out_specs = [pl.BlockSpec((None, tm * n_chunks, HEAD_DIM), lambda i, j: (j, i, 0))] + [
            pl.BlockSpec((None, tm, tn), lambda i, j: (j, i, 0)) for _ in out_dtypes[1:]]
        out_shape = [jax.ShapeDtypeStruct((n // KV_WIDTH, m * n_chunks, HEAD_DIM), F32)] + [
            jax.ShapeDtypeStruct((n // KV_WIDTH, m, KV_WIDTH), dt) for dt in out_dtypes[1:]]
    else:
        out_specs = [pl.BlockSpec((tm, tn), lambda i, j: (i, j)) for _ in out_dtypes]
        out_shape = [jax.ShapeDtypeStruct((m, n), dt) for dt in out_dtypes]
    return pl.pallas_call(
        functools.partial(_proj_kernel, mode=mode),
        grid=(m // tm, n // tn),
        in_specs=[pl.BlockSpec((tm, d), lambda i, j: (i, 0)),
                  _gain_spec(d, norm_slot),
                  _mod_spec(tm, rows_per_mod, d, mod_slot), _mod_spec(tm, rows_per_mod, d, mod_slot + 1),
                  pl.BlockSpec((None, d, tn), lambda i, j: (layer, 0, col0 // tn + j)),
                  pl.BlockSpec((tm, HEAD_DIM), lambda i, j: (i % seq_tiles, 0)),
                  pl.BlockSpec((tm, HEAD_DIM), lambda i, j: (i % seq_tiles, 0))],
        out_specs=out_specs,
        out_shape=out_shape,
        scratch_shapes=[pltpu.VMEM((tm, d), MXU_DTYPE)],
        compiler_params=_params("arbitrary", "arbitrary"),
        name="proj_" + mode,
    )(x, norm_tab, mod_tab, mod_tab, w, cos, sin)


def _glu_kernel(x_ref, ng_ref, sh_ref, sc_ref, wa_ref, wb_ref, o_ref, h_scr):
    @pl.when(pl.program_id(1) == 0)
    def _():
        h_scr[...] = _modulated(x_ref, ng_ref, sh_ref, sc_ref).astype(h_scr.dtype)

    panel = min(PROJ_PANEL, h_scr.shape[0])
    for p0 in range(0, h_scr.shape[0], panel):
        h = h_scr[p0:p0 + panel, :]
        o_ref[p0:p0 + panel, :] = _mxu(h, wa_ref[...]) * jax.nn.sigmoid(_mxu(h, wb_ref[...]))


def _glu_proj(x, norm_tab, mod_tab, w_pw1, *, norm_slot, mod_slot, layer, rows_per_mod, tm, tn):
    m, d = x.shape
    dc = w_pw1.shape[-1] // 2
    return pl.pallas_call(
        _glu_kernel,
        grid=(m // tm, dc // tn),
        in_specs=[pl.BlockSpec((tm, d), lambda i, j: (i, 0)),
                  _gain_spec(d, norm_slot),
                  _mod_spec(tm, rows_per_mod, d, mod_slot), _mod_spec(tm, rows_per_mod, d, mod_slot + 1),
                  pl.BlockSpec((None, d, tn), lambda i, j: (layer, 0, j)),
                  pl.BlockSpec((None, d, tn), lambda i, j: (layer, 0, dc // tn + j))],
        out_specs=pl.BlockSpec((tm, tn), lambda i, j: (i, j)),
        out_shape=jax.ShapeDtypeStruct((m, dc), F32),
        scratch_shapes=[pltpu.VMEM((tm, d), MXU_DTYPE)],
        compiler_params=_params("arbitrary", "arbitrary"),
        name="glu_proj",
    )(x, norm_tab, mod_tab, mod_tab, w_pw1, w_pw1)


def _linres_kernel(a_ref, w_ref, x_ref, gt_ref, o_ref):
    tm = a_ref.shape[0]
    panel = min(PROJ_PANEL, tm)
    for p0 in range(0, tm, panel):
        rows = slice(p0, p0 + panel)
        gate = gt_ref[:, 0, :] if gt_ref.shape[0] == 1 else gt_ref[rows, 0, :]
        o_ref[rows, :] = x_ref[rows, :] + gate * _mxu(a_ref[rows, :], w_ref[...])


def _linres(a, w, x, mod_tab, *, gate_slot, layer, rows_per_mod, tm, tn):
    m, k = a.shape
    d = x.shape[1]
    gspec = _mod_spec(tm, rows_per_mod, tn, gate_slot, by_column=True)
    return pl.pallas_call(
        _linres_kernel,
        grid=(m // tm, d // tn),
        in_specs=[pl.BlockSpec((tm, k), lambda i, j: (i, 0)),
                  pl.BlockSpec((None, k, tn), lambda i, j: (layer, 0, j)),
                  pl.BlockSpec((tm, tn), lambda i, j: (i, j)),
                  gspec],
        out_specs=pl.BlockSpec((tm, tn), lambda i, j: (i, j)),
        out_shape=jax.ShapeDtypeStruct((m, d), F32),
        compiler_params=_params("arbitrary", "arbitrary"),
        name="linres",
    )(a, w, x, mod_tab)


def _compress_kernel(x_ref, pe_ref, w1_ref, w2_ref, o_ref):
    n_chunks = KV_WIDTH // HEAD_DIM
    nb = x_ref.shape[0] // (CMP_BLOCK * n_chunks)
    for kv in range(2):
        acc = jnp.zeros((N_KV_HEADS * nb, HEAD_DIM), F32)
        for c in range(CMP_BLOCK):
            xc = jnp.concatenate([x_ref[pl.ds(c * n_chunks + 2 * g + kv, nb, stride=CMP_BLOCK * n_chunks), :]
                                  for g in range(N_KV_HEADS)], axis=0) + pe_ref[kv, c:c + 1, :]
            acc += _mxu(xc, w1_ref[kv, c * HEAD_DIM:(c + 1) * HEAD_DIM, :])
        y = _mxu(_silu(acc), w2_ref[kv])
        for g in range(N_KV_HEADS):
            o_ref[:, (2 * g + kv) * HEAD_DIM:(2 * g + kv + 1) * HEAD_DIM] = y[g * nb:(g + 1) * nb].astype(o_ref.dtype)


def _compress_rows(rows, pe, w1, w2, *, layer, n_seq):
    n_chunks = KV_WIDTH // HEAD_DIM
    n_blocks = rows.shape[0] // (CMP_BLOCK * n_chunks)
    nb = n_blocks // n_seq
    return pl.pallas_call(
        _compress_kernel,
        grid=(n_seq,),
        in_specs=[pl.BlockSpec((nb * CMP_BLOCK * n_chunks, HEAD_DIM), lambda i: (i, 0)),
                  pl.BlockSpec((None, 2, CMP_BLOCK, HEAD_DIM), lambda i: (layer, 0, 0, 0)),
                  pl.BlockSpec((None, 2, CMP_BLOCK * HEAD_DIM, HEAD_DIM), lambda i: (layer, 0, 0, 0)),
                  pl.BlockSpec((None, 2, HEAD_DIM, HEAD_DIM), lambda i: (layer, 0, 0, 0))],
        out_specs=pl.BlockSpec((nb, KV_WIDTH), lambda i: (i, 0)),
        out_shape=jax.ShapeDtypeStruct((n_blocks, KV_WIDTH), MXU_DTYPE),
        compiler_params=_params("arbitrary"),
        name="compress_rows",
    )(rows, pe, w1, w2)


def _compress_pages_kernel(pt_ref, *refs, blocks_per_page):
    del pt_ref
    pages = refs[:PAGES_PER_STEP]
    pe_ref, w1_ref, w2_ref, o_ref, t_scr = refs[PAGES_PER_STEP:]
    nb = PAGES_PER_STEP * blocks_per_page
    pages_per_group = SUBLANES // blocks_per_page
    n_groups = PAGES_PER_STEP // pages_per_group
    rows = SUBLANES * CMP_BLOCK
    out_row = lax.broadcasted_iota(jnp.int32, (rows, rows), 0)
    in_row = lax.broadcasted_iota(jnp.int32, (rows, rows), 1)
    perm = jnp.where(in_row == (out_row % SUBLANES) * CMP_BLOCK + out_row // SUBLANES, 1.0, 0.0)
    pe_rows = jnp.concatenate([pe_ref[kv] for _ in range(N_KV_HEADS) for kv in range(2)], axis=1)
    pe_rows = jnp.concatenate([pe_rows] * SUBLANES, axis=0)
    n_chunks = KV_WIDTH // HEAD_DIM
    page_rows = pages[0].shape[0] // n_chunks
    for gp in range(n_groups):
        x = jnp.concatenate(
            [jnp.concatenate([pages[gp * pages_per_group + k][pl.ds(ch, page_rows, stride=n_chunks), :]
                              for ch in range(n_chunks)], axis=1)
             for k in range(pages_per_group)], axis=0)
        t_scr[gp] = _mxu(perm, x + pe_rows)
    def block_rows(c, kv):
        return jnp.concatenate(
            [t_scr[:, c * SUBLANES:(c + 1) * SUBLANES,
                   (g * 2 + kv) * HEAD_DIM:(g * 2 + kv + 1) * HEAD_DIM].reshape(nb, HEAD_DIM)
             for g in range(N_KV_HEADS)], axis=0)

    for kv in range(2):
        acc = jnp.zeros((N_KV_HEADS * nb, HEAD_DIM), F32)
        for c in range(0, CMP_BLOCK, 2):
            xc = jnp.concatenate([block_rows(c, kv), block_rows(c + 1, kv)], axis=1)
            acc += _mxu(xc, w1_ref[kv, c * HEAD_DIM:(c + 2) * HEAD_DIM, :])
        y = _mxu(_silu(acc), w2_ref[kv])
        for g in range(N_KV_HEADS):
            o_ref[:, (g * 2 + kv) * HEAD_DIM:(g * 2 + kv + 1) * HEAD_DIM] = (
                y[g * nb:(g + 1) * nb].astype(o_ref.dtype))


def _compress_pages(pool, page_table, pe, w1, w2, *, layer):
    width = KV_WIDTH
    n_chunks = width // HEAD_DIM
    page = pool.shape[2] // n_chunks
    b, n_pages = page_table.shape
    bpp = page // CMP_BLOCK
    nb = PAGES_PER_STEP * bpp
    assert n_pages % PAGES_PER_STEP == 0 and page % CMP_BLOCK == 0 and SUBLANES % bpp == 0
    n_groups = nb // SUBLANES

    def page_spec(p):
        return pl.BlockSpec((None, None, page * n_chunks, HEAD_DIM),
                            lambda bi, gi, pt: (layer, pt[bi * n_pages + gi * PAGES_PER_STEP + p], 0, 0))

    grid_spec = pltpu.PrefetchScalarGridSpec(
        num_scalar_prefetch=1,
        grid=(b, n_pages // PAGES_PER_STEP),
        in_specs=[page_spec(p) for p in range(PAGES_PER_STEP)] + [
            pl.BlockSpec((None, 2, CMP_BLOCK, HEAD_DIM), lambda bi, gi, pt: (layer, 0, 0, 0)),
            pl.BlockSpec((None, 2, CMP_BLOCK * HEAD_DIM, HEAD_DIM), lambda bi, gi, pt: (layer, 0, 0, 0)),
            pl.BlockSpec((None, 2, HEAD_DIM, HEAD_DIM), lambda bi, gi, pt: (layer, 0, 0, 0))],
        out_specs=pl.BlockSpec((None, nb, width), lambda bi, gi, pt: (bi, gi, 0)),
        scratch_shapes=[pltpu.VMEM((n_groups, SUBLANES * CMP_BLOCK, width), F32)],
    )
    return pl.pallas_call(
        functools.partial(_compress_pages_kernel, blocks_per_page=bpp),
        grid_spec=grid_spec,
        out_shape=jax.ShapeDtypeStruct((b, n_pages * bpp, width), MXU_DTYPE),
        compiler_params=_params("arbitrary", "arbitrary"),
        name="compress_pages",
    )(page_table.reshape(-1), *([pool] * PAGES_PER_STEP), pe, w1, w2)


def _pair_sums(imp):
    out = []
    for k in range(imp.shape[1] // LANES):
        x = imp[:, k * LANES:(k + 1) * LANES]
        even = lax.broadcasted_iota(jnp.int32, x.shape, 1) % 2 == 0
        out.append(x + jnp.where(even, pltpu.roll(x, LANES - 1, 1), pltpu.roll(x, 1, 1)))
    return out[0] if len(out) == 1 else jnp.concatenate(out, axis=1)


def _block_scores(imp, q_pos, n_sel_blocks):
    lane = lax.broadcasted_iota(jnp.int32, imp.shape, 1)
    blk = lane // SEL_RATIO
    cur = q_pos // SEL_BLOCK
    forced = (blk == 0) | (blk == cur) | (blk == cur - 1)
    valid = blk * SEL_BLOCK <= q_pos
    score = jnp.where(forced, FORCED_SCORE, jnp.where(valid, _pair_sums(imp), -1.0))
    eligible = (lane % SEL_RATIO == 0) & (blk < n_sel_blocks)
    return jnp.where(eligible, score, KNOCKED_OUT)


def _take_top(work):
    lane = lax.broadcasted_iota(jnp.int32, work.shape, 1).astype(F32)
    top = jnp.max(work, axis=-1, keepdims=True)
    pick = jnp.min(jnp.where(work == top, lane, float(work.shape[1])), axis=-1, keepdims=True)
    return lane == pick, pick


def _with_ones(v):
    return jnp.concatenate([v, jnp.ones(v.shape, v.dtype)], axis=1)


def _attn_prompt_kernel(q_ref, ck_ref, cv_ref, ks_ref, vs_ref, kw_ref, vw_ref, g_ref, o_ref, *, tq, tk, seq):
    i = pl.program_id(2)
    n_cmp = ck_ref.shape[0]
    n_sel_blocks = -(-seq // SEL_BLOCK)
    n_take = min(N_SELECT, n_sel_blocks)
    assert SEL_RATIO == 2 and n_cmp * CMP_BLOCK == seq and n_cmp <= LANES
    q = q_ref[...]
    q4 = jnp.concatenate([q[:, h * HEAD_DIM:(h + 1) * HEAD_DIM] for h in range(GROUP_SIZE)], axis=0)
    q_pos = i * tq + lax.broadcasted_iota(jnp.int32, (tq, 1), 0)

    pad = jnp.zeros((LANES - n_cmp, HEAD_DIM), ck_ref.dtype)
    ck = jnp.concatenate([ck_ref[...], pad], axis=0) if n_cmp < LANES else ck_ref[...]
    cv = jnp.concatenate([cv_ref[...], pad], axis=0) if n_cmp < LANES else cv_ref[...]
    lane = lax.broadcasted_iota(jnp.int32, (1, LANES), 1)
    vis = ((lane + 1) * CMP_BLOCK - 1 <= q_pos) & (lane < n_cmp)
    s = _mxu_nt(q4, ck).reshape(GROUP_SIZE, tq, LANES)
    p = jnp.where(vis[None], _softmax_scaled(jnp.where(vis[None], s, NEG_INF)), 0.0)
    o_c = _mxu(p.reshape(GROUP_SIZE * tq, LANES), cv).reshape(GROUP_SIZE, tq, HEAD_DIM)

    def top_blocks():
        q_pos_t = i * tq + lax.broadcasted_iota(jnp.int32, (1, tq), 1)
        row = lax.broadcasted_iota(jnp.int32, (n_cmp, 1), 0)
        vis_t = (row + 1) * CMP_BLOCK - 1 <= q_pos_t
        imp_t = jnp.zeros((n_cmp, tq), F32)
        for h in range(GROUP_SIZE):
            s_t = jnp.where(vis_t, _mxu_nt(ck_ref[...], q[:, h * HEAD_DIM:(h + 1) * HEAD_DIM]), NEG_INF)
            e = jnp.exp2((s_t - jnp.max(s_t, axis=0, keepdims=True)) * EXP2_SCALE)
            imp_t = imp_t + jnp.where(vis_t, e / jnp.sum(e, axis=0, keepdims=True), 0.0)
        pair = imp_t + pltpu.roll(imp_t, n_cmp - 1, 0)
        blk = row // SEL_RATIO
        cur = q_pos_t // SEL_BLOCK
        forced = (blk == 0) | (blk == cur) | (blk == cur - 1)
        score = jnp.where(forced, FORCED_SCORE, jnp.where(blk * SEL_BLOCK <= q_pos_t, pair, -1.0))
        work = jnp.where((row % SEL_RATIO == 0) & (blk < n_sel_blocks), score, KNOCKED_OUT)
        row_f = row.astype(F32)
        sel_t = jnp.zeros((n_cmp, tq), F32)
        for _ in range(n_take):
            top = jnp.max(work, axis=0, keepdims=True)
            pick = jnp.min(jnp.where(work == top, row_f, float(n_cmp)), axis=0, keepdims=True)
            hit = row_f == pick
            work = jnp.where(hit, KNOCKED_OUT, work)
            sel_t = jnp.where(hit, 1.0, sel_t)
        if n_cmp < LANES:
            sel_t = jnp.concatenate([sel_t, jnp.zeros((LANES - n_cmp, tq), F32)], axis=0)
        return sel_t.T

    def all_blocks():
        return jnp.where((lane % SEL_RATIO == 0) & (lane // SEL_RATIO < n_sel_blocks), 1.0,
                         jnp.zeros((tq, LANES), F32))

    sel = lax.cond(((i + 1) * tq - 1) // SEL_BLOCK + 1 <= n_take, all_blocks, top_blocks).astype(MXU_DTYPE)

    def sel_chunk(kc, carry):
        m_i, l_i, acc = carry
        start = pl.multiple_of(kc * tk, tk)
        k_pos = start + lax.broadcasted_iota(jnp.int32, (1, tk), 1)
        expand = lax.broadcasted_iota(jnp.int32, (LANES, 1), 0) == SEL_RATIO * (k_pos // SEL_BLOCK)
        picked = jnp.dot(sel, jnp.where(expand, 1.0, 0.0).astype(MXU_DTYPE), preferred_element_type=F32)
        bias = jnp.where((picked > 0.5) & (k_pos <= q_pos), 0.0, NEG_INF)[None]
        sc = _mxu_nt(q4, ks_ref[pl.ds(start, tk), :]).reshape(GROUP_SIZE, tq, tk) + bias
        m_new = jnp.maximum(m_i, jnp.max(sc, axis=-1, keepdims=True))
        alpha = jnp.exp2((m_i - m_new) * EXP2_SCALE)
        e = jnp.exp2((sc - m_new) * EXP2_SCALE)
        l_new = alpha * l_i + jnp.sum(e, axis=-1, keepdims=True)
        pv = _mxu(e.reshape(GROUP_SIZE * tq, tk), vs_ref[pl.ds(start, tk), :])
        return m_new, l_new, alpha * acc + pv.reshape(GROUP_SIZE, tq, HEAD_DIM)

    init = (jnp.full((GROUP_SIZE, tq, 1), NEG_INF, F32), jnp.zeros((GROUP_SIZE, tq, 1), F32),
            jnp.zeros((GROUP_SIZE, tq, HEAD_DIM), F32))
    _, l_s, acc_s = lax.fori_loop(0, ((i + 1) * tq + tk - 1) // tk, sel_chunk, init)
    o_s = acc_s / l_s

    span = tq + WINDOW
    w_start = pl.multiple_of(jnp.maximum(i * tq - WINDOW, 0), tq)
    dist = q_pos - (w_start + lax.broadcasted_iota(jnp.int32, (1, span), 1))
    bias = jnp.where((dist >= 0) & (dist < WINDOW), 0.0, NEG_INF)[None]
    sw = _mxu_nt(q4, kw_ref[pl.ds(w_start, span), :]).reshape(GROUP_SIZE, tq, span) + bias
    ew = jnp.exp2((sw - jnp.max(sw, axis=-1, keepdims=True)) * EXP2_SCALE)
    pv = _mxu(ew.reshape(GROUP_SIZE * tq, span), _with_ones(vw_ref[pl.ds(w_start, span), :]))
    o_w = (pv[:, :HEAD_DIM] / pv[:, HEAD_DIM:]).reshape(GROUP_SIZE, tq, HEAD_DIM)

    gates = g_ref[...]
    for h in range(GROUP_SIZE):
        o = (gates[:, 3 * h:3 * h + 1] * o_c[h] + gates[:, 3 * h + 1:3 * h + 2] * o_s[h]
             + gates[:, 3 * h + 2:3 * h + 3] * o_w[h])
        o_ref[:, h * HEAD_DIM:(h + 1) * HEAD_DIM] = o.astype(o_ref.dtype)


def _attn_prompt(q, ckv, kv, gates, *, tq, tk):
    b, t, _ = q.shape
    n_cmp = ckv.shape[1]
    assert t % tq == 0 and t % tk == 0 and t >= tq + WINDOW and WINDOW % tq == 0 and tq % ROW_BLOCK == 0
    gw = GROUP_SIZE * HEAD_DIM

    def kv_spec(branch, part):
        return pl.BlockSpec((None, None, t, HEAD_DIM), lambda bi, g, i: (branch, bi, 0, 2 * g + part))

    return pl.pallas_call(
        functools.partial(_attn_prompt_kernel, tq=tq, tk=tk, seq=t),
        grid=(b, N_KV_HEADS, t // tq),
        in_specs=[pl.BlockSpec((None, tq, gw), lambda bi, g, i: (bi, i, g)),
                  pl.BlockSpec((None, n_cmp, HEAD_DIM), lambda bi, g, i: (bi, 0, 2 * g)),
                  pl.BlockSpec((None, n_cmp, HEAD_DIM), lambda bi, g, i: (bi, 0, 2 * g + 1)),
                  kv_spec(1, 0), kv_spec(1, 1), kv_spec(2, 0), kv_spec(2, 1),
                  pl.BlockSpec((None, tq, LANES), lambda bi, g, i: (bi, i, g))],
        out_specs=pl.BlockSpec((None, tq, gw), lambda bi, g, i: (bi, i, g)),
        out_shape=jax.ShapeDtypeStruct((b, t, Q_WIDTH), MXU_DTYPE),
        compiler_params=_params("arbitrary", "arbitrary", "arbitrary"),
        name="attn_prompt",
    )(q, ckv, ckv, kv, kv, kv, kv, gates)


def _attn_cmp_sample_kernel(q_ref, ckv_ref, o_ref, idx_ref, *, past, n_total, width):
    t_rows = q_ref.shape[0]
    n_cmp = ckv_ref.shape[0]
    n_sel_blocks = -(-n_total // SEL_BLOCK)
    n_take = min(N_SELECT, n_sel_blocks)
    assert SEL_RATIO == 2 and n_take <= LANES and SEL_RATIO * n_sel_blocks <= width
    q_pos = past + lax.broadcasted_iota(jnp.int32, (t_rows, 1), 0)
    lane = lax.broadcasted_iota(jnp.int32, (1, width), 1)
    vis = ((lane + 1) * CMP_BLOCK - 1 <= q_pos) & (lane < n_cmp)
    out_lane = lax.broadcasted_iota(jnp.int32, (1, LANES), 1)
    pad = jnp.zeros((width - n_cmp, HEAD_DIM), ckv_ref.dtype)
    imps = []
    for g in range(N_KV_HEADS):
        ck = jnp.concatenate([ckv_ref[:, 2 * g * HEAD_DIM:(2 * g + 1) * HEAD_DIM], pad], axis=0)
        cv = jnp.concatenate([ckv_ref[:, (2 * g + 1) * HEAD_DIM:(2 * g + 2) * HEAD_DIM], pad], axis=0)
        imp = jnp.zeros((t_rows, width), F32)
        for h in range(GROUP_SIZE):
            cols = slice((g * GROUP_SIZE + h) * HEAD_DIM, (g * GROUP_SIZE + h + 1) * HEAD_DIM)
            s = _mxu_nt(q_ref[:, cols], ck) * ATTN_SCALE
            p = jnp.where(vis, _softmax_rows(jnp.where(vis, s, NEG_INF)), 0.0)
            o_ref[:, cols] = _mxu(p, cv)
            imp = imp + p
        imps.append(imp)
    work = _block_scores(jnp.concatenate(imps, axis=0), jnp.concatenate([q_pos] * N_KV_HEADS, axis=0), n_sel_blocks)
    picks = jnp.zeros((N_KV_HEADS * t_rows, LANES), F32)
    for r in range(n_take):
        hit, pick = _take_top(work)
        work = jnp.where(hit, KNOCKED_OUT, work)
        picks = jnp.where(out_lane == r, pick, picks)
    idx = (picks * (1.0 / SEL_RATIO)).astype(jnp.int32)
    for g in range(N_KV_HEADS):
        idx_ref[g] = idx[g * t_rows:(g + 1) * t_rows]


def _attn_cmp_sample(q, ckv, *, past, n_total):
    b, t, qw = q.shape
    n_cmp = ckv.shape[1]
    width = -(-max(n_cmp, SEL_RATIO * -(-n_total // SEL_BLOCK)) // LANES) * LANES
    return pl.pallas_call(
        functools.partial(_attn_cmp_sample_kernel, past=past, n_total=n_total, width=width),
        grid=(b,),
        in_specs=[pl.BlockSpec((None, t, qw), lambda bi: (bi, 0, 0)),
                  pl.BlockSpec((None, n_cmp, KV_WIDTH), lambda bi: (bi, 0, 0))],
        out_specs=[pl.BlockSpec((None, t, qw), lambda bi: (bi, 0, 0)),
                   pl.BlockSpec((None, N_KV_HEADS, t, LANES), lambda bi: (bi, 0, 0, 0))],
        out_shape=[jax.ShapeDtypeStruct(q.shape, F32),
                   jax.ShapeDtypeStruct((b, N_KV_HEADS, t, LANES), jnp.int32)],
        compiler_params=_params("arbitrary"),
        name="attn_cmp_sample",
    )(q, ckv)


def _attn_win_sample_kernel(q_ref, buf_ref, new_ref, o_ref, *, t_new):
    t_rows = q_ref.shape[0]
    n_chunks = KV_WIDTH // HEAD_DIM
    n_buf = buf_ref.shape[0] // n_chunks
    t_idx = lax.broadcasted_iota(jnp.int32, (GROUP_SIZE * t_rows, 1), 0) % t_rows
    dist_a = t_idx + n_buf - lax.broadcasted_iota(jnp.int32, (1, n_buf), 1)
    row_b = lax.broadcasted_iota(jnp.int32, (1, new_ref.shape[0]), 1)
    mask_a = (dist_a >= 0) & (dist_a < WINDOW)
    mask_b = (row_b <= t_idx) & (t_idx - row_b < WINDOW) & (row_b < t_new)
    for g in range(N_KV_HEADS):
        k_a = buf_ref[pl.ds(2 * g, n_buf, stride=n_chunks), :]
        v_a = buf_ref[pl.ds(2 * g + 1, n_buf, stride=n_chunks), :]
        k_b = new_ref[:, 2 * g * HEAD_DIM:(2 * g + 1) * HEAD_DIM]
        v_b = new_ref[:, (2 * g + 1) * HEAD_DIM:(2 * g + 2) * HEAD_DIM]
        head_cols = [slice((g * GROUP_SIZE + h) * HEAD_DIM, (g * GROUP_SIZE + h + 1) * HEAD_DIM)
                     for h in range(GROUP_SIZE)]
        q = jnp.concatenate([q_ref[:, cols] for cols in head_cols], axis=0)
        s_a = jnp.where(mask_a, _mxu_nt(q, k_a) * ATTN_SCALE, NEG_INF)
        s_b = jnp.where(mask_b, _mxu_nt(q, k_b) * ATTN_SCALE, NEG_INF)
        m = jnp.maximum(jnp.max(s_a, axis=-1, keepdims=True), jnp.max(s_b, axis=-1, keepdims=True))
        e_a, e_b = jnp.exp(s_a - m), jnp.exp(s_b - m)
        l = jnp.sum(e_a, axis=-1, keepdims=True) + jnp.sum(e_b, axis=-1, keepdims=True)
        o = _mxu(e_a / l, v_a) + _mxu(e_b / l, v_b)
        for h, cols in enumerate(head_cols):
            o_ref[:, cols] = o[h * t_rows:(h + 1) * t_rows]


def _attn_win_sample(q, win_buf, kv_new, *, layer, t_new):
    b, t, qw = q.shape
    buf_rows = win_buf.shape[2]
    return pl.pallas_call(
        functools.partial(_attn_win_sample_kernel, t_new=t_new),
        grid=(b,),
        in_specs=[pl.BlockSpec((None, t, qw), lambda bi: (bi, 0, 0)),
                  pl.BlockSpec((None, None, buf_rows, HEAD_DIM), lambda bi: (layer, bi, 0, 0)),
                  pl.BlockSpec((None, kv_new.shape[1], KV_WIDTH), lambda bi: (bi, 0, 0))],
        out_specs=pl.BlockSpec((None, t, qw), lambda bi: (bi, 0, 0)),
        out_shape=jax.ShapeDtypeStruct(q.shape, F32),
        compiler_params=_params("arbitrary"),
        name="attn_win_sample",
    )(q, win_buf, kv_new)


def _attn_sel_sample_kernel(idx_ref, pt_ref, q_ref, *refs, past, t_new, n_slots, t_steps):
    del pt_ref
    blocks = refs[:n_slots]
    new_ref, oc_ref, ow_ref, g_ref, o_ref = refs[n_slots:]
    bi, t, gi = pl.program_id(0), pl.program_id(1), pl.program_id(2)
    base = ((bi * t_steps + t) * N_KV_HEADS + gi) * n_slots
    n_past_blocks = past // SEL_BLOCK
    q_pos = past + t
    q = q_ref[...]
    n_chunks = KV_WIDTH // HEAD_DIM
    lane = lax.broadcasted_iota(jnp.int32, (1, n_slots * SEL_BLOCK), 1)
    pos = lane % SEL_BLOCK
    has_new = jnp.int32(0)
    for n in range(n_slots):
        blk = idx_ref[base + n]
        start = jnp.where(blk < n_past_blocks, blk * SEL_BLOCK, q_pos + 1)
        pos = pos + jnp.where(lane // SEL_BLOCK == n, start, 0)
        has_new = has_new | (blk == n_past_blocks).astype(jnp.int32)
    mask_a = pos <= q_pos
    new = new_ref[...]
    row_b = lax.broadcasted_iota(jnp.int32, (1, new.shape[0]), 1)
    mask_b = (row_b <= t) & (row_b < t_new) & (has_new > 0)
    s_b = jnp.where(mask_b, _mxu_nt(q, new[:, :HEAD_DIM]) * ATTN_SCALE, NEG_INF)
    gates = g_ref[...]

    for g in range(N_KV_HEADS):
        @pl.when(gi == g)
        def _(g=g):
            keys = jnp.concatenate([blk[pl.ds(2 * g, SEL_BLOCK, stride=n_chunks), :] for blk in blocks], axis=0)
            vals = jnp.concatenate([blk[pl.ds(2 * g + 1, SEL_BLOCK, stride=n_chunks), :] for blk in blocks], axis=0)
            s_a = jnp.where(mask_a, _mxu_nt(q, keys) * ATTN_SCALE, NEG_INF)
            m = jnp.maximum(jnp.max(s_a, axis=-1, keepdims=True), jnp.max(s_b, axis=-1, keepdims=True))
            e_a, e_b = jnp.exp(s_a - m), jnp.exp(s_b - m)
            l = jnp.sum(e_a, axis=-1, keepdims=True) + jnp.sum(e_b, axis=-1, keepdims=True)
            o_s = _mxu(e_a / l, vals) + _mxu(e_b / l, new[:, HEAD_DIM:])
            o_ref[...] = gates[:, 0:1] * oc_ref[...] + gates[:, 1:2] * o_s + gates[:, 2:3] * ow_ref[...]


def _attn_sel_sample(idx, page_table, q, pool, kv_new, o_c, o_w, gates, *, layer, past, t_new):
    b, t, g, hg, hd = q.shape
    n_slots = idx.shape[-1]
    n_pages = page_table.shape[1]
    page = past // n_pages
    per_page = page // SEL_BLOCK
    n_past_blocks = past // SEL_BLOCK

    def slot_spec(n):
        def index(bi, ti, gi, idx_s, pt_s):
            blk = jnp.minimum(idx_s[((bi * t + ti) * g + gi) * n_slots + n], n_past_blocks - 1)
            phys = pt_s[bi * n_pages + blk // per_page]
            return (layer, phys, blk % per_page, 0)
        return pl.BlockSpec((None, None, SEL_BLOCK * (KV_WIDTH // HEAD_DIM), HEAD_DIM), index)

    head_spec = pl.BlockSpec((None, None, None, hg, hd), lambda bi, ti, gi, idx_s, pt_s: (bi, ti, gi, 0, 0))
    grid_spec = pltpu.PrefetchScalarGridSpec(
        num_scalar_prefetch=2,
        grid=(b, t, g),
        in_specs=[head_spec] + [slot_spec(n) for n in range(n_slots)] + [
            pl.BlockSpec((None, kv_new.shape[1], 2 * HEAD_DIM), lambda bi, ti, gi, idx_s, pt_s: (bi, 0, gi)),
            head_spec, head_spec,
            pl.BlockSpec((None, None, None, hg, 3), lambda bi, ti, gi, idx_s, pt_s: (bi, ti, gi, 0, 0))],
        out_specs=head_spec,
    )
    return pl.pallas_call(
        functools.partial(_attn_sel_sample_kernel, past=past, t_new=t_new, n_slots=n_slots, t_steps=t),
        grid_spec=grid_spec,
        out_shape=jax.ShapeDtypeStruct(q.shape, F32),
        compiler_params=_params("arbitrary", "arbitrary", "arbitrary"),
        name="attn_sel_sample",
    )(idx.reshape(-1), page_table.reshape(-1), q, *([pool] * n_slots), kv_new, o_c, o_w, gates)


def _dwconv_kernel(cur_ref, prev_ref, buf_ref, w_ref, b_ref, lg_ref, lb_ref, o_ref, full_scr, y_scr, *,
                   tt, halo, width):
    i = pl.program_id(1)
    lead = halo - (width - 1)
    full_scr[0:halo, :] = prev_ref[...]

    @pl.when(i == 0)
    def _():
        full_scr[0:halo, :] = buf_ref[...]

    full_scr[halo:halo + tt, :] = cur_ref[...]
    for c0 in range(0, cur_ref.shape[1], DWCONV_COLS):
        cols = slice(c0, c0 + DWCONV_COLS)
        acc = jnp.zeros((tt, DWCONV_COLS), F32) + b_ref[:, cols]
        for r in range(SUBLANES):
            offsets = [o for o in range(r, lead + width, SUBLANES) if o >= lead]
            rows = tt + (SUBLANES if r else 0)
            group = None
            for o in offsets:
                term = full_scr[o - r:o - r + rows, cols] * w_ref[o - lead:o - lead + 1, cols]
                group = term if group is None else group + term
            if group is not None:
                acc = acc + group[r:r + tt]
        y_scr[:, cols] = acc
    y = y_scr[...]
    mu = jnp.mean(y, axis=-1, keepdims=True)
    var = jnp.mean(jnp.square(y - mu), axis=-1, keepdims=True)
    yn = (y - mu) * lax.rsqrt(var + LN_EPS) * lg_ref[...] + lb_ref[...]
    o_ref[...] = _silu(yn).astype(o_ref.dtype)


def _dwconv(glu, buf, w_dw, b_dw, ln_g, ln_b, *, layer, tt):
    b, t, d = glu.shape
    width = w_dw.shape[1]
    halo = buf.shape[1]
    assert tt % halo == 0 and t % tt == 0 and halo >= width - 1
    per = tt // halo

    def vec_spec():
        return pl.BlockSpec((None, 1, d), lambda bi, i: (layer, 0, 0))

    return pl.pallas_call(
        functools.partial(_dwconv_kernel, tt=tt, halo=halo, width=width),
        grid=(b, t // tt),
        in_specs=[pl.BlockSpec((None, tt, d), lambda bi, i: (bi, i, 0)),
                  pl.BlockSpec((None, halo, d), lambda bi, i: (bi, jnp.maximum(i * per - 1, 0), 0)),
                  pl.BlockSpec((None, halo, d), lambda bi, i: (bi, 0, 0)),
                  pl.BlockSpec((None, width, d), lambda bi, i: (layer, 0, 0)),
                  vec_spec(), vec_spec(), vec_spec()],
        out_specs=pl.BlockSpec((None, tt, d), lambda bi, i: (bi, i, 0)),
        out_shape=jax.ShapeDtypeStruct((b, t, d), MXU_DTYPE),
        scratch_shapes=[pltpu.VMEM((halo + tt, d), F32), pltpu.VMEM((tt, d), F32)],
        compiler_params=_params("arbitrary", "arbitrary"),
        name="dwconv",
    )(glu, glu, buf, w_dw, b_dw, ln_g, ln_b)


def _rope_tables(pos):
    half = HEAD_DIM // 2
    inv = ROPE_THETA ** (-jnp.arange(half, dtype=F32) / half)
    ang = pos.astype(F32)[:, None] * inv[None, :]
    cos, sin = jnp.cos(ang), jnp.sin(ang)
    return jnp.concatenate([cos, cos], axis=1), jnp.concatenate([-sin, sin], axis=1)


def _group_gate_weights(w_gate_cols):
    d = w_gate_cols.shape[0]
    wg = w_gate_cols.reshape(d, N_KV_HEADS, GATE_COLS)
    return jnp.pad(wg, ((0, 0), (0, 0), (0, LANES - GATE_COLS))).reshape(d, N_KV_HEADS * LANES)


def _run_trunk(x, mod, pos, prm, caches, tiles):
    b, t, d = x.shape
    m = b * t
    depth = prm["w_mod"].shape[0]
    tm = tiles["tm"]
    rpm = t if caches is None else 1
    cos, sin = _rope_tables(pos)
    if caches is not None:
        cos, sin = jnp.tile(cos, (b, 1)), jnp.tile(sin, (b, 1))
    rows_per_seq = t if caches is None else m
    x = x.reshape(m, d)
    new_c, new_s, new_w, new_conv = [], [], [], []

    mod_tab = jnp.transpose(mod, (0, 2, 1, 3)).reshape(depth * N_MOD, b, d)
    if caches is not None:
        mod_tab = jnp.repeat(mod_tab, t, axis=1)
    mod_tab = mod_tab[:, :, None, :]
    norm_tab = prm["norm_g"].reshape(depth * 3, 1, d)

    def ffn(x, layer, which, k0):
        return _ffn(x, norm_tab, mod_tab, prm["final_norm_g"].reshape(1, d), prm["ffn_w_gate"], prm["ffn_w_up"],
                    prm["ffn_w_down"], norm_slot=layer * 3 + which * 2, mod_slot=layer * N_MOD + k0, layer=layer,
                    which=which, rows_per_mod=rpm, tm=tiles["tm_ffn"], tf=tiles["tf"],
                    final_norm=(layer == depth - 1 and which == 1))

    for i in range(depth):
        x = ffn(x, i, 0, 0)
        a = i // 2
        mixer = dict(norm_slot=i * 3 + 1, mod_slot=i * N_MOD + 3, rows_per_mod=rpm, tm=tm)
        if i % 2 == 0:
            w_in = prm["attn_w_in"]
            common = dict(rows_per_seq=rows_per_seq, **mixer)
            q_dtype = MXU_DTYPE if caches is None else F32
            q, = _proj(x, norm_tab, mod_tab, w_in, cos, sin, layer=a, mode="q", col0=0, n=Q_WIDTH,
                       out_dtypes=[q_dtype], tn=tiles["tn_out"], **common)
            kv_outs = [F32, MXU_DTYPE] if caches is None else [F32]
            kv = _proj(x, norm_tab, mod_tab, w_in, cos, sin, layer=a, mode="kv", col0=Q_WIDTH, n=3 * KV_WIDTH,
                       out_dtypes=kv_outs, tn=KV_WIDTH, **common)
            gates, = _proj(x, norm_tab, mod_tab, _group_gate_weights(prm["attn_gate_w"][a])[None], cos, sin, layer=0,
                           mode="sigmoid", col0=0, n=N_KV_HEADS * LANES, out_dtypes=[F32],
                           tn=N_KV_HEADS * LANES, **common)
            kv32 = kv[0]
            new_c.append(kv32[0].reshape(b, t, N_KV_HEADS, 2, HEAD_DIM))
            new_s.append(kv32[1].reshape(b, t, N_KV_HEADS, 2, HEAD_DIM))
            if caches is None:
                ckv = _compress_rows(kv32[0], prm["cmp_pe"], prm["cmp_w1"], prm["cmp_w2"], layer=a, n_seq=b)
                o = _attn_prompt(q.reshape(b, t, Q_WIDTH), ckv.reshape(b, t // CMP_BLOCK, KV_WIDTH),
                                 kv[1].reshape(3, b, t, KV_WIDTH), gates.reshape(b, t, N_KV_HEADS * LANES),
                                 tq=tiles["tq"], tk=tiles["tk"]).reshape(m, Q_WIDTH)
                new_w.append(kv32[2].reshape(b, t, N_KV_HEADS, 2, HEAD_DIM)[:, t - min(WINDOW, t):])
            else:
                pool_c, pool_s, win_buf, _, page_table = caches
                n_layers, n_pool, page = pool_c.shape[:3]
                past = page_table.shape[1] * page
                assert t < CMP_BLOCK and page % SEL_BLOCK == 0
                ckv = _compress_pages(pool_c.reshape(n_layers, n_pool, page * (KV_WIDTH // HEAD_DIM), HEAD_DIM),
                                      page_table, prm["cmp_pe"], prm["cmp_w1"], prm["cmp_w2"], layer=a)
                q3 = q.reshape(b, t, Q_WIDTH)
                o_c, idx = _attn_cmp_sample(q3, ckv, past=past, n_total=past + t)
                t_pad = -(-t // SUBLANES) * SUBLANES
                pad_rows = lambda r: jnp.pad(r.reshape(b, t, KV_WIDTH), ((0, 0), (0, t_pad - t), (0, 0)))
                n_buf = win_buf.shape[2]
                o_w = _attn_win_sample(q3, win_buf.reshape(n_layers, b, n_buf * (KV_WIDTH // HEAD_DIM), HEAD_DIM),
                                       pad_rows(kv32[2]), layer=a, t_new=t)
                n_take = min(N_SELECT, -(-(past + t) // SEL_BLOCK))
                idx = jnp.transpose(idx[..., :n_take], (0, 2, 1, 3))
                heads = lambda v: v.reshape(b, t, N_KV_HEADS, GROUP_SIZE, HEAD_DIM)
                g5 = gates.reshape(b, t, N_KV_HEADS, LANES)[..., :GATE_COLS].reshape(b, t, N_KV_HEADS, GROUP_SIZE, 3)
                o = _attn_sel_sample(idx, page_table, heads(q3),
                                     pool_s.reshape(n_layers, n_pool, page * (KV_WIDTH // HEAD_DIM), HEAD_DIM),
                                     pad_rows(kv32[1]), heads(o_c), heads(o_w), g5, layer=a, past=past,
                                     t_new=t).reshape(m, Q_WIDTH)
                new_w.append(kv32[2].reshape(b, t, N_KV_HEADS, 2, HEAD_DIM))
            x = _linres(o, prm["attn_w_out"], x, mod_tab, gate_slot=i * N_MOD + 5, layer=a, rows_per_mod=rpm, tm=tm,
                        tn=tiles["tn_out"])
        else:
            glu = _glu_proj(x, norm_tab, mod_tab, prm["conv_w_pw1"], layer=a, tn=tiles["tn_out"], **mixer)
            glu = glu.reshape(b, t, d)
            width = prm["conv_w_dw"].shape[1]
            halo = DWCONV_HALO
            if caches is None:
                buf = jnp.zeros((b, width - 1, d), F32)
                t_conv = t
                cur = glu
            else:
                buf = caches[3][a]
                t_conv = halo
                cur = jnp.pad(glu, ((0, 0), (0, t_conv - t), (0, 0)))
            if t >= width - 1:
                new_conv.append(glu[:, t - (width - 1):])
            else:
                new_conv.append(jnp.concatenate([buf[:, t:], glu], axis=1))
            buf = jnp.pad(buf, ((0, 0), (halo - (width - 1), 0), (0, 0)))
            vec = lambda v: v.reshape(v.shape[0], 1, d)
            act = _dwconv(cur, buf, prm["conv_w_dw"], vec(prm["conv_b_dw"]), vec(prm["conv_ln_g"]),
                          vec(prm["conv_ln_b"]), layer=a, tt=min(tiles["tt"], t_conv))
            act = act[:, :t].reshape(m, d)
            x = _linres(act, prm["conv_w_pw2"], x, mod_tab, gate_slot=i * N_MOD + 5, layer=a, rows_per_mod=rpm, tm=tm,
                        tn=tiles["tn_out"])
        x = ffn(x, i, 1, 6)
    y = x.reshape(b, t, d)
    new_w = jnp.stack(new_w)
    if caches is not None:
        keep = min(WINDOW, caches[2].shape[2] + t) - t
        new_w = jnp.concatenate([caches[2][:, :, caches[2].shape[2] - keep:], new_w], axis=2)
    return y, jnp.stack(new_c), jnp.stack(new_s), new_w, jnp.stack(new_conv)


PROMPT_TILES = dict(tm=1024, tm_ffn=1024, tf=512, tq=256, tk=512, tn_out=1024, tt=256)
SAMPLE_TILES = dict(tm=32, tm_ffn=32, tf=512, tn_out=1024, tt=32)


def kernel(x_prompt, x_sample, cache_cmp_kv, cache_sel_kv, cache_win_kv, state_conv, page_table, c_prompt, c_sample, w_mod, b_mod, norm_g, ffn_w_gate, ffn_w_up, ffn_w_down, attn_w_in, attn_w_out, cmp_pe, cmp_w1, cmp_w2, conv_w_pw1, conv_w_dw, conv_b_dw, conv_ln_g, conv_ln_b, conv_w_pw2, final_norm_g):
    mxu = lambda w: w.astype(MXU_DTYPE)
    prm = {"w_mod": w_mod, "norm_g": norm_g, "ffn_w_gate": mxu(ffn_w_gate), "ffn_w_up": mxu(ffn_w_up),
           "ffn_w_down": mxu(ffn_w_down), "attn_w_in": mxu(attn_w_in), "attn_gate_w": attn_w_in[..., -N_HEADS * 3:],
           "attn_w_out": mxu(attn_w_out), "cmp_pe": cmp_pe,
           "cmp_w1": cmp_w1, "cmp_w2": cmp_w2, "conv_w_pw1": mxu(conv_w_pw1), "conv_w_dw": conv_w_dw,
           "conv_b_dw": conv_b_dw, "conv_ln_g": conv_ln_g, "conv_ln_b": conv_ln_b, "conv_w_pw2": mxu(conv_w_pw2),
           "final_norm_g": final_norm_g}
    depth, d, _ = w_mod.shape
    bp, tp = x_prompt.shape[:2]
    bs, ts = x_sample.shape[:2]
    past = page_table.shape[1] * cache_cmp_kv.shape[2]

    n_req = bp + bs
    r_pad = -(-n_req // 8) * 8
    c_all = jnp.pad(jnp.concatenate([c_prompt, c_sample], axis=0), ((0, r_pad - n_req), (0, 0)))
    mod = _mod_vectors(c_all, w_mod, b_mod).reshape(depth, r_pad, N_MOD, d)

    pos_p = jnp.arange(tp, dtype=jnp.int32)
    pos_s = past + jnp.arange(ts, dtype=jnp.int32)
    y_p, p_cmp, p_sel, p_win, p_conv = _run_trunk(x_prompt, mod[:, :bp], pos_p, prm, None, PROMPT_TILES)
    y_s, s_cmp, s_sel, s_win, s_conv = _run_trunk(
        x_sample, mod[:, bp:n_req], pos_s, prm,
        (cache_cmp_kv, cache_sel_kv, cache_win_kv, state_conv, page_table), SAMPLE_TILES)
    return (y_p, y_s, p_cmp, p_sel, p_win, p_conv, s_cmp, s_sel, s_win, s_conv)
```

```python
import functools

import jax
import jax.numpy as jnp
from jax import lax
from jax.experimental import pallas as pl
from jax.experimental.pallas import tpu as pltpu

F32 = jnp.float32
MXU_DTYPE = jnp.bfloat16
VMEM_LIMIT_BYTES = 60 * 1024 * 1024
LANES = 128
SUBLANES = 8

N_HEADS = 16
HEAD_DIM = 128
N_KV_HEADS = 4
GROUP_SIZE = N_HEADS // N_KV_HEADS
CMP_BLOCK = 32
SEL_BLOCK = 64
SEL_RATIO = SEL_BLOCK // CMP_BLOCK
N_SELECT = 16
WINDOW = 512
ROPE_THETA = 10000.0
N_MOD = 9
RMS_EPS = 1e-6
LN_EPS = 1e-5
NEG_INF = -1e30
FORCED_SCORE = 1e9
KNOCKED_OUT = -3e38
ATTN_SCALE = HEAD_DIM ** -0.5
Q_WIDTH = N_HEADS * HEAD_DIM
KV_WIDTH = 2 * N_KV_HEADS * HEAD_DIM
GATE_COLS = GROUP_SIZE * 3
PAGES_PER_STEP = 16
DWCONV_COLS = 128
ROW_BLOCK = 32
FFN_PANEL = 512
PROJ_PANEL = 256
EXP2_SCALE = ATTN_SCALE * 1.4426950408889634


def _params(*sem):
    return pltpu.CompilerParams(dimension_semantics=sem, vmem_limit_bytes=VMEM_LIMIT_BYTES)


def _mxu(a, b):
    return jnp.dot(a.astype(MXU_DTYPE), b.astype(MXU_DTYPE), preferred_element_type=F32)


def _mxu_nt(a, b):
    return lax.dot_general(a.astype(MXU_DTYPE), b.astype(MXU_DTYPE),
                           (((1,), (1,)), ((), ())), preferred_element_type=F32)


def _silu(x):
    return x * jax.nn.sigmoid(x)


def _softmax_rows(s):
    e = jnp.exp(s - jnp.max(s, axis=-1, keepdims=True))
    return e / jnp.sum(e, axis=-1, keepdims=True)


def _softmax_scaled(s):
    e = jnp.exp2((s - jnp.max(s, axis=-1, keepdims=True)) * EXP2_SCALE)
    return e / jnp.sum(e, axis=-1, keepdims=True)


def _mod_spec(tm, rows_per_mod, width, slot, by_column=False):
    if rows_per_mod >= tm:
        assert rows_per_mod % tm == 0
        block_rows, idx = 1, (lambda i: (i * tm) // rows_per_mod)
    else:
        assert rows_per_mod == 1
        block_rows, idx = tm, (lambda i: i)
    return pl.BlockSpec((None, block_rows, 1, width), lambda i, j: (slot, idx(i), 0, j if by_column else 0))


def _gain_spec(d, slot):
    return pl.BlockSpec((None, 1, d), lambda i, j: (slot, 0, 0))


def _modulated(x_ref, ng_ref, sh_ref, sc_ref):
    x = x_ref[...]
    gain = ng_ref[...] * (1.0 + sc_ref[:, 0, :])
    return x * lax.rsqrt(jnp.mean(x * x, axis=-1, keepdims=True) + RMS_EPS) * gain + sh_ref[:, 0, :]


def _mod_kernel(c_ref, w_ref, b_ref, o_ref):
    o_ref[...] = _mxu(_silu(c_ref[...]), w_ref[...]) + b_ref[...]


def _mod_vectors(c_all, w_mod, b_mod, tn=1024):
    depth, d, n = w_mod.shape
    r = c_all.shape[0]
    return pl.pallas_call(
        _mod_kernel,
        grid=(depth, n // tn),
        in_specs=[pl.BlockSpec((r, d), lambda l, j: (0, 0)),
                  pl.BlockSpec((None, d, tn), lambda l, j: (l, 0, j)),
                  pl.BlockSpec((None, 1, tn), lambda l, j: (l, 0, j))],
        out_specs=pl.BlockSpec((None, r, tn), lambda l, j: (l, 0, j)),
        out_shape=jax.ShapeDtypeStruct((depth, r, n), F32),
        compiler_params=_params("arbitrary", "arbitrary"),
        name="mod_vectors",
    )(c_all, w_mod, b_mod.reshape(depth, 1, n))


def _ffn_kernel(x_ref, ng_ref, sh_ref, sc_ref, gt_ref, fg_ref, wg_ref, wu_ref, wd_ref, o_ref, h_scr, *, final_norm):
    f = pl.program_id(1)

    def panels(first):
        panel = min(FFN_PANEL, h_scr.shape[0])
        for p0 in range(0, h_scr.shape[0], panel):
            rows = slice(p0, p0 + panel)
            h = h_scr[rows, :]
            g = _mxu(h, wg_ref[...])
            u = _mxu(h, wu_ref[...])
            a = jnp.concatenate([(_silu(g[r0:r0 + ROW_BLOCK]) * u[r0:r0 + ROW_BLOCK]).astype(MXU_DTYPE)
                                 for r0 in range(0, panel, ROW_BLOCK)], axis=0)
            y = _mxu(a, wd_ref[...])
            if first:
                o_ref[rows, :] = y
            else:
                o_ref[rows, :] += y

    @pl.when(f == 0)
    def _():
        h_scr[...] = _modulated(x_ref, ng_ref, sh_ref, sc_ref).astype(h_scr.dtype)
        panels(True)

    @pl.when(f > 0)
    def _():
        panels(False)

    @pl.when(f == pl.num_programs(1) - 1)
    def _():
        y = x_ref[...] + (0.5 * gt_ref[:, 0, :]) * o_ref[...]
        if final_norm:
            y = y * lax.rsqrt(jnp.mean(y * y, axis=-1, keepdims=True) + RMS_EPS) * fg_ref[...]
        o_ref[...] = y


def _ffn(x, norm_tab, mod_tab, final_g, w_gate, w_up, w_down, *, norm_slot, mod_slot, layer, which, rows_per_mod,
         tm, tf, final_norm):
    m, d = x.shape
    dff = w_gate.shape[-1]
    return pl.pallas_call(
        functools.partial(_ffn_kernel, final_norm=final_norm),
        grid=(m // tm, dff // tf),
        in_specs=[pl.BlockSpec((tm, d), lambda i, f: (i, 0)),
                  _gain_spec(d, norm_slot),
                  _mod_spec(tm, rows_per_mod, d, mod_slot), _mod_spec(tm, rows_per_mod, d, mod_slot + 1),
                  _mod_spec(tm, rows_per_mod, d, mod_slot + 2),
                  pl.BlockSpec((1, d), lambda i, f: (0, 0)),
                  pl.BlockSpec((None, None, d, tf), lambda i, f: (layer, which, 0, f)),
                  pl.BlockSpec((None, None, d, tf), lambda i, f: (layer, which, 0, f)),
                  pl.BlockSpec((None, None, tf, d), lambda i, f: (layer, which, f, 0))],
        out_specs=pl.BlockSpec((tm, d), lambda i, f: (i, 0)),
        out_shape=jax.ShapeDtypeStruct((m, d), F32),
        scratch_shapes=[pltpu.VMEM((tm, d), MXU_DTYPE)],
        compiler_params=_params("arbitrary", "arbitrary"),
        name="ffn",
    )(x, norm_tab, mod_tab, mod_tab, mod_tab, final_g, w_gate, w_up, w_down)


def _rope_chunk(z, cos, sin):
    return z * cos + pltpu.roll(z, HEAD_DIM // 2, 1) * sin


def _proj_kernel(x_ref, ng_ref, sh_ref, sc_ref, w_ref, cos_ref, sin_ref, *rest, mode):
    out_refs, h_scr = rest[:-1], rest[-1]

    @pl.when(pl.program_id(1) == 0)
    def _():
        h_scr[...] = _modulated(x_ref, ng_ref, sh_ref, sc_ref).astype(h_scr.dtype)

    tm = h_scr.shape[0]
    panel = min(PROJ_PANEL, tm)
    n_chunks = w_ref.shape[1] // HEAD_DIM
    for p0 in range(0, tm, panel):
        rows = slice(p0, p0 + panel)
        z = _mxu(h_scr[rows, :], w_ref[...])
        if mode == "sigmoid":
            out_refs[0][rows, :] = jax.nn.sigmoid(z)
            continue
        cos, sin = cos_ref[rows, :], sin_ref[rows, :]
        for c in range(n_chunks):
            zc = z[:, c * HEAD_DIM:(c + 1) * HEAD_DIM]
            if mode == "q" or c % 2 == 0:
                zc = _rope_chunk(zc, cos, sin)
            for k, o in enumerate(out_refs):
                if mode == "kv" and k == 0:
                    o[pl.ds(p0 * n_chunks + c, panel, stride=n_chunks), :] = zc
                else:
                    o[rows, c * HEAD_DIM:(c + 1) * HEAD_DIM] = zc.astype(o.dtype)


def _proj(x, norm_tab, mod_tab, w, cos, sin, *, norm_slot, mod_slot, layer, mode, col0, n, out_dtypes, rows_per_mod,
          rows_per_seq, tm, tn):
    m, d = x.shape
    seq_tiles = rows_per_seq // tm
    assert col0 % tn == 0 and n % tn == 0 and rows_per_seq % tm == 0
    if mode == "kv":
        assert tn == KV_WIDTH and out_dtypes[0] == F32
        n_chunks = KV_WIDTH // HEAD_DIM
        out_specs = [pl.BlockSpec((None, tm * n_chunks, HEAD_DIM), lambda i, j: (j, i, 0))] + [
            pl.BlockSpec((None, tm, tn), lambda i, j: (j, i, 0)) for _ in out_dtypes[1:]]
        out_shape = [jax.ShapeDtypeStruct((n // KV_WIDTH, m * n_chunks, HEAD_DIM), F32)] + [
            jax.ShapeDtypeStruct((n // KV_WIDTH, m, KV_WIDTH), dt) for dt in out_dtypes[1:]]
    else:
        out_specs = [pl.BlockSpec((tm, tn), lambda i, j: (i, j)) for _ in out_dtypes]
        out_shape = [jax.ShapeDtypeStruct((m, n), dt) for dt in out_dtypes]
    return pl.pallas_call(
        functools.partial(_proj_kernel, mode=mode),
        grid=(m // tm, n // tn),
        in_specs=[pl.BlockSpec((tm, d), lambda i, j: (i, 0)),
                  _gain_spec(d, norm_slot),
                  _mod_spec(tm, rows_per_mod, d, mod_slot), _mod_spec(tm, rows_per_mod, d, mod_slot + 1),
                  pl.BlockSpec((None, d, tn), lambda i, j: (layer, 0, col0 // tn + j)),
                  pl.BlockSpec((tm, HEAD_DIM), lambda i, j: (i % seq_tiles, 0)),
                  pl.BlockSpec((tm, HEAD_DIM), lambda i, j: (i % seq_tiles, 0))],
        out_specs=out_specs,
        out_shape=out_shape,
        scratch_shapes=[pltpu.VMEM((tm, d), MXU_DTYPE)],
        compiler_params=_params("arbitrary", "arbitrary"),
        name="proj_" + mode,
    )(x, norm_tab, mod_tab, mod_tab, w, cos, sin)


def _q_gates_kernel(x_ref, ng_ref, sh_ref, sc_ref, w_ref, wg_ref, cos_ref, sin_ref, q_ref, g_ref, h_scr, *, n_q):
    j = pl.program_id(1)

    @pl.when(j == 0)
    def _():
        h_scr[...] = _modulated(x_ref, ng_ref, sh_ref, sc_ref).astype(h_scr.dtype)

    tm = h_scr.shape[0]
    panel = min(PROJ_PANEL, tm)

    @pl.when(j < n_q)
    def _():
        for p0 in range(0, tm, panel):
            rows = slice(p0, p0 + panel)
            z = _mxu(h_scr[rows, :], w_ref[...])
            cos, sin = cos_ref[rows, :], sin_ref[rows, :]
            for c in range(z.shape[1] // HEAD_DIM):
                cols = slice(c * HEAD_DIM, (c + 1) * HEAD_DIM)
                q_ref[rows, cols] = _rope_chunk(z[:, cols], cos, sin).astype(q_ref.dtype)

    @pl.when(j == n_q)
    def _():
        for p0 in range(0, tm, panel):
            rows = slice(p0, p0 + panel)
            g_ref[rows, :] = jax.nn.sigmoid(_mxu(h_scr[rows, :], wg_ref[...]))


def _q_gates_proj(x, norm_tab, mod_tab, w, w_gates, cos, sin, *, norm_slot, mod_slot, layer, q_dtype, rows_per_mod,
                  rows_per_seq, tm, tn):
    m, d = x.shape
    n_q = Q_WIDTH // tn
    gw = w_gates.shape[1]
    seq_tiles = rows_per_seq // tm
    assert Q_WIDTH % tn == 0 and rows_per_seq % tm == 0
    return pl.pallas_call(
        functools.partial(_q_gates_kernel, n_q=n_q),
        grid=(m // tm, n_q + 1),
        in_specs=[pl.BlockSpec((tm, d), lambda i, j: (i, 0)),
                  _gain_spec(d, norm_slot),
                  _mod_spec(tm, rows_per_mod, d, mod_slot), _mod_spec(tm, rows_per_mod, d, mod_slot + 1),
                  pl.BlockSpec((None, d, tn), lambda i, j: (layer, 0, jnp.minimum(j, n_q - 1))),
                  pl.BlockSpec((d, gw), lambda i, j: (0, 0)),
                  pl.BlockSpec((tm, HEAD_DIM), lambda i, j: (i % seq_tiles, 0)),
                  pl.BlockSpec((tm, HEAD_DIM), lambda i, j: (i % seq_tiles, 0))],
        out_specs=[pl.BlockSpec((tm, tn), lambda i, j: (i, jnp.minimum(j, n_q - 1))),
                   pl.BlockSpec((tm, gw), lambda i, j: (i, 0))],
        out_shape=[jax.ShapeDtypeStruct((m, Q_WIDTH), q_dtype), jax.ShapeDtypeStruct((m, gw), F32)],
        scratch_shapes=[pltpu.VMEM((tm, d), MXU_DTYPE)],
        compiler_params=_params("arbitrary", "arbitrary"),
        name="proj_q_gates",
    )(x, norm_tab, mod_tab, mod_tab, w, w_gates, cos, sin)


def _glu_kernel(x_ref, ng_ref, sh_ref, sc_ref, wa_ref, wb_ref, o_ref, h_scr):
    @pl.when(pl.program_id(1) == 0)
    def _():
        h_scr[...] = _modulated(x_ref, ng_ref, sh_ref, sc_ref).astype(h_scr.dtype)

    panel = min(PROJ_PANEL, h_scr.shape[0])
    for p0 in range(0, h_scr.shape[0], panel):
        h = h_scr[p0:p0 + panel, :]
        o_ref[p0:p0 + panel, :] = _mxu(h, wa_ref[...]) * jax.nn.sigmoid(_mxu(h, wb_ref[...]))


def _glu_proj(x, norm_tab, mod_tab, w_pw1, *, norm_slot, mod_slot, layer, rows_per_mod, tm, tn):
    m, d = x.shape
    dc = w_pw1.shape[-1] // 2
    return pl.pallas_call(
        _glu_kernel,
        grid=(m // tm, dc // tn),
        in_specs=[pl.BlockSpec((tm, d), lambda i, j: (i, 0)),
                  _gain_spec(d, norm_slot),
                  _mod_spec(tm, rows_per_mod, d, mod_slot), _mod_spec(tm, rows_per_mod, d, mod_slot + 1),
                  pl.BlockSpec((None, d, tn), lambda i, j: (layer, 0, j)),
                  pl.BlockSpec((None, d, tn), lambda i, j: (layer, 0, dc // tn + j))],
        out_specs=pl.BlockSpec((tm, tn), lambda i, j: (i, j)),
        out_shape=jax.ShapeDtypeStruct((m, dc), F32),
        scratch_shapes=[pltpu.VMEM((tm, d), MXU_DTYPE)],
        compiler_params=_params("arbitrary", "arbitrary"),
        name="glu_proj",
    )(x, norm_tab, mod_tab, mod_tab, w_pw1, w_pw1)


def _linres_kernel(a_ref, w_ref, x_ref, gt_ref, o_ref):
    tm = a_ref.shape[0]
    panel = min(PROJ_PANEL, tm)
    for p0 in range(0, tm, panel):
        rows = slice(p0, p0 + panel)
        gate = gt_ref[:, 0, :] if gt_ref.shape[0] == 1 else gt_ref[rows, 0, :]
        o_ref[rows, :] = x_ref[rows, :] + gate * _mxu(a_ref[rows, :], w_ref[...])


def _linres(a, w, x, mod_tab, *, gate_slot, layer, rows_per_mod, tm, tn):
    m, k = a.shape
    d = x.shape[1]
    gspec = _mod_spec(tm, rows_per_mod, tn, gate_slot, by_column=True)
    return pl.pallas_call(
        _linres_kernel,
        grid=(m // tm, d // tn),
        in_specs=[pl.BlockSpec((tm, k), lambda i, j: (i, 0)),
                  pl.BlockSpec((None, k, tn), lambda i, j: (layer, 0, j)),
                  pl.BlockSpec((tm, tn), lambda i, j: (i, j)),
                  gspec],
        out_specs=pl.BlockSpec((tm, tn), lambda i, j: (i, j)),
        out_shape=jax.ShapeDtypeStruct((m, d), F32),
        compiler_params=_params("arbitrary", "arbitrary"),
        name="linres",
    )(a, w, x, mod_tab)


def _compress_kernel(x_ref, pe_ref, w1_ref, w2_ref, o_ref):
    n_chunks = KV_WIDTH // HEAD_DIM
    nb = x_ref.shape[0] // (CMP_BLOCK * n_chunks)
    for kv in range(2):
        acc = jnp.zeros((N_KV_HEADS * nb, HEAD_DIM), F32)
        for c in range(CMP_BLOCK):
            xc = jnp.concatenate([x_ref[pl.ds(c * n_chunks + 2 * g + kv, nb, stride=CMP_BLOCK * n_chunks), :]
                                  for g in range(N_KV_HEADS)], axis=0) + pe_ref[kv, c:c + 1, :]
            acc += _mxu(xc, w1_ref[kv, c * HEAD_DIM:(c + 1) * HEAD_DIM, :])
        y = _mxu(_silu(acc), w2_ref[kv])
        for g in range(N_KV_HEADS):
            o_ref[:, (2 * g + kv) * HEAD_DIM:(2 * g + kv + 1) * HEAD_DIM] = y[g * nb:(g + 1) * nb].astype(o_ref.dtype)


def _compress_rows(rows, pe, w1, w2, *, layer, n_seq):
    n_chunks = KV_WIDTH // HEAD_DIM
    n_blocks = rows.shape[0] // (CMP_BLOCK * n_chunks)
    nb = n_blocks // n_seq
    return pl.pallas_call(
        _compress_kernel,
        grid=(n_seq,),
        in_specs=[pl.BlockSpec((nb * CMP_BLOCK * n_chunks, HEAD_DIM), lambda i: (i, 0)),
                  pl.BlockSpec((None, 2, CMP_BLOCK, HEAD_DIM), lambda i: (layer, 0, 0, 0)),
                  pl.BlockSpec((None, 2, CMP_BLOCK * HEAD_DIM, HEAD_DIM), lambda i: (layer, 0, 0, 0)),
                  pl.BlockSpec((None, 2, HEAD_DIM, HEAD_DIM), lambda i: (layer, 0, 0, 0))],
        out_specs=pl.BlockSpec((nb, KV_WIDTH), lambda i: (i, 0)),
        out_shape=jax.ShapeDtypeStruct((n_blocks, KV_WIDTH), MXU_DTYPE),
        compiler_params=_params("arbitrary"),
        name="compress_rows",
    )(rows, pe, w1, w2)


def _compress_pages_kernel(pt_ref, *refs, blocks_per_page):
    del pt_ref
    pages = refs[:PAGES_PER_STEP]
    pe_ref, w1_ref, w2_ref, o_ref, t_scr = refs[PAGES_PER_STEP:]
    nb = PAGES_PER_STEP * blocks_per_page
    pages_per_group = SUBLANES // blocks_per_page
    n_groups = PAGES_PER_STEP // pages_per_group
    rows = SUBLANES * CMP_BLOCK
    out_row = lax.broadcasted_iota(jnp.int32, (rows, rows), 0)
    in_row = lax.broadcasted_iota(jnp.int32, (rows, rows), 1)
    perm = jnp.where(in_row == (out_row % SUBLANES) * CMP_BLOCK + out_row // SUBLANES, 1.0, 0.0)
    pe_rows = jnp.concatenate([pe_ref[kv] for _ in range(N_KV_HEADS) for kv in range(2)], axis=1)
    pe_rows = jnp.concatenate([pe_rows] * SUBLANES, axis=0)
    n_chunks = KV_WIDTH // HEAD_DIM
    page_rows = pages[0].shape[0] // n_chunks
    for gp in range(n_groups):
        x = jnp.concatenate(
            [jnp.concatenate([pages[gp * pages_per_group + k][pl.ds(ch, page_rows, stride=n_chunks), :]
                              for ch in range(n_chunks)], axis=1)
             for k in range(pages_per_group)], axis=0)
        t_scr[gp] = _mxu(perm, x + pe_rows)
    def block_rows(c, kv):
        return jnp.concatenate(
            [t_scr[:, c * SUBLANES:(c + 1) * SUBLANES,
                   (g * 2 + kv) * HEAD_DIM:(g * 2 + kv + 1) * HEAD_DIM].reshape(nb, HEAD_DIM)
             for g in range(N_KV_HEADS)], axis=0)

    for kv in range(2):
        acc = jnp.zeros((N_KV_HEADS * nb, HEAD_DIM), F32)
        for c in range(0, CMP_BLOCK, 2):
            xc = jnp.concatenate([block_rows(c, kv), block_rows(c + 1, kv)], axis=1)
            acc += _mxu(xc, w1_ref[kv, c * HEAD_DIM:(c + 2) * HEAD_DIM, :])
        y = _mxu(_silu(acc), w2_ref[kv])
        for g in range(N_KV_HEADS):
            o_ref[:, (g * 2 + kv) * HEAD_DIM:(g * 2 + kv + 1) * HEAD_DIM] = (
                y[g * nb:(g + 1) * nb].astype(o_ref.dtype))


def _compress_pages(pool, page_table, pe, w1, w2, *, layer):
    width = KV_WIDTH
    n_chunks = width // HEAD_DIM
    page = pool.shape[2] // n_chunks
    b, n_pages = page_table.shape
    bpp = page // CMP_BLOCK
    nb = PAGES_PER_STEP * bpp
    assert n_pages % PAGES_PER_STEP == 0 and page % CMP_BLOCK == 0 and SUBLANES % bpp == 0
    n_groups = nb // SUBLANES

    def page_spec(p):
        return pl.BlockSpec((None, None, page * n_chunks, HEAD_DIM),
                            lambda bi, gi, pt: (layer, pt[bi * n_pages + gi * PAGES_PER_STEP + p], 0, 0))

    grid_spec = pltpu.PrefetchScalarGridSpec(
        num_scalar_prefetch=1,
        grid=(b, n_pages // PAGES_PER_STEP),
        in_specs=[page_spec(p) for p in range(PAGES_PER_STEP)] + [
            pl.BlockSpec((None, 2, CMP_BLOCK, HEAD_DIM), lambda bi, gi, pt: (layer, 0, 0, 0)),
            pl.BlockSpec((None, 2, CMP_BLOCK * HEAD_DIM, HEAD_DIM), lambda bi, gi, pt: (layer, 0, 0, 0)),
            pl.BlockSpec((None, 2, HEAD_DIM, HEAD_DIM), lambda bi, gi, pt: (layer, 0, 0, 0))],
        out_specs=pl.BlockSpec((None, nb, width), lambda bi, gi, pt: (bi, gi, 0)),
        scratch_shapes=[pltpu.VMEM((n_groups, SUBLANES * CMP_BLOCK, width), F32)],
    )
    return pl.pallas_call(
        functools.partial(_compress_pages_kernel, blocks_per_page=bpp),
        grid_spec=grid_spec,
        out_shape=jax.ShapeDtypeStruct((b, n_pages * bpp, width), MXU_DTYPE),
        compiler_params=_params("arbitrary", "arbitrary"),
        name="compress_pages",
    )(page_table.reshape(-1), *([pool] * PAGES_PER_STEP), pe, w1, w2)


def _pair_sums(imp):
    out = []
    for k in range(imp.shape[1] // LANES):
        x = imp[:, k * LANES:(k + 1) * LANES]
        even = lax.broadcasted_iota(jnp.int32, x.shape, 1) % 2 == 0
        out.append(x + jnp.where(even, pltpu.roll(x, LANES - 1, 1), pltpu.roll(x, 1, 1)))
    return out[0] if len(out) == 1 else jnp.concatenate(out, axis=1)


def _block_scores(imp, q_pos, n_sel_blocks):
    lane = lax.broadcasted_iota(jnp.int32, imp.shape, 1)
    blk = lane // SEL_RATIO
    cur = q_pos // SEL_BLOCK
    forced = (blk == 0) | (blk == cur) | (blk == cur - 1)
    valid = blk * SEL_BLOCK <= q_pos
    score = jnp.where(forced, FORCED_SCORE, jnp.where(valid, _pair_sums(imp), -1.0))
    eligible = (lane % SEL_RATIO == 0) & (blk < n_sel_blocks)
    return jnp.where(eligible, score, KNOCKED_OUT)


def _take_top(work):
    lane = lax.broadcasted_iota(jnp.int32, work.shape, 1).astype(F32)
    top = jnp.max(work, axis=-1, keepdims=True)
    pick = jnp.min(jnp.where(work == top, lane, float(work.shape[1])), axis=-1, keepdims=True)
    return lane == pick, pick


def _with_ones(v):
    return jnp.concatenate([v, jnp.ones(v.shape, v.dtype)], axis=1)


def _attn_prompt_kernel(q_ref, ck_ref, cv_ref, ks_ref, vs_ref, kw_ref, vw_ref, g_ref, o_ref, *, tq, tk, seq):
    i = pl.program_id(2)
    n_cmp = ck_ref.shape[0]
    n_sel_blocks = -(-seq // SEL_BLOCK)
    n_take = min(N_SELECT, n_sel_blocks)
    assert SEL_RATIO == 2 and n_cmp * CMP_BLOCK == seq and n_cmp <= LANES
    q = q_ref[...]
    q4 = jnp.concatenate([q[:, h * HEAD_DIM:(h + 1) * HEAD_DIM] for h in range(GROUP_SIZE)], axis=0)
    q_pos = i * tq + lax.broadcasted_iota(jnp.int32, (tq, 1), 0)

    pad = jnp.zeros((LANES - n_cmp, HEAD_DIM), ck_ref.dtype)
    ck = jnp.concatenate([ck_ref[...], pad], axis=0) if n_cmp < LANES else ck_ref[...]
    cv = jnp.concatenate([cv_ref[...], pad], axis=0) if n_cmp < LANES else cv_ref[...]
    lane = lax.broadcasted_iota(jnp.int32, (1, LANES), 1)
    vis = ((lane + 1) * CMP_BLOCK - 1 <= q_pos) & (lane < n_cmp)
    s = _mxu_nt(q4, ck).reshape(GROUP_SIZE, tq, LANES)
    p = jnp.where(vis[None], _softmax_scaled(jnp.where(vis[None], s, NEG_INF)), 0.0)
    o_c = _mxu(p.reshape(GROUP_SIZE * tq, LANES), cv).reshape(GROUP_SIZE, tq, HEAD_DIM)

    def top_blocks():
        q_pos_t = i * tq + lax.broadcasted_iota(jnp.int32, (1, tq), 1)
        row = lax.broadcasted_iota(jnp.int32, (n_cmp, 1), 0)
        vis_t = (row + 1) * CMP_BLOCK - 1 <= q_pos_t
        imp_t = jnp.zeros((n_cmp, tq), F32)
        for h in range(GROUP_SIZE):
            s_t = jnp.where(vis_t, _mxu_nt(ck_ref[...], q[:, h * HEAD_DIM:(h + 1) * HEAD_DIM]), NEG_INF)
            e = jnp.exp2((s_t - jnp.max(s_t, axis=0, keepdims=True)) * EXP2_SCALE)
            imp_t = imp_t + jnp.where(vis_t, e / jnp.sum(e, axis=0, keepdims=True), 0.0)
        pair = imp_t + pltpu.roll(imp_t, n_cmp - 1, 0)
        blk = row // SEL_RATIO
        cur = q_pos_t // SEL_BLOCK
        forced = (blk == 0) | (blk == cur) | (blk == cur - 1)
        score = jnp.where(forced, FORCED_SCORE, jnp.where(blk * SEL_BLOCK <= q_pos_t, pair, -1.0))
        work = jnp.where((row % SEL_RATIO == 0) & (blk < n_sel_blocks), score, KNOCKED_OUT)
        row_f = row.astype(F32)
        sel_t = jnp.zeros((n_cmp, tq), F32)
        for _ in range(n_take):
            top = jnp.max(work, axis=0, keepdims=True)
            pick = jnp.min(jnp.where(work == top, row_f, float(n_cmp)), axis=0, keepdims=True)
            hit = row_f == pick
            work = jnp.where(hit, KNOCKED_OUT, work)
            sel_t = jnp.where(hit, 1.0, sel_t)
        if n_cmp < LANES:
            sel_t = jnp.concatenate([sel_t, jnp.zeros((LANES - n_cmp, tq), F32)], axis=0)
        return sel_t.T

    def all_blocks():
        return jnp.where((lane % SEL_RATIO == 0) & (lane // SEL_RATIO < n_sel_blocks), 1.0,
                         jnp.zeros((tq, LANES), F32))

    sel = lax.cond(((i + 1) * tq - 1) // SEL_BLOCK + 1 <= n_take, all_blocks, top_blocks).astype(MXU_DTYPE)

    def sel_chunk(kc, carry):
        m_i, l_i, acc = carry
        start = pl.multiple_of(kc * tk, tk)
        k_pos = start + lax.broadcasted_iota(jnp.int32, (1, tk), 1)
        expand = lax.broadcasted_iota(jnp.int32, (LANES, 1), 0) == SEL_RATIO * (k_pos // SEL_BLOCK)
        picked = jnp.dot(sel, jnp.where(expand, 1.0, 0.0).astype(MXU_DTYPE), preferred_element_type=F32)
        bias = jnp.where((picked > 0.5) & (k_pos <= q_pos), 0.0, NEG_INF)[None]
        sc = _mxu_nt(q4, ks_ref[pl.ds(start, tk), :]).reshape(GROUP_SIZE, tq, tk) + bias
        m_new = jnp.maximum(m_i, jnp.max(sc, axis=-1, keepdims=True))
        alpha = jnp.exp2((m_i - m_new) * EXP2_SCALE)
        e = jnp.exp2((sc - m_new) * EXP2_SCALE)
        l_new = alpha * l_i + jnp.sum(e, axis=-1, keepdims=True)
        pv = _mxu(e.reshape(GROUP_SIZE * tq, tk), vs_ref[pl.ds(start, tk), :])
        return m_new, l_new, alpha * acc + pv.reshape(GROUP_SIZE, tq, HEAD_DIM)

    init = (jnp.full((GROUP_SIZE, tq, 1), NEG_INF, F32), jnp.zeros((GROUP_SIZE, tq, 1), F32),
            jnp.zeros((GROUP_SIZE, tq, HEAD_DIM), F32))
    _, l_s, acc_s = lax.fori_loop(0, ((i + 1) * tq + tk - 1) // tk, sel_chunk, init)
    o_s = acc_s / l_s

    span = tq + WINDOW
    w_start = pl.multiple_of(jnp.maximum(i * tq - WINDOW, 0), tq)
    dist = q_pos - (w_start + lax.broadcasted_iota(jnp.int32, (1, span), 1))
    bias = jnp.where((dist >= 0) & (dist < WINDOW), 0.0, NEG_INF)[None]
    sw = _mxu_nt(q4, kw_ref[pl.ds(w_start, span), :]).reshape(GROUP_SIZE, tq, span) + bias
    ew = jnp.exp2((sw - jnp.max(sw, axis=-1, keepdims=True)) * EXP2_SCALE)
    pv = _mxu(ew.reshape(GROUP_SIZE * tq, span), _with_ones(vw_ref[pl.ds(w_start, span), :]))
    o_w = (pv[:, :HEAD_DIM] / pv[:, HEAD_DIM:]).reshape(GROUP_SIZE, tq, HEAD_DIM)

    gates = g_ref[...]
    for h in range(GROUP_SIZE):
        o = (gates[:, 3 * h:3 * h + 1] * o_c[h] + gates[:, 3 * h + 1:3 * h + 2] * o_s[h]
             + gates[:, 3 * h + 2:3 * h + 3] * o_w[h])
        o_ref[:, h * HEAD_DIM:(h + 1) * HEAD_DIM] = o.astype(o_ref.dtype)


def _attn_prompt(q, ckv, kv, gates, *, tq, tk):
    b, t, _ = q.shape
    n_cmp = ckv.shape[1]
    assert t % tq == 0 and t % tk == 0 and t >= tq + WINDOW and WINDOW % tq == 0 and tq % ROW_BLOCK == 0
    gw = GROUP_SIZE * HEAD_DIM

    def kv_spec(branch, part):
        return pl.BlockSpec((None, None, t, HEAD_DIM), lambda bi, g, i: (branch, bi, 0, 2 * g + part))

    return pl.pallas_call(
        functools.partial(_attn_prompt_kernel, tq=tq, tk=tk, seq=t),
        grid=(b, N_KV_HEADS, t // tq),
        in_specs=[pl.BlockSpec((None, tq, gw), lambda bi, g, i: (bi, i, g)),
                  pl.BlockSpec((None, n_cmp, HEAD_DIM), lambda bi, g, i: (bi, 0, 2 * g)),
                  pl.BlockSpec((None, n_cmp, HEAD_DIM), lambda bi, g, i: (bi, 0, 2 * g + 1)),
                  kv_spec(1, 0), kv_spec(1, 1), kv_spec(2, 0), kv_spec(2, 1),
                  pl.BlockSpec((None, tq, LANES), lambda bi, g, i: (bi, i, g))],
        out_specs=pl.BlockSpec((None, tq, gw), lambda bi, g, i: (bi, i, g)),
        out_shape=jax.ShapeDtypeStruct((b, t, Q_WIDTH), MXU_DTYPE),
        compiler_params=_params("arbitrary", "arbitrary", "arbitrary"),
        name="attn_prompt",
    )(q, ckv, ckv, kv, kv, kv, kv, gates)


def _attn_cmp_sample_kernel(q_ref, ckv_ref, o_ref, idx_ref, *, past, n_total, width):
    t_rows = q_ref.shape[0]
    n_cmp = ckv_ref.shape[0]
    n_sel_blocks = -(-n_total // SEL_BLOCK)
    n_take = min(N_SELECT, n_sel_blocks)
    assert SEL_RATIO == 2 and n_take <= LANES and SEL_RATIO * n_sel_blocks <= width
    q_pos = past + lax.broadcasted_iota(jnp.int32, (t_rows, 1), 0)
    lane = lax.broadcasted_iota(jnp.int32, (1, width), 1)
    vis = ((lane + 1) * CMP_BLOCK - 1 <= q_pos) & (lane < n_cmp)
    out_lane = lax.broadcasted_iota(jnp.int32, (1, LANES), 1)
    pad = jnp.zeros((width - n_cmp, HEAD_DIM), ckv_ref.dtype)
    imps = []
    for g in range(N_KV_HEADS):
        ck = jnp.concatenate([ckv_ref[:, 2 * g * HEAD_DIM:(2 * g + 1) * HEAD_DIM], pad], axis=0)
        cv = jnp.concatenate([ckv_ref[:, (2 * g + 1) * HEAD_DIM:(2 * g + 2) * HEAD_DIM], pad], axis=0)
        imp = jnp.zeros((t_rows, width), F32)
        for h in range(GROUP_SIZE):
            cols = slice((g * GROUP_SIZE + h) * HEAD_DIM, (g * GROUP_SIZE + h + 1) * HEAD_DIM)
            s = _mxu_nt(q_ref[:, cols], ck) * ATTN_SCALE
            p = jnp.where(vis, _softmax_rows(jnp.where(vis, s, NEG_INF)), 0.0)
            o_ref[:, cols] = _mxu(p, cv)
            imp = imp + p
        imps.append(imp)
    work = _block_scores(jnp.concatenate(imps, axis=0), jnp.concatenate([q_pos] * N_KV_HEADS, axis=0), n_sel_blocks)
    picks = jnp.zeros((N_KV_HEADS * t_rows, LANES), F32)
    for r in range(n_take):
        hit, pick = _take_top(work)
        work = jnp.where(hit, KNOCKED_OUT, work)
        picks = jnp.where(out_lane == r, pick, picks)
    idx = (picks * (1.0 / SEL_RATIO)).astype(jnp.int32)
    for g in range(N_KV_HEADS):
        idx_ref[g] = idx[g * t_rows:(g + 1) * t_rows]


def _attn_cmp_sample(q, ckv, *, past, n_total):
    b, t, qw = q.shape
    n_cmp = ckv.shape[1]
    width = -(-max(n_cmp, SEL_RATIO * -(-n_total // SEL_BLOCK)) // LANES) * LANES
    return pl.pallas_call(
        functools.partial(_attn_cmp_sample_kernel, past=past, n_total=n_total, width=width),
        grid=(b,),
        in_specs=[pl.BlockSpec((None, t, qw), lambda bi: (bi, 0, 0)),
                  pl.BlockSpec((None, n_cmp, KV_WIDTH), lambda bi: (bi, 0, 0))],
        out_specs=[pl.BlockSpec((None, t, qw), lambda bi: (bi, 0, 0)),
                   pl.BlockSpec((None, N_KV_HEADS, t, LANES), lambda bi: (bi, 0, 0, 0))],
        out_shape=[jax.ShapeDtypeStruct(q.shape, F32),
                   jax.ShapeDtypeStruct((b, N_KV_HEADS, t, LANES), jnp.int32)],
        compiler_params=_params("arbitrary"),
        name="attn_cmp_sample",
    )(q, ckv)


def _attn_win_sample_kernel(q_ref, buf_ref, new_ref, o_ref, *, t_new):
    t_rows = q_ref.shape[0]
    n_chunks = KV_WIDTH // HEAD_DIM
    n_buf = buf_ref.shape[0] // n_chunks
    t_idx = lax.broadcasted_iota(jnp.int32, (GROUP_SIZE * t_rows, 1), 0) % t_rows
    dist_a = t_idx + n_buf - lax.broadcasted_iota(jnp.int32, (1, n_buf), 1)
    row_b = lax.broadcasted_iota(jnp.int32, (1, new_ref.shape[0]), 1)
    mask_a = (dist_a >= 0) & (dist_a < WINDOW)
    mask_b = (row_b <= t_idx) & (t_idx - row_b < WINDOW) & (row_b < t_new)
    for g in range(N_KV_HEADS):
        k_a = buf_ref[pl.ds(2 * g, n_buf, stride=n_chunks), :]
        v_a = buf_ref[pl.ds(2 * g + 1, n_buf, stride=n_chunks), :]
        k_b = new_ref[:, 2 * g * HEAD_DIM:(2 * g + 1) * HEAD_DIM]
        v_b = new_ref[:, (2 * g + 1) * HEAD_DIM:(2 * g + 2) * HEAD_DIM]
        head_cols = [slice((g * GROUP_SIZE + h) * HEAD_DIM, (g * GROUP_SIZE + h + 1) * HEAD_DIM)
                     for h in range(GROUP_SIZE)]
        q = jnp.concatenate([q_ref[:, cols] for cols in head_cols], axis=0)
        s_a = jnp.where(mask_a, _mxu_nt(q, k_a) * ATTN_SCALE, NEG_INF)
        s_b = jnp.where(mask_b, _mxu_nt(q, k_b) * ATTN_SCALE, NEG_INF)
        m = jnp.maximum(jnp.max(s_a, axis=-1, keepdims=True), jnp.max(s_b, axis=-1, keepdims=True))
        e_a, e_b = jnp.exp(s_a - m), jnp.exp(s_b - m)
        l = jnp.sum(e_a, axis=-1, keepdims=True) + jnp.sum(e_b, axis=-1, keepdims=True)
        o = _mxu(e_a / l, v_a) + _mxu(e_b / l, v_b)
        for h, cols in enumerate(head_cols):
            o_ref[:, cols] = o[h * t_rows:(h + 1) * t_rows]


def _attn_win_sample(q, win_buf, kv_new, *, layer, t_new):
    b, t, qw = q.shape
    buf_rows = win_buf.shape[2]
    return pl.pallas_call(
        functools.partial(_attn_win_sample_kernel, t_new=t_new),
        grid=(b,),
        in_specs=[pl.BlockSpec((None, t, qw), lambda bi: (bi, 0, 0)),
                  pl.BlockSpec((None, None, buf_rows, HEAD_DIM), lambda bi: (layer, bi, 0, 0)),
                  pl.BlockSpec((None, kv_new.shape[1], KV_WIDTH), lambda bi: (bi, 0, 0))],
        out_specs=pl.BlockSpec((None, t, qw), lambda bi: (bi, 0, 0)),
        out_shape=jax.ShapeDtypeStruct(q.shape, F32),
        compiler_params=_params("arbitrary"),
        name="attn_win_sample",
    )(q, win_buf, kv_new)


def _attn_sel_sample_kernel(idx_ref, pt_ref, q_ref, *refs, past, t_new, n_slots, t_steps):
    del pt_ref
    blocks = refs[:n_slots]
    new_ref, oc_ref, ow_ref, g_ref, o_ref = refs[n_slots:]
    bi, t, gi = pl.program_id(0), pl.program_id(1), pl.program_id(2)
    base = ((bi * t_steps + t) * N_KV_HEADS + gi) * n_slots
    n_past_blocks = past // SEL_BLOCK
    q_pos = past + t
    q = q_ref[...]
    n_chunks = KV_WIDTH // HEAD_DIM
    lane = lax.broadcasted_iota(jnp.int32, (1, n_slots * SEL_BLOCK), 1)
    pos = lane % SEL_BLOCK
    has_new = jnp.int32(0)
    for n in range(n_slots):
        blk = idx_ref[base + n]
        start = jnp.where(blk < n_past_blocks, blk * SEL_BLOCK, q_pos + 1)
        pos = pos + jnp.where(lane // SEL_BLOCK == n, start, 0)
        has_new = has_new | (blk == n_past_blocks).astype(jnp.int32)
    mask_a = pos <= q_pos
    new = new_ref[...]
    row_b = lax.broadcasted_iota(jnp.int32, (1, new.shape[0]), 1)
    mask_b = (row_b <= t) & (row_b < t_new) & (has_new > 0)
    s_b = jnp.where(mask_b, _mxu_nt(q, new[:, :HEAD_DIM]) * ATTN_SCALE, NEG_INF)
    gates = g_ref[...]

    for g in range(N_KV_HEADS):
        @pl.when(gi == g)
        def _(g=g):
            keys = jnp.concatenate([blk[pl.ds(2 * g, SEL_BLOCK, stride=n_chunks), :] for blk in blocks], axis=0)
            vals = jnp.concatenate([blk[pl.ds(2 * g + 1, SEL_BLOCK, stride=n_chunks), :] for blk in blocks], axis=0)
            s_a = jnp.where(mask_a, _mxu_nt(q, keys) * ATTN_SCALE, NEG_INF)
            m = jnp.maximum(jnp.max(s_a, axis=-1, keepdims=True), jnp.max(s_b, axis=-1, keepdims=True))
            e_a, e_b = jnp.exp(s_a - m), jnp.exp(s_b - m)
            l = jnp.sum(e_a, axis=-1, keepdims=True) + jnp.sum(e_b, axis=-1, keepdims=True)
            o_s = _mxu(e_a / l, vals) + _mxu(e_b / l, new[:, HEAD_DIM:])
            o_ref[...] = gates[:, 0:1] * oc_ref[...] + gates[:, 1:2] * o_s + gates[:, 2:3] * ow_ref[...]


def _attn_sel_sample(idx, page_table, q, pool, kv_new, o_c, o_w, gates, *, layer, past, t_new):
    b, t, g, hg, hd = q.shape
    n_slots = idx.shape[-1]
    n_pages = page_table.shape[1]
    page = past // n_pages
    per_page = page // SEL_BLOCK
    n_past_blocks = past // SEL_BLOCK

    def slot_spec(n):
        def index(bi, ti, gi, idx_s, pt_s):
            blk = jnp.minimum(idx_s[((bi * t + ti) * g + gi) * n_slots + n], n_past_blocks - 1)
            phys = pt_s[bi * n_pages + blk // per_page]
            return (layer, phys, blk % per_page, 0)
        return pl.BlockSpec((None, None, SEL_BLOCK * (KV_WIDTH // HEAD_DIM), HEAD_DIM), index)

    head_spec = pl.BlockSpec((None, None, None, hg, hd), lambda bi, ti, gi, idx_s, pt_s: (bi, ti, gi, 0, 0))
    grid_spec = pltpu.PrefetchScalarGridSpec(
        num_scalar_prefetch=2,
        grid=(b, t, g),
        in_specs=[head_spec] + [slot_spec(n) for n in range(n_slots)] + [
            pl.BlockSpec((None, kv_new.shape[1], 2 * HEAD_DIM), lambda bi, ti, gi, idx_s, pt_s: (bi, 0, gi)),
            head_spec, head_spec,
            pl.BlockSpec((None, None, None, hg, 3), lambda bi, ti, gi, idx_s, pt_s: (bi, ti, gi, 0, 0))],
        out_specs=head_spec,
    )
    return pl.pallas_call(
        functools.partial(_attn_sel_sample_kernel, past=past, t_new=t_new, n_slots=n_slots, t_steps=t),
        grid_spec=grid_spec,
        out_shape=jax.ShapeDtypeStruct(q.shape, F32),
        compiler_params=_params("arbitrary", "arbitrary", "arbitrary"),
        name="attn_sel_sample",
    )(idx.reshape(-1), page_table.reshape(-1), q, *([pool] * n_slots), kv_new, o_c, o_w, gates)


def _dwconv_kernel(cur_ref, prev_ref, buf_ref, w_ref, b_ref, lg_ref, lb_ref, o_ref, full_scr, y_scr, *,
                   tt, halo, width):
    i = pl.program_id(1)
    lead = halo - (width - 1)
    full_scr[0:halo, :] = prev_ref[...]

    @pl.when(i == 0)
    def _():
        full_scr[0:halo, :] = buf_ref[...]

    full_scr[halo:halo + tt, :] = cur_ref[...]
    for c0 in range(0, cur_ref.shape[1], DWCONV_COLS):
        cols = slice(c0, c0 + DWCONV_COLS)
        acc = jnp.zeros((tt, DWCONV_COLS), F32) + b_ref[:, cols]
        for r in range(SUBLANES):
            offsets = [o for o in range(r, lead + width, SUBLANES) if o >= lead]
            rows = tt + (SUBLANES if r else 0)
            group = None
            for o in offsets:
                term = full_scr[o - r:o - r + rows, cols] * w_ref[o - lead:o - lead + 1, cols]
                group = term if group is None else group + term
            if group is not None:
                acc = acc + group[r:r + tt]
        y_scr[:, cols] = acc
    y = y_scr[...]
    mu = jnp.mean(y, axis=-1, keepdims=True)
    var = jnp.mean(jnp.square(y - mu), axis=-1, keepdims=True)
    yn = (y - mu) * lax.rsqrt(var + LN_EPS) * lg_ref[...] + lb_ref[...]
    o_ref[...] = _silu(yn).astype(o_ref.dtype)


def _dwconv(glu, buf, w_dw, b_dw, ln_g, ln_b, *, layer, tt):
    b, t, d = glu.shape
    width = w_dw.shape[1]
    halo = buf.shape[1]
    assert tt % halo == 0 and t % tt == 0 and halo >= width - 1
    per = tt // halo

    def vec_spec():
        return pl.BlockSpec((None, 1, d), lambda bi, i: (layer, 0, 0))

    return pl.pallas_call(
        functools.partial(_dwconv_kernel, tt=tt, halo=halo, width=width),
        grid=(b, t // tt),
        in_specs=[pl.BlockSpec((None, tt, d), lambda bi, i: (bi, i, 0)),
                  pl.BlockSpec((None, halo, d), lambda bi, i: (bi, jnp.maximum(i * per - 1, 0), 0)),
                  pl.BlockSpec((None, halo, d), lambda bi, i: (bi, 0, 0)),
                  pl.BlockSpec((None, width, d), lambda bi, i: (layer, 0, 0)),
                  vec_spec(), vec_spec(), vec_spec()],
        out_specs=pl.BlockSpec((None, tt, d), lambda bi, i: (bi, i, 0)),
        out_shape=jax.ShapeDtypeStruct((b, t, d), MXU_DTYPE),
        scratch_shapes=[pltpu.VMEM((halo + tt, d), F32), pltpu.VMEM((tt, d), F32)],
        compiler_params=_params("arbitrary", "arbitrary"),
        name="dwconv",
    )(glu, glu, buf, w_dw, b_dw, ln_g, ln_b)


def _rope_tables(pos):
    half = HEAD_DIM // 2
    inv = ROPE_THETA ** (-jnp.arange(half, dtype=F32) / half)
    ang = pos.astype(F32)[:, None] * inv[None, :]
    cos, sin = jnp.cos(ang), jnp.sin(ang)
    return jnp.concatenate([cos, cos], axis=1), jnp.concatenate([-sin, sin], axis=1)


def _group_gate_weights(w_gate_cols):
    d = w_gate_cols.shape[0]
    wg = w_gate_cols.reshape(d, N_KV_HEADS, GATE_COLS)
    return jnp.pad(wg, ((0, 0), (0, 0), (0, LANES - GATE_COLS))).reshape(d, N_KV_HEADS * LANES)


def _run_trunk(x, mod, pos, prm, caches, tiles):
    b, t, d = x.shape
    m = b * t
    depth = prm["w_mod"].shape[0]
    tm = tiles["tm"]
    rpm = t if caches is None else 1
    cos, sin = _rope_tables(pos)
    if caches is not None:
        cos, sin = jnp.tile(cos, (b, 1)), jnp.tile(sin, (b, 1))
    rows_per_seq = t if caches is None else m
    x = x.reshape(m, d)
    new_c, new_s, new_w, new_conv = [], [], [], []

    mod_tab = jnp.transpose(mod, (0, 2, 1, 3)).reshape(depth * N_MOD, b, d)
    if caches is not None:
        mod_tab = jnp.repeat(mod_tab, t, axis=1)
    mod_tab = mod_tab[:, :, None, :]
    norm_tab = prm["norm_g"].reshape(depth * 3, 1, d)

    def ffn(x, layer, which, k0):
        return _ffn(x, norm_tab, mod_tab, prm["final_norm_g"].reshape(1, d), prm["ffn_w_gate"], prm["ffn_w_up"],
                    prm["ffn_w_down"], norm_slot=layer * 3 + which * 2, mod_slot=layer * N_MOD + k0, layer=layer,
                    which=which, rows_per_mod=rpm, tm=tiles["tm_ffn"], tf=tiles["tf"],
                    final_norm=(layer == depth - 1 and which == 1))

    for i in range(depth):
        x = ffn(x, i, 0, 0)
        a = i // 2
        mixer = dict(norm_slot=i * 3 + 1, mod_slot=i * N_MOD + 3, rows_per_mod=rpm, tm=tm)
        if i % 2 == 0:
            w_in = prm["attn_w_in"]
            common = dict(rows_per_seq=rows_per_seq, **mixer)
            q_dtype = MXU_DTYPE if caches is None else F32
            q, gates = _q_gates_proj(x, norm_tab, mod_tab, w_in, _group_gate_weights(prm["attn_gate_w"][a]), cos, sin,
                                     layer=a, q_dtype=q_dtype, tn=tiles["tn_out"], **common)
            kv_outs = [F32, MXU_DTYPE] if caches is None else [F32]
            kv = _proj(x, norm_tab, mod_tab, w_in, cos, sin, layer=a, mode="kv", col0=Q_WIDTH, n=3 * KV_WIDTH,
                       out_dtypes=kv_outs, tn=KV_WIDTH, **common)
            kv32 = kv[0]
            new_c.append(kv32[0].reshape(b, t, N_KV_HEADS, 2, HEAD_DIM))
            new_s.append(kv32[1].reshape(b, t, N_KV_HEADS, 2, HEAD_DIM))
            if caches is None:
                ckv = _compress_rows(kv32[0], prm["cmp_pe"], prm["cmp_w1"], prm["cmp_w2"], layer=a, n_seq=b)
                o = _attn_prompt(q.reshape(b, t, Q_WIDTH), ckv.reshape(b, t // CMP_BLOCK, KV_WIDTH),
                                 kv[1].reshape(3, b, t, KV_WIDTH), gates.reshape(b, t, N_KV_HEADS * LANES),
                                 tq=tiles["tq"], tk=tiles["tk"]).reshape(m, Q_WIDTH)
                new_w.append(kv32[2].reshape(b, t, N_KV_HEADS, 2, HEAD_DIM)[:, t - min(WINDOW, t):])
            else:
                pool_c, pool_s, win_buf, _, page_table = caches
                n_layers, n_pool, page = pool_c.shape[:3]
                past = page_table.shape[1] * page
                assert t < CMP_BLOCK and page % SEL_BLOCK == 0
                ckv = _compress_pages(pool_c.reshape(n_layers, n_pool, page * (KV_WIDTH // HEAD_DIM), HEAD_DIM),
                                      page_table, prm["cmp_pe"], prm["cmp_w1"], prm["cmp_w2"], layer=a)
                q3 = q.reshape(b, t, Q_WIDTH)
                o_c, idx = _attn_cmp_sample(q3, ckv, past=past, n_total=past + t)
                t_pad = -(-t // SUBLANES) * SUBLANES
                pad_rows = lambda r: jnp.pad(r.reshape(b, t, KV_WIDTH), ((0, 0), (0, t_pad - t), (0, 0)))
                n_buf = win_buf.shape[2]
                o_w = _attn_win_sample(q3, win_buf.reshape(n_layers, b, n_buf * (KV_WIDTH // HEAD_DIM), HEAD_DIM),
                                       pad_rows(kv32[2]), layer=a, t_new=t)
                n_take = min(N_SELECT, -(-(past + t) // SEL_BLOCK))
                idx = jnp.transpose(idx[..., :n_take], (0, 2, 1, 3))
                heads = lambda v: v.reshape(b, t, N_KV_HEADS, GROUP_SIZE, HEAD_DIM)
                g5 = gates.reshape(b, t, N_KV_HEADS, LANES)[..., :GATE_COLS].reshape(b, t, N_KV_HEADS, GROUP_SIZE, 3)
                o = _attn_sel_sample(idx, page_table, heads(q3),
                                     pool_s.reshape(n_layers, n_pool, page * (KV_WIDTH // HEAD_DIM), HEAD_DIM),
                                     pad_rows(kv32[1]), heads(o_c), heads(o_w), g5, layer=a, past=past,
                                     t_new=t).reshape(m, Q_WIDTH)
                new_w.append(kv32[2].reshape(b, t, N_KV_HEADS, 2, HEAD_DIM))
            x = _linres(o, prm["attn_w_out"], x, mod_tab, gate_slot=i * N_MOD + 5, layer=a, rows_per_mod=rpm, tm=tm,
                        tn=tiles["tn_out"])
        else:
            glu = _glu_proj(x, norm_tab, mod_tab, prm["conv_w_pw1"], layer=a, tn=tiles["tn_out"], **mixer)
            glu = glu.reshape(b, t, d)
            width = prm["conv_w_dw"].shape[1]
            halo = 32
            if caches is None:
                buf = jnp.zeros((b, width - 1, d), F32)
                t_conv = t
                cur = glu
            else:
                buf = caches[3][a]
                t_conv = halo
                cur = jnp.pad(glu, ((0, 0), (0, t_conv - t), (0, 0)))
            if t >= width - 1:
                new_conv.append(glu[:, t - (width - 1):])
            else:
                new_conv.append(jnp.concatenate([buf[:, t:], glu], axis=1))
            buf = jnp.pad(buf, ((0, 0), (halo - (width - 1), 0), (0, 0)))
            vec = lambda v: v.reshape(v.shape[0], 1, d)
            act = _dwconv(cur, buf, prm["conv_w_dw"], vec(prm["conv_b_dw"]), vec(prm["conv_ln_g"]),
                          vec(prm["conv_ln_b"]), layer=a, tt=min(tiles["tt"], t_conv))
            act = act[:, :t].reshape(m, d)
            x = _linres(act, prm["conv_w_pw2"], x, mod_tab, gate_slot=i * N_MOD + 5, layer=a, rows_per_mod=rpm, tm=tm,
                        tn=tiles["tn_out"])
        x = ffn(x, i, 1, 6)
    y = x.reshape(b, t, d)
    new_w = jnp.stack(new_w)
    if caches is not None:
        keep = min(WINDOW, caches[2].shape[2] + t) - t
        new_w = jnp.concatenate([caches[2][:, :, caches[2].shape[2] - keep:], new_w], axis=2)
    return y, jnp.stack(new_c), jnp.stack(new_s), new_w, jnp.stack(new_conv)


PROMPT_TILES = dict(tm=1024, tm_ffn=1024, tf=512, tq=256, tk=512, tn_out=1024, tt=128)
SAMPLE_TILES = dict(tm=32, tm_ffn=32, tf=512, tn_out=1024, tt=32)


def kernel(x_prompt, x_sample, cache_cmp_kv, cache_sel_kv, cache_win_kv, state_conv, page_table, c_prompt, c_sample, w_mod, b_mod, norm_g, ffn_w_gate, ffn_w_up, ffn_w_down, attn_w_in, attn_w_out, cmp_pe, cmp_w1, cmp_w2, conv_w_pw1, conv_w_dw, conv_b_dw, conv_ln_g, conv_ln_b, conv_w_pw2, final_norm_g):
    mxu = lambda w: w.astype(MXU_DTYPE)
    prm = {"w_mod": w_mod, "norm_g": norm_g, "ffn_w_gate": mxu(ffn_w_gate), "ffn_w_up": mxu(ffn_w_up),
           "ffn_w_down": mxu(ffn_w_down), "attn_w_in": mxu(attn_w_in), "attn_gate_w": attn_w_in[..., -N_HEADS * 3:],
           "attn_w_out": mxu(attn_w_out), "cmp_pe": cmp_pe,
           "cmp_w1": cmp_w1, "cmp_w2": cmp_w2, "conv_w_pw1": mxu(conv_w_pw1), "conv_w_dw": conv_w_dw,
           "conv_b_dw": conv_b_dw, "conv_ln_g": conv_ln_g, "conv_ln_b": conv_ln_b, "conv_w_pw2": mxu(conv_w_pw2),
           "final_norm_g": final_norm_g}
    depth, d, _ = w_mod.shape
    bp, tp = x_prompt.shape[:2]
    bs, ts = x_sample.shape[:2]
    past = page_table.shape[1] * cache_cmp_kv.shape[2]

    n_req = bp + bs
    r_pad = -(-n_req // 8) * 8
    c_all = jnp.pad(jnp.concatenate([c_prompt, c_sample], axis=0), ((0, r_pad - n_req), (0, 0)))
    mod = _mod_vectors(c_all, w_mod, b_mod).reshape(depth, r_pad, N_MOD, d)

    pos_p = jnp.arange(tp, dtype=jnp.int32)
    pos_s = past + jnp.arange(ts, dtype=jnp.int32)
    y_p, p_cmp, p_sel, p_win, p_conv = _run_trunk(x_prompt, mod[:, :bp], pos_p, prm, None, PROMPT_TILES)
    y_s, s_cmp, s_sel, s_win, s_conv = _run_trunk(
        x_sample, mod[:, bp:n_req], pos_s, prm,
        (cache_cmp_kv, cache_sel_kv, cache_win_kv, state_conv, page_table), SAMPLE_TILES)
    return (y_p, y_s, p_cmp, p_sel, p_win, p_conv, s_cmp, s_sel, s_win, s_conv)
```
